```python
import math
import jax, jax.numpy as jnp
from jax import lax
import numpy as np

D_MODEL = 1024
BATCH = 8
SEQ = 8192
DEPTH = 2

HEAD_DIM = 64
BLOCK = 128
A_GROUPS = ((128, 1), (512, 4), (2048, 16))
A_HEADS = 8
A_TOTAL_HEADS = len(A_GROUPS) * A_HEADS
B_Q_HEADS = 8
B_KV_HEADS = 2
B_WINDOW = 128
C_HEADS = 8
C_Q_RANK = 256
C_KV_RANK = 128
C_NOPE = 64
C_ROPE = 32
C_V = 64
ROPE_BASE = 10000.0
REL_BUCKETS = 32
REL_MAX_DIST = 2048
N_REL_HEADS = A_TOTAL_HEADS + B_Q_HEADS
N_BRANCH = 3
BRANCH_WIDTH = 512
D_FF = 2816
CONV_WIDTH = 3
ALPHA = (2 * DEPTH) ** 0.25
BETA = (8 * DEPTH) ** -0.25
LN_EPS = 1e-5
RMS_EPS = 1e-6
NEG = -1e30

A_QKV_COLS = len(A_GROUPS) * 3 * A_HEADS * HEAD_DIM
B_Q_COLS = B_Q_HEADS * HEAD_DIM
B_KV_COLS = 2 * B_KV_HEADS * HEAD_DIM
C_DQ_COLS = C_Q_RANK
C_DKV_COLS = C_KV_RANK + C_ROPE
GATE_COLS = N_BRANCH * D_MODEL
IN_SPLITS = (A_QKV_COLS,
             A_QKV_COLS + B_Q_COLS,
             A_QKV_COLS + B_Q_COLS + B_KV_COLS,
             A_QKV_COLS + B_Q_COLS + B_KV_COLS + C_DQ_COLS,
             A_QKV_COLS + B_Q_COLS + B_KV_COLS + C_DQ_COLS + C_DKV_COLS)
D_IN = IN_SPLITS[-1] + GATE_COLS

kernel_name = "hybrid_dilated_swa_mla_convglu_block"


def layer_norm(x, g, b):
    xf = x.astype(jnp.float32)
    mu = xf.mean(-1, keepdims=True)
    var = jnp.square(xf - mu).mean(-1, keepdims=True)
    y = (xf - mu) * lax.rsqrt(var + LN_EPS) * g.astype(jnp.float32) + b.astype(jnp.float32)
    return y.astype(x.dtype)


def rms_norm(x, g):
    xf = x.astype(jnp.float32)
    y = xf * lax.rsqrt(jnp.mean(xf * xf, -1, keepdims=True) + RMS_EPS) * g.astype(jnp.float32)
    return y.astype(x.dtype)


def t5_bucket(dist):
    n = jnp.maximum(dist, 0)
    max_exact = REL_BUCKETS // 2
    scaled = jnp.log(jnp.maximum(n, 1).astype(jnp.float32) / max_exact) / math.log(REL_MAX_DIST / max_exact)
    large = max_exact + (scaled * (REL_BUCKETS - max_exact)).astype(jnp.int32)
    return jnp.where(n < max_exact, n, jnp.minimum(large, REL_BUCKETS - 1))


def apply_rope(x, cos, sin):
    x1, x2 = jnp.split(x, 2, axis=-1)
    c = cos[:, None, :].astype(x.dtype)
    s = sin[:, None, :].astype(x.dtype)
    return jnp.concatenate([x1 * c - x2 * s, x1 * s + x2 * c], axis=-1)


def dilated_group_attention(q, k, v, bias_table, window, dilation):
    b, s, h, dh = q.shape
    w = window // dilation
    span = w * dilation
    s_pad = -(-s // span) * span
    n_sub = s_pad // dilation
    nb = n_sub // w

    def to_blocks(t):
        t = jnp.pad(t, ((0, 0), (0, s_pad - s), (0, 0), (0, 0)))
        t = t.reshape(b, n_sub, dilation, h, dh).transpose(0, 2, 1, 3, 4)
        return t.reshape(b, dilation, nb, w, h, dh)

    def with_prev(t):
        prev = jnp.pad(t[:, :, :-1], ((0, 0), (0, 0), (1, 0), (0, 0), (0, 0), (0, 0)))
        return jnp.concatenate([prev, t], axis=3)

    qb = to_blocks(q)
    kb = with_prev(to_blocks(k))
    vb = with_prev(to_blocks(v))
    logits = jnp.einsum('brnqhd,brnkhd->brnhqk', qb, kb).astype(jnp.float32) * (dh ** -0.5)
    qi = jnp.arange(w)[:, None]
    ki = jnp.arange(2 * w)[None, :]
    step = w + qi - ki
    band = (step >= 0) & (step <= w)
    valid = band[None] & ((jnp.arange(nb)[:, None, None] > 0) | (ki >= w)[None])
    bias = bias_table[t5_bucket(step * dilation)].astype(jnp.float32).transpose(2, 0, 1)
    logits = jnp.where(valid[None, None, :, None], logits + bias, NEG)
    m = logits.max(-1, keepdims=True)
    p = jnp.exp(logits - m)
    l = p.sum(-1, keepdims=True)
    o = jnp.einsum('brnhqk,brnkhd->brnqhd', (p / l).astype(v.dtype), vb)
    lse = (m + jnp.log(l))[..., 0]
    o = o.reshape(b, dilation, n_sub, h, dh).transpose(0, 2, 1, 3, 4).reshape(b, s_pad, h, dh)[:, :s]
    lse = lse.transpose(0, 1, 2, 4, 3).reshape(b, dilation, n_sub, h).transpose(0, 2, 1, 3)
    lse = lse.reshape(b, s_pad, h)[:, :s]
    return o, lse


def sliding_window_sink_attention(q, k, v, sinks, bias_table):
    b, s, hq, dh = q.shape
    hkv = k.shape[2]
    g = hq // hkv
    nb = s // BLOCK
    qb = q.reshape(b, nb, BLOCK, hkv, g, dh)

    def with_prev(t):
        t = t.reshape(b, nb, BLOCK, hkv, dh)
        prev = jnp.pad(t[:, :-1], ((0, 0), (1, 0), (0, 0), (0, 0), (0, 0)))
        return jnp.concatenate([prev, t], axis=2)

    kb = with_prev(k)
    vb = with_prev(v)
    logits = jnp.einsum('bnqhgd,bnchd->bnhgqc', qb, kb).astype(jnp.float32) * (dh ** -0.5)
    qi = jnp.arange(BLOCK)[:, None]
    ci = jnp.arange(2 * BLOCK)[None, :]
    dist = BLOCK + qi - ci
    valid = ((dist >= 0) & (dist < B_WINDOW))[None] & ((jnp.arange(nb)[:, None, None] > 0) | (ci >= BLOCK)[None])
    bias = bias_table[t5_bucket(dist)].astype(jnp.float32).transpose(2, 0, 1).reshape(hkv, g, BLOCK, 2 * BLOCK)
    logits = jnp.where(valid[None, :, None, None], logits + bias, NEG)
    sink = sinks.astype(jnp.float32).reshape(1, 1, hkv, g, 1, 1)
    m = jnp.maximum(logits.max(-1, keepdims=True), sink)
    p = jnp.exp(logits - m)
    denom = p.sum(-1, keepdims=True) + jnp.exp(sink - m)
    o = jnp.einsum('bnhgqc,bnchd->bnqhgd', (p / denom).astype(v.dtype), vb)
    return o.reshape(b, s, hq * dh)


def mla_attention(cq, c_kv, k_rope, q_norm_g, kv_norm_g, w_uq, w_ukv, cos, sin):
    b, s, _ = cq.shape
    q = (rms_norm(cq, q_norm_g) @ w_uq).reshape(b, s, C_HEADS, C_NOPE + C_ROPE)
    q_nope = q[..., :C_NOPE]
    q_rope = apply_rope(q[..., C_NOPE:], cos, sin)
    kv = (rms_norm(c_kv, kv_norm_g) @ w_ukv).reshape(b, s, C_HEADS, C_NOPE + C_V)
    k_nope = kv[..., :C_NOPE]
    v = kv[..., C_NOPE:]
    k_r = apply_rope(k_rope[:, :, None, :], cos, sin)[:, :, 0]
    nb = s // BLOCK
    qn_b = q_nope.reshape(b, nb, BLOCK, C_HEADS, C_NOPE).swapaxes(0, 1)
    qr_b = q_rope.reshape(b, nb, BLOCK, C_HEADS, C_ROPE).swapaxes(0, 1)
    kpos = jnp.arange(s)
    scale = (C_NOPE + C_ROPE) ** -0.5

    def one_block(args):
        i, qn, qr = args
        logits = (jnp.einsum('bqhd,bkhd->bhqk', qn, k_nope)
                  + jnp.einsum('bqhr,bkr->bhqk', qr, k_r)).astype(jnp.float32) * scale
        qpos = i * BLOCK + jnp.arange(BLOCK)
        logits = jnp.where(kpos[None, :] <= qpos[:, None], logits, NEG)
        p = jax.nn.softmax(logits, axis=-1).astype(v.dtype)
        return jnp.einsum('bhqk,bkhd->bqhd', p, v)

    o = lax.map(one_block, (jnp.arange(nb), qn_b, qr_b))
    return o.swapaxes(0, 1).reshape(b, s, C_HEADS * C_V)


def hybrid_mixer(x, w_in, b_gate, sinks, q_norm_g, kv_norm_g, w_uq, w_ukv, w_branch, w_out,
                 rel_table, cos, sin):
    b, s, _ = x.shape
    proj = x @ w_in
    a_qkv, b_q, b_kv, c_q, c_dkv, gate_pre = jnp.split(proj, IN_SPLITS, axis=-1)

    a_qkv = a_qkv.reshape(b, s, len(A_GROUPS), 3, A_HEADS, HEAD_DIM)
    outs, lses = [], []
    for gi, (window, dil) in enumerate(A_GROUPS):
        o_g, lse_g = dilated_group_attention(a_qkv[:, :, gi, 0], a_qkv[:, :, gi, 1], a_qkv[:, :, gi, 2],
                                             rel_table[:, gi * A_HEADS:(gi + 1) * A_HEADS], window, dil)
        outs.append(o_g)
        lses.append(lse_g)
    wts = jax.nn.softmax(jnp.stack(lses, axis=0), axis=0)
    o_a = jnp.einsum('gbsh,gbshd->bshd', wts, jnp.stack(outs, axis=0).astype(jnp.float32))
    o_a = o_a.astype(x.dtype).reshape(b, s, A_HEADS * HEAD_DIM)

    b_k, b_v = jnp.split(b_kv, 2, axis=-1)
    o_b = sliding_window_sink_attention(b_q.reshape(b, s, B_Q_HEADS, HEAD_DIM),
                                        b_k.reshape(b, s, B_KV_HEADS, HEAD_DIM),
                                        b_v.reshape(b, s, B_KV_HEADS, HEAD_DIM),
                                        sinks, rel_table[:, A_TOTAL_HEADS:])

    c_kv, k_rope = jnp.split(c_dkv, [C_KV_RANK], axis=-1)
    o_c = mla_attention(c_q, c_kv, k_rope, q_norm_g, kv_norm_g, w_uq, w_ukv, cos, sin)

    gates = jax.nn.sigmoid((gate_pre + b_gate).astype(jnp.float32)).astype(x.dtype)
    gates = gates.reshape(b, s, N_BRANCH, D_MODEL)
    merged = (gates[:, :, 0] * (o_a @ w_branch[0])
              + gates[:, :, 1] * (o_b @ w_branch[1])
              + gates[:, :, 2] * (o_c @ w_branch[2]))
    return merged @ w_out


def conv_glu_ffn(x, w_up, conv_w, conv_b, w_down):
    u = x @ w_up
    c = u.shape[-1]
    u = lax.conv_general_dilated(u, conv_w[:, None, :].astype(u.dtype), window_strides=(1,),
                                 padding=[(CONV_WIDTH - 1, 0)],
                                 dimension_numbers=('NWC', 'WIO', 'NWC'),
                                 feature_group_count=c) + conv_b
    gate, val = jnp.split(u, 2, axis=-1)
    return (jax.nn.silu(gate) * val) @ w_down


def _fwd_setup_inputs(seed: int = 0) -> dict:
    key = jax.random.key(seed)
    ks = jax.random.split(key, 20)
    f32 = jnp.float32
    nrm = lambda k, shape, scale: jax.random.normal(k, shape, f32) * scale
    return {
        'x': jax.random.normal(ks[0], (BATCH, SEQ, D_MODEL), f32),
        'rel_table': nrm(ks[1], (REL_BUCKETS, N_REL_HEADS), 0.2),
        'w_in': nrm(ks[2], (DEPTH, D_MODEL, D_IN), D_MODEL ** -0.5),
        'b_gate': nrm(ks[3], (DEPTH, GATE_COLS), 0.01),
        'sinks': nrm(ks[4], (DEPTH, B_Q_HEADS), 0.5),
        'q_norm_g': 1.0 + nrm(ks[5], (DEPTH, C_Q_RANK), 0.05),
        'kv_norm_g': 1.0 + nrm(ks[6], (DEPTH, C_KV_RANK), 0.05),
        'w_uq': nrm(ks[7], (DEPTH, C_Q_RANK, C_HEADS * (C_NOPE + C_ROPE)), C_Q_RANK ** -0.5),
        'w_ukv': nrm(ks[8], (DEPTH, C_KV_RANK, C_HEADS * (C_NOPE + C_V)), C_KV_RANK ** -0.5),
        'w_branch': nrm(ks[9], (DEPTH, N_BRANCH, BRANCH_WIDTH, D_MODEL), BRANCH_WIDTH ** -0.5 * BETA),
        'w_out': nrm(ks[10], (DEPTH, D_MODEL, D_MODEL), D_MODEL ** -0.5 * BETA),
        'ln1_g': 1.0 + nrm(ks[11], (DEPTH, D_MODEL), 0.05),
        'ln1_b': nrm(ks[12], (DEPTH, D_MODEL), 0.01),
        'w_ffn_up': nrm(ks[13], (DEPTH, D_MODEL, 2 * D_FF), D_MODEL ** -0.5),
        'conv_w': nrm(ks[14], (DEPTH, CONV_WIDTH, 2 * D_FF), CONV_WIDTH ** -0.5),
        'conv_b': nrm(ks[15], (DEPTH, 2 * D_FF), 0.01),
        'w_ffn_down': nrm(ks[16], (DEPTH, D_FF, D_MODEL), D_FF ** -0.5 * BETA),
        'ln2_g': 1.0 + nrm(ks[17], (DEPTH, D_MODEL), 0.05),
        'ln2_b': nrm(ks[18], (DEPTH, D_MODEL), 0.01),
    }


def _fwd_reference(x, rel_table, w_in, b_gate, sinks, q_norm_g, kv_norm_g, w_uq, w_ukv, w_branch, w_out,
              ln1_g, ln1_b, w_ffn_up, conv_w, conv_b, w_ffn_down, ln2_g, ln2_b):
    s = x.shape[1]
    pos = jnp.arange(s, dtype=jnp.float32)
    inv_freq = ROPE_BASE ** (-jnp.arange(0, C_ROPE, 2, dtype=jnp.float32) / C_ROPE)
    ang = pos[:, None] * inv_freq[None, :]
    cos, sin = jnp.cos(ang), jnp.sin(ang)
    for l in range(DEPTH):
        mix = hybrid_mixer(x, w_in[l], b_gate[l], sinks[l], q_norm_g[l], kv_norm_g[l], w_uq[l], w_ukv[l],
                           w_branch[l], w_out[l], rel_table, cos, sin)
        x = layer_norm(ALPHA * x + mix, ln1_g[l], ln1_b[l])
        ff = conv_glu_ffn(x, w_ffn_up[l], conv_w[l], conv_b[l], w_ffn_down[l])
        x = layer_norm(ALPHA * x + ff, ln2_g[l], ln2_b[l])
    return x


import jax as _jax
import jax.numpy as _jnp

TWIN_FORMAT = 'train_step'
FWD_PARAMS = ['x', 'rel_table', 'w_in', 'b_gate', 'sinks', 'q_norm_g', 'kv_norm_g', 'w_uq', 'w_ukv', 'w_branch', 'w_out', 'ln1_g', 'ln1_b', 'w_ffn_up', 'conv_w', 'conv_b', 'w_ffn_down', 'ln2_g', 'ln2_b']
TWIN_WEIGHTS = ['rel_table', 'w_in', 'b_gate', 'sinks', 'q_norm_g', 'kv_norm_g', 'w_uq', 'w_ukv', 'w_branch', 'w_out', 'ln1_g', 'ln1_b', 'w_ffn_up', 'conv_w', 'conv_b', 'w_ffn_down', 'ln2_g', 'ln2_b']
TWIN_DIFF_INPUT = 'x'
TWIN_INPUTS = ['x', 'rel_table', 'w_in', 'b_gate', 'sinks', 'q_norm_g', 'kv_norm_g', 'w_uq', 'w_ukv', 'w_branch', 'w_out', 'ln1_g', 'ln1_b', 'w_ffn_up', 'conv_w', 'conv_b', 'w_ffn_down', 'ln2_g', 'ln2_b', 'loss_target', 'm_rel_table', 'm_w_in', 'm_b_gate', 'm_sinks', 'm_q_norm_g', 'm_kv_norm_g', 'm_w_uq', 'm_w_ukv', 'm_w_branch', 'm_w_out', 'm_ln1_g', 'm_ln1_b', 'm_w_ffn_up', 'm_conv_w', 'm_conv_b', 'm_w_ffn_down', 'm_ln2_g', 'm_ln2_b', 'v_rel_table', 'v_w_in', 'v_b_gate', 'v_sinks', 'v_q_norm_g', 'v_kv_norm_g', 'v_w_uq', 'v_w_ukv', 'v_w_branch', 'v_w_out', 'v_ln1_g', 'v_ln1_b', 'v_w_ffn_up', 'v_conv_w', 'v_conv_b', 'v_w_ffn_down', 'v_ln2_g', 'v_ln2_b']
TWIN_OUTPUTS = ['loss', 'grad_x', 'grad_rel_table', 'grad_w_in', 'grad_b_gate', 'grad_sinks', 'grad_q_norm_g', 'grad_kv_norm_g', 'grad_w_uq', 'grad_w_ukv', 'grad_w_branch', 'grad_w_out', 'grad_ln1_g', 'grad_ln1_b', 'grad_w_ffn_up', 'grad_conv_w', 'grad_conv_b', 'grad_w_ffn_down', 'grad_ln2_g', 'grad_ln2_b', 'delta_rel_table', 'delta_w_in', 'delta_b_gate', 'delta_sinks', 'delta_q_norm_g', 'delta_kv_norm_g', 'delta_w_uq', 'delta_w_ukv', 'delta_w_branch', 'delta_w_out', 'delta_ln1_g', 'delta_ln1_b', 'delta_w_ffn_up', 'delta_conv_w', 'delta_conv_b', 'delta_w_ffn_down', 'delta_ln2_g', 'delta_ln2_b', 'new_m_rel_table', 'new_m_w_in', 'new_m_b_gate', 'new_m_sinks', 'new_m_q_norm_g', 'new_m_kv_norm_g', 'new_m_w_uq', 'new_m_w_ukv', 'new_m_w_branch', 'new_m_w_out', 'new_m_ln1_g', 'new_m_ln1_b', 'new_m_w_ffn_up', 'new_m_conv_w', 'new_m_conv_b', 'new_m_w_ffn_down', 'new_m_ln2_g', 'new_m_ln2_b', 'new_v_rel_table', 'new_v_w_in', 'new_v_b_gate', 'new_v_sinks', 'new_v_q_norm_g', 'new_v_kv_norm_g', 'new_v_w_uq', 'new_v_w_ukv', 'new_v_w_branch', 'new_v_w_out', 'new_v_ln1_g', 'new_v_ln1_b', 'new_v_w_ffn_up', 'new_v_conv_w', 'new_v_conv_b', 'new_v_w_ffn_down', 'new_v_ln2_g', 'new_v_ln2_b']
TWIN_LEAF_KINDS = {'loss': 'loss', 'grad_x': 'grad_x', 'grad_rel_table': 'grad_w', 'grad_w_in': 'grad_w', 'grad_b_gate': 'grad_w', 'grad_sinks': 'grad_w', 'grad_q_norm_g': 'grad_w', 'grad_kv_norm_g': 'grad_w', 'grad_w_uq': 'grad_w', 'grad_w_ukv': 'grad_w', 'grad_w_branch': 'grad_w', 'grad_w_out': 'grad_w', 'grad_ln1_g': 'grad_w', 'grad_ln1_b': 'grad_w', 'grad_w_ffn_up': 'grad_w', 'grad_conv_w': 'grad_w', 'grad_conv_b': 'grad_w', 'grad_w_ffn_down': 'grad_w', 'grad_ln2_g': 'grad_w', 'grad_ln2_b': 'grad_w', 'delta_rel_table': 'delta_w', 'delta_w_in': 'delta_w', 'delta_b_gate': 'delta_w', 'delta_sinks': 'delta_w', 'delta_q_norm_g': 'delta_w', 'delta_kv_norm_g': 'delta_w', 'delta_w_uq': 'delta_w', 'delta_w_ukv': 'delta_w', 'delta_w_branch': 'delta_w', 'delta_w_out': 'delta_w', 'delta_ln1_g': 'delta_w', 'delta_ln1_b': 'delta_w', 'delta_w_ffn_up': 'delta_w', 'delta_conv_w': 'delta_w', 'delta_conv_b': 'delta_w', 'delta_w_ffn_down': 'delta_w', 'delta_ln2_g': 'delta_w', 'delta_ln2_b': 'delta_w', 'new_m_rel_table': 'new_m', 'new_m_w_in': 'new_m', 'new_m_b_gate': 'new_m', 'new_m_sinks': 'new_m', 'new_m_q_norm_g': 'new_m', 'new_m_kv_norm_g': 'new_m', 'new_m_w_uq': 'new_m', 'new_m_w_ukv': 'new_m', 'new_m_w_branch': 'new_m', 'new_m_w_out': 'new_m', 'new_m_ln1_g': 'new_m', 'new_m_ln1_b': 'new_m', 'new_m_w_ffn_up': 'new_m', 'new_m_conv_w': 'new_m', 'new_m_conv_b': 'new_m', 'new_m_w_ffn_down': 'new_m', 'new_m_ln2_g': 'new_m', 'new_m_ln2_b': 'new_m', 'new_v_rel_table': 'new_v', 'new_v_w_in': 'new_v', 'new_v_b_gate': 'new_v', 'new_v_sinks': 'new_v', 'new_v_q_norm_g': 'new_v', 'new_v_kv_norm_g': 'new_v', 'new_v_w_uq': 'new_v', 'new_v_w_ukv': 'new_v', 'new_v_w_branch': 'new_v', 'new_v_w_out': 'new_v', 'new_v_ln1_g': 'new_v', 'new_v_ln1_b': 'new_v', 'new_v_w_ffn_up': 'new_v', 'new_v_conv_w': 'new_v', 'new_v_conv_b': 'new_v', 'new_v_w_ffn_down': 'new_v', 'new_v_ln2_g': 'new_v', 'new_v_ln2_b': 'new_v'}


def _forward(args):
    return _fwd_reference(*[args[k] for k in FWD_PARAMS])


def _output_shape():
    def fwd():
        inp = _fwd_setup_inputs(0)
        return _fwd_reference(*[inp[k] for k in FWD_PARAMS])
    out = _jax.eval_shape(fwd)
    return out.shape, out.dtype

N_MICROBATCH = 1
ADAM_LR = 0.001
ADAM_B1 = 0.9
ADAM_B2 = 0.999
ADAM_EPS = 1e-08
ADAM_WD = 0.01
ADAM_STEP = 10
PER_EXAMPLE_BATCH_AXIS = {'x': 0, 'loss_target': 0}
SHARED_INPUTS = []
_WEIGHT_DTYPES = {'rel_table': _jnp.float32, 'w_in': _jnp.float32, 'b_gate': _jnp.float32, 'sinks': _jnp.float32, 'q_norm_g': _jnp.float32, 'kv_norm_g': _jnp.float32, 'w_uq': _jnp.float32, 'w_ukv': _jnp.float32, 'w_branch': _jnp.float32, 'w_out': _jnp.float32, 'ln1_g': _jnp.float32, 'ln1_b': _jnp.float32, 'w_ffn_up': _jnp.float32, 'conv_w': _jnp.float32, 'conv_b': _jnp.float32, 'w_ffn_down': _jnp.float32, 'ln2_g': _jnp.float32, 'ln2_b': _jnp.float32}
MOMENT_SCALE = {'rel_table': 7.843820e-03, 'w_in': 4.960174e-03, 'b_gate': 2.051620e-03, 'sinks': 5.498038e-03, 'q_norm_g': 9.283391e-03, 'kv_norm_g': 2.095322e-02, 'w_uq': 5.425811e-03, 'w_ukv': 7.002047e-03, 'w_branch': 1.059538e-02, 'w_out': 1.829817e-02, 'ln1_g': 5.917872e+00, 'ln1_b': 4.886930e-01, 'w_ffn_up': 3.268367e-02, 'conv_w': 3.308917e-02, 'conv_b': 3.454084e-02, 'w_ffn_down': 1.074907e-01, 'ln2_g': 4.601451e+01, 'ln2_b': 7.518248e-01}


def _to_microbatches(a, axis):
    t = _jnp.moveaxis(a, axis, 0)
    t = t.reshape((N_MICROBATCH, t.shape[0] // N_MICROBATCH) + t.shape[1:])
    return _jnp.moveaxis(t, 1, axis + 1)


def setup_inputs(seed: int = 0) -> dict:
    inp = _fwd_setup_inputs(seed)
    key = _jax.random.fold_in(_jax.random.key(seed), 7919)
    shape, _ = _output_shape()
    out = dict(inp)
    out["loss_target"] = _jax.random.normal(_jax.random.fold_in(key, 0), shape, _jnp.float32)
    for i, name in enumerate(TWIN_WEIGHTS):
        w = inp[name].astype(_jnp.float32)
        if MOMENT_SCALE is None:
            s = _jnp.sqrt(_jnp.mean(_jnp.square(w)) + 1e-30)
        else:
            s = MOMENT_SCALE[name]
        km, kv = _jax.random.split(_jax.random.fold_in(key, i + 1))
        out[name] = w
        out["m_" + name] = s * _jax.random.normal(km, w.shape, _jnp.float32)
        out["v_" + name] = (s * s) * _jax.random.uniform(kv, w.shape, _jnp.float32, 0.5, 1.5)
    if N_MICROBATCH > 1:
        for name, axis in PER_EXAMPLE_BATCH_AXIS.items():
            out[name] = _to_microbatches(out[name], axis)
    return {'x': out['x'], 'rel_table': out['rel_table'], 'w_in': out['w_in'], 'b_gate': out['b_gate'], 'sinks': out['sinks'], 'q_norm_g': out['q_norm_g'], 'kv_norm_g': out['kv_norm_g'], 'w_uq': out['w_uq'], 'w_ukv': out['w_ukv'], 'w_branch': out['w_branch'], 'w_out': out['w_out'], 'ln1_g': out['ln1_g'], 'ln1_b': out['ln1_b'], 'w_ffn_up': out['w_ffn_up'], 'conv_w': out['conv_w'], 'conv_b': out['conv_b'], 'w_ffn_down': out['w_ffn_down'], 'ln2_g': out['ln2_g'], 'ln2_b': out['ln2_b'], 'loss_target': out['loss_target'], 'm_rel_table': out['m_rel_table'], 'm_w_in': out['m_w_in'], 'm_b_gate': out['m_b_gate'], 'm_sinks': out['m_sinks'], 'm_q_norm_g': out['m_q_norm_g'], 'm_kv_norm_g': out['m_kv_norm_g'], 'm_w_uq': out['m_w_uq'], 'm_w_ukv': out['m_w_ukv'], 'm_w_branch': out['m_w_branch'], 'm_w_out': out['m_w_out'], 'm_ln1_g': out['m_ln1_g'], 'm_ln1_b': out['m_ln1_b'], 'm_w_ffn_up': out['m_w_ffn_up'], 'm_conv_w': out['m_conv_w'], 'm_conv_b': out['m_conv_b'], 'm_w_ffn_down': out['m_w_ffn_down'], 'm_ln2_g': out['m_ln2_g'], 'm_ln2_b': out['m_ln2_b'], 'v_rel_table': out['v_rel_table'], 'v_w_in': out['v_w_in'], 'v_b_gate': out['v_b_gate'], 'v_sinks': out['v_sinks'], 'v_q_norm_g': out['v_q_norm_g'], 'v_kv_norm_g': out['v_kv_norm_g'], 'v_w_uq': out['v_w_uq'], 'v_w_ukv': out['v_w_ukv'], 'v_w_branch': out['v_w_branch'], 'v_w_out': out['v_w_out'], 'v_ln1_g': out['v_ln1_g'], 'v_ln1_b': out['v_ln1_b'], 'v_w_ffn_up': out['v_w_ffn_up'], 'v_conv_w': out['v_conv_w'], 'v_conv_b': out['v_conv_b'], 'v_w_ffn_down': out['v_w_ffn_down'], 'v_ln2_g': out['v_ln2_g'], 'v_ln2_b': out['v_ln2_b']}


def _loss(weights, diff, rest, loss_target):
    with _jax.named_scope("forward"):
        args = {**rest, TWIN_DIFF_INPUT: diff, **{k: w.astype(_WEIGHT_DTYPES[k]) for k, w in weights.items()}}
        y = _forward(args)
    with _jax.named_scope("loss_head"):
        err = _jnp.square(y.astype(_jnp.float32) - loss_target)
        return 0.5 * _jnp.sum(_jnp.mean(err, axis=-1)) if err.ndim else 0.5 * err


def _adamw(w, g, m, v):
    m = ADAM_B1 * m + (1.0 - ADAM_B1) * g
    v = ADAM_B2 * v + (1.0 - ADAM_B2) * _jnp.square(g)
    m_hat = m / (1.0 - ADAM_B1 ** ADAM_STEP)
    v_hat = v / (1.0 - ADAM_B2 ** ADAM_STEP)
    delta = -ADAM_LR * (m_hat / (_jnp.sqrt(v_hat) + ADAM_EPS) + ADAM_WD * w)
    return delta, m, v


def reference(x, rel_table, w_in, b_gate, sinks, q_norm_g, kv_norm_g, w_uq, w_ukv, w_branch, w_out, ln1_g, ln1_b, w_ffn_up, conv_w, conv_b, w_ffn_down, ln2_g, ln2_b, loss_target, m_rel_table, m_w_in, m_b_gate, m_sinks, m_q_norm_g, m_kv_norm_g, m_w_uq, m_w_ukv, m_w_branch, m_w_out, m_ln1_g, m_ln1_b, m_w_ffn_up, m_conv_w, m_conv_b, m_w_ffn_down, m_ln2_g, m_ln2_b, v_rel_table, v_w_in, v_b_gate, v_sinks, v_q_norm_g, v_kv_norm_g, v_w_uq, v_w_ukv, v_w_branch, v_w_out, v_ln1_g, v_ln1_b, v_w_ffn_up, v_conv_w, v_conv_b, v_w_ffn_down, v_ln2_g, v_ln2_b):
    given = dict(x=x, rel_table=rel_table, w_in=w_in, b_gate=b_gate, sinks=sinks, q_norm_g=q_norm_g, kv_norm_g=kv_norm_g, w_uq=w_uq, w_ukv=w_ukv, w_branch=w_branch, w_out=w_out, ln1_g=ln1_g, ln1_b=ln1_b, w_ffn_up=w_ffn_up, conv_w=conv_w, conv_b=conv_b, w_ffn_down=w_ffn_down, ln2_g=ln2_g, ln2_b=ln2_b, loss_target=loss_target, m_rel_table=m_rel_table, m_w_in=m_w_in, m_b_gate=m_b_gate, m_sinks=m_sinks, m_q_norm_g=m_q_norm_g, m_kv_norm_g=m_kv_norm_g, m_w_uq=m_w_uq, m_w_ukv=m_w_ukv, m_w_branch=m_w_branch, m_w_out=m_w_out, m_ln1_g=m_ln1_g, m_ln1_b=m_ln1_b, m_w_ffn_up=m_w_ffn_up, m_conv_w=m_conv_w, m_conv_b=m_conv_b, m_w_ffn_down=m_w_ffn_down, m_ln2_g=m_ln2_g, m_ln2_b=m_ln2_b, v_rel_table=v_rel_table, v_w_in=v_w_in, v_b_gate=v_b_gate, v_sinks=v_sinks, v_q_norm_g=v_q_norm_g, v_kv_norm_g=v_kv_norm_g, v_w_uq=v_w_uq, v_w_ukv=v_w_ukv, v_w_branch=v_w_branch, v_w_out=v_w_out, v_ln1_g=v_ln1_g, v_ln1_b=v_ln1_b, v_w_ffn_up=v_w_ffn_up, v_conv_w=v_conv_w, v_conv_b=v_conv_b, v_w_ffn_down=v_w_ffn_down, v_ln2_g=v_ln2_g, v_ln2_b=v_ln2_b)
    weights = {n: given[n] for n in TWIN_WEIGHTS}
    shared = {n: given[n] for n in SHARED_INPUTS}
    per_example = {n: given[n] for n in ['x']}
    grad_fn = _jax.value_and_grad(_loss, argnums=(0, 1))

    def one_microbatch(ex, loss_target):
        ex = dict(ex)
        diff = ex.pop(TWIN_DIFF_INPUT)
        return grad_fn(weights, diff, {**shared, **ex}, loss_target)

    if N_MICROBATCH == 1:
        loss, (grad_w, grad_x) = one_microbatch(per_example, given["loss_target"])
    else:
        def body(carry, xs):
            loss_sum, grad_sum = carry
            l_k, (gw_k, gx_k) = one_microbatch(xs[0], xs[1])
            with _jax.named_scope("update"):
                return (loss_sum + l_k, _jax.tree.map(_jnp.add, grad_sum, gw_k)), gx_k

        init = (_jnp.zeros((), _jnp.float32), _jax.tree.map(_jnp.zeros_like, weights))
        (loss, grad_w), grad_x = _jax.lax.scan(body, init, (per_example, given["loss_target"]))
    with _jax.named_scope("update"):
        delta_w, new_m, new_v = {}, {}, {}
        for n in TWIN_WEIGHTS:
            delta_w[n], new_m[n], new_v[n] = _adamw(weights[n], grad_w[n], given["m_" + n], given["v_" + n])
    return (loss, grad_x, *[grad_w[n] for n in TWIN_WEIGHTS], *[delta_w[n] for n in TWIN_WEIGHTS],
            *[new_m[n] for n in TWIN_WEIGHTS], *[new_v[n] for n in TWIN_WEIGHTS])
```

```python
import functools
import math

import numpy as np
import jax
import jax.numpy as jnp
from jax import lax
from jax.experimental import pallas as pl
from jax.experimental.pallas import tpu as pltpu

F32 = jnp.float32
BF16 = jnp.bfloat16

N_DEV = 8
DEPTH = 2
HEAD_DIM = 64
BLOCK = 128
A_GROUPS = ((128, 1), (512, 4), (2048, 16))
N_HEADS = 8
HW = N_HEADS * HEAD_DIM
B_KV_HEADS = 2
C_Q_RANK = 256
C_KV_RANK = 128
C_NOPE = 64
C_ROPE = 32
ROPE_BASE = 10000.0
REL_BUCKETS = 32
REL_MAX_DIST = 2048
ALPHA = (2 * DEPTH) ** 0.25
LN_EPS = 1e-5
RMS_EPS = 1e-6
NEG = -1e30
ADAM_LR = 0.001
ADAM_B1 = 0.9
ADAM_B2 = 0.999
ADAM_EPS = 1e-08
ADAM_WD = 0.01
ADAM_STEP = 10

LANE = 128
TM = 256
VMEM_LIMIT = 56 * 1024 * 1024
MESH_ID = pl.DeviceIdType.MESH

WEIGHTS = ('rel_table', 'w_in', 'b_gate', 'sinks', 'q_norm_g', 'kv_norm_g', 'w_uq', 'w_ukv', 'w_branch',
           'w_out', 'ln1_g', 'ln1_b', 'w_ffn_up', 'conv_w', 'conv_b', 'w_ffn_down', 'ln2_g', 'ln2_b')
SHARDED = {'w_in': 2, 'w_uq': 2, 'w_ukv': 2, 'w_branch': 3, 'w_out': 1, 'w_ffn_up': 2, 'conv_w': 2,
           'w_ffn_down': 1}
GATHER_BF16 = ('w_in', 'w_uq', 'w_ukv', 'w_branch', 'w_out', 'w_ffn_up', 'w_ffn_down')


def _params(sem=None):
    return pltpu.CompilerParams(dimension_semantics=sem, vmem_limit_bytes=VMEM_LIMIT)


def _pick(n, target):
    if n <= target:
        return n
    best = None
    for t in range(LANE, target + 1, LANE):
        if n % t == 0:
            best = t
    assert best is not None, (n, target)
    return best


def _dot(a, b, ca, cb):
    return lax.dot_general(a.astype(BF16), b.astype(BF16), (((ca,), (cb,)), ((), ())),
                           preferred_element_type=F32)


@jax.custom_vjp
def mm(a, b):
    return _dot(a, b, 1, 0)


def _mm_fwd(a, b):
    return _dot(a, b, 1, 0), (a, b)


def _mm_bwd(res, g):
    a, b = res
    return _dot(g, b, 1, 1), _dot(a, g, 0, 0)


mm.defvjp(_mm_fwd, _mm_bwd)


@jax.custom_vjp
def mm_nt(a, b):
    return _dot(a, b, 1, 1)


def _mm_nt_fwd(a, b):
    return _dot(a, b, 1, 1), (a, b)


def _mm_nt_bwd(res, g):
    a, b = res
    return _dot(g, b, 1, 0), _dot(g, a, 0, 0)


mm_nt.defvjp(_mm_nt_fwd, _mm_nt_bwd)


def _split_impl(x, n):
    w = x.shape[-1] // n
    return tuple(x[:, i * w:(i + 1) * w] for i in range(n))


@functools.partial(jax.custom_vjp, nondiff_argnums=(1,))
def split_lanes(x, n):
    return _split_impl(x, n)


def _split_fwd(x, n):
    return _split_impl(x, n), None


def _split_bwd(n, _, gs):
    return (jnp.concatenate(gs, axis=-1),)


split_lanes.defvjp(_split_fwd, _split_bwd)


@jax.custom_vjp
def concat_lanes(xs):
    return jnp.concatenate(xs, axis=-1)


def _concat_fwd(xs):
    return jnp.concatenate(xs, axis=-1), len(xs)


def _concat_bwd(n, g):
    return (_split_impl(g, n),)


concat_lanes.defvjp(_concat_fwd, _concat_bwd)


@jax.custom_vjp
def concat_rows(a, b):
    return jnp.concatenate([a, b], axis=0)


def _crow_fwd(a, b):
    return jnp.concatenate([a, b], axis=0), a.shape[0]


def _crow_bwd(na, g):
    return g[:na], g[na:]


concat_rows.defvjp(_crow_fwd, _crow_bwd)


def _rot_half(x):
    n = x.shape[-1]
    lane = lax.broadcasted_iota(jnp.int32, (1, n), 1) % LANE
    lo = (lane >= C_NOPE) & (lane < C_NOPE + C_ROPE // 2)
    hi = (lane >= C_NOPE + C_ROPE // 2) & (lane < C_NOPE + C_ROPE)
    up = pltpu.roll(x, n - C_ROPE // 2, 1)
    dn = pltpu.roll(x, C_ROPE // 2, 1)
    return jnp.where(lo, -up, jnp.where(hi, dn, 0.0))


@jax.custom_vjp
def rope(x, c, s):
    return x * c + _rot_half(x) * s


def _rope_fwd(x, c, s):
    return x * c + _rot_half(x) * s, (c, s)


def _rope_bwd(res, g):
    c, s = res
    return g * c - _rot_half(g * s), jnp.zeros_like(c), jnp.zeros_like(s)


rope.defvjp(_rope_fwd, _rope_bwd)


def _sigmoid(x):
    return 1.0 / (1.0 + jnp.exp(-x))


def matmul(name, a, b, mode, out_dtype=F32, add=None, scale=1.0):
    if mode == 'nn':
        (m, k), n = a.shape, b.shape[1]
    elif mode == 'nt':
        (m, k), n = a.shape, b.shape[0]
    else:
        (k, m), n = a.shape, b.shape[1]
    tm, tn, tk = _pick(m, 512), _pick(n, 512), _pick(k, 1024)
    nk = k // tk
    a_spec = pl.BlockSpec((tk, tm), lambda i, j, kk: (kk, i)) if mode == 'tn' else \
        pl.BlockSpec((tm, tk), lambda i, j, kk: (i, kk))
    b_spec = pl.BlockSpec((tn, tk), lambda i, j, kk: (j, kk)) if mode == 'nt' else \
        pl.BlockSpec((tk, tn), lambda i, j, kk: (kk, j))
    o_spec = pl.BlockSpec((tm, tn), lambda i, j, kk: (i, j))
    ca = 0 if mode == 'tn' else 1
    cb = 1 if mode == 'nt' else 0
    has_add = add is not None

    def body(*refs):
        if has_add:
            a_ref, b_ref, add_ref, o_ref, acc_ref = refs
        else:
            a_ref, b_ref, o_ref, acc_ref = refs
        kk = pl.program_id(2)

        @pl.when(kk == 0)
        def _():
            acc_ref[...] = jnp.zeros_like(acc_ref)

        acc_ref[...] += _dot(a_ref[...], b_ref[...], ca, cb)

        @pl.when(kk == nk - 1)
        def _():
            r = acc_ref[...]
            if has_add:
                r = r + scale * add_ref[...]
            o_ref[...] = r.astype(out_dtype)

    ins = [a, b] + ([add] if has_add else [])
    specs = [a_spec, b_spec] + ([o_spec] if has_add else [])
    return pl.pallas_call(
        body, name=name, grid=(m // tm, n // tn, nk), in_specs=specs, out_specs=o_spec,
        out_shape=jax.ShapeDtypeStruct((m, n), out_dtype), scratch_shapes=[pltpu.VMEM((tm, tn), F32)],
        compiler_params=_params(("parallel", "parallel", "arbitrary")))(*ins)


def _row_spec(width, col_block=0, tm=TM):
    return pl.BlockSpec((tm, width), lambda i: (i, col_block))


def _full_spec(shape):
    nd = len(shape)
    return pl.BlockSpec(shape, lambda i: (0,) * nd)


def rows_fwd(name, fn, rows, params, outs, tm=TM):
    s = rows[0][0].shape[0]
    nr, npar = len(rows), len(params)

    def body(*refs):
        r = [refs[i][...].astype(F32) for i in range(nr)]
        p = [refs[nr + i][...] for i in range(npar)]
        res = fn(*r, *p)
        for o_ref, val in zip(refs[nr + npar:], res):
            o_ref[...] = val.astype(o_ref.dtype)

    return pl.pallas_call(
        body, name=name, grid=(s // tm,),
        in_specs=[_row_spec(w, cb, tm) for _, w, cb in rows] + [_full_spec(p.shape) for p in params],
        out_specs=[_row_spec(w, 0, tm) for w, _ in outs],
        out_shape=[jax.ShapeDtypeStruct((s, w), dt) for w, dt in outs],
        compiler_params=_params(("parallel",)))(*[a for a, _, _ in rows], *params)


def rows_bwd(name, fn, rows, params, cts, row_grads, tm=TM):
    s = rows[0][0].shape[0]
    nr, npar, nct, nrg = len(rows), len(params), len(cts), len(row_grads)

    def body(*refs):
        r = [refs[i][...].astype(F32) for i in range(nr)]
        p = [refs[nr + i][...].astype(F32) for i in range(npar)]
        g = tuple(refs[nr + npar + i][...].astype(F32) for i in range(nct))
        _, vjp = jax.vjp(lambda *a: tuple(fn(*a)), *r, *p)
        grads = vjp(g)
        outs = refs[nr + npar + nct:]
        for k, (idx, _) in enumerate(row_grads):
            outs[k][...] = grads[idx].astype(outs[k].dtype)

        @pl.when(pl.program_id(0) == 0)
        def _():
            for k in range(npar):
                outs[nrg + k][...] = jnp.zeros_like(outs[nrg + k])

        for k in range(npar):
            outs[nrg + k][...] += grads[nr + k]

    return pl.pallas_call(
        body, name=name, grid=(s // tm,),
        in_specs=[_row_spec(w, cb, tm) for _, w, cb in rows] + [_full_spec(p.shape) for p in params]
        + [_row_spec(w, 0, tm) for _, w in cts],
        out_specs=[_row_spec(rows[idx][1], 0, tm) for idx, _ in row_grads] + [_full_spec(p.shape) for p in params],
        out_shape=[jax.ShapeDtypeStruct((s, rows[idx][1]), dt) for idx, dt in row_grads]
        + [jax.ShapeDtypeStruct(p.shape, F32) for p in params],
        compiler_params=_params(("arbitrary",)))(*[a for a, _, _ in rows], *params, *[a for a, _ in cts])


def ln_fn(x, r, g, b):
    z = ALPHA * x + r
    mu = jnp.mean(z, axis=-1, keepdims=True)
    zc = z - mu
    var = jnp.mean(zc * zc, axis=-1, keepdims=True)
    return (zc * lax.rsqrt(var + LN_EPS) * g + b,)


def _rms(x, g):
    return x * lax.rsqrt(jnp.mean(x * x, axis=-1, keepdims=True) + RMS_EPS) * g


def mla_q_fn(cq, c, s, g, wq):
    q = mm(_rms(cq, g), wq)
    return (rope(q, jnp.concatenate([c] * N_HEADS, axis=-1), jnp.concatenate([s] * N_HEADS, axis=-1)),)


@jax.custom_vjp
def tile_heads(x):
    return jnp.concatenate([x] * N_HEADS, axis=-1)


def _tile_fwd(x):
    return jnp.concatenate([x] * N_HEADS, axis=-1), None


def _tile_bwd(_, g):
    parts = _split_impl(g, N_HEADS)
    acc = parts[0]
    for p in parts[1:]:
        acc = acc + p
    return (acc,)


tile_heads.defvjp(_tile_fwd, _tile_bwd)


def mla_kv_fn(ckv, kr, c, s, g, wk, wv):
    n = _rms(ckv, g)
    return mm(n, wk) + tile_heads(rope(kr, c, s)), mm(n, wv)


def merge_fn(y0, y1, y2, g0, g1, g2, b0, b1, b2):
    return (_sigmoid(g0 + b0) * y0 + _sigmoid(g1 + b1) * y1 + _sigmoid(g2 + b2) * y2,)


def combine_fn(o1, o2, o3, l1, l2, l3):
    mx = lax.stop_gradient(jnp.maximum(jnp.maximum(l1, l2), l3))
    e1, e2, e3 = jnp.exp(l1 - mx), jnp.exp(l2 - mx), jnp.exp(l3 - mx)
    return ((e1 * o1 + e2 * o2 + e3 * o3) / (e1 + e2 + e3),)


def loss_head(y, t):
    s, d = y.shape

    def body(y_ref, t_ref, dy_ref, loss_ref):
        err = y_ref[...] - t_ref[...]
        dy_ref[...] = err * (1.0 / d)

        @pl.when(pl.program_id(0) == 0)
        def _():
            loss_ref[...] = jnp.zeros_like(loss_ref)

        loss_ref[...] += jnp.sum(err * err) * (0.5 / d)

    return pl.pallas_call(
        body, name="loss_head", grid=(s // TM,), in_specs=[_row_spec(d), _row_spec(d)],
        out_specs=[_row_spec(d), _full_spec((8, LANE))],
        out_shape=[jax.ShapeDtypeStruct((s, d), F32), jax.ShapeDtypeStruct((8, LANE), F32)],
        compiler_params=_params(("arbitrary",)))(y, t)


CONV_TN = 256


def _conv_taps(u_ext, w_ref, b_ref):
    n = u_ext.shape[0]
    return (w_ref[0:1, :] * pltpu.roll(u_ext, 2, 0) + w_ref[1:2, :] * pltpu.roll(u_ext, 1, 0)
            + w_ref[2:3, :] * u_ext + b_ref[...])


def conv_glu_fwd(u, conv_w, conv_b):
    s, f2 = u.shape
    f = f2 // 2
    tn, tm = CONV_TN, TM
    off = f // tn
    hb = tm // 8

    def body(ug, uv, hg, hv, wg, wv, bg, bv, o_ref):
        first = pl.program_id(0) == 0

        def conv(u_ref, h_ref, w_ref, b_ref):
            halo = jnp.where(first, 0.0, h_ref[...])
            ext = jnp.concatenate([halo, u_ref[...]], axis=0)
            return _conv_taps(ext, w_ref, b_ref)[8:]

        cg, cv = conv(ug, hg, wg, bg), conv(uv, hv, wv, bv)
        o_ref[...] = (cg * _sigmoid(cg) * cv).astype(o_ref.dtype)

    blk = lambda o: pl.BlockSpec((tm, tn), lambda i, j: (i, j + o))
    halo = lambda o: pl.BlockSpec((8, tn), lambda i, j: (jnp.maximum(i * hb - 1, 0), j + o))
    par = lambda r, o: pl.BlockSpec((r, tn), lambda i, j: (0, j + o))
    return pl.pallas_call(
        body, name="conv_glu_fwd", grid=(s // tm, off),
        in_specs=[blk(0), blk(off), halo(0), halo(off), par(3, 0), par(3, off), par(1, 0), par(1, off)],
        out_specs=pl.BlockSpec((tm, tn), lambda i, j: (i, j)),
        out_shape=jax.ShapeDtypeStruct((s, f), BF16),
        compiler_params=_params(("parallel", "parallel")))(u, u, u, u, conv_w, conv_w, conv_b, conv_b)


def conv_glu_bwd(u, dh, conv_w, conv_b):
    s, f2 = u.shape
    f = f2 // 2
    tn, tm = CONV_TN, TM
    off = f // tn
    hb = tm // 8
    n_rows = s // tm

    def body(ug, uv, pg, pv, ng, nv, dh_ref, dhn_ref, wg, wv, bg, bv, dug, duv, dwg, dwv, dbg, dbv):
        i = pl.program_id(1)
        first, last = i == 0, i == n_rows - 1

        def ext(u_ref, p_ref, n_ref):
            return jnp.concatenate([jnp.where(first, 0.0, p_ref[...]), u_ref[...], n_ref[...]], axis=0)

        eg, ev = ext(ug, pg, ng), ext(uv, pv, nv)
        cg, cv = _conv_taps(eg, wg, bg), _conv_taps(ev, wv, bv)
        dhe = jnp.concatenate([jnp.zeros((8, tn), F32), dh_ref[...], jnp.where(last, 0.0, dhn_ref[...])], axis=0)
        sg = _sigmoid(cg)
        dcg = dhe * cv * (sg * (1.0 + cg * (1.0 - sg)))
        dcv = dhe * (cg * sg)
        n = tm + 16

        @pl.when(i == 0)
        def _():
            for r in (dwg, dwv, dbg, dbv):
                r[...] = jnp.zeros_like(r)

        def back(dc, e, w_ref, du_ref, dw_ref, db_ref):
            du = w_ref[2:3, :] * dc + w_ref[1:2, :] * pltpu.roll(dc, n - 1, 0) + w_ref[0:1, :] * pltpu.roll(dc, n - 2, 0)
            du_ref[...] = du[8:8 + tm].astype(du_ref.dtype)
            own = dc[8:8 + tm]
            dw_ref[0:1, :] += jnp.sum(own * pltpu.roll(e, 2, 0)[8:8 + tm], axis=0, keepdims=True)
            dw_ref[1:2, :] += jnp.sum(own * pltpu.roll(e, 1, 0)[8:8 + tm], axis=0, keepdims=True)
            dw_ref[2:3, :] += jnp.sum(own * e[8:8 + tm], axis=0, keepdims=True)
            db_ref[...] += jnp.sum(own, axis=0, keepdims=True)

        back(dcg, eg, wg, dug, dwg, dbg)
        back(dcv, ev, wv, duv, dwv, dbv)

    blk = lambda o: pl.BlockSpec((tm, tn), lambda j, i: (i, j + o))
    prev = lambda o: pl.BlockSpec((8, tn), lambda j, i: (jnp.maximum(i * hb - 1, 0), j + o))
    nxt = lambda o: pl.BlockSpec((8, tn), lambda j, i: (jnp.minimum((i + 1) * hb, s // 8 - 1), j + o))
    par = lambda r, o: pl.BlockSpec((r, tn), lambda j, i: (0, j + o))
    return pl.pallas_call(
        body, name="conv_glu_bwd", grid=(off, n_rows),
        in_specs=[blk(0), blk(off), prev(0), prev(off), nxt(0), nxt(off), blk(0), nxt(0),
                  par(3, 0), par(3, off), par(1, 0), par(1, off)],
        out_specs=[blk(0), blk(0), par(3, 0), par(3, 0), par(1, 0), par(1, 0)],
        out_shape=[jax.ShapeDtypeStruct((s, f), BF16)] * 2 + [jax.ShapeDtypeStruct((3, f), F32)] * 2
        + [jax.ShapeDtypeStruct((1, f), F32)] * 2,
        compiler_params=_params(("parallel", "arbitrary")))(u, u, u, u, u, u, dh, dh, conv_w, conv_w, conv_b, conv_b)


def _band_fn(q, kp, kc, vp, vc, bias, sink, first, inclusive):
    k = concat_rows(kp, kc)
    v = concat_rows(vp, vc)
    qi = lax.broadcasted_iota(jnp.int32, (BLOCK, 2 * BLOCK), 0)
    ki = lax.broadcasted_iota(jnp.int32, (BLOCK, 2 * BLOCK), 1)
    step = BLOCK + qi - ki
    valid = (step >= 0) & ((step <= BLOCK) if inclusive else (step < BLOCK))
    valid = valid & (jnp.logical_not(first) | (ki >= BLOCK))
    lane = lax.broadcasted_iota(jnp.int32, (1, LANE), 1)
    scale = HEAD_DIM ** -0.5
    qs, ks, vs = split_lanes(q, 4), split_lanes(k, 4), split_lanes(v, 4)
    o_parts, l_parts = [], []
    for pair in range(4):
        o_acc, l_acc = None, None
        for sub in range(2):
            h = 2 * pair + sub
            mh = ((lane >= sub * HEAD_DIM) & (lane < (sub + 1) * HEAD_DIM)).astype(F32)
            logits = mm_nt(qs[pair] * mh, ks[pair]) * scale + bias[h]
            logits = jnp.where(valid, logits, NEG)
            m = lax.stop_gradient(jnp.max(logits, axis=-1, keepdims=True))
            if sink is not None:
                m = jnp.maximum(m, lax.stop_gradient(sink[h]))
            p = jnp.exp(logits - m)
            den = jnp.sum(p, axis=-1, keepdims=True)
            if sink is not None:
                den = den + jnp.exp(sink[h] - m)
            oh = mm(p / den, vs[pair] * mh)
            lh = (m + jnp.log(den)) * mh
            o_acc = oh if o_acc is None else o_acc + oh
            l_acc = lh if l_acc is None else l_acc + lh
        o_parts.append(o_acc)
        l_parts.append(l_acc)
    return concat_lanes(tuple(o_parts)), concat_lanes(tuple(l_parts))


def _band_specs(pcols, dil, cq, ck, cv):
    cpb = pcols // HW
    q = pl.BlockSpec((BLOCK, HW), lambda r, n: (n, r * cpb + cq))
    kp = pl.BlockSpec((BLOCK, HW), lambda r, n: (jnp.maximum(n - 1, 0), r * cpb + ck))
    kc = pl.BlockSpec((BLOCK, HW), lambda r, n: (n, r * cpb + ck))
    vp = pl.BlockSpec((BLOCK, HW), lambda r, n: (jnp.maximum(n - 1, 0), r * cpb + cv))
    vc = pl.BlockSpec((BLOCK, HW), lambda r, n: (n, r * cpb + cv))
    return [q, kp, kc, vp, vc]


def _const_spec(shape):
    nd = len(shape)
    return pl.BlockSpec(shape, lambda r, n: (0,) * nd)


def band_fwd(name, proj, dil, cols, bias, sink, inclusive):
    s, pcols = proj.shape
    nb = s // (BLOCK * dil)
    view = proj.reshape(s // dil, dil * pcols)
    has_sink = sink is not None

    def body(*refs):
        q, kp, kc, vp, vc, b_ref = refs[:6]
        s_ref = refs[6] if has_sink else None
        o_ref, l_ref = refs[-2:]
        first = pl.program_id(1) == 0
        bias_l = tuple(b_ref[h] for h in range(N_HEADS))
        sink_l = tuple(s_ref[h] for h in range(N_HEADS)) if has_sink else None
        o, l = _band_fn(q[...], kp[...], kc[...], vp[...], vc[...], bias_l, sink_l, first, inclusive)
        o_ref[...] = o
        l_ref[...] = l

    out_spec = pl.BlockSpec((BLOCK, HW), lambda r, n: (n, r))
    ins = [view] * 5 + [bias] + ([sink] if has_sink else [])
    specs = _band_specs(pcols, dil, *cols) + [_const_spec(bias.shape)] + ([_const_spec(sink.shape)] if has_sink else [])
    o, l = pl.pallas_call(
        body, name=name, grid=(dil, nb), in_specs=specs, out_specs=[out_spec, out_spec],
        out_shape=[jax.ShapeDtypeStruct((s // dil, dil * HW), F32)] * 2,
        compiler_params=_params(("parallel", "parallel")))(*ins)
    return o.reshape(s, HW), l.reshape(s, HW)


def band_bwd(name, proj, dil, cols, bias, sink, inclusive, do, dl):
    s, pcols = proj.shape
    nb = s // (BLOCK * dil)
    view = proj.reshape(s // dil, dil * pcols)
    has_sink, has_dl = sink is not None, dl is not None
    n_in = 6 + int(has_sink) + 1 + int(has_dl)

    def body(*refs):
        q, kp, kc, vp, vc, b_ref = refs[:6]
        s_ref = refs[6] if has_sink else None
        do_ref = refs[6 + int(has_sink)]
        dl_ref = refs[7 + int(has_sink)] if has_dl else None
        outs = refs[n_in:]
        first = pl.program_id(1) == 0
        bias_l = tuple(b_ref[h] for h in range(N_HEADS))
        sink_l = tuple(s_ref[h] for h in range(N_HEADS)) if has_sink else ()

        def fn(q_, kp_, kc_, vp_, vc_, b_, s_):
            o, l = _band_fn(q_, kp_, kc_, vp_, vc_, b_, s_ if has_sink else None, first, inclusive)
            return (o, l) if has_dl else (o,)

        _, vjp = jax.vjp(fn, q[...], kp[...], kc[...], vp[...], vc[...], bias_l, sink_l)
        ct = (do_ref[...], dl_ref[...]) if has_dl else (do_ref[...],)
        g = vjp(ct)
        for k in range(5):
            outs[k][...] = g[k]

        @pl.when((pl.program_id(0) == 0) & (pl.program_id(1) == 0))
        def _():
            outs[5][...] = jnp.zeros_like(outs[5])
            if has_sink:
                outs[6][...] = jnp.zeros_like(outs[6])

        for h in range(N_HEADS):
            outs[5][h] += g[5][h]
            if has_sink:
                outs[6][h] += g[6][h]

    blk = pl.BlockSpec((BLOCK, HW), lambda r, n: (n, r))
    dview = lambda a: a.reshape(s // dil, dil * HW)
    ins = [view] * 5 + [bias] + ([sink] if has_sink else []) + [dview(do)] + ([dview(dl)] if has_dl else [])
    specs = _band_specs(pcols, dil, *cols) + [_const_spec(bias.shape)] \
        + ([_const_spec(sink.shape)] if has_sink else []) + [blk] * (1 + int(has_dl))
    out_shape = [jax.ShapeDtypeStruct((s // dil, dil * HW), F32)] * 5 + [jax.ShapeDtypeStruct(bias.shape, F32)] \
        + ([jax.ShapeDtypeStruct(sink.shape, F32)] if has_sink else [])
    out_specs = [blk] * 5 + [_const_spec(bias.shape)] + ([_const_spec(sink.shape)] if has_sink else [])
    res = pl.pallas_call(
        body, name=name, grid=(dil, nb), in_specs=specs, out_specs=out_specs, out_shape=out_shape,
        compiler_params=_params(("arbitrary", "arbitrary")))(*ins)
    grads = [r.reshape(s, HW) for r in res[:5]]
    return grads, res[5], (res[6] if has_sink else None)


def shift_add(name, own, prev, dil):
    s = own.shape[0]
    nb = s // (BLOCK * dil)

    def body(a_ref, b_ref, o_ref):
        lastb = pl.program_id(1) == nb - 1
        o_ref[...] = (a_ref[...] + jnp.where(lastb, 0.0, b_ref[...])).astype(o_ref.dtype)

    blk = pl.BlockSpec((BLOCK, HW), lambda r, n: (n, r))
    nxt = pl.BlockSpec((BLOCK, HW), lambda r, n: (jnp.minimum(n + 1, nb - 1), r))
    out = pl.pallas_call(
        body, name=name, grid=(dil, nb), in_specs=[blk, nxt], out_specs=blk,
        out_shape=jax.ShapeDtypeStruct((s // dil, dil * HW), BF16),
        compiler_params=_params(("parallel", "parallel")))(own.reshape(s // dil, dil * HW), prev.reshape(s // dil, dil * HW))
    return out.reshape(s, HW)


def rel_grad(name, dbias, bucket):
    def body(d_ref, b_ref, o_ref):
        lane = lax.broadcasted_iota(jnp.int32, (1, LANE), 1)
        bk = b_ref[...]
        for h in range(N_HEADS):
            d = d_ref[h]
            row = jnp.zeros((1, LANE), F32)
            for b in range(REL_BUCKETS):
                tot = jnp.sum(jnp.where(bk == b, d, 0.0))
                row = row + jnp.where(lane == b, tot, 0.0)
            o_ref[h:h + 1, :] = row

    return pl.pallas_call(
        body, name=name, out_shape=jax.ShapeDtypeStruct((N_HEADS, LANE), F32),
        in_specs=[pl.BlockSpec(memory_space=pltpu.VMEM)] * 2, out_specs=pl.BlockSpec(memory_space=pltpu.VMEM),
        compiler_params=_params())(dbias, bucket)


MLA_T = 512
MLA_SCALE = (C_NOPE + C_ROPE) ** -0.5


def _causal_mask(i, j, t):
    r = lax.broadcasted_iota(jnp.int32, (t, t), 0) + i * t
    c = lax.broadcasted_iota(jnp.int32, (t, t), 1) + j * t
    return c <= r


def mla_fwd(q, k, v):
    s = q.shape[0]
    t = min(MLA_T, s)
    nq = s // t

    def body(q_ref, k_ref, v_ref, o_ref, l_ref, m_s, l_s, acc_s):
        i, j = pl.program_id(1), pl.program_id(2)

        @pl.when(j == 0)
        def _():
            m_s[...] = jnp.full_like(m_s, NEG)
            l_s[...] = jnp.zeros_like(l_s)
            acc_s[...] = jnp.zeros_like(acc_s)

        @pl.when(j <= i)
        def _():
            sc = _dot(q_ref[...], k_ref[...], 1, 1) * MLA_SCALE
            sc = jnp.where(_causal_mask(i, j, t), sc, NEG)
            m_old = m_s[...]
            m_new = jnp.maximum(m_old, jnp.max(sc, axis=-1, keepdims=True))
            a = jnp.exp(m_old - m_new)
            p = jnp.exp(sc - m_new[:, 0:1])
            l_s[...] = a * l_s[...] + jnp.sum(p, axis=-1, keepdims=True)
            acc_s[...] = a * acc_s[...] + _dot(p, v_ref[...], 1, 0)
            m_s[...] = m_new

        @pl.when(j == i)
        def _():
            o_ref[...] = acc_s[...] / l_s[...]
            l_ref[...] = m_s[...] + jnp.log(l_s[...])

    qs = pl.BlockSpec((t, LANE), lambda h, i, j: (i, h))
    ks = pl.BlockSpec((t, LANE), lambda h, i, j: (jnp.minimum(j, i), h))
    return pl.pallas_call(
        body, name="mla_fwd", grid=(N_HEADS, nq, nq), in_specs=[qs, ks, ks], out_specs=[qs, qs],
        out_shape=[jax.ShapeDtypeStruct((s, N_HEADS * LANE), F32)] * 2,
        scratch_shapes=[pltpu.VMEM((t, LANE), F32)] * 3,
        compiler_params=_params(("parallel", "parallel", "arbitrary")))(q, k, v)


def _mla_p_ds(q_ref, k_ref, v_ref, do_ref, l_ref, d_col, i, j, t):
    sc = _dot(q_ref[...], k_ref[...], 1, 1) * MLA_SCALE
    sc = jnp.where(_causal_mask(i, j, t), sc, NEG)
    p = jnp.exp(sc - l_ref[...][:, 0:1])
    dp = _dot(do_ref[...], v_ref[...], 1, 1)
    ds = p * (dp - d_col) * MLA_SCALE
    return p, ds


def mla_bwd_dq(q, k, v, o, lse, do):
    s = q.shape[0]
    t = min(MLA_T, s)
    nq = s // t

    def body(q_ref, k_ref, v_ref, o_ref, l_ref, do_ref, dq_ref, acc_s, d_s):
        i, j = pl.program_id(1), pl.program_id(2)

        @pl.when(j == 0)
        def _():
            acc_s[...] = jnp.zeros_like(acc_s)
            d_s[...] = jnp.sum(do_ref[...] * o_ref[...], axis=-1, keepdims=True) + jnp.zeros_like(d_s)

        @pl.when(j <= i)
        def _():
            _, ds = _mla_p_ds(q_ref, k_ref, v_ref, do_ref, l_ref, d_s[...][:, 0:1], i, j, t)
            acc_s[...] += _dot(ds, k_ref[...], 1, 0)

        @pl.when(j == i)
        def _():
            dq_ref[...] = acc_s[...]

    qs = pl.BlockSpec((t, LANE), lambda h, i, j: (i, h))
    ks = pl.BlockSpec((t, LANE), lambda h, i, j: (jnp.minimum(j, i), h))
    return pl.pallas_call(
        body, name="mla_bwd_dq", grid=(N_HEADS, nq, nq), in_specs=[qs, ks, ks, qs, qs, qs], out_specs=qs,
        out_shape=jax.ShapeDtypeStruct((s, N_HEADS * LANE), F32),
        scratch_shapes=[pltpu.VMEM((t, LANE), F32)] * 2,
        compiler_params=_params(("parallel", "parallel", "arbitrary")))(q, k, v, o, lse, do)


def mla_bwd_dkv(q, k, v, o, lse, do):
    s = q.shape[0]
    t = min(MLA_T, s)
    nq = s // t

    def body(q_ref, k_ref, v_ref, o_ref, l_ref, do_ref, dk_ref, dv_ref, dk_s, dv_s):
        j, ii = pl.program_id(1), pl.program_id(2)
        i = nq - 1 - ii

        @pl.when(ii == 0)
        def _():
            dk_s[...] = jnp.zeros_like(dk_s)
            dv_s[...] = jnp.zeros_like(dv_s)

        @pl.when(i >= j)
        def _():
            d_col = jnp.sum(do_ref[...] * o_ref[...], axis=-1, keepdims=True)
            p, ds = _mla_p_ds(q_ref, k_ref, v_ref, do_ref, l_ref, d_col, i, j, t)
            dv_s[...] += _dot(p, do_ref[...], 0, 0)
            dk_s[...] += _dot(ds, q_ref[...], 0, 0)

        @pl.when(i == j)
        def _():
            dk_ref[...] = dk_s[...]
            dv_ref[...] = dv_s[...]

    ks = pl.BlockSpec((t, LANE), lambda h, j, ii: (j, h))
    qs = pl.BlockSpec((t, LANE), lambda h, j, ii: (jnp.maximum(nq - 1 - ii, j), h))
    return pl.pallas_call(
        body, name="mla_bwd_dkv", grid=(N_HEADS, nq, nq), in_specs=[qs, ks, ks, qs, qs, qs], out_specs=[ks, ks],
        out_shape=[jax.ShapeDtypeStruct((s, N_HEADS * LANE), F32)] * 2,
        scratch_shapes=[pltpu.VMEM((t, LANE), F32)] * 2,
        compiler_params=_params(("parallel", "parallel", "arbitrary")))(q, k, v, o, lse, do)


def _peer(k):
    x, y, c = lax.axis_index("x"), lax.axis_index("y"), lax.axis_index("c")
    px, py, pc = (x ^ ((k >> 2) & 1)), (y ^ ((k >> 1) & 1)), (c ^ (k & 1))
    return (px, py, pc), 4 * px + 2 * py + pc


def all_gather(name, block):
    r = block.shape[0]

    def body(x_ref, out_ref, send_sems, recv_sems, local_sem):
        me, me_idx = _peer(0)
        mine = pltpu.make_async_copy(x_ref, out_ref.at[me_idx], local_sem)
        mine.start()
        sends, recvs = [], []
        for k in range(1, N_DEV):
            peer, peer_idx = _peer(k)
            sends.append(pltpu.make_async_remote_copy(
                src_ref=x_ref, dst_ref=out_ref.at[me_idx], send_sem=send_sems.at[k - 1], recv_sem=recv_sems.at[k - 1],
                device_id=peer, device_id_type=MESH_ID))
            recvs.append(pltpu.make_async_remote_copy(
                src_ref=x_ref, dst_ref=out_ref.at[peer_idx], send_sem=send_sems.at[k - 1],
                recv_sem=recv_sems.at[k - 1], device_id=peer, device_id_type=MESH_ID))
        for cp in sends:
            cp.start()
        for cp in recvs:
            cp.wait_recv()
        for cp in sends:
            cp.wait_send()
        mine.wait()

    return pl.pallas_call(
        body, name=name, out_shape=jax.ShapeDtypeStruct((N_DEV, r, LANE), block.dtype),
        in_specs=[pl.BlockSpec(memory_space=pl.ANY)], out_specs=pl.BlockSpec(memory_space=pl.ANY),
        scratch_shapes=[pltpu.SemaphoreType.DMA((N_DEV - 1,)), pltpu.SemaphoreType.DMA((N_DEV - 1,)),
                        pltpu.SemaphoreType.DMA])(block)


def all_to_all(name, parts):
    r = parts.shape[1]

    def body(x_ref, out_ref, send_sems, recv_sems, local_sem):
        me, me_idx = _peer(0)
        mine = pltpu.make_async_copy(x_ref.at[me_idx], out_ref.at[me_idx], local_sem)
        mine.start()
        sends, recvs = [], []
        for k in range(1, N_DEV):
            peer, peer_idx = _peer(k)
            sends.append(pltpu.make_async_remote_copy(
                src_ref=x_ref.at[peer_idx], dst_ref=out_ref.at[me_idx], send_sem=send_sems.at[k - 1],
                recv_sem=recv_sems.at[k - 1], device_id=peer, device_id_type=MESH_ID))
            recvs.append(pltpu.make_async_remote_copy(
                src_ref=x_ref.at[me_idx], dst_ref=out_ref.at[peer_idx], send_sem=send_sems.at[k - 1],
                recv_sem=recv_sems.at[k - 1], device_id=peer, device_id_type=MESH_ID))
        for cp in sends:
            cp.start()
        for cp in recvs:
            cp.wait_recv()
        for cp in sends:
            cp.wait_send()
        mine.wait()

    return pl.pallas_call(
        body, name=name, out_shape=jax.ShapeDtypeStruct((N_DEV, r, LANE), parts.dtype),
        in_specs=[pl.BlockSpec(memory_space=pl.ANY)], out_specs=pl.BlockSpec(memory_space=pl.ANY),
        scratch_shapes=[pltpu.SemaphoreType.DMA((N_DEV - 1,)), pltpu.SemaphoreType.DMA((N_DEV - 1,)),
                        pltpu.SemaphoreType.DMA])(parts)


def adamw_sum(name, parts, w, m, v):
    r = w.shape[0]
    tr = _pick(r, 512) if r % LANE == 0 else r
    c1 = 1.0 - ADAM_B1 ** ADAM_STEP
    c2 = 1.0 - ADAM_B2 ** ADAM_STEP

    def body(p_ref, w_ref, m_ref, v_ref, g_out, d_out, m_out, v_out):
        g = p_ref[0]
        for d in range(1, N_DEV):
            g = g + p_ref[d]
        mn = ADAM_B1 * m_ref[...] + (1.0 - ADAM_B1) * g
        vn = ADAM_B2 * v_ref[...] + (1.0 - ADAM_B2) * (g * g)
        g_out[...] = g
        m_out[...] = mn
        v_out[...] = vn
        d_out[...] = -ADAM_LR * ((mn / c1) / (jnp.sqrt(vn / c2) + ADAM_EPS) + ADAM_WD * w_ref[...])

    row = pl.BlockSpec((tr, LANE), lambda i: (i, 0))
    return pl.pallas_call(
        body, name=name, grid=(r // tr,), in_specs=[pl.BlockSpec((N_DEV, tr, LANE), lambda i: (0, i, 0)), row, row, row],
        out_specs=[row] * 4, out_shape=[jax.ShapeDtypeStruct((r, LANE), F32)] * 4,
        compiler_params=_params(("parallel",)))(parts, w, m, v)


def _pack(arrays, dtype, row_mult):
    flat = jnp.concatenate([a.astype(dtype).reshape(-1) for a in arrays])
    n = flat.shape[0]
    quantum = row_mult * LANE
    total = -(-n // quantum) * quantum
    return jnp.pad(flat, (0, total - n)).reshape(total // LANE, LANE)


def _unpack(packed, shapes):
    flat = packed.reshape(-1)
    out, pos = [], 0
    for shp in shapes:
        n = int(np.prod(shp))
        out.append(flat[pos:pos + n].reshape(shp))
        pos += n
    return out


def _unshard(gathered, axis):
    return jnp.concatenate([gathered[d] for d in range(N_DEV)], axis=axis)


def _to_shards(full, axis):
    return jnp.stack(jnp.split(full, N_DEV, axis=axis), axis=0)


def _proj_layout(d_model):
    a_cols = len(A_GROUPS) * 3 * HW
    o_bq = a_cols
    o_bk = o_bq + HW
    o_bv = o_bk + B_KV_HEADS * HEAD_DIM
    o_cq = o_bv + B_KV_HEADS * HEAD_DIM
    o_ckv = o_cq + C_Q_RANK
    o_kr = o_ckv + C_KV_RANK
    o_gate = o_kr + C_ROPE
    d_in = o_gate + 3 * d_model
    zero = d_in
    idx = list(range(o_gate, d_in)) + list(range(0, a_cols)) + list(range(o_bq, o_bq + HW))
    rep = N_HEADS // B_KV_HEADS
    idx += [o_bk + (h // rep) * HEAD_DIM + d for h in range(N_HEADS) for d in range(HEAD_DIM)]
    idx += [o_bv + (h // rep) * HEAD_DIM + d for h in range(N_HEADS) for d in range(HEAD_DIM)]
    idx += list(range(o_cq, o_cq + C_Q_RANK)) + list(range(o_ckv, o_ckv + C_KV_RANK))
    idx += [zero] * C_NOPE + list(range(o_kr, o_kr + C_ROPE)) + [zero] * (LANE - C_NOPE - C_ROPE)
    return np.asarray(idx, np.int32), d_in


def _unproj_grad(dw_p, d_model):
    idx, d_in = _proj_layout(d_model)
    cols = []
    for c in range(d_in):
        cols.append(np.nonzero(idx == c)[0])
    first = np.asarray([c[0] for c in cols], np.int32)
    out = jnp.take(dw_p, first, axis=1)
    multi = [c for c in range(d_in) if len(cols[c]) > 1]
    lo, hi = multi[0], multi[-1] + 1
    assert multi == list(range(lo, hi))
    extra = out[:, lo:hi]
    for r in range(1, len(cols[lo])):
        extra = extra + jnp.take(dw_p, np.asarray([cols[c][r] for c in range(lo, hi)], np.int32), axis=1)
    return jnp.concatenate([out[:, :lo], extra, out[:, hi:]], axis=1)


def _pad_heads(w, per_head, lo, hi):
    r = w.shape[0]
    t = w.reshape(r, N_HEADS, per_head)[:, :, lo:hi]
    return jnp.pad(t, ((0, 0), (0, 0), (0, LANE - (hi - lo)))).reshape(r, N_HEADS * LANE)


def _unpad_heads(w, width):
    r = w.shape[0]
    return w.reshape(r, N_HEADS, LANE)[:, :, :width]


def _t5_bucket(dist):
    n = jnp.maximum(dist, 0)
    max_exact = REL_BUCKETS // 2
    scaled = jnp.log(jnp.maximum(n, 1).astype(F32) / max_exact) / math.log(REL_MAX_DIST / max_exact)
    large = max_exact + (scaled * (REL_BUCKETS - max_exact)).astype(jnp.int32)
    return jnp.where(n < max_exact, n, jnp.minimum(large, REL_BUCKETS - 1))


def _buckets(dil):
    qi = jnp.arange(BLOCK)[:, None]
    ki = jnp.arange(2 * BLOCK)[None, :]
    return _t5_bucket((BLOCK + qi - ki) * dil).astype(jnp.int32)


def _rope_tables(s):
    pos = jnp.arange(s, dtype=F32)
    inv_freq = ROPE_BASE ** (-jnp.arange(0, C_ROPE, 2, dtype=F32) / C_ROPE)
    ang = pos[:, None] * inv_freq[None, :]
    cos, sin = jnp.cos(ang), jnp.sin(ang)
    ones = jnp.ones((s, C_NOPE), F32)
    tail = LANE - C_NOPE - C_ROPE
    c = jnp.concatenate([ones, cos, cos, jnp.ones((s, tail), F32)], axis=1)
    sn = jnp.concatenate([jnp.zeros((s, C_NOPE), F32), sin, sin, jnp.zeros((s, tail), F32)], axis=1)
    return c, sn


def _cols(d_model):
    g = 3 * d_model // HW
    a = [(g + 3 * i, g + 3 * i + 1, g + 3 * i + 2) for i in range(len(A_GROUPS))]
    b = (g + 9, g + 10, g + 11)
    cq_lane = (g + 12) * HW
    return a, b, cq_lane


def _layer_weights(full, l, d_model):
    idx, d_in = _proj_layout(d_model)
    w_in = full['w_in'][l]
    w_in_p = jnp.take(jnp.pad(w_in, ((0, 0), (0, 1))), idx, axis=1)
    per = C_NOPE + C_ROPE
    w_uq = full['w_uq'][l]
    w_ukv = full['w_ukv'][l]
    wb = full['w_branch'][l]
    wb2 = jnp.pad(wb[2].reshape(N_HEADS, HEAD_DIM, d_model), ((0, 0), (0, LANE - HEAD_DIM), (0, 0)))
    return dict(
        w_in_p=w_in_p, wq=_pad_heads(w_uq, per, 0, per), wk=_pad_heads(w_ukv, 2 * C_NOPE, 0, C_NOPE),
        wv=_pad_heads(w_ukv, 2 * C_NOPE, C_NOPE, 2 * C_NOPE), wb0=wb[0], wb1=wb[1],
        wb2=wb2.reshape(N_HEADS * LANE, d_model), w_out=full['w_out'][l], w_up=full['w_ffn_up'][l],
        w_down=full['w_ffn_down'][l], conv_w=full['conv_w'][l])


def _layer_fwd(l, x, lw, small, tabs):
    s, d = x.shape
    a_cols, b_cols, cq_lane = _cols(d)
    tag = f"l{l}_"
    proj = matmul(tag + "proj", x, lw['w_in_p'], 'nn')
    res = dict(x=x, proj=proj)
    outs, lses = [], []
    for gi, (window, dil) in enumerate(A_GROUPS):
        o, lg = band_fwd(tag + f"a{gi}_fwd", proj, dil, a_cols[gi], tabs['bias'][gi], None, True)
        outs.append(o)
        lses.append(lg)
    res['a_o'], res['a_l'] = outs, lses
    (o_a,) = rows_fwd(tag + "a_combine", combine_fn, [(o, HW, 0) for o in outs] + [(lg, HW, 0) for lg in lses], [],
                      [(HW, BF16)])
    sink = small['sinks'][l].reshape(N_HEADS, 1, 1)
    o_b, _ = band_fwd(tag + "b_fwd", proj, 1, b_cols, tabs['bias'][3], sink, False)
    gq = small['q_norm_g'][l].reshape(1, -1)
    gkv = small['kv_norm_g'][l].reshape(1, -1)
    (qc,) = rows_fwd(tag + "c_q", mla_q_fn, [(proj, C_Q_RANK, cq_lane // C_Q_RANK), (tabs['cos'], LANE, 0),
                                            (tabs['sin'], LANE, 0)], [gq, lw['wq']], [(N_HEADS * LANE, BF16)])
    ckv_blk = (cq_lane + C_Q_RANK) // LANE
    kc, vc = rows_fwd(tag + "c_kv", mla_kv_fn, [(proj, LANE, ckv_blk), (proj, LANE, ckv_blk + 1), (tabs['cos'], LANE, 0),
                                                (tabs['sin'], LANE, 0)], [gkv, lw['wk'], lw['wv']],
                      [(N_HEADS * LANE, BF16)] * 2)
    o_c, lse_c = mla_fwd(qc, kc, vc)
    res.update(o_a=o_a, o_b=o_b, o_c=o_c, lse_c=lse_c, qc=qc, kc=kc, vc=vc)
    bg = small['b_gate'][l].reshape(3, 1, d)
    ys = [matmul(tag + f"branch{i}", o, lw[f'wb{i}'], 'nn') for i, o in enumerate((o_a, o_b, o_c))]
    (merged,) = rows_fwd(tag + "merge", merge_fn, [(y, d, 0) for y in ys] + [(proj, d, i) for i in range(3)],
                         [bg[0], bg[1], bg[2]], [(d, BF16)])
    mix = matmul(tag + "out_proj", merged, lw['w_out'], 'nn')
    ln1 = [small['ln1_g'][l].reshape(1, d), small['ln1_b'][l].reshape(1, d)]
    (x1,) = rows_fwd(tag + "ln1", ln_fn, [(x, d, 0), (mix, d, 0)], ln1, [(d, F32)])
    u = matmul(tag + "ffn_up", x1, lw['w_up'], 'nn')
    h = conv_glu_fwd(u, lw['conv_w'], small['conv_b'][l].reshape(1, -1))
    ff = matmul(tag + "ffn_down", h, lw['w_down'], 'nn')
    ln2 = [small['ln2_g'][l].reshape(1, d), small['ln2_b'][l].reshape(1, d)]
    (x2,) = rows_fwd(tag + "ln2", ln_fn, [(x1, d, 0), (ff, d, 0)], ln2, [(d, F32)])
    res.update(merged=merged, mix=mix, x1=x1, u=u, h=h, ff=ff, ys=ys)
    return x2, res


def _layer_bwd(l, dy, res, lw, small, tabs):
    x, proj = res['x'], res['proj']
    s, d = x.shape
    a_cols, b_cols, cq_lane = _cols(d)
    tag = f"l{l}_"
    g = {}
    ln2 = [small['ln2_g'][l].reshape(1, d), small['ln2_b'][l].reshape(1, d)]
    dff, g['ln2_g'], g['ln2_b'] = rows_bwd(tag + "ln2_bwd", ln_fn, [(res['x1'], d, 0), (res['ff'], d, 0)], ln2,
                                           [(dy, d)], [(1, F32)])
    g['w_ffn_down'] = matmul(tag + "dw_down", res['h'], dff, 'tn')
    dh = matmul(tag + "dh", dff, lw['w_down'], 'nt')
    conv_b = small['conv_b'][l].reshape(1, -1)
    dug, duv, dwg, dwv, dbg, dbv = conv_glu_bwd(res['u'], dh, lw['conv_w'], conv_b)
    du = jnp.concatenate([dug, duv], axis=1)
    g['conv_w'] = jnp.concatenate([dwg, dwv], axis=1)
    g['conv_b'] = jnp.concatenate([dbg, dbv], axis=1).reshape(-1)
    g['w_ffn_up'] = matmul(tag + "dw_up", res['x1'], du, 'tn')
    dx1 = matmul(tag + "dx1", du, lw['w_up'], 'nt', add=dff, scale=ALPHA)
    ln1 = [small['ln1_g'][l].reshape(1, d), small['ln1_b'][l].reshape(1, d)]
    dmix, g['ln1_g'], g['ln1_b'] = rows_bwd(tag + "ln1_bwd", ln_fn, [(x, d, 0), (res['mix'], d, 0)], ln1,
                                            [(dx1, d)], [(1, F32)])
    g['w_out'] = matmul(tag + "dw_out", res['merged'], dmix, 'tn')
    dmerged = matmul(tag + "dmerged", dmix, lw['w_out'], 'nt')
    bg = small['b_gate'][l].reshape(3, 1, d)
    mrows = [(y, d, 0) for y in res['ys']] + [(proj, d, i) for i in range(3)]
    (dy0, dy1, dy2, dg0, dg1, dg2, db0, db1, db2) = rows_bwd(
        tag + "merge_bwd", merge_fn, mrows, [bg[0], bg[1], bg[2]], [(dmerged, d)], [(i, BF16) for i in range(6)])
    g['b_gate'] = jnp.concatenate([db0, db1, db2], axis=1).reshape(-1)
    branch_in = (res['o_a'], res['o_b'], res['o_c'])
    dwb = [matmul(tag + f"dw_branch{i}", o, dyi, 'tn') for i, (o, dyi) in enumerate(zip(branch_in, (dy0, dy1, dy2)))]
    dwb[2] = dwb[2].reshape(N_HEADS, LANE, d)[:, :HEAD_DIM].reshape(HW, d)
    g['w_branch'] = jnp.stack(dwb, axis=0)
    do_a, do_b, do_c = [matmul(tag + f"do_branch{i}", dyi, lw[f'wb{i}'], 'nt')
                        for i, dyi in enumerate((dy0, dy1, dy2))]
    dqc = mla_bwd_dq(res['qc'], res['kc'], res['vc'], res['o_c'], res['lse_c'], do_c)
    dkc, dvc = mla_bwd_dkv(res['qc'], res['kc'], res['vc'], res['o_c'], res['lse_c'], do_c)
    gq = small['q_norm_g'][l].reshape(1, -1)
    gkv = small['kv_norm_g'][l].reshape(1, -1)
    dcq, dgq, dwq = rows_bwd(tag + "c_q_bwd", mla_q_fn,
                             [(proj, C_Q_RANK, cq_lane // C_Q_RANK), (tabs['cos'], LANE, 0), (tabs['sin'], LANE, 0)],
                             [gq, lw['wq']], [(dqc, N_HEADS * LANE)], [(0, BF16)])
    ckv_blk = (cq_lane + C_Q_RANK) // LANE
    dckv, dkr, dgkv, dwk, dwv = rows_bwd(
        tag + "c_kv_bwd", mla_kv_fn,
        [(proj, LANE, ckv_blk), (proj, LANE, ckv_blk + 1), (tabs['cos'], LANE, 0), (tabs['sin'], LANE, 0)],
        [gkv, lw['wk'], lw['wv']], [(dkc, N_HEADS * LANE), (dvc, N_HEADS * LANE)], [(0, BF16), (1, BF16)])
    g['q_norm_g'], g['kv_norm_g'] = dgq.reshape(-1), dgkv.reshape(-1)
    per = C_NOPE + C_ROPE
    g['w_uq'] = _unpad_heads(dwq, per).reshape(C_Q_RANK, N_HEADS * per)
    g['w_ukv'] = jnp.concatenate([_unpad_heads(dwk, C_NOPE), _unpad_heads(dwv, C_NOPE)], axis=2).reshape(
        C_KV_RANK, N_HEADS * 2 * C_NOPE)
    sink = small['sinks'][l].reshape(N_HEADS, 1, 1)
    (dq, dkp, dko, dvp, dvo), dbias_b, dsink = band_bwd(tag + "b_bwd", proj, 1, b_cols, tabs['bias'][3], sink, False,
                                                        do_b, None)
    g['sinks'] = dsink.reshape(-1)
    pieces_b = [dq.astype(BF16), shift_add(tag + "b_dk", dko, dkp, 1), shift_add(tag + "b_dv", dvo, dvp, 1)]
    dbias = [None] * 4
    dbias[3] = dbias_b
    combo = [(o, HW, 0) for o in res['a_o']] + [(lg, HW, 0) for lg in res['a_l']]
    a_cts = rows_bwd(tag + "a_combine_bwd", combine_fn, combo, [], [(do_a, HW)], [(i, F32) for i in range(6)])
    pieces_a = []
    for gi, (window, dil) in enumerate(A_GROUPS):
        (dq, dkp, dko, dvp, dvo), dbias[gi], _ = band_bwd(tag + f"a{gi}_bwd", proj, dil, a_cols[gi], tabs['bias'][gi],
                                                          None, True, a_cts[gi], a_cts[3 + gi])
        pieces_a += [dq.astype(BF16), shift_add(tag + f"a{gi}_dk", dko, dkp, dil),
                     shift_add(tag + f"a{gi}_dv", dvo, dvp, dil)]
    g['dbias'] = dbias
    dproj = jnp.concatenate([dg0, dg1, dg2] + pieces_a + pieces_b + [dcq, dckv, dkr], axis=1)
    dw_in_p = matmul(tag + "dw_in", x, dproj, 'tn')
    g['w_in'] = _unproj_grad(dw_in_p, d)
    dx = matmul(tag + "dx", dproj, lw['w_in_p'], 'nt', add=dmix, scale=ALPHA)
    return dx, g


def kernel(x, rel_table, w_in, b_gate, sinks, q_norm_g, kv_norm_g, w_uq, w_ukv, w_branch, w_out, ln1_g, ln1_b, w_ffn_up, conv_w, conv_b, w_ffn_down, ln2_g, ln2_b, loss_target, m_rel_table, m_w_in, m_b_gate, m_sinks, m_q_norm_g, m_kv_norm_g, m_w_uq, m_w_ukv, m_w_branch, m_w_out, m_ln1_g, m_ln1_b, m_w_ffn_up, m_conv_w, m_conv_b, m_w_ffn_down, m_ln2_g, m_ln2_b, v_rel_table, v_w_in, v_b_gate, v_sinks, v_q_norm_g, v_kv_norm_g, v_w_uq, v_w_ukv, v_w_branch, v_w_out, v_ln1_g, v_ln1_b, v_w_ffn_up, v_conv_w, v_conv_b, v_w_ffn_down, v_ln2_g, v_ln2_b):
    args = locals()
    w = {n: args[n] for n in WEIGHTS}
    mom = {n: args["m_" + n] for n in WEIGHTS}
    var = {n: args["v_" + n] for n in WEIGHTS}
    s, d = x.shape[1], x.shape[2]
    xs = x.reshape(s, d)
    target = loss_target.reshape(s, d)

    big = [n for n in WEIGHTS if n in SHARDED]
    small_names = [n for n in WEIGHTS if n not in SHARDED]
    gathered = all_gather("gather_weights", _pack([w[n] for n in GATHER_BF16], BF16, 16))
    shard_shapes = [w[n].shape for n in GATHER_BF16]
    parts = [_unpack(gathered[dv], shard_shapes) for dv in range(N_DEV)]
    full = {n: jnp.concatenate([parts[dv][i] for dv in range(N_DEV)], axis=SHARDED[n])
            for i, n in enumerate(GATHER_BF16)}
    conv_all = all_gather("gather_conv_w", _pack([w['conv_w']], F32, 8))
    full['conv_w'] = jnp.concatenate([_unpack(conv_all[dv], [w['conv_w'].shape])[0] for dv in range(N_DEV)], axis=2)
    small = {n: w[n] for n in small_names}
    loss_part, grad_x, g_full, g_small = _local_grads(xs, target, full, small)
    grad_x = grad_x.reshape(x.shape)

    per_dev = jnp.concatenate([_to_shards(g_full[n], SHARDED[n]).reshape(N_DEV, -1) for n in big], axis=1)
    n_el = per_dev.shape[1]
    quantum = 512 * LANE
    tot = -(-n_el // quantum) * quantum
    per_dev = jnp.pad(per_dev, ((0, 0), (0, tot - n_el))).reshape(N_DEV, tot // LANE, LANE)
    recv = all_to_all("exchange_grads", per_dev)
    pk = lambda tree: _pack([tree[n] for n in big], F32, 512)
    big_out = adamw_sum("adamw_sharded", recv, pk(w), pk(mom), pk(var))
    big_shapes = [w[n].shape for n in big]
    big_res = [dict(zip(big, _unpack(o, big_shapes))) for o in big_out]

    small_pack = _pack([g_small[n].reshape(w[n].shape) for n in small_names] + [loss_part[0, 0:1]], F32, 8)
    small_all = all_gather("gather_small_grads", small_pack)
    pks = lambda tree: _pack([tree[n] for n in small_names] + [jnp.zeros((1,), F32)], F32, 8)
    small_out = adamw_sum("adamw_replicated", small_all, pks(w), pks(mom), pks(var))
    small_shapes = [w[n].shape for n in small_names] + [(1,)]
    small_res = [dict(zip(small_names + ['loss'], _unpack(o, small_shapes))) for o in small_out]

    loss = small_res[0]['loss'].reshape(())
    out = [loss, grad_x]
    for k in range(4):
        out += [big_res[k][n] if n in SHARDED else small_res[k][n] for n in WEIGHTS]
    return tuple(out)


def _local_grads(xs, target, full, small):
    s, d = xs.shape
    rel_table = small['rel_table']
    big = [n for n in WEIGHTS if n in SHARDED]
    small_names = [n for n in WEIGHTS if n not in SHARDED]

    cos, sin = _rope_tables(s)
    buckets = [_buckets(dil) for _, dil in A_GROUPS] + [_buckets(1)]
    heads = [rel_table[:, gi * N_HEADS:(gi + 1) * N_HEADS] for gi in range(4)]
    bias = [jnp.transpose(heads[gi][buckets[gi]], (2, 0, 1)) for gi in range(4)]
    tabs = dict(cos=cos, sin=sin, bias=bias)

    act, saved, lws = xs, [], []
    for l in range(DEPTH):
        lw = _layer_weights(full, l, d)
        act, res = _layer_fwd(l, act, lw, small, tabs)
        saved.append(res)
        lws.append(lw)
    dy, loss_part = loss_head(act, target)

    grads = [None] * DEPTH
    for l in reversed(range(DEPTH)):
        dy, grads[l] = _layer_bwd(l, dy, saved[l], lws[l], small, tabs)
    grad_x = dy

    rel_cols = []
    for gi in range(4):
        both = grads[0]['dbias'][gi] + grads[1]['dbias'][gi] if DEPTH == 2 else grads[0]['dbias'][gi]
        rel_cols.append(rel_grad(f"rel_grad{gi}", both, buckets[gi])[:, :REL_BUCKETS].T)
    g_rel = jnp.concatenate(rel_cols, axis=1)

    def stacked(n):
        return jnp.stack([grads[l][n] for l in range(DEPTH)], axis=0)

    g_full = {n: stacked(n) for n in big}
    g_small = {'rel_table': g_rel}
    for n in small_names:
        if n != 'rel_table':
            g_small[n] = stacked(n)
    return loss_part, grad_x, g_full, g_small
```

```python
import functools
import math

import numpy as np
import jax
import jax.numpy as jnp
from jax import lax
from jax.experimental import pallas as pl
from jax.experimental.pallas import tpu as pltpu

F32 = jnp.float32
BF16 = jnp.bfloat16

N_DEV = 8
DEPTH = 2
HEAD_DIM = 64
BLOCK = 128
A_GROUPS = ((128, 1), (512, 4), (2048, 16))
N_HEADS = 8
HW = N_HEADS * HEAD_DIM
B_KV_HEADS = 2
C_Q_RANK = 256
C_KV_RANK = 128
C_NOPE = 64
C_ROPE = 32
ROPE_BASE = 10000.0
REL_BUCKETS = 32
REL_MAX_DIST = 2048
ALPHA = (2 * DEPTH) ** 0.25
LN_EPS = 1e-5
RMS_EPS = 1e-6
NEG = -1e30
ADAM_LR = 0.001
ADAM_B1 = 0.9
ADAM_B2 = 0.999
ADAM_EPS = 1e-08
ADAM_WD = 0.01
ADAM_STEP = 10

LANE = 128
TM = 256
VMEM_LIMIT = 56 * 1024 * 1024
MESH_ID = pl.DeviceIdType.MESH

WEIGHTS = ('rel_table', 'w_in', 'b_gate', 'sinks', 'q_norm_g', 'kv_norm_g', 'w_uq', 'w_ukv', 'w_branch',
           'w_out', 'ln1_g', 'ln1_b', 'w_ffn_up', 'conv_w', 'conv_b', 'w_ffn_down', 'ln2_g', 'ln2_b')
SHARDED = {'w_in': 2, 'w_uq': 2, 'w_ukv': 2, 'w_branch': 3, 'w_out': 1, 'w_ffn_up': 2, 'conv_w': 2,
           'w_ffn_down': 1}
GATHER_BF16 = ('w_in', 'w_uq', 'w_ukv', 'w_branch', 'w_out', 'w_ffn_up', 'w_ffn_down')


def _params(sem=None):
    return pltpu.CompilerParams(dimension_semantics=sem, vmem_limit_bytes=VMEM_LIMIT)


def _pick(n, target):
    if n <= target:
        return n
    best = None
    for t in range(LANE, target + 1, LANE):
        if n % t == 0:
            best = t
    assert best is not None, (n, target)
    return best


def _dot(a, b, ca, cb):
    return lax.dot_general(a.astype(BF16), b.astype(BF16), (((ca,), (cb,)), ((), ())),
                           preferred_element_type=F32)


@jax.custom_vjp
def mm(a, b):
    return _dot(a, b, 1, 0)


def _mm_fwd(a, b):
    return _dot(a, b, 1, 0), (a, b)


def _mm_bwd(res, g):
    a, b = res
    return _dot(g, b, 1, 1), _dot(a, g, 0, 0)


mm.defvjp(_mm_fwd, _mm_bwd)


@jax.custom_vjp
def mm_nt(a, b):
    return _dot(a, b, 1, 1)


def _mm_nt_fwd(a, b):
    return _dot(a, b, 1, 1), (a, b)


def _mm_nt_bwd(res, g):
    a, b = res
    return _dot(g, b, 1, 0), _dot(g, a, 0, 0)


mm_nt.defvjp(_mm_nt_fwd, _mm_nt_bwd)


def _split_impl(x, n):
    w = x.shape[-1] // n
    return tuple(x[:, i * w:(i + 1) * w] for i in range(n))


@functools.partial(jax.custom_vjp, nondiff_argnums=(1,))
def split_lanes(x, n):
    return _split_impl(x, n)


def _split_fwd(x, n):
    return _split_impl(x, n), None


def _split_bwd(n, _, gs):
    return (jnp.concatenate(gs, axis=-1),)


split_lanes.defvjp(_split_fwd, _split_bwd)


@jax.custom_vjp
def concat_lanes(xs):
    return jnp.concatenate(xs, axis=-1)


def _concat_fwd(xs):
    return jnp.concatenate(xs, axis=-1), len(xs)


def _concat_bwd(n, g):
    return (_split_impl(g, n),)


concat_lanes.defvjp(_concat_fwd, _concat_bwd)


@jax.custom_vjp
def concat_rows(a, b):
    return jnp.concatenate([a, b], axis=0)


def _crow_fwd(a, b):
    return jnp.concatenate([a, b], axis=0), a.shape[0]


def _crow_bwd(na, g):
    return g[:na], g[na:]


concat_rows.defvjp(_crow_fwd, _crow_bwd)


def _rot_half(x):
    n = x.shape[-1]
    lane = lax.broadcasted_iota(jnp.int32, (1, n), 1) % LANE
    lo = (lane >= C_NOPE) & (lane < C_NOPE + C_ROPE // 2)
    hi = (lane >= C_NOPE + C_ROPE // 2) & (lane < C_NOPE + C_ROPE)
    up = pltpu.roll(x, n - C_ROPE // 2, 1)
    dn = pltpu.roll(x, C_ROPE // 2, 1)
    return jnp.where(lo, -up, jnp.where(hi, dn, 0.0))


@jax.custom_vjp
def rope(x, c, s):
    return x * c + _rot_half(x) * s


def _rope_fwd(x, c, s):
    return x * c + _rot_half(x) * s, (c, s)


def _rope_bwd(res, g):
    c, s = res
    return g * c - _rot_half(g * s), jnp.zeros_like(c), jnp.zeros_like(s)


rope.defvjp(_rope_fwd, _rope_bwd)


def _sigmoid(x):
    return 1.0 / (1.0 + jnp.exp(-x))


def matmul(name, a, b, mode, out_dtype=F32, add=None, scale=1.0):
    if mode == 'nn':
        (m, k), n = a.shape, b.shape[1]
    elif mode == 'nt':
        (m, k), n = a.shape, b.shape[0]
    else:
        (k, m), n = a.shape, b.shape[1]
    tm, tn = _pick(m, 1024), _pick(n, 1024)
    tk = _pick(k, 1536)
    nk = k // tk
    a_spec = pl.BlockSpec((tk, tm), lambda i, j, kk: (kk, i)) if mode == 'tn' else \
        pl.BlockSpec((tm, tk), lambda i, j, kk: (i, kk))
    b_spec = pl.BlockSpec((tn, tk), lambda i, j, kk: (j, kk)) if mode == 'nt' else \
        pl.BlockSpec((tk, tn), lambda i, j, kk: (kk, j))
    o_spec = pl.BlockSpec((tm, tn), lambda i, j, kk: (i, j))
    ca = 0 if mode == 'tn' else 1
    cb = 1 if mode == 'nt' else 0
    has_add = add is not None

    def body(*refs):
        a_ref, b_ref = refs[:2]
        add_ref = refs[2] if has_add else None
        o_ref = refs[2 + int(has_add)]

        def finish(r):
            if has_add:
                r = r + scale * add_ref[...]
            o_ref[...] = r.astype(out_dtype)

        if nk == 1:
            finish(_dot(a_ref[...], b_ref[...], ca, cb))
            return
        acc_ref = refs[-1]
        kk = pl.program_id(2)

        @pl.when(kk == 0)
        def _():
            acc_ref[...] = jnp.zeros_like(acc_ref)

        acc_ref[...] += _dot(a_ref[...], b_ref[...], ca, cb)

        @pl.when(kk == nk - 1)
        def _():
            finish(acc_ref[...])

    ins = [a, b] + ([add] if has_add else [])
    specs = [a_spec, b_spec] + ([o_spec] if has_add else [])
    return pl.pallas_call(
        body, name=name, grid=(m // tm, n // tn, nk), in_specs=specs, out_specs=o_spec,
        out_shape=jax.ShapeDtypeStruct((m, n), out_dtype),
        scratch_shapes=[pltpu.VMEM((tm, tn), F32)] if nk > 1 else [],
        compiler_params=_params(("parallel", "parallel", "arbitrary")))(*ins)


def _row_spec(width, col_block=0, tm=TM):
    return pl.BlockSpec((tm, width), lambda i: (i, col_block))


def _full_spec(shape):
    nd = len(shape)
    return pl.BlockSpec(shape, lambda i: (0,) * nd)


def rows_fwd(name, fn, rows, params, outs, tm=TM):
    s = rows[0][0].shape[0]
    nr, npar = len(rows), len(params)

    def body(*refs):
        r = [refs[i][...].astype(F32) for i in range(nr)]
        p = [refs[nr + i][...] for i in range(npar)]
        res = fn(*r, *p)
        for o_ref, val in zip(refs[nr + npar:], res):
            o_ref[...] = val.astype(o_ref.dtype)

    return pl.pallas_call(
        body, name=name, grid=(s // tm,),
        in_specs=[_row_spec(w, cb, tm) for _, w, cb in rows] + [_full_spec(p.shape) for p in params],
        out_specs=[_row_spec(w, 0, tm) for w, _ in outs],
        out_shape=[jax.ShapeDtypeStruct((s, w), dt) for w, dt in outs],
        compiler_params=_params(("parallel",)))(*[a for a, _, _ in rows], *params)


def rows_bwd(name, fn, rows, params, cts, row_grads, tm=TM):
    s = rows[0][0].shape[0]
    nr, npar, nct, nrg = len(rows), len(params), len(cts), len(row_grads)

    def body(*refs):
        r = [refs[i][...].astype(F32) for i in range(nr)]
        p = [refs[nr + i][...].astype(F32) for i in range(npar)]
        g = tuple(refs[nr + npar + i][...].astype(F32) for i in range(nct))
        _, vjp = jax.vjp(lambda *a: tuple(fn(*a)), *r, *p)
        grads = vjp(g)
        outs = refs[nr + npar + nct:]
        for k, (idx, _) in enumerate(row_grads):
            outs[k][...] = grads[idx].astype(outs[k].dtype)

        @pl.when(pl.program_id(0) == 0)
        def _():
            for k in range(npar):
                outs[nrg + k][...] = jnp.zeros_like(outs[nrg + k])

        for k in range(npar):
            outs[nrg + k][...] += grads[nr + k]

    return pl.pallas_call(
        body, name=name, grid=(s // tm,),
        in_specs=[_row_spec(w, cb, tm) for _, w, cb in rows] + [_full_spec(p.shape) for p in params]
        + [_row_spec(w, 0, tm) for _, w in cts],
        out_specs=[_row_spec(rows[idx][1], 0, tm) for idx, _ in row_grads] + [_full_spec(p.shape) for p in params],
        out_shape=[jax.ShapeDtypeStruct((s, rows[idx][1]), dt) for idx, dt in row_grads]
        + [jax.ShapeDtypeStruct(p.shape, F32) for p in params],
        compiler_params=_params(("arbitrary",)))(*[a for a, _, _ in rows], *params, *[a for a, _ in cts])


def ln_fn(x, r, g, b):
    z = ALPHA * x + r
    mu = jnp.mean(z, axis=-1, keepdims=True)
    zc = z - mu
    var = jnp.mean(zc * zc, axis=-1, keepdims=True)
    return (zc * lax.rsqrt(var + LN_EPS) * g + b,)


def _rms(x, g):
    return x * lax.rsqrt(jnp.mean(x * x, axis=-1, keepdims=True) + RMS_EPS) * g


def mla_q_fn(cq, c, s, g, wq):
    q = mm(_rms(cq, g), wq)
    return (rope(q, jnp.concatenate([c] * N_HEADS, axis=-1), jnp.concatenate([s] * N_HEADS, axis=-1)) * MLA_SCALE,)


@jax.custom_vjp
def tile_heads(x):
    return jnp.concatenate([x] * N_HEADS, axis=-1)


def _tile_fwd(x):
    return jnp.concatenate([x] * N_HEADS, axis=-1), None


def _tile_bwd(_, g):
    parts = _split_impl(g, N_HEADS)
    acc = parts[0]
    for p in parts[1:]:
        acc = acc + p
    return (acc,)


tile_heads.defvjp(_tile_fwd, _tile_bwd)


def mla_kv_fn(ckv, kr, c, s, g, wk, wv):
    n = _rms(ckv, g)
    return mm(n, wk) + tile_heads(rope(kr, c, s)), mm(n, wv)


def merge_fn(y0, y1, y2, g0, g1, g2, b0, b1, b2):
    return (_sigmoid(g0 + b0) * y0 + _sigmoid(g1 + b1) * y1 + _sigmoid(g2 + b2) * y2,)


def combine_fn(o1, o2, o3, l1, l2, l3):
    mx = lax.stop_gradient(jnp.maximum(jnp.maximum(l1, l2), l3))
    e1, e2, e3 = jnp.exp(l1 - mx), jnp.exp(l2 - mx), jnp.exp(l3 - mx)
    return ((e1 * o1 + e2 * o2 + e3 * o3) / (e1 + e2 + e3),)


def loss_head(y, t):
    s, d = y.shape

    def body(y_ref, t_ref, dy_ref, loss_ref):
        err = y_ref[...] - t_ref[...]
        dy_ref[...] = err * (1.0 / d)

        @pl.when(pl.program_id(0) == 0)
        def _():
            loss_ref[...] = jnp.zeros_like(loss_ref)

        loss_ref[...] += jnp.sum(err * err) * (0.5 / d)

    return pl.pallas_call(
        body, name="loss_head", grid=(s // TM,), in_specs=[_row_spec(d), _row_spec(d)],
        out_specs=[_row_spec(d), _full_spec((8, LANE))],
        out_shape=[jax.ShapeDtypeStruct((s, d), F32), jax.ShapeDtypeStruct((8, LANE), F32)],
        compiler_params=_params(("arbitrary",)))(y, t)


CONV_TN = 256


def _conv_taps(u_ext, w_ref, b_ref):
    n = u_ext.shape[0]
    return (w_ref[0:1, :] * pltpu.roll(u_ext, 2, 0) + w_ref[1:2, :] * pltpu.roll(u_ext, 1, 0)
            + w_ref[2:3, :] * u_ext + b_ref[...])


def conv_glu_fwd(u, conv_w, conv_b):
    s, f2 = u.shape
    f = f2 // 2
    tn, tm = CONV_TN, TM
    off = f // tn
    hb = tm // 8

    def body(ug, uv, hg, hv, wg, wv, bg, bv, o_ref):
        first = pl.program_id(0) == 0

        def conv(u_ref, h_ref, w_ref, b_ref):
            halo = jnp.where(first, 0.0, h_ref[...])
            ext = jnp.concatenate([halo, u_ref[...]], axis=0)
            return _conv_taps(ext, w_ref, b_ref)[8:]

        cg, cv = conv(ug, hg, wg, bg), conv(uv, hv, wv, bv)
        o_ref[...] = (cg * _sigmoid(cg) * cv).astype(o_ref.dtype)

    blk = lambda o: pl.BlockSpec((tm, tn), lambda i, j: (i, j + o))
    halo = lambda o: pl.BlockSpec((8, tn), lambda i, j: (jnp.maximum(i * hb - 1, 0), j + o))
    par = lambda r, o: pl.BlockSpec((r, tn), lambda i, j: (0, j + o))
    return pl.pallas_call(
        body, name="conv_glu_fwd", grid=(s // tm, off),
        in_specs=[blk(0), blk(off), halo(0), halo(off), par(3, 0), par(3, off), par(1, 0), par(1, off)],
        out_specs=pl.BlockSpec((tm, tn), lambda i, j: (i, j)),
        out_shape=jax.ShapeDtypeStruct((s, f), BF16),
        compiler_params=_params(("parallel", "parallel")))(u, u, u, u, conv_w, conv_w, conv_b, conv_b)


def conv_glu_bwd(u, dh, conv_w, conv_b):
    s, f2 = u.shape
    f = f2 // 2
    tn, tm = CONV_TN, TM
    off = f // tn
    hb = tm // 8
    n_rows = s // tm

    def body(ug, uv, pg, pv, ng, nv, dh_ref, dhn_ref, wg, wv, bg, bv, dug, duv, dwg, dwv, dbg, dbv):
        i = pl.program_id(1)
        first, last = i == 0, i == n_rows - 1

        def ext(u_ref, p_ref, n_ref):
            return jnp.concatenate([jnp.where(first, 0.0, p_ref[...]), u_ref[...], n_ref[...]], axis=0)

        eg, ev = ext(ug, pg, ng), ext(uv, pv, nv)
        cg, cv = _conv_taps(eg, wg, bg), _conv_taps(ev, wv, bv)
        dhe = jnp.concatenate([jnp.zeros((8, tn), F32), dh_ref[...], jnp.where(last, 0.0, dhn_ref[...])], axis=0)
        sg = _sigmoid(cg)
        dcg = dhe * cv * (sg * (1.0 + cg * (1.0 - sg)))
        dcv = dhe * (cg * sg)
        n = tm + 16

        @pl.when(i == 0)
        def _():
            for r in (dwg, dwv, dbg, dbv):
                r[...] = jnp.zeros_like(r)

        def back(dc, e, w_ref, du_ref, dw_ref, db_ref):
            du = w_ref[2:3, :] * dc + w_ref[1:2, :] * pltpu.roll(dc, n - 1, 0) + w_ref[0:1, :] * pltpu.roll(dc, n - 2, 0)
            du_ref[...] = du[8:8 + tm].astype(du_ref.dtype)
            own = dc[8:8 + tm]
            dw_ref[0:1, :] += jnp.sum(own * pltpu.roll(e, 2, 0)[8:8 + tm], axis=0, keepdims=True)
            dw_ref[1:2, :] += jnp.sum(own * pltpu.roll(e, 1, 0)[8:8 + tm], axis=0, keepdims=True)
            dw_ref[2:3, :] += jnp.sum(own * e[8:8 + tm], axis=0, keepdims=True)
            db_ref[...] += jnp.sum(own, axis=0, keepdims=True)

        back(dcg, eg, wg, dug, dwg, dbg)
        back(dcv, ev, wv, duv, dwv, dbv)

    blk = lambda o: pl.BlockSpec((tm, tn), lambda j, i: (i, j + o))
    prev = lambda o: pl.BlockSpec((8, tn), lambda j, i: (jnp.maximum(i * hb - 1, 0), j + o))
    nxt = lambda o: pl.BlockSpec((8, tn), lambda j, i: (jnp.minimum((i + 1) * hb, s // 8 - 1), j + o))
    par = lambda r, o: pl.BlockSpec((r, tn), lambda j, i: (0, j + o))
    return pl.pallas_call(
        body, name="conv_glu_bwd", grid=(off, n_rows),
        in_specs=[blk(0), blk(off), prev(0), prev(off), nxt(0), nxt(off), blk(0), nxt(0),
                  par(3, 0), par(3, off), par(1, 0), par(1, off)],
        out_specs=[blk(0), blk(0), par(3, 0), par(3, 0), par(1, 0), par(1, 0)],
        out_shape=[jax.ShapeDtypeStruct((s, f), BF16)] * 2 + [jax.ShapeDtypeStruct((3, f), F32)] * 2
        + [jax.ShapeDtypeStruct((1, f), F32)] * 2,
        compiler_params=_params(("parallel", "arbitrary")))(u, u, u, u, u, u, dh, dh, conv_w, conv_w, conv_b, conv_b)


def _band_fn(q, kp, kc, vp, vc, bias, sink, first, inclusive):
    k = concat_rows(kp, kc)
    v = concat_rows(vp, vc)
    qi = lax.broadcasted_iota(jnp.int32, (BLOCK, 2 * BLOCK), 0)
    ki = lax.broadcasted_iota(jnp.int32, (BLOCK, 2 * BLOCK), 1)
    step = BLOCK + qi - ki
    valid = (step >= 0) & ((step <= BLOCK) if inclusive else (step < BLOCK))
    valid = valid & (jnp.logical_not(first) | (ki >= BLOCK))
    lane = lax.broadcasted_iota(jnp.int32, (1, LANE), 1)
    scale = HEAD_DIM ** -0.5
    qs, ks, vs = split_lanes(q, 4), split_lanes(k, 4), split_lanes(v, 4)
    o_parts, l_parts = [], []
    for pair in range(4):
        o_acc, l_acc = None, None
        for sub in range(2):
            h = 2 * pair + sub
            mh = ((lane >= sub * HEAD_DIM) & (lane < (sub + 1) * HEAD_DIM)).astype(F32)
            logits = mm_nt(qs[pair] * mh, ks[pair]) * scale + bias[h]
            logits = jnp.where(valid, logits, NEG)
            m = lax.stop_gradient(jnp.max(logits, axis=-1, keepdims=True))
            if sink is not None:
                m = jnp.maximum(m, lax.stop_gradient(sink[h]))
            p = jnp.exp(logits - m)
            den = jnp.sum(p, axis=-1, keepdims=True)
            if sink is not None:
                den = den + jnp.exp(sink[h] - m)
            oh = mm(p / den, vs[pair] * mh)
            lh = (m + jnp.log(den)) * mh
            o_acc = oh if o_acc is None else o_acc + oh
            l_acc = lh if l_acc is None else l_acc + lh
        o_parts.append(o_acc)
        l_parts.append(l_acc)
    return concat_lanes(tuple(o_parts)), concat_lanes(tuple(l_parts))


def _band_specs(nb, cq, ck, cv):
    q = pl.BlockSpec((BLOCK, HW), lambda r, n: (r * nb + n, cq))
    kp = pl.BlockSpec((BLOCK, HW), lambda r, n: (r * nb + jnp.maximum(n - 1, 0), ck))
    kc = pl.BlockSpec((BLOCK, HW), lambda r, n: (r * nb + n, ck))
    vp = pl.BlockSpec((BLOCK, HW), lambda r, n: (r * nb + jnp.maximum(n - 1, 0), cv))
    vc = pl.BlockSpec((BLOCK, HW), lambda r, n: (r * nb + n, cv))
    return [q, kp, kc, vp, vc]


def to_classes(a, dil):
    if dil == 1:
        return a
    s, c = a.shape
    return a.reshape(s // dil, dil, c).transpose(1, 0, 2).reshape(s, c)


def from_classes(a, dil):
    if dil == 1:
        return a
    s, c = a.shape
    return a.reshape(dil, s // dil, c).transpose(1, 0, 2).reshape(s, c)


def _const_spec(shape):
    nd = len(shape)
    return pl.BlockSpec(shape, lambda r, n: (0,) * nd)


def band_fwd(name, qkv, dil, cols, bias, sink, inclusive):
    s = qkv.shape[0]
    nb = s // (BLOCK * dil)
    has_sink = sink is not None

    def body(*refs):
        q, kp, kc, vp, vc, b_ref = refs[:6]
        s_ref = refs[6] if has_sink else None
        o_ref, l_ref = refs[-2:]
        first = pl.program_id(1) == 0
        bias_l = tuple(b_ref[h] for h in range(N_HEADS))
        sink_l = tuple(s_ref[h] for h in range(N_HEADS)) if has_sink else None
        o, l = _band_fn(q[...], kp[...], kc[...], vp[...], vc[...], bias_l, sink_l, first, inclusive)
        o_ref[...] = o
        l_ref[...] = l

    out_spec = pl.BlockSpec((BLOCK, HW), lambda r, n: (r * nb + n, 0))
    ins = [qkv] * 5 + [bias] + ([sink] if has_sink else [])
    specs = _band_specs(nb, *cols) + [_const_spec(bias.shape)] + ([_const_spec(sink.shape)] if has_sink else [])
    return pl.pallas_call(
        body, name=name, grid=(dil, nb), in_specs=specs, out_specs=[out_spec, out_spec],
        out_shape=[jax.ShapeDtypeStruct((s, HW), F32)] * 2,
        compiler_params=_params(("parallel", "parallel")))(*ins)


def band_bwd(name, qkv, dil, cols, bias, sink, inclusive, do, dl):
    s = qkv.shape[0]
    nb = s // (BLOCK * dil)
    has_sink, has_dl = sink is not None, dl is not None
    n_in = 6 + int(has_sink) + 1 + int(has_dl)

    def body(*refs):
        q, kp, kc, vp, vc, b_ref = refs[:6]
        s_ref = refs[6] if has_sink else None
        do_ref = refs[6 + int(has_sink)]
        dl_ref = refs[7 + int(has_sink)] if has_dl else None
        outs = refs[n_in:]
        first = pl.program_id(1) == 0
        bias_l = tuple(b_ref[h] for h in range(N_HEADS))
        sink_l = tuple(s_ref[h] for h in range(N_HEADS)) if has_sink else ()

        def fn(q_, kp_, kc_, vp_, vc_, b_, s_):
            o, l = _band_fn(q_, kp_, kc_, vp_, vc_, b_, s_ if has_sink else None, first, inclusive)
            return (o, l) if has_dl else (o,)

        _, vjp = jax.vjp(fn, q[...], kp[...], kc[...], vp[...], vc[...], bias_l, sink_l)
        ct = (do_ref[...], dl_ref[...]) if has_dl else (do_ref[...],)
        g = vjp(ct)
        for k in range(5):
            outs[k][...] = g[k]

        @pl.when((pl.program_id(0) == 0) & (pl.program_id(1) == 0))
        def _():
            outs[5][...] = jnp.zeros_like(outs[5])
            if has_sink:
                outs[6][...] = jnp.zeros_like(outs[6])

        for h in range(N_HEADS):
            outs[5][h] += g[5][h]
            if has_sink:
                outs[6][h] += g[6][h]

    blk = pl.BlockSpec((BLOCK, HW), lambda r, n: (r * nb + n, 0))
    ins = [qkv] * 5 + [bias] + ([sink] if has_sink else []) + [do] + ([dl] if has_dl else [])
    specs = _band_specs(nb, *cols) + [_const_spec(bias.shape)] \
        + ([_const_spec(sink.shape)] if has_sink else []) + [blk] * (1 + int(has_dl))
    out_shape = [jax.ShapeDtypeStruct((s, HW), F32)] * 5 + [jax.ShapeDtypeStruct(bias.shape, F32)] \
        + ([jax.ShapeDtypeStruct(sink.shape, F32)] if has_sink else [])
    out_specs = [blk] * 5 + [_const_spec(bias.shape)] + ([_const_spec(sink.shape)] if has_sink else [])
    res = pl.pallas_call(
        body, name=name, grid=(dil, nb), in_specs=specs, out_specs=out_specs, out_shape=out_shape,
        compiler_params=_params(("arbitrary", "arbitrary")))(*ins)
    return list(res[:5]), res[5], (res[6] if has_sink else None)


def shift_add(name, own, prev, dil):
    s = own.shape[0]
    nb = s // (BLOCK * dil)

    def body(a_ref, b_ref, o_ref):
        lastb = pl.program_id(1) == nb - 1
        o_ref[...] = (a_ref[...] + jnp.where(lastb, 0.0, b_ref[...])).astype(o_ref.dtype)

    blk = pl.BlockSpec((BLOCK, HW), lambda r, n: (r * nb + n, 0))
    nxt = pl.BlockSpec((BLOCK, HW), lambda r, n: (r * nb + jnp.minimum(n + 1, nb - 1), 0))
    return pl.pallas_call(
        body, name=name, grid=(dil, nb), in_specs=[blk, nxt], out_specs=blk,
        out_shape=jax.ShapeDtypeStruct((s, HW), BF16),
        compiler_params=_params(("parallel", "parallel")))(own, prev)


def rel_grad(name, dbias, bucket):
    def body(d_ref, b_ref, o_ref):
        lane = lax.broadcasted_iota(jnp.int32, (1, LANE), 1)
        bk = b_ref[...]
        for h in range(N_HEADS):
            d = d_ref[h]
            row = jnp.zeros((1, LANE), F32)
            for b in range(REL_BUCKETS):
                tot = jnp.sum(jnp.where(bk == b, d, 0.0))
                row = row + jnp.where(lane == b, tot, 0.0)
            o_ref[h:h + 1, :] = row

    return pl.pallas_call(
        body, name=name, out_shape=jax.ShapeDtypeStruct((N_HEADS, LANE), F32),
        in_specs=[pl.BlockSpec(memory_space=pltpu.VMEM)] * 2, out_specs=pl.BlockSpec(memory_space=pltpu.VMEM),
        compiler_params=_params())(dbias, bucket)


MLA_T = 512
MLA_SCALE = (C_NOPE + C_ROPE) ** -0.5


def _diag_mask(t):
    r = lax.broadcasted_iota(jnp.int32, (t, t), 0)
    c = lax.broadcasted_iota(jnp.int32, (t, t), 1)
    return c <= r


def mla_fwd(q, k, v):
    s = q.shape[0]
    t = min(MLA_T, s)
    nq = s // t

    def body(q_ref, k_ref, v_ref, o_ref, l_ref, m_s, l_s, acc_s):
        i = pl.program_id(1)
        m_s[...] = jnp.full_like(m_s, NEG)
        l_s[...] = jnp.zeros_like(l_s)
        acc_s[...] = jnp.zeros_like(acc_s)
        qb = q_ref[...]

        def chunk(j, masked):
            rows = pl.ds(pl.multiple_of(j * t, t), t)
            sc = _dot(qb, k_ref[rows, :], 1, 1)
            if masked:
                sc = jnp.where(_diag_mask(t), sc, NEG)
            m_old = m_s[...]
            m_new = jnp.maximum(m_old, jnp.max(sc, axis=-1, keepdims=True))
            a = jnp.exp(m_old - m_new)
            p = jnp.exp(sc - m_new[:, 0:1])
            l_s[...] = a * l_s[...] + jnp.sum(p, axis=-1, keepdims=True)
            acc_s[...] = a * acc_s[...] + _dot(p, v_ref[rows, :], 1, 0)
            m_s[...] = m_new

        def step(j, carry):
            chunk(j, False)
            return carry

        lax.fori_loop(0, i, step, 0)
        chunk(i, True)
        o_ref[...] = acc_s[...] / l_s[...]
        l_ref[...] = m_s[...] + jnp.log(l_s[...])

    qs = pl.BlockSpec((t, LANE), lambda h, i: (i, h))
    ks = pl.BlockSpec((s, LANE), lambda h, i: (0, h))
    return pl.pallas_call(
        body, name="mla_fwd", grid=(N_HEADS, nq), in_specs=[qs, ks, ks], out_specs=[qs, qs],
        out_shape=[jax.ShapeDtypeStruct((s, N_HEADS * LANE), F32)] * 2,
        scratch_shapes=[pltpu.VMEM((t, LANE), F32)] * 3,
        compiler_params=_params(("parallel", "parallel")))(q, k, v)


def mla_delta_fn(do, o):
    parts = []
    for a, b in zip(_split_impl(do, N_HEADS), _split_impl(o, N_HEADS)):
        parts.append(jnp.sum(a * b, axis=-1, keepdims=True) + jnp.zeros_like(a))
    return jnp.concatenate(parts, axis=-1), do


def mla_bwd(q, k, v, do, lse, delta):
    s = q.shape[0]
    t = min(MLA_T, s)
    nq = s // t

    def body(q_ref, k_ref, v_ref, do_ref, l_ref, d_ref, dq_ref, dk_ref, dv_ref, dk_s, dv_s):
        j = pl.program_id(1)

        @pl.when(j == 0)
        def _():
            dq_ref[...] = jnp.zeros_like(dq_ref)

        dk_s[...] = jnp.zeros_like(dk_s)
        dv_s[...] = jnp.zeros_like(dv_s)
        kb, vb = k_ref[...], v_ref[...]

        def chunk(i, masked):
            rows = pl.ds(pl.multiple_of(i * t, t), t)
            qi, doi = q_ref[rows, :], do_ref[rows, :]
            sc = _dot(qi, kb, 1, 1)
            if masked:
                sc = jnp.where(_diag_mask(t), sc, NEG)
            p = jnp.exp(sc - l_ref[rows, :][:, 0:1])
            ds = p * (_dot(doi, vb, 1, 1) - d_ref[rows, :][:, 0:1])
            dv_s[...] += _dot(p, doi, 0, 0)
            dk_s[...] += _dot(ds, qi, 0, 0)
            dq_ref[rows, :] += _dot(ds, kb, 1, 0)

        def step(i, carry):
            chunk(i, False)
            return carry

        chunk(j, True)
        lax.fori_loop(j + 1, nq, step, 0)
        dk_ref[...] = dk_s[...]
        dv_ref[...] = dv_s[...]

    ks = pl.BlockSpec((t, LANE), lambda h, j: (j, h))
    full = pl.BlockSpec((s, LANE), lambda h, j: (0, h))
    return pl.pallas_call(
        body, name="mla_bwd", grid=(N_HEADS, nq), in_specs=[full, ks, ks, full, full, full], out_specs=[full, ks, ks],
        out_shape=[jax.ShapeDtypeStruct((s, N_HEADS * LANE), F32)] * 3,
        scratch_shapes=[pltpu.VMEM((t, LANE), F32)] * 2,
        compiler_params=_params(("parallel", "arbitrary")))(q, k, v, do, lse, delta)


def _peer(k):
    x, y, c = lax.axis_index("x"), lax.axis_index("y"), lax.axis_index("c")
    px, py, pc = (x ^ ((k >> 2) & 1)), (y ^ ((k >> 1) & 1)), (c ^ (k & 1))
    return (px, py, pc), 4 * px + 2 * py + pc


def all_gather(name, block):
    r = block.shape[0]

    def body(x_ref, out_ref, send_sems, recv_sems, local_sem):
        me, me_idx = _peer(0)
        mine = pltpu.make_async_copy(x_ref, out_ref.at[me_idx], local_sem)
        mine.start()
        sends, recvs = [], []
        for k in range(1, N_DEV):
            peer, peer_idx = _peer(k)
            sends.append(pltpu.make_async_remote_copy(
                src_ref=x_ref, dst_ref=out_ref.at[me_idx], send_sem=send_sems.at[k - 1], recv_sem=recv_sems.at[k - 1],
                device_id=peer, device_id_type=MESH_ID))
            recvs.append(pltpu.make_async_remote_copy(
                src_ref=x_ref, dst_ref=out_ref.at[peer_idx], send_sem=send_sems.at[k - 1],
                recv_sem=recv_sems.at[k - 1], device_id=peer, device_id_type=MESH_ID))
        for cp in sends:
            cp.start()
        for cp in recvs:
            cp.wait_recv()
        for cp in sends:
            cp.wait_send()
        mine.wait()

    return pl.pallas_call(
        body, name=name, out_shape=jax.ShapeDtypeStruct((N_DEV, r, LANE), block.dtype),
        in_specs=[pl.BlockSpec(memory_space=pl.ANY)], out_specs=pl.BlockSpec(memory_space=pl.ANY),
        scratch_shapes=[pltpu.SemaphoreType.DMA((N_DEV - 1,)), pltpu.SemaphoreType.DMA((N_DEV - 1,)),
                        pltpu.SemaphoreType.DMA])(block)


def all_to_all(name, parts):
    r = parts.shape[1]

    def body(x_ref, out_ref, send_sems, recv_sems, local_sem):
        me, me_idx = _peer(0)
        mine = pltpu.make_async_copy(x_ref.at[me_idx], out_ref.at[me_idx], local_sem)
        mine.start()
        sends, recvs = [], []
        for k in range(1, N_DEV):
            peer, peer_idx = _peer(k)
            sends.append(pltpu.make_async_remote_copy(
                src_ref=x_ref.at[peer_idx], dst_ref=out_ref.at[me_idx], send_sem=send_sems.at[k - 1],
                recv_sem=recv_sems.at[k - 1], device_id=peer, device_id_type=MESH_ID))
            recvs.append(pltpu.make_async_remote_copy(
                src_ref=x_ref.at[me_idx], dst_ref=out_ref.at[peer_idx], send_sem=send_sems.at[k - 1],
                recv_sem=recv_sems.at[k - 1], device_id=peer, device_id_type=MESH_ID))
        for cp in sends:
            cp.start()
        for cp in recvs:
            cp.wait_recv()
        for cp in sends:
            cp.wait_send()
        mine.wait()

    return pl.pallas_call(
        body, name=name, out_shape=jax.ShapeDtypeStruct((N_DEV, r, LANE), parts.dtype),
        in_specs=[pl.BlockSpec(memory_space=pl.ANY)], out_specs=pl.BlockSpec(memory_space=pl.ANY),
        scratch_shapes=[pltpu.SemaphoreType.DMA((N_DEV - 1,)), pltpu.SemaphoreType.DMA((N_DEV - 1,)),
                        pltpu.SemaphoreType.DMA])(parts)


def adamw_sum(name, parts, w, m, v):
    r = w.shape[0]
    tr = _pick(r, 512) if r % LANE == 0 else r
    c1 = 1.0 - ADAM_B1 ** ADAM_STEP
    c2 = 1.0 - ADAM_B2 ** ADAM_STEP

    def body(p_ref, w_ref, m_ref, v_ref, g_out, d_out, m_out, v_out):
        g = p_ref[0]
        for d in range(1, N_DEV):
            g = g + p_ref[d]
        mn = ADAM_B1 * m_ref[...] + (1.0 - ADAM_B1) * g
        vn = ADAM_B2 * v_ref[...] + (1.0 - ADAM_B2) * (g * g)
        g_out[...] = g
        m_out[...] = mn
        v_out[...] = vn
        d_out[...] = -ADAM_LR * ((mn / c1) / (jnp.sqrt(vn / c2) + ADAM_EPS) + ADAM_WD * w_ref[...])

    row = pl.BlockSpec((tr, LANE), lambda i: (i, 0))
    return pl.pallas_call(
        body, name=name, grid=(r // tr,), in_specs=[pl.BlockSpec((N_DEV, tr, LANE), lambda i: (0, i, 0)), row, row, row],
        out_specs=[row] * 4, out_shape=[jax.ShapeDtypeStruct((r, LANE), F32)] * 4,
        compiler_params=_params(("parallel",)))(parts, w, m, v)


def _pack(arrays, dtype, row_mult):
    flat = jnp.concatenate([a.astype(dtype).reshape(-1) for a in arrays])
    n = flat.shape[0]
    quantum = row_mult * LANE
    total = -(-n // quantum) * quantum
    return jnp.pad(flat, (0, total - n)).reshape(total // LANE, LANE)


def _unpack(packed, shapes):
    flat = packed.reshape(-1)
    out, pos = [], 0
    for shp in shapes:
        n = int(np.prod(shp))
        out.append(flat[pos:pos + n].reshape(shp))
        pos += n
    return out


def _unshard(gathered, axis):
    return jnp.concatenate([gathered[d] for d in range(N_DEV)], axis=axis)


def _to_shards(full, axis):
    return jnp.stack(jnp.split(full, N_DEV, axis=axis), axis=0)


A_COLS = len(A_GROUPS) * 3 * HW
KV_COLS = B_KV_HEADS * HEAD_DIM
KV_REP = N_HEADS // B_KV_HEADS
O_BQ = A_COLS
O_BK = O_BQ + HW
O_BV = O_BK + KV_COLS
O_CQ = O_BV + KV_COLS
O_CKV = O_CQ + C_Q_RANK
O_KR = O_CKV + C_KV_RANK
O_GATE = O_KR + C_ROPE


def _proj_weight(w_in):
    d = w_in.shape[0]

    def per_q_head(lo):
        t = w_in[:, lo:lo + KV_COLS].reshape(d, B_KV_HEADS, 1, HEAD_DIM)
        return jnp.broadcast_to(t, (d, B_KV_HEADS, KV_REP, HEAD_DIM)).reshape(d, HW)

    kr = jnp.pad(w_in[:, O_KR:O_GATE], ((0, 0), (C_NOPE, LANE - C_NOPE - C_ROPE)))
    return jnp.concatenate([w_in[:, O_GATE:], w_in[:, :O_BK], per_q_head(O_BK), per_q_head(O_BV),
                            w_in[:, O_CQ:O_KR], kr], axis=1)


def _unproj_grad(dw_p, d_model):
    d = dw_p.shape[0]
    g = 3 * d_model
    p_bk = g + O_BK
    p_bv = p_bk + HW
    p_cq = p_bv + HW
    p_kr = p_cq + C_Q_RANK + C_KV_RANK

    def sum_heads(lo):
        return dw_p[:, lo:lo + HW].reshape(d, B_KV_HEADS, KV_REP, HEAD_DIM).sum(axis=2).reshape(d, KV_COLS)

    return jnp.concatenate([dw_p[:, g:p_bk], sum_heads(p_bk), sum_heads(p_bv), dw_p[:, p_cq:p_kr],
                            dw_p[:, p_kr + C_NOPE:p_kr + C_NOPE + C_ROPE], dw_p[:, :g]], axis=1)


def _pad_heads(w, per_head, lo, hi):
    r = w.shape[0]
    t = w.reshape(r, N_HEADS, per_head)[:, :, lo:hi]
    return jnp.pad(t, ((0, 0), (0, 0), (0, LANE - (hi - lo)))).reshape(r, N_HEADS * LANE)


def _unpad_heads(w, width):
    r = w.shape[0]
    return w.reshape(r, N_HEADS, LANE)[:, :, :width]


def _t5_bucket(dist):
    n = jnp.maximum(dist, 0)
    max_exact = REL_BUCKETS // 2
    scaled = jnp.log(jnp.maximum(n, 1).astype(F32) / max_exact) / math.log(REL_MAX_DIST / max_exact)
    large = max_exact + (scaled * (REL_BUCKETS - max_exact)).astype(jnp.int32)
    return jnp.where(n < max_exact, n, jnp.minimum(large, REL_BUCKETS - 1))


def _buckets(dil):
    qi = jnp.arange(BLOCK)[:, None]
    ki = jnp.arange(2 * BLOCK)[None, :]
    return _t5_bucket((BLOCK + qi - ki) * dil).astype(jnp.int32)


def _rope_tables(s):
    pos = jnp.arange(s, dtype=F32)
    inv_freq = ROPE_BASE ** (-jnp.arange(0, C_ROPE, 2, dtype=F32) / C_ROPE)
    ang = pos[:, None] * inv_freq[None, :]
    cos, sin = jnp.cos(ang), jnp.sin(ang)
    ones = jnp.ones((s, C_NOPE), F32)
    tail = LANE - C_NOPE - C_ROPE
    c = jnp.concatenate([ones, cos, cos, jnp.ones((s, tail), F32)], axis=1)
    sn = jnp.concatenate([jnp.zeros((s, C_NOPE), F32), sin, sin, jnp.zeros((s, tail), F32)], axis=1)
    return c, sn


def _cols(d_model):
    g = 3 * d_model // HW
    a = [(g + 3 * i, g + 3 * i + 1, g + 3 * i + 2) for i in range(len(A_GROUPS))]
    b = (g + 9, g + 10, g + 11)
    cq_lane = (g + 12) * HW
    return a, b, cq_lane


def _layer_weights(full, l, d_model):
    w_in_p = _proj_weight(full['w_in'][l])
    per = C_NOPE + C_ROPE
    w_uq = full['w_uq'][l]
    w_ukv = full['w_ukv'][l]
    wb = full['w_branch'][l]
    wb2 = jnp.pad(wb[2].reshape(N_HEADS, HEAD_DIM, d_model), ((0, 0), (0, LANE - HEAD_DIM), (0, 0)))
    return dict(
        w_in_p=w_in_p, wq=_pad_heads(w_uq, per, 0, per), wk=_pad_heads(w_ukv, 2 * C_NOPE, 0, C_NOPE),
        wv=_pad_heads(w_ukv, 2 * C_NOPE, C_NOPE, 2 * C_NOPE), wb0=wb[0], wb1=wb[1],
        wb2=wb2.reshape(N_HEADS * LANE, d_model), w_out=full['w_out'][l], w_up=full['w_ffn_up'][l],
        w_down=full['w_ffn_down'][l], conv_w=full['conv_w'][l])


def _layer_fwd(l, x, lw, small, tabs):
    s, d = x.shape
    a_cols, b_cols, cq_lane = _cols(d)
    tag = f"l{l}_"
    proj = matmul(tag + "proj", x, lw['w_in_p'], 'nn')
    res = dict(x=x, proj=proj)
    outs, lses, slabs = [], [], []
    for gi, (window, dil) in enumerate(A_GROUPS):
        if dil == 1:
            slab, cols = proj, a_cols[gi]
        else:
            slab, cols = to_classes(proj[:, a_cols[gi][0] * HW:(a_cols[gi][2] + 1) * HW], dil), (0, 1, 2)
        o, lg = band_fwd(tag + f"a{gi}_fwd", slab, dil, cols, tabs['bias'][gi], None, True)
        outs.append(from_classes(o, dil))
        lses.append(from_classes(lg, dil))
        slabs.append((slab, cols))
    res['a_o'], res['a_l'], res['a_slabs'] = outs, lses, slabs
    (o_a,) = rows_fwd(tag + "a_combine", combine_fn, [(o, HW, 0) for o in outs] + [(lg, HW, 0) for lg in lses], [],
                      [(HW, BF16)])
    sink = small['sinks'][l].reshape(N_HEADS, 1, 1)
    o_b, _ = band_fwd(tag + "b_fwd", proj, 1, b_cols, tabs['bias'][3], sink, False)
    gq = small['q_norm_g'][l].reshape(1, -1)
    gkv = small['kv_norm_g'][l].reshape(1, -1)
    (qc,) = rows_fwd(tag + "c_q", mla_q_fn, [(proj, C_Q_RANK, cq_lane // C_Q_RANK), (tabs['cos'], LANE, 0),
                                            (tabs['sin'], LANE, 0)], [gq, lw['wq']], [(N_HEADS * LANE, BF16)])
    ckv_blk = (cq_lane + C_Q_RANK) // LANE
    kc, vc = rows_fwd(tag + "c_kv", mla_kv_fn, [(proj, LANE, ckv_blk), (proj, LANE, ckv_blk + 1), (tabs['cos'], LANE, 0),
                                                (tabs['sin'], LANE, 0)], [gkv, lw['wk'], lw['wv']],
                      [(N_HEADS * LANE, BF16)] * 2)
    o_c, lse_c = mla_fwd(qc, kc, vc)
    res.update(o_a=o_a, o_b=o_b, o_c=o_c, lse_c=lse_c, qc=qc, kc=kc, vc=vc)
    bg = small['b_gate'][l].reshape(3, 1, d)
    ys = [matmul(tag + f"branch{i}", o, lw[f'wb{i}'], 'nn') for i, o in enumerate((o_a, o_b, o_c))]
    (merged,) = rows_fwd(tag + "merge", merge_fn, [(y, d, 0) for y in ys] + [(proj, d, i) for i in range(3)],
                         [bg[0], bg[1], bg[2]], [(d, BF16)])
    mix = matmul(tag + "out_proj", merged, lw['w_out'], 'nn')
    ln1 = [small['ln1_g'][l].reshape(1, d), small['ln1_b'][l].reshape(1, d)]
    (x1,) = rows_fwd(tag + "ln1", ln_fn, [(x, d, 0), (mix, d, 0)], ln1, [(d, F32)])
    u = matmul(tag + "ffn_up", x1, lw['w_up'], 'nn')
    h = conv_glu_fwd(u, lw['conv_w'], small['conv_b'][l].reshape(1, -1))
    ff = matmul(tag + "ffn_down", h, lw['w_down'], 'nn')
    ln2 = [small['ln2_g'][l].reshape(1, d), small['ln2_b'][l].reshape(1, d)]
    (x2,) = rows_fwd(tag + "ln2", ln_fn, [(x1, d, 0), (ff, d, 0)], ln2, [(d, F32)])
    res.update(merged=merged, mix=mix, x1=x1, u=u, h=h, ff=ff, ys=ys)
    return x2, res


def _layer_bwd(l, dy, res, lw, small, tabs):
    x, proj = res['x'], res['proj']
    s, d = x.shape
    a_cols, b_cols, cq_lane = _cols(d)
    tag = f"l{l}_"
    g = {}
    ln2 = [small['ln2_g'][l].reshape(1, d), small['ln2_b'][l].reshape(1, d)]
    dff, g['ln2_g'], g['ln2_b'] = rows_bwd(tag + "ln2_bwd", ln_fn, [(res['x1'], d, 0), (res['ff'], d, 0)], ln2,
                                           [(dy, d)], [(1, F32)])
    g['w_ffn_down'] = matmul(tag + "dw_down", res['h'], dff, 'tn')
    dh = matmul(tag + "dh", dff, lw['w_down'], 'nt')
    conv_b = small['conv_b'][l].reshape(1, -1)
    dug, duv, dwg, dwv, dbg, dbv = conv_glu_bwd(res['u'], dh, lw['conv_w'], conv_b)
    du = jnp.concatenate([dug, duv], axis=1)
    g['conv_w'] = jnp.concatenate([dwg, dwv], axis=1)
    g['conv_b'] = jnp.concatenate([dbg, dbv], axis=1).reshape(-1)
    g['w_ffn_up'] = matmul(tag + "dw_up", res['x1'], du, 'tn')
    dx1 = matmul(tag + "dx1", du, lw['w_up'], 'nt', add=dff, scale=ALPHA)
    ln1 = [small['ln1_g'][l].reshape(1, d), small['ln1_b'][l].reshape(1, d)]
    dmix, g['ln1_g'], g['ln1_b'] = rows_bwd(tag + "ln1_bwd", ln_fn, [(x, d, 0), (res['mix'], d, 0)], ln1,
                                            [(dx1, d)], [(1, F32)])
    g['w_out'] = matmul(tag + "dw_out", res['merged'], dmix, 'tn')
    dmerged = matmul(tag + "dmerged", dmix, lw['w_out'], 'nt')
    bg = small['b_gate'][l].reshape(3, 1, d)
    mrows = [(y, d, 0) for y in res['ys']] + [(proj, d, i) for i in range(3)]
    (dy0, dy1, dy2, dg0, dg1, dg2, db0, db1, db2) = rows_bwd(
        tag + "merge_bwd", merge_fn, mrows, [bg[0], bg[1], bg[2]], [(dmerged, d)], [(i, BF16) for i in range(6)])
    g['b_gate'] = jnp.concatenate([db0, db1, db2], axis=1).reshape(-1)
    branch_in = (res['o_a'], res['o_b'], res['o_c'])
    dwb = [matmul(tag + f"dw_branch{i}", o, dyi, 'tn') for i, (o, dyi) in enumerate(zip(branch_in, (dy0, dy1, dy2)))]
    dwb[2] = dwb[2].reshape(N_HEADS, LANE, d)[:, :HEAD_DIM].reshape(HW, d)
    g['w_branch'] = jnp.stack(dwb, axis=0)
    do_a, do_b, do_c = [matmul(tag + f"do_branch{i}", dyi, lw[f'wb{i}'], 'nt')
                        for i, dyi in enumerate((dy0, dy1, dy2))]
    cw = N_HEADS * LANE
    delta, do_c16 = rows_fwd(tag + "c_delta", mla_delta_fn, [(do_c, cw, 0), (res['o_c'], cw, 0)], [],
                             [(cw, F32), (cw, BF16)])
    dqc, dkc, dvc = mla_bwd(res['qc'], res['kc'], res['vc'], do_c16, res['lse_c'], delta)
    gq = small['q_norm_g'][l].reshape(1, -1)
    gkv = small['kv_norm_g'][l].reshape(1, -1)
    dcq, dgq, dwq = rows_bwd(tag + "c_q_bwd", mla_q_fn,
                             [(proj, C_Q_RANK, cq_lane // C_Q_RANK), (tabs['cos'], LANE, 0), (tabs['sin'], LANE, 0)],
                             [gq, lw['wq']], [(dqc, N_HEADS * LANE)], [(0, BF16)])
    ckv_blk = (cq_lane + C_Q_RANK) // LANE
    dckv, dkr, dgkv, dwk, dwv = rows_bwd(
        tag + "c_kv_bwd", mla_kv_fn,
        [(proj, LANE, ckv_blk), (proj, LANE, ckv_blk + 1), (tabs['cos'], LANE, 0), (tabs['sin'], LANE, 0)],
        [gkv, lw['wk'], lw['wv']], [(dkc, N_HEADS * LANE), (dvc, N_HEADS * LANE)], [(0, BF16), (1, BF16)])
    g['q_norm_g'], g['kv_norm_g'] = dgq.reshape(-1), dgkv.reshape(-1)
    per = C_NOPE + C_ROPE
    g['w_uq'] = _unpad_heads(dwq, per).reshape(C_Q_RANK, N_HEADS * per)
    g['w_ukv'] = jnp.concatenate([_unpad_heads(dwk, C_NOPE), _unpad_heads(dwv, C_NOPE)], axis=2).reshape(
        C_KV_RANK, N_HEADS * 2 * C_NOPE)
    sink = small['sinks'][l].reshape(N_HEADS, 1, 1)
    (dq, dkp, dko, dvp, dvo), dbias_b, dsink = band_bwd(tag + "b_bwd", proj, 1, b_cols, tabs['bias'][3], sink, False,
                                                        do_b, None)
    g['sinks'] = dsink.reshape(-1)
    pieces_b = [dq.astype(BF16), shift_add(tag + "b_dk", dko, dkp, 1), shift_add(tag + "b_dv", dvo, dvp, 1)]
    dbias = [None] * 4
    dbias[3] = dbias_b
    combo = [(o, HW, 0) for o in res['a_o']] + [(lg, HW, 0) for lg in res['a_l']]
    a_cts = rows_bwd(tag + "a_combine_bwd", combine_fn, combo, [], [(do_a, HW)], [(i, F32) for i in range(6)])
    pieces_a = []
    for gi, (window, dil) in enumerate(A_GROUPS):
        slab, cols = res['a_slabs'][gi]
        (dq, dkp, dko, dvp, dvo), dbias[gi], _ = band_bwd(
            tag + f"a{gi}_bwd", slab, dil, cols, tabs['bias'][gi], None, True, to_classes(a_cts[gi], dil),
            to_classes(a_cts[3 + gi], dil))
        group = [dq.astype(BF16), shift_add(tag + f"a{gi}_dk", dko, dkp, dil),
                 shift_add(tag + f"a{gi}_dv", dvo, dvp, dil)]
        pieces_a += [from_classes(t, dil) for t in group]
    g['dbias'] = dbias
    dproj = jnp.concatenate([dg0, dg1, dg2] + pieces_a + pieces_b + [dcq, dckv, dkr], axis=1)
    dw_in_p = matmul(tag + "dw_in", x, dproj, 'tn')
    g['w_in'] = _unproj_grad(dw_in_p, d)
    dx = matmul(tag + "dx", dproj, lw['w_in_p'], 'nt', add=dmix, scale=ALPHA)
    return dx, g


def kernel(x, rel_table, w_in, b_gate, sinks, q_norm_g, kv_norm_g, w_uq, w_ukv, w_branch, w_out, ln1_g, ln1_b, w_ffn_up, conv_w, conv_b, w_ffn_down, ln2_g, ln2_b, loss_target, m_rel_table, m_w_in, m_b_gate, m_sinks, m_q_norm_g, m_kv_norm_g, m_w_uq, m_w_ukv, m_w_branch, m_w_out, m_ln1_g, m_ln1_b, m_w_ffn_up, m_conv_w, m_conv_b, m_w_ffn_down, m_ln2_g, m_ln2_b, v_rel_table, v_w_in, v_b_gate, v_sinks, v_q_norm_g, v_kv_norm_g, v_w_uq, v_w_ukv, v_w_branch, v_w_out, v_ln1_g, v_ln1_b, v_w_ffn_up, v_conv_w, v_conv_b, v_w_ffn_down, v_ln2_g, v_ln2_b):
    args = locals()
    w = {n: args[n] for n in WEIGHTS}
    mom = {n: args["m_" + n] for n in WEIGHTS}
    var = {n: args["v_" + n] for n in WEIGHTS}
    s, d = x.shape[1], x.shape[2]
    xs = x.reshape(s, d)
    target = loss_target.reshape(s, d)

    big = [n for n in WEIGHTS if n in SHARDED]
    small_names = [n for n in WEIGHTS if n not in SHARDED]
    gathered = all_gather("gather_weights", _pack([w[n] for n in GATHER_BF16], BF16, 16))
    shard_shapes = [w[n].shape for n in GATHER_BF16]
    parts = [_unpack(gathered[dv], shard_shapes) for dv in range(N_DEV)]
    full = {n: jnp.concatenate([parts[dv][i] for dv in range(N_DEV)], axis=SHARDED[n])
            for i, n in enumerate(GATHER_BF16)}
    conv_all = all_gather("gather_conv_w", _pack([w['conv_w']], F32, 8))
    full['conv_w'] = jnp.concatenate([_unpack(conv_all[dv], [w['conv_w'].shape])[0] for dv in range(N_DEV)], axis=2)
    small = {n: w[n] for n in small_names}
    loss_part, grad_x, g_full, g_small = _local_grads(xs, target, full, small)
    grad_x = grad_x.reshape(x.shape)

    per_dev = jnp.concatenate([_to_shards(g_full[n], SHARDED[n]).reshape(N_DEV, -1) for n in big], axis=1)
    n_el = per_dev.shape[1]
    quantum = 512 * LANE
    tot = -(-n_el // quantum) * quantum
    per_dev = jnp.pad(per_dev, ((0, 0), (0, tot - n_el))).reshape(N_DEV, tot // LANE, LANE)
    recv = all_to_all("exchange_grads", per_dev)
    pk = lambda tree: _pack([tree[n] for n in big], F32, 512)
    big_out = adamw_sum("adamw_sharded", recv, pk(w), pk(mom), pk(var))
    big_shapes = [w[n].shape for n in big]
    big_res = [dict(zip(big, _unpack(o, big_shapes))) for o in big_out]

    small_pack = _pack([g_small[n].reshape(w[n].shape) for n in small_names] + [loss_part[0, 0:1]], F32, 8)
    small_all = all_gather("gather_small_grads", small_pack)
    pks = lambda tree: _pack([tree[n] for n in small_names] + [jnp.zeros((1,), F32)], F32, 8)
    small_out = adamw_sum("adamw_replicated", small_all, pks(w), pks(mom), pks(var))
    small_shapes = [w[n].shape for n in small_names] + [(1,)]
    small_res = [dict(zip(small_names + ['loss'], _unpack(o, small_shapes))) for o in small_out]

    loss = small_res[0]['loss'].reshape(())
    out = [loss, grad_x]
    for k in range(4):
        out += [big_res[k][n] if n in SHARDED else small_res[k][n] for n in WEIGHTS]
    return tuple(out)


def _local_grads(xs, target, full, small):
    s, d = xs.shape
    rel_table = small['rel_table']
    big = [n for n in WEIGHTS if n in SHARDED]
    small_names = [n for n in WEIGHTS if n not in SHARDED]

    cos, sin = _rope_tables(s)
    buckets = [_buckets(dil) for _, dil in A_GROUPS] + [_buckets(1)]
    def lookup(table, bucket):
        out = jnp.zeros((N_HEADS,) + bucket.shape, F32)
        for b in range(REL_BUCKETS):
            out = jnp.where((bucket == b)[None], table[b][:, None, None], out)
        return out

    bias = [lookup(rel_table[:, gi * N_HEADS:(gi + 1) * N_HEADS], buckets[gi]) for gi in range(4)]
    tabs = dict(cos=cos, sin=sin, bias=bias)

    act, saved, lws = xs, [], []
    for l in range(DEPTH):
        lw = _layer_weights(full, l, d)
        act, res = _layer_fwd(l, act, lw, small, tabs)
        saved.append(res)
        lws.append(lw)
    dy, loss_part = loss_head(act, target)

    grads = [None] * DEPTH
    for l in reversed(range(DEPTH)):
        dy, grads[l] = _layer_bwd(l, dy, saved[l], lws[l], small, tabs)
    grad_x = dy

    rel_cols = []
    for gi in range(4):
        both = grads[0]['dbias'][gi] + grads[1]['dbias'][gi] if DEPTH == 2 else grads[0]['dbias'][gi]
        rel_cols.append(rel_grad(f"rel_grad{gi}", both, buckets[gi])[:, :REL_BUCKETS].T)
    g_rel = jnp.concatenate(rel_cols, axis=1)

    def stacked(n):
        return jnp.stack([grads[l][n] for l in range(DEPTH)], axis=0)

    g_full = {n: stacked(n) for n in big}
    g_small = {'rel_table': g_rel}
    for n in small_names:
        if n != 'rel_table':
            g_small[n] = stacked(n)
    return loss_part, grad_x, g_full, g_small
```

```python
import functools
import math

import numpy as np
import jax
import jax.numpy as jnp
from jax import lax
from jax.experimental import pallas as pl
from jax.experimental.pallas import tpu as pltpu

F32 = jnp.float32
BF16 = jnp.bfloat16

N_DEV = 8
DEPTH = 2
HEAD_DIM = 64
BLOCK = 128
A_GROUPS = ((128, 1), (512, 4), (2048, 16))
N_HEADS = 8
HW = N_HEADS * HEAD_DIM
B_KV_HEADS = 2
C_Q_RANK = 256
C_KV_RANK = 128
C_NOPE = 64
C_ROPE = 32
ROPE_BASE = 10000.0
REL_BUCKETS = 32
REL_MAX_DIST = 2048
ALPHA = (2 * DEPTH) ** 0.25
LN_EPS = 1e-5
RMS_EPS = 1e-6
NEG = -1e30
ADAM_LR = 0.001
ADAM_B1 = 0.9
ADAM_B2 = 0.999
ADAM_EPS = 1e-08
ADAM_WD = 0.01
ADAM_STEP = 10

LANE = 128
TM = 256
VMEM_LIMIT = 56 * 1024 * 1024
MESH_ID = pl.DeviceIdType.MESH

WEIGHTS = ('rel_table', 'w_in', 'b_gate', 'sinks', 'q_norm_g', 'kv_norm_g', 'w_uq', 'w_ukv', 'w_branch',
           'w_out', 'ln1_g', 'ln1_b', 'w_ffn_up', 'conv_w', 'conv_b', 'w_ffn_down', 'ln2_g', 'ln2_b')
SHARDED = {'w_in': 2, 'w_uq': 2, 'w_ukv': 2, 'w_branch': 3, 'w_out': 1, 'w_ffn_up': 2, 'conv_w': 2,
           'w_ffn_down': 1}
GATHER_BF16 = ('w_in', 'w_uq', 'w_ukv', 'w_branch', 'w_out', 'w_ffn_up', 'w_ffn_down')


def _params(sem=None):
    return pltpu.CompilerParams(dimension_semantics=sem, vmem_limit_bytes=VMEM_LIMIT)


def _pick(n, target):
    if n <= target:
        return n
    best = None
    for t in range(LANE, target + 1, LANE):
        if n % t == 0:
            best = t
    assert best is not None, (n, target)
    return best


def _dot(a, b, ca, cb):
    return lax.dot_general(a.astype(BF16), b.astype(BF16), (((ca,), (cb,)), ((), ())),
                           preferred_element_type=F32)


@jax.custom_vjp
def mm(a, b):
    return _dot(a, b, 1, 0)


def _mm_fwd(a, b):
    return _dot(a, b, 1, 0), (a, b)


def _mm_bwd(res, g):
    a, b = res
    return _dot(g, b, 1, 1), _dot(a, g, 0, 0)


mm.defvjp(_mm_fwd, _mm_bwd)


@jax.custom_vjp
def mm_nt(a, b):
    return _dot(a, b, 1, 1)


def _mm_nt_fwd(a, b):
    return _dot(a, b, 1, 1), (a, b)


def _mm_nt_bwd(res, g):
    a, b = res
    return _dot(g, b, 1, 0), _dot(g, a, 0, 0)


mm_nt.defvjp(_mm_nt_fwd, _mm_nt_bwd)


def _split_impl(x, n):
    w = x.shape[-1] // n
    return tuple(x[:, i * w:(i + 1) * w] for i in range(n))


@functools.partial(jax.custom_vjp, nondiff_argnums=(1,))
def split_lanes(x, n):
    return _split_impl(x, n)


def _split_fwd(x, n):
    return _split_impl(x, n), None


def _split_bwd(n, _, gs):
    return (jnp.concatenate(gs, axis=-1),)


split_lanes.defvjp(_split_fwd, _split_bwd)


@jax.custom_vjp
def concat_lanes(xs):
    return jnp.concatenate(xs, axis=-1)


def _concat_fwd(xs):
    return jnp.concatenate(xs, axis=-1), len(xs)


def _concat_bwd(n, g):
    return (_split_impl(g, n),)


concat_lanes.defvjp(_concat_fwd, _concat_bwd)


@jax.custom_vjp
def concat_rows(a, b):
    return jnp.concatenate([a, b], axis=0)


def _crow_fwd(a, b):
    return jnp.concatenate([a, b], axis=0), a.shape[0]


def _crow_bwd(na, g):
    return g[:na], g[na:]


concat_rows.defvjp(_crow_fwd, _crow_bwd)


def _rot_half(x):
    n = x.shape[-1]
    lane = lax.broadcasted_iota(jnp.int32, (1, n), 1) % LANE
    lo = (lane >= C_NOPE) & (lane < C_NOPE + C_ROPE // 2)
    hi = (lane >= C_NOPE + C_ROPE // 2) & (lane < C_NOPE + C_ROPE)
    up = pltpu.roll(x, n - C_ROPE // 2, 1)
    dn = pltpu.roll(x, C_ROPE // 2, 1)
    return jnp.where(lo, -up, jnp.where(hi, dn, 0.0))


@jax.custom_vjp
def rope(x, c, s):
    return x * c + _rot_half(x) * s


def _rope_fwd(x, c, s):
    return x * c + _rot_half(x) * s, (c, s)


def _rope_bwd(res, g):
    c, s = res
    return g * c - _rot_half(g * s), jnp.zeros_like(c), jnp.zeros_like(s)


rope.defvjp(_rope_fwd, _rope_bwd)


def _sigmoid(x):
    return 0.5 * jnp.tanh(0.5 * x) + 0.5


def matmul(name, a, b, mode, out_dtype=F32, add=None, scale=1.0):
    if mode == 'nn':
        (m, k), n = a.shape, b.shape[1]
    elif mode == 'nt':
        (m, k), n = a.shape, b.shape[0]
    else:
        (k, m), n = a.shape, b.shape[1]
    tm, tn = _pick(m, 1024), _pick(n, 1024)
    tk = _pick(k, 1536)
    nk = k // tk
    a_spec = pl.BlockSpec((tk, tm), lambda i, j, kk: (kk, i)) if mode == 'tn' else \
        pl.BlockSpec((tm, tk), lambda i, j, kk: (i, kk))
    b_spec = pl.BlockSpec((tn, tk), lambda i, j, kk: (j, kk)) if mode == 'nt' else \
        pl.BlockSpec((tk, tn), lambda i, j, kk: (kk, j))
    o_spec = pl.BlockSpec((tm, tn), lambda i, j, kk: (i, j))
    ca = 0 if mode == 'tn' else 1
    cb = 1 if mode == 'nt' else 0
    has_add = add is not None

    def body(*refs):
        a_ref, b_ref = refs[:2]
        add_ref = refs[2] if has_add else None
        o_ref = refs[2 + int(has_add)]

        def finish(r):
            if has_add:
                r = r + scale * add_ref[...]
            o_ref[...] = r.astype(out_dtype)

        if nk == 1:
            finish(_dot(a_ref[...], b_ref[...], ca, cb))
            return
        acc_ref = refs[-1]
        kk = pl.program_id(2)

        @pl.when(kk == 0)
        def _():
            acc_ref[...] = jnp.zeros_like(acc_ref)

        acc_ref[...] += _dot(a_ref[...], b_ref[...], ca, cb)

        @pl.when(kk == nk - 1)
        def _():
            finish(acc_ref[...])

    ins = [a, b] + ([add] if has_add else [])
    specs = [a_spec, b_spec] + ([o_spec] if has_add else [])
    return pl.pallas_call(
        body, name=name, grid=(m // tm, n // tn, nk), in_specs=specs, out_specs=o_spec,
        out_shape=jax.ShapeDtypeStruct((m, n), out_dtype),
        scratch_shapes=[pltpu.VMEM((tm, tn), F32)] if nk > 1 else [],
        compiler_params=_params(("parallel", "parallel", "arbitrary")))(*ins)


def _row_spec(width, col_block=0, tm=TM):
    return pl.BlockSpec((tm, width), lambda i: (i, col_block))


def _full_spec(shape):
    nd = len(shape)
    return pl.BlockSpec(shape, lambda i: (0,) * nd)


def rows_fwd(name, fn, rows, params, outs, tm=TM):
    s = rows[0][0].shape[0]
    nr, npar = len(rows), len(params)

    def body(*refs):
        r = [refs[i][...].astype(F32) for i in range(nr)]
        p = [refs[nr + i][...] for i in range(npar)]
        res = fn(*r, *p)
        for o_ref, val in zip(refs[nr + npar:], res):
            o_ref[...] = val.astype(o_ref.dtype)

    return pl.pallas_call(
        body, name=name, grid=(s // tm,),
        in_specs=[_row_spec(w, cb, tm) for _, w, cb in rows] + [_full_spec(p.shape) for p in params],
        out_specs=[_row_spec(w, 0, tm) for w, _ in outs],
        out_shape=[jax.ShapeDtypeStruct((s, w), dt) for w, dt in outs],
        compiler_params=_params(("parallel",)))(*[a for a, _, _ in rows], *params)


def rows_bwd(name, fn, rows, params, cts, row_grads, tm=TM):
    s = rows[0][0].shape[0]
    nr, npar, nct, nrg = len(rows), len(params), len(cts), len(row_grads)

    def body(*refs):
        r = [refs[i][...].astype(F32) for i in range(nr)]
        p = [refs[nr + i][...].astype(F32) for i in range(npar)]
        g = tuple(refs[nr + npar + i][...].astype(F32) for i in range(nct))
        _, vjp = jax.vjp(lambda *a: tuple(fn(*a)), *r, *p)
        grads = vjp(g)
        outs = refs[nr + npar + nct:]
        for k, (idx, _) in enumerate(row_grads):
            outs[k][...] = grads[idx].astype(outs[k].dtype)

        @pl.when(pl.program_id(0) == 0)
        def _():
            for k in range(npar):
                outs[nrg + k][...] = jnp.zeros_like(outs[nrg + k])

        for k in range(npar):
            outs[nrg + k][...] += grads[nr + k]

    return pl.pallas_call(
        body, name=name, grid=(s // tm,),
        in_specs=[_row_spec(w, cb, tm) for _, w, cb in rows] + [_full_spec(p.shape) for p in params]
        + [_row_spec(w, 0, tm) for _, w in cts],
        out_specs=[_row_spec(rows[idx][1], 0, tm) for idx, _ in row_grads] + [_full_spec(p.shape) for p in params],
        out_shape=[jax.ShapeDtypeStruct((s, rows[idx][1]), dt) for idx, dt in row_grads]
        + [jax.ShapeDtypeStruct(p.shape, F32) for p in params],
        compiler_params=_params(("arbitrary",)))(*[a for a, _, _ in rows], *params, *[a for a, _ in cts])


def ln_fn(x, r, g, b):
    z = ALPHA * x + r
    mu = jnp.mean(z, axis=-1, keepdims=True)
    zc = z - mu
    var = jnp.mean(zc * zc, axis=-1, keepdims=True)
    return (zc * lax.rsqrt(var + LN_EPS) * g + b,)


def _rms(x, g):
    return x * lax.rsqrt(jnp.mean(x * x, axis=-1, keepdims=True) + RMS_EPS) * g


def mla_q_fn(cq, c, s, g, wq):
    q = mm(_rms(cq, g), wq)
    return (rope(q, jnp.concatenate([c] * N_HEADS, axis=-1), jnp.concatenate([s] * N_HEADS, axis=-1)) * MLA_SCALE,)


@jax.custom_vjp
def tile_heads(x):
    return jnp.concatenate([x] * N_HEADS, axis=-1)


def _tile_fwd(x):
    return jnp.concatenate([x] * N_HEADS, axis=-1), None


def _tile_bwd(_, g):
    parts = _split_impl(g, N_HEADS)
    acc = parts[0]
    for p in parts[1:]:
        acc = acc + p
    return (acc,)


tile_heads.defvjp(_tile_fwd, _tile_bwd)


def mla_kv_fn(ckv, kr, c, s, g, wk, wv):
    n = _rms(ckv, g)
    return mm(n, wk) + tile_heads(rope(kr, c, s)), mm(n, wv)


def merge_fn(y0, y1, y2, g0, g1, g2, b0, b1, b2):
    return (_sigmoid(g0 + b0) * y0 + _sigmoid(g1 + b1) * y1 + _sigmoid(g2 + b2) * y2,)


def combine_fn(o1, o2, o3, l1, l2, l3):
    mx = lax.stop_gradient(jnp.maximum(jnp.maximum(l1, l2), l3))
    e1, e2, e3 = jnp.exp(l1 - mx), jnp.exp(l2 - mx), jnp.exp(l3 - mx)
    return ((e1 * o1 + e2 * o2 + e3 * o3) / (e1 + e2 + e3),)


def loss_head(y, t):
    s, d = y.shape

    def body(y_ref, t_ref, dy_ref, loss_ref):
        err = y_ref[...] - t_ref[...]
        dy_ref[...] = err * (1.0 / d)

        @pl.when(pl.program_id(0) == 0)
        def _():
            loss_ref[...] = jnp.zeros_like(loss_ref)

        loss_ref[...] += jnp.sum(err * err) * (0.5 / d)

    return pl.pallas_call(
        body, name="loss_head", grid=(s // TM,), in_specs=[_row_spec(d), _row_spec(d)],
        out_specs=[_row_spec(d), _full_spec((8, LANE))],
        out_shape=[jax.ShapeDtypeStruct((s, d), F32), jax.ShapeDtypeStruct((8, LANE), F32)],
        compiler_params=_params(("arbitrary",)))(y, t)


CONV_TN = 256


def _conv_taps(u_ext, w_ref, b_ref):
    n = u_ext.shape[0]
    return (w_ref[0:1, :] * pltpu.roll(u_ext, 2, 0) + w_ref[1:2, :] * pltpu.roll(u_ext, 1, 0)
            + w_ref[2:3, :] * u_ext + b_ref[...])


def conv_glu_fwd(u, conv_w, conv_b):
    s, f2 = u.shape
    f = f2 // 2
    tn, tm = CONV_TN, TM
    off = f // tn
    hb = tm // 8

    def body(ug, uv, hg, hv, wg, wv, bg, bv, o_ref):
        first = pl.program_id(0) == 0

        def conv(u_ref, h_ref, w_ref, b_ref):
            halo = jnp.where(first, 0.0, h_ref[...])
            ext = jnp.concatenate([halo, u_ref[...]], axis=0)
            return _conv_taps(ext, w_ref, b_ref)[8:]

        cg, cv = conv(ug, hg, wg, bg), conv(uv, hv, wv, bv)
        o_ref[...] = (cg * _sigmoid(cg) * cv).astype(o_ref.dtype)

    blk = lambda o: pl.BlockSpec((tm, tn), lambda i, j: (i, j + o))
    halo = lambda o: pl.BlockSpec((8, tn), lambda i, j: (jnp.maximum(i * hb - 1, 0), j + o))
    par = lambda r, o: pl.BlockSpec((r, tn), lambda i, j: (0, j + o))
    return pl.pallas_call(
        body, name="conv_glu_fwd", grid=(s // tm, off),
        in_specs=[blk(0), blk(off), halo(0), halo(off), par(3, 0), par(3, off), par(1, 0), par(1, off)],
        out_specs=pl.BlockSpec((tm, tn), lambda i, j: (i, j)),
        out_shape=jax.ShapeDtypeStruct((s, f), BF16),
        compiler_params=_params(("parallel", "parallel")))(u, u, u, u, conv_w, conv_w, conv_b, conv_b)


def conv_glu_bwd(u, dh, conv_w, conv_b):
    s, f2 = u.shape
    f = f2 // 2
    tn, tm = CONV_TN, TM
    off = f // tn
    hb = tm // 8
    n_rows = s // tm

    def body(ug, uv, pg, pv, ng, nv, dh_ref, dhn_ref, wg, wv, bg, bv, dug, duv, dwg, dwv, dbg, dbv):
        i = pl.program_id(1)
        first, last = i == 0, i == n_rows - 1

        def ext(u_ref, p_ref, n_ref):
            return jnp.concatenate([jnp.where(first, 0.0, p_ref[...]), u_ref[...], n_ref[...]], axis=0)

        eg, ev = ext(ug, pg, ng), ext(uv, pv, nv)
        cg, cv = _conv_taps(eg, wg, bg), _conv_taps(ev, wv, bv)
        dhe = jnp.concatenate([jnp.zeros((8, tn), F32), dh_ref[...], jnp.where(last, 0.0, dhn_ref[...])], axis=0)
        sg = _sigmoid(cg)
        dcg = dhe * cv * (sg * (1.0 + cg * (1.0 - sg)))
        dcv = dhe * (cg * sg)
        n = tm + 16

        @pl.when(i == 0)
        def _():
            for r in (dwg, dwv, dbg, dbv):
                r[...] = jnp.zeros_like(r)

        def back(dc, e, w_ref, du_ref, dw_ref, db_ref):
            du = w_ref[2:3, :] * dc + w_ref[1:2, :] * pltpu.roll(dc, n - 1, 0) + w_ref[0:1, :] * pltpu.roll(dc, n - 2, 0)
            du_ref[...] = du[8:8 + tm].astype(du_ref.dtype)
            own = dc[8:8 + tm]
            dw_ref[0:1, :] += jnp.sum(own * pltpu.roll(e, 2, 0)[8:8 + tm], axis=0, keepdims=True)
            dw_ref[1:2, :] += jnp.sum(own * pltpu.roll(e, 1, 0)[8:8 + tm], axis=0, keepdims=True)
            dw_ref[2:3, :] += jnp.sum(own * e[8:8 + tm], axis=0, keepdims=True)
            db_ref[...] += jnp.sum(own, axis=0, keepdims=True)

        back(dcg, eg, wg, dug, dwg, dbg)
        back(dcv, ev, wv, duv, dwv, dbv)

    blk = lambda o: pl.BlockSpec((tm, tn), lambda j, i: (i, j + o))
    prev = lambda o: pl.BlockSpec((8, tn), lambda j, i: (jnp.maximum(i * hb - 1, 0), j + o))
    nxt = lambda o: pl.BlockSpec((8, tn), lambda j, i: (jnp.minimum((i + 1) * hb, s // 8 - 1), j + o))
    par = lambda r, o: pl.BlockSpec((r, tn), lambda j, i: (0, j + o))
    return pl.pallas_call(
        body, name="conv_glu_bwd", grid=(off, n_rows),
        in_specs=[blk(0), blk(off), prev(0), prev(off), nxt(0), nxt(off), blk(0), nxt(0),
                  par(3, 0), par(3, off), par(1, 0), par(1, off)],
        out_specs=[blk(0), blk(0), par(3, 0), par(3, 0), par(1, 0), par(1, 0)],
        out_shape=[jax.ShapeDtypeStruct((s, f), BF16)] * 2 + [jax.ShapeDtypeStruct((3, f), F32)] * 2
        + [jax.ShapeDtypeStruct((1, f), F32)] * 2,
        compiler_params=_params(("parallel", "arbitrary")))(u, u, u, u, u, u, dh, dh, conv_w, conv_w, conv_b, conv_b)


def _band_fn(q, kp, kc, vp, vc, bias, sink, first, inclusive):
    k = concat_rows(kp, kc)
    v = concat_rows(vp, vc)
    qi = lax.broadcasted_iota(jnp.int32, (BLOCK, 2 * BLOCK), 0)
    ki = lax.broadcasted_iota(jnp.int32, (BLOCK, 2 * BLOCK), 1)
    step = BLOCK + qi - ki
    valid = (step >= 0) & ((step <= BLOCK) if inclusive else (step < BLOCK))
    valid = valid & (jnp.logical_not(first) | (ki >= BLOCK))
    lane = lax.broadcasted_iota(jnp.int32, (1, LANE), 1)
    scale = HEAD_DIM ** -0.5
    qs, ks, vs = split_lanes(q, 4), split_lanes(k, 4), split_lanes(v, 4)
    o_parts, l_parts = [], []
    for pair in range(4):
        o_acc, l_acc = None, None
        for sub in range(2):
            h = 2 * pair + sub
            mh = ((lane >= sub * HEAD_DIM) & (lane < (sub + 1) * HEAD_DIM)).astype(F32)
            logits = mm_nt(qs[pair] * mh, ks[pair]) * scale + bias[h]
            logits = jnp.where(valid, logits, NEG)
            m = lax.stop_gradient(jnp.max(logits, axis=-1, keepdims=True))
            if sink is not None:
                m = jnp.maximum(m, lax.stop_gradient(sink[h]))
            p = jnp.exp(logits - m)
            den = jnp.sum(p, axis=-1, keepdims=True)
            if sink is not None:
                den = den + jnp.exp(sink[h] - m)
            oh = mm(p / den, vs[pair] * mh)
            lh = (m + jnp.log(den)) * mh
            o_acc = oh if o_acc is None else o_acc + oh
            l_acc = lh if l_acc is None else l_acc + lh
        o_parts.append(o_acc)
        l_parts.append(l_acc)
    return concat_lanes(tuple(o_parts)), concat_lanes(tuple(l_parts))


def _band_specs(nb, cq, ck, cv):
    q = pl.BlockSpec((BLOCK, HW), lambda r, n: (r * nb + n, cq))
    kp = pl.BlockSpec((BLOCK, HW), lambda r, n: (r * nb + jnp.maximum(n - 1, 0), ck))
    kc = pl.BlockSpec((BLOCK, HW), lambda r, n: (r * nb + n, ck))
    vp = pl.BlockSpec((BLOCK, HW), lambda r, n: (r * nb + jnp.maximum(n - 1, 0), cv))
    vc = pl.BlockSpec((BLOCK, HW), lambda r, n: (r * nb + n, cv))
    return [q, kp, kc, vp, vc]


def to_classes(a, dil):
    if dil == 1:
        return a
    s, c = a.shape
    return a.reshape(s // dil, dil, c).transpose(1, 0, 2).reshape(s, c)


def from_classes(a, dil):
    if dil == 1:
        return a
    s, c = a.shape
    return a.reshape(dil, s // dil, c).transpose(1, 0, 2).reshape(s, c)


def _const_spec(shape):
    nd = len(shape)
    return pl.BlockSpec(shape, lambda r, n: (0,) * nd)


def band_fwd(name, qkv, dil, cols, bias, sink, inclusive):
    s = qkv.shape[0]
    nb = s // (BLOCK * dil)
    has_sink = sink is not None

    def body(*refs):
        q, kp, kc, vp, vc, b_ref = refs[:6]
        s_ref = refs[6] if has_sink else None
        o_ref, l_ref = refs[-2:]
        first = pl.program_id(1) == 0
        bias_l = tuple(b_ref[h] for h in range(N_HEADS))
        sink_l = tuple(s_ref[h] for h in range(N_HEADS)) if has_sink else None
        o, l = _band_fn(q[...], kp[...], kc[...], vp[...], vc[...], bias_l, sink_l, first, inclusive)
        o_ref[...] = o
        l_ref[...] = l

    out_spec = pl.BlockSpec((BLOCK, HW), lambda r, n: (r * nb + n, 0))
    ins = [qkv] * 5 + [bias] + ([sink] if has_sink else [])
    specs = _band_specs(nb, *cols) + [_const_spec(bias.shape)] + ([_const_spec(sink.shape)] if has_sink else [])
    return pl.pallas_call(
        body, name=name, grid=(dil, nb), in_specs=specs, out_specs=[out_spec, out_spec],
        out_shape=[jax.ShapeDtypeStruct((s, HW), F32)] * 2,
        compiler_params=_params(("parallel", "parallel")))(*ins)


def band_bwd(name, qkv, dil, cols, bias, sink, inclusive, do, dl):
    s = qkv.shape[0]
    nb = s // (BLOCK * dil)
    has_sink, has_dl = sink is not None, dl is not None
    n_in = 6 + int(has_sink) + 1 + int(has_dl)

    def body(*refs):
        q, kp, kc, vp, vc, b_ref = refs[:6]
        s_ref = refs[6] if has_sink else None
        do_ref = refs[6 + int(has_sink)]
        dl_ref = refs[7 + int(has_sink)] if has_dl else None
        outs = refs[n_in:]
        first = pl.program_id(1) == 0
        bias_l = tuple(b_ref[h] for h in range(N_HEADS))
        sink_l = tuple(s_ref[h] for h in range(N_HEADS)) if has_sink else ()

        def fn(q_, kp_, kc_, vp_, vc_, b_, s_):
            o, l = _band_fn(q_, kp_, kc_, vp_, vc_, b_, s_ if has_sink else None, first, inclusive)
            return (o, l) if has_dl else (o,)

        _, vjp = jax.vjp(fn, q[...], kp[...], kc[...], vp[...], vc[...], bias_l, sink_l)
        ct = (do_ref[...], dl_ref[...]) if has_dl else (do_ref[...],)
        g = vjp(ct)
        for k in range(5):
            outs[k][...] = g[k]

        @pl.when((pl.program_id(0) == 0) & (pl.program_id(1) == 0))
        def _():
            outs[5][...] = jnp.zeros_like(outs[5])
            if has_sink:
                outs[6][...] = jnp.zeros_like(outs[6])

        for h in range(N_HEADS):
            outs[5][h] += g[5][h]
            if has_sink:
                outs[6][h] += g[6][h]

    blk = pl.BlockSpec((BLOCK, HW), lambda r, n: (r * nb + n, 0))
    ins = [qkv] * 5 + [bias] + ([sink] if has_sink else []) + [do] + ([dl] if has_dl else [])
    specs = _band_specs(nb, *cols) + [_const_spec(bias.shape)] \
        + ([_const_spec(sink.shape)] if has_sink else []) + [blk] * (1 + int(has_dl))
    out_shape = [jax.ShapeDtypeStruct((s, HW), F32)] * 5 + [jax.ShapeDtypeStruct(bias.shape, F32)] \
        + ([jax.ShapeDtypeStruct(sink.shape, F32)] if has_sink else [])
    out_specs = [blk] * 5 + [_const_spec(bias.shape)] + ([_const_spec(sink.shape)] if has_sink else [])
    res = pl.pallas_call(
        body, name=name, grid=(dil, nb), in_specs=specs, out_specs=out_specs, out_shape=out_shape,
        compiler_params=_params(("arbitrary", "arbitrary")))(*ins)
    return list(res[:5]), res[5], (res[6] if has_sink else None)


def shift_add(name, own, prev, dil):
    s = own.shape[0]
    nb = s // (BLOCK * dil)

    def body(a_ref, b_ref, o_ref):
        lastb = pl.program_id(1) == nb - 1
        o_ref[...] = (a_ref[...] + jnp.where(lastb, 0.0, b_ref[...])).astype(o_ref.dtype)

    blk = pl.BlockSpec((BLOCK, HW), lambda r, n: (r * nb + n, 0))
    nxt = pl.BlockSpec((BLOCK, HW), lambda r, n: (r * nb + jnp.minimum(n + 1, nb - 1), 0))
    return pl.pallas_call(
        body, name=name, grid=(dil, nb), in_specs=[blk, nxt], out_specs=blk,
        out_shape=jax.ShapeDtypeStruct((s, HW), BF16),
        compiler_params=_params(("parallel", "parallel")))(own, prev)


def rel_grad(name, dbias, bucket):
    def body(d_ref, b_ref, o_ref):
        lane = lax.broadcasted_iota(jnp.int32, (1, LANE), 1)
        bk = b_ref[...]
        for h in range(N_HEADS):
            d = d_ref[h]
            row = jnp.zeros((1, LANE), F32)
            for b in range(REL_BUCKETS):
                tot = jnp.sum(jnp.where(bk == b, d, 0.0))
                row = row + jnp.where(lane == b, tot, 0.0)
            o_ref[h:h + 1, :] = row

    return pl.pallas_call(
        body, name=name, out_shape=jax.ShapeDtypeStruct((N_HEADS, LANE), F32),
        in_specs=[pl.BlockSpec(memory_space=pltpu.VMEM)] * 2, out_specs=pl.BlockSpec(memory_space=pltpu.VMEM),
        compiler_params=_params())(dbias, bucket)


MLA_T = 512
MLA_HPS = 2
MLA_SCALE = (C_NOPE + C_ROPE) ** -0.5


def _diag_mask(t):
    r = lax.broadcasted_iota(jnp.int32, (t, t), 0)
    c = lax.broadcasted_iota(jnp.int32, (t, t), 1)
    return c <= r


def mla_fwd(q, k, v):
    s = q.shape[0]
    t = min(MLA_T, s)
    nq = s // t

    def body(q_ref, k_ref, v_ref, o_ref, l_ref, m_s, l_s, acc_s):
        i = pl.program_id(1)
        m_s[...] = jnp.full_like(m_s, NEG)
        l_s[...] = jnp.zeros_like(l_s)
        acc_s[...] = jnp.zeros_like(acc_s)

        def chunk(j, masked):
            rows = pl.ds(pl.multiple_of(j * t, t), t)
            for hh in range(MLA_HPS):
                ln = slice(hh * LANE, (hh + 1) * LANE)
                sc = _dot(q_ref[:, ln], k_ref[rows, ln], 1, 1)
                if masked:
                    sc = jnp.where(_diag_mask(t), sc, NEG)
                m_old = m_s[:, ln]
                m_new = jnp.maximum(m_old, jnp.max(sc, axis=-1, keepdims=True))
                a = jnp.exp(m_old - m_new)
                p = jnp.exp(sc - m_new[:, 0:1])
                l_s[:, ln] = a * l_s[:, ln] + jnp.sum(p, axis=-1, keepdims=True)
                acc_s[:, ln] = a * acc_s[:, ln] + _dot(p, v_ref[rows, ln], 1, 0)
                m_s[:, ln] = m_new

        def step(j, carry):
            chunk(j, False)
            return carry

        lax.fori_loop(0, i, step, 0)
        chunk(i, True)
        o_ref[...] = acc_s[...] / l_s[...]
        l_ref[...] = m_s[...] + jnp.log(l_s[...])

    w = MLA_HPS * LANE
    qs = pl.BlockSpec((t, w), lambda h, i: (i, h))
    ks = pl.BlockSpec((s, w), lambda h, i: (0, h))
    return pl.pallas_call(
        body, name="mla_fwd", grid=(N_HEADS // MLA_HPS, nq), in_specs=[qs, ks, ks], out_specs=[qs, qs],
        out_shape=[jax.ShapeDtypeStruct((s, N_HEADS * LANE), F32)] * 2,
        scratch_shapes=[pltpu.VMEM((t, w), F32)] * 3,
        compiler_params=_params(("parallel", "parallel")))(q, k, v)


def mla_delta_fn(do, o):
    parts = []
    for a, b in zip(_split_impl(do, N_HEADS), _split_impl(o, N_HEADS)):
        parts.append(jnp.sum(a * b, axis=-1, keepdims=True) + jnp.zeros_like(a))
    return jnp.concatenate(parts, axis=-1), do


def mla_bwd(q, k, v, do, lse, delta):
    s = q.shape[0]
    t = min(MLA_T, s)
    nq = s // t

    def body(q_ref, k_ref, v_ref, do_ref, l_ref, d_ref, dq_ref, dk_ref, dv_ref, dk_s, dv_s):
        j = pl.program_id(1)

        @pl.when(j == 0)
        def _():
            dq_ref[...] = jnp.zeros_like(dq_ref)

        dk_s[...] = jnp.zeros_like(dk_s)
        dv_s[...] = jnp.zeros_like(dv_s)
        kb, vb = k_ref[...], v_ref[...]

        def chunk(i, masked):
            rows = pl.ds(pl.multiple_of(i * t, t), t)
            qi, doi = q_ref[rows, :], do_ref[rows, :]
            sc = _dot(qi, kb, 1, 1)
            if masked:
                sc = jnp.where(_diag_mask(t), sc, NEG)
            p = jnp.exp(sc - l_ref[rows, :][:, 0:1])
            ds = p * (_dot(doi, vb, 1, 1) - d_ref[rows, :][:, 0:1])
            dv_s[...] += _dot(p, doi, 0, 0)
            dk_s[...] += _dot(ds, qi, 0, 0)
            dq_ref[rows, :] += _dot(ds, kb, 1, 0)

        def pair(p, carry):
            chunk(j + 1 + 2 * p, False)
            chunk(j + 2 + 2 * p, False)
            return carry

        chunk(j, True)
        n_off = nq - 1 - j
        lax.fori_loop(0, n_off // 2, pair, 0)

        @pl.when(n_off % 2 == 1)
        def _():
            chunk(nq - 1, False)

        dk_ref[...] = dk_s[...]
        dv_ref[...] = dv_s[...]

    ks = pl.BlockSpec((t, LANE), lambda h, j: (j, h))
    full = pl.BlockSpec((s, LANE), lambda h, j: (0, h))
    return pl.pallas_call(
        body, name="mla_bwd", grid=(N_HEADS, nq), in_specs=[full, ks, ks, full, full, full], out_specs=[full, ks, ks],
        out_shape=[jax.ShapeDtypeStruct((s, N_HEADS * LANE), F32)] * 3,
        scratch_shapes=[pltpu.VMEM((t, LANE), F32)] * 2,
        compiler_params=_params(("parallel", "arbitrary")))(q, k, v, do, lse, delta)


def _peer(k):
    x, y, c = lax.axis_index("x"), lax.axis_index("y"), lax.axis_index("c")
    px, py, pc = (x ^ ((k >> 2) & 1)), (y ^ ((k >> 1) & 1)), (c ^ (k & 1))
    return (px, py, pc), 4 * px + 2 * py + pc


def exchange(name, arrays, scatter):
    n = len(arrays)
    out_shape = [jax.ShapeDtypeStruct(a.shape if scatter else (N_DEV,) + a.shape, a.dtype) for a in arrays]

    def body(*refs):
        ins, outs = refs[:n], refs[n:2 * n]
        send_sems, recv_sems, local_sems = refs[2 * n:]
        me, me_idx = _peer(0)
        mine = [pltpu.make_async_copy(ins[a].at[me_idx] if scatter else ins[a], outs[a].at[me_idx], local_sems.at[a])
                for a in range(n)]
        for cp in mine:
            cp.start()
        sends, recvs = [], []
        for k in range(1, N_DEV):
            peer, peer_idx = _peer(k)
            for a in range(n):
                sem = (k - 1) * n + a
                src = ins[a].at[peer_idx] if scatter else ins[a]
                sends.append(pltpu.make_async_remote_copy(
                    src_ref=src, dst_ref=outs[a].at[me_idx], send_sem=send_sems.at[sem], recv_sem=recv_sems.at[sem],
                    device_id=peer, device_id_type=MESH_ID))
                recvs.append(pltpu.make_async_remote_copy(
                    src_ref=src, dst_ref=outs[a].at[peer_idx], send_sem=send_sems.at[sem], recv_sem=recv_sems.at[sem],
                    device_id=peer, device_id_type=MESH_ID))
        for cp in sends:
            cp.start()
        for cp in recvs:
            cp.wait_recv()
        for cp in sends:
            cp.wait_send()
        for cp in mine:
            cp.wait()

    n_sem = (N_DEV - 1) * n
    return pl.pallas_call(
        body, name=name, out_shape=out_shape,
        in_specs=[pl.BlockSpec(memory_space=pl.ANY)] * n, out_specs=[pl.BlockSpec(memory_space=pl.ANY)] * n,
        scratch_shapes=[pltpu.SemaphoreType.DMA((n_sem,)), pltpu.SemaphoreType.DMA((n_sem,)),
                        pltpu.SemaphoreType.DMA((n,))])(*arrays)


ADAMW_BLOCK_ELEMS = 128 * 1024


def adamw_sum(name, parts, w, m, v):
    r, c = w.shape
    fits = [t for t in range(16, r + 1, 16) if r % t == 0 and t * c <= ADAMW_BLOCK_ELEMS]
    tr = max(fits) if fits else r
    c1 = 1.0 - ADAM_B1 ** ADAM_STEP
    c2 = 1.0 - ADAM_B2 ** ADAM_STEP

    def body(p_ref, w_ref, m_ref, v_ref, g_out, d_out, m_out, v_out):
        g = p_ref[0].astype(F32)
        for d in range(1, N_DEV):
            g = g + p_ref[d].astype(F32)
        mn = ADAM_B1 * m_ref[...] + (1.0 - ADAM_B1) * g
        vn = ADAM_B2 * v_ref[...] + (1.0 - ADAM_B2) * (g * g)
        g_out[...] = g
        m_out[...] = mn
        v_out[...] = vn
        d_out[...] = -ADAM_LR * ((mn / c1) / (jnp.sqrt(vn / c2) + ADAM_EPS) + ADAM_WD * w_ref[...])

    row = pl.BlockSpec((tr, c), lambda i: (i, 0))
    return pl.pallas_call(
        body, name=name, grid=(r // tr,), in_specs=[pl.BlockSpec((N_DEV, tr, c), lambda i: (0, i, 0)), row, row, row],
        out_specs=[row] * 4, out_shape=[jax.ShapeDtypeStruct((r, c), F32)] * 4,
        compiler_params=_params(("parallel",)))(parts, w, m, v)


def _pack(arrays, dtype, row_mult):
    flat = jnp.concatenate([a.astype(dtype).reshape(-1) for a in arrays])
    n = flat.shape[0]
    quantum = row_mult * LANE
    total = -(-n // quantum) * quantum
    return jnp.pad(flat, (0, total - n)).reshape(total // LANE, LANE)


def _unpack(packed, shapes):
    flat = packed.reshape(-1)
    out, pos = [], 0
    for shp in shapes:
        n = int(np.prod(shp))
        out.append(flat[pos:pos + n].reshape(shp))
        pos += n
    return out


def _rows(a):
    return a.reshape(-1, a.shape[-1])


def _unshard(gathered, shard_shape, axis):
    t = jnp.moveaxis(gathered.reshape((N_DEV,) + tuple(shard_shape)), 0, axis)
    return t.reshape(tuple(shard_shape[:axis]) + (N_DEV * shard_shape[axis],) + tuple(shard_shape[axis + 1:]))


def _to_shards(full, axis):
    shp = full.shape
    t = full.reshape(shp[:axis] + (N_DEV, shp[axis] // N_DEV) + shp[axis + 1:])
    t = jnp.moveaxis(t, axis, 0)
    return t.reshape(N_DEV, -1, t.shape[-1])


A_COLS = len(A_GROUPS) * 3 * HW
KV_COLS = B_KV_HEADS * HEAD_DIM
KV_REP = N_HEADS // B_KV_HEADS
O_BQ = A_COLS
O_BK = O_BQ + HW
O_BV = O_BK + KV_COLS
O_CQ = O_BV + KV_COLS
O_CKV = O_CQ + C_Q_RANK
O_KR = O_CKV + C_KV_RANK
O_GATE = O_KR + C_ROPE


def _proj_weight(w_in):
    d = w_in.shape[0]

    def per_q_head(lo):
        t = w_in[:, lo:lo + KV_COLS].reshape(d, B_KV_HEADS, 1, HEAD_DIM)
        return jnp.broadcast_to(t, (d, B_KV_HEADS, KV_REP, HEAD_DIM)).reshape(d, HW)

    kr = jnp.pad(w_in[:, O_KR:O_GATE], ((0, 0), (C_NOPE, LANE - C_NOPE - C_ROPE)))
    return jnp.concatenate([w_in[:, O_GATE:], w_in[:, :O_BK], per_q_head(O_BK), per_q_head(O_BV),
                            w_in[:, O_CQ:O_KR], kr], axis=1)


def _unproj_grad(dw_p, d_model):
    d = dw_p.shape[0]
    g = 3 * d_model
    p_bk = g + O_BK
    p_bv = p_bk + HW
    p_cq = p_bv + HW
    p_kr = p_cq + C_Q_RANK + C_KV_RANK

    def sum_heads(lo):
        return dw_p[:, lo:lo + HW].reshape(d, B_KV_HEADS, KV_REP, HEAD_DIM).sum(axis=2).reshape(d, KV_COLS)

    return jnp.concatenate([dw_p[:, g:p_bk], sum_heads(p_bk), sum_heads(p_bv), dw_p[:, p_cq:p_kr],
                            dw_p[:, p_kr + C_NOPE:p_kr + C_NOPE + C_ROPE], dw_p[:, :g]], axis=1)


def _pad_heads(w, per_head, lo, hi):
    r = w.shape[0]
    t = w.reshape(r, N_HEADS, per_head)[:, :, lo:hi]
    return jnp.pad(t, ((0, 0), (0, 0), (0, LANE - (hi - lo)))).reshape(r, N_HEADS * LANE)


def _unpad_heads(w, width):
    r = w.shape[0]
    return w.reshape(r, N_HEADS, LANE)[:, :, :width]


def _t5_bucket(dist):
    n = jnp.maximum(dist, 0)
    max_exact = REL_BUCKETS // 2
    scaled = jnp.log(jnp.maximum(n, 1).astype(F32) / max_exact) / math.log(REL_MAX_DIST / max_exact)
    large = max_exact + (scaled * (REL_BUCKETS - max_exact)).astype(jnp.int32)
    return jnp.where(n < max_exact, n, jnp.minimum(large, REL_BUCKETS - 1))


def _buckets(dil):
    qi = jnp.arange(BLOCK)[:, None]
    ki = jnp.arange(2 * BLOCK)[None, :]
    return _t5_bucket((BLOCK + qi - ki) * dil).astype(jnp.int32)


def _rope_tables(s):
    pos = jnp.arange(s, dtype=F32)
    inv_freq = ROPE_BASE ** (-jnp.arange(0, C_ROPE, 2, dtype=F32) / C_ROPE)
    ang = pos[:, None] * inv_freq[None, :]
    cos, sin = jnp.cos(ang), jnp.sin(ang)
    ones = jnp.ones((s, C_NOPE), F32)
    tail = LANE - C_NOPE - C_ROPE
    c = jnp.concatenate([ones, cos, cos, jnp.ones((s, tail), F32)], axis=1)
    sn = jnp.concatenate([jnp.zeros((s, C_NOPE), F32), sin, sin, jnp.zeros((s, tail), F32)], axis=1)
    return c, sn


def _cols(d_model):
    g = 3 * d_model // HW
    a = [(g + 3 * i, g + 3 * i + 1, g + 3 * i + 2) for i in range(len(A_GROUPS))]
    b = (g + 9, g + 10, g + 11)
    cq_lane = (g + 12) * HW
    return a, b, cq_lane


def _layer_weights(full, l, d_model):
    w_in_p = _proj_weight(full['w_in'][l])
    per = C_NOPE + C_ROPE
    w_uq = full['w_uq'][l]
    w_ukv = full['w_ukv'][l]
    wb = full['w_branch'][l]
    wb2 = jnp.pad(wb[2].reshape(N_HEADS, HEAD_DIM, d_model), ((0, 0), (0, LANE - HEAD_DIM), (0, 0)))
    return dict(
        w_in_p=w_in_p, wq=_pad_heads(w_uq, per, 0, per), wk=_pad_heads(w_ukv, 2 * C_NOPE, 0, C_NOPE),
        wv=_pad_heads(w_ukv, 2 * C_NOPE, C_NOPE, 2 * C_NOPE), wb0=wb[0], wb1=wb[1],
        wb2=wb2.reshape(N_HEADS * LANE, d_model), w_out=full['w_out'][l], w_up=full['w_ffn_up'][l],
        w_down=full['w_ffn_down'][l], conv_w=full['conv_w'][l])


def _layer_fwd(l, x, lw, small, tabs):
    s, d = x.shape
    a_cols, b_cols, cq_lane = _cols(d)
    tag = f"l{l}_"
    proj = matmul(tag + "proj", x, lw['w_in_p'], 'nn')
    res = dict(x=x, proj=proj)
    outs, lses, slabs = [], [], []
    for gi, (window, dil) in enumerate(A_GROUPS):
        if dil == 1:
            slab, cols = proj, a_cols[gi]
        else:
            slab, cols = to_classes(proj[:, a_cols[gi][0] * HW:(a_cols[gi][2] + 1) * HW], dil), (0, 1, 2)
        o, lg = band_fwd(tag + f"a{gi}_fwd", slab, dil, cols, tabs['bias'][gi], None, True)
        outs.append(from_classes(o, dil))
        lses.append(from_classes(lg, dil))
        slabs.append((slab, cols))
    res['a_o'], res['a_l'], res['a_slabs'] = outs, lses, slabs
    (o_a,) = rows_fwd(tag + "a_combine", combine_fn, [(o, HW, 0) for o in outs] + [(lg, HW, 0) for lg in lses], [],
                      [(HW, BF16)])
    sink = small['sinks'][l].reshape(N_HEADS, 1, 1)
    o_b, _ = band_fwd(tag + "b_fwd", proj, 1, b_cols, tabs['bias'][3], sink, False)
    gq = small['q_norm_g'][l].reshape(1, -1)
    gkv = small['kv_norm_g'][l].reshape(1, -1)
    (qc,) = rows_fwd(tag + "c_q", mla_q_fn, [(proj, C_Q_RANK, cq_lane // C_Q_RANK), (tabs['cos'], LANE, 0),
                                            (tabs['sin'], LANE, 0)], [gq, lw['wq']], [(N_HEADS * LANE, BF16)])
    ckv_blk = (cq_lane + C_Q_RANK) // LANE
    kc, vc = rows_fwd(tag + "c_kv", mla_kv_fn, [(proj, LANE, ckv_blk), (proj, LANE, ckv_blk + 1), (tabs['cos'], LANE, 0),
                                                (tabs['sin'], LANE, 0)], [gkv, lw['wk'], lw['wv']],
                      [(N_HEADS * LANE, BF16)] * 2)
    o_c, lse_c = mla_fwd(qc, kc, vc)
    res.update(o_a=o_a, o_b=o_b, o_c=o_c, lse_c=lse_c, qc=qc, kc=kc, vc=vc)
    bg = small['b_gate'][l].reshape(3, 1, d)
    ys = [matmul(tag + f"branch{i}", o, lw[f'wb{i}'], 'nn') for i, o in enumerate((o_a, o_b, o_c))]
    (merged,) = rows_fwd(tag + "merge", merge_fn, [(y, d, 0) for y in ys] + [(proj, d, i) for i in range(3)],
                         [bg[0], bg[1], bg[2]], [(d, BF16)])
    mix = matmul(tag + "out_proj", merged, lw['w_out'], 'nn')
    ln1 = [small['ln1_g'][l].reshape(1, d), small['ln1_b'][l].reshape(1, d)]
    (x1,) = rows_fwd(tag + "ln1", ln_fn, [(x, d, 0), (mix, d, 0)], ln1, [(d, F32)])
    u = matmul(tag + "ffn_up", x1, lw['w_up'], 'nn')
    h = conv_glu_fwd(u, lw['conv_w'], small['conv_b'][l].reshape(1, -1))
    ff = matmul(tag + "ffn_down", h, lw['w_down'], 'nn')
    ln2 = [small['ln2_g'][l].reshape(1, d), small['ln2_b'][l].reshape(1, d)]
    (x2,) = rows_fwd(tag + "ln2", ln_fn, [(x1, d, 0), (ff, d, 0)], ln2, [(d, F32)])
    res.update(merged=merged, mix=mix, x1=x1, u=u, h=h, ff=ff, ys=ys)
    return x2, res


def _layer_bwd(l, dy, res, lw, small, tabs):
    x, proj = res['x'], res['proj']
    s, d = x.shape
    a_cols, b_cols, cq_lane = _cols(d)
    tag = f"l{l}_"
    g = {}
    ln2 = [small['ln2_g'][l].reshape(1, d), small['ln2_b'][l].reshape(1, d)]
    dff, g['ln2_g'], g['ln2_b'] = rows_bwd(tag + "ln2_bwd", ln_fn, [(res['x1'], d, 0), (res['ff'], d, 0)], ln2,
                                           [(dy, d)], [(1, F32)])
    g['w_ffn_down'] = matmul(tag + "dw_down", res['h'], dff, 'tn')
    dh = matmul(tag + "dh", dff, lw['w_down'], 'nt')
    conv_b = small['conv_b'][l].reshape(1, -1)
    dug, duv, dwg, dwv, dbg, dbv = conv_glu_bwd(res['u'], dh, lw['conv_w'], conv_b)
    du = jnp.concatenate([dug, duv], axis=1)
    g['conv_w'] = jnp.concatenate([dwg, dwv], axis=1)
    g['conv_b'] = jnp.concatenate([dbg, dbv], axis=1).reshape(-1)
    g['w_ffn_up'] = matmul(tag + "dw_up", res['x1'], du, 'tn')
    dx1 = matmul(tag + "dx1", du, lw['w_up'], 'nt', add=dff, scale=ALPHA)
    ln1 = [small['ln1_g'][l].reshape(1, d), small['ln1_b'][l].reshape(1, d)]
    dmix, g['ln1_g'], g['ln1_b'] = rows_bwd(tag + "ln1_bwd", ln_fn, [(x, d, 0), (res['mix'], d, 0)], ln1,
                                            [(dx1, d)], [(1, F32)])
    g['w_out'] = matmul(tag + "dw_out", res['merged'], dmix, 'tn')
    dmerged = matmul(tag + "dmerged", dmix, lw['w_out'], 'nt')
    bg = small['b_gate'][l].reshape(3, 1, d)
    mrows = [(y, d, 0) for y in res['ys']] + [(proj, d, i) for i in range(3)]
    (dy0, dy1, dy2, dg0, dg1, dg2, db0, db1, db2) = rows_bwd(
        tag + "merge_bwd", merge_fn, mrows, [bg[0], bg[1], bg[2]], [(dmerged, d)], [(i, BF16) for i in range(6)])
    g['b_gate'] = jnp.concatenate([db0, db1, db2], axis=1).reshape(-1)
    branch_in = (res['o_a'], res['o_b'], res['o_c'])
    dwb = [matmul(tag + f"dw_branch{i}", o, dyi, 'tn') for i, (o, dyi) in enumerate(zip(branch_in, (dy0, dy1, dy2)))]
    dwb[2] = dwb[2].reshape(N_HEADS, LANE, d)[:, :HEAD_DIM].reshape(HW, d)
    g['w_branch'] = jnp.stack(dwb, axis=0)
    do_a, do_b, do_c = [matmul(tag + f"do_branch{i}", dyi, lw[f'wb{i}'], 'nt')
                        for i, dyi in enumerate((dy0, dy1, dy2))]
    cw = N_HEADS * LANE
    delta, do_c16 = rows_fwd(tag + "c_delta", mla_delta_fn, [(do_c, cw, 0), (res['o_c'], cw, 0)], [],
                             [(cw, F32), (cw, BF16)])
    dqc, dkc, dvc = mla_bwd(res['qc'], res['kc'], res['vc'], do_c16, res['lse_c'], delta)
    gq = small['q_norm_g'][l].reshape(1, -1)
    gkv = small['kv_norm_g'][l].reshape(1, -1)
    dcq, dgq, dwq = rows_bwd(tag + "c_q_bwd", mla_q_fn,
                             [(proj, C_Q_RANK, cq_lane // C_Q_RANK), (tabs['cos'], LANE, 0), (tabs['sin'], LANE, 0)],
                             [gq, lw['wq']], [(dqc, N_HEADS * LANE)], [(0, BF16)])
    ckv_blk = (cq_lane + C_Q_RANK) // LANE
    dckv, dkr, dgkv, dwk, dwv = rows_bwd(
        tag + "c_kv_bwd", mla_kv_fn,
        [(proj, LANE, ckv_blk), (proj, LANE, ckv_blk + 1), (tabs['cos'], LANE, 0), (tabs['sin'], LANE, 0)],
        [gkv, lw['wk'], lw['wv']], [(dkc, N_HEADS * LANE), (dvc, N_HEADS * LANE)], [(0, BF16), (1, BF16)])
    g['q_norm_g'], g['kv_norm_g'] = dgq.reshape(-1), dgkv.reshape(-1)
    per = C_NOPE + C_ROPE
    g['w_uq'] = _unpad_heads(dwq, per).reshape(C_Q_RANK, N_HEADS * per)
    g['w_ukv'] = jnp.concatenate([_unpad_heads(dwk, C_NOPE), _unpad_heads(dwv, C_NOPE)], axis=2).reshape(
        C_KV_RANK, N_HEADS * 2 * C_NOPE)
    sink = small['sinks'][l].reshape(N_HEADS, 1, 1)
    (dq, dkp, dko, dvp, dvo), dbias_b, dsink = band_bwd(tag + "b_bwd", proj, 1, b_cols, tabs['bias'][3], sink, False,
                                                        do_b, None)
    g['sinks'] = dsink.reshape(-1)
    pieces_b = [dq.astype(BF16), shift_add(tag + "b_dk", dko, dkp, 1), shift_add(tag + "b_dv", dvo, dvp, 1)]
    dbias = [None] * 4
    dbias[3] = dbias_b
    combo = [(o, HW, 0) for o in res['a_o']] + [(lg, HW, 0) for lg in res['a_l']]
    a_cts = rows_bwd(tag + "a_combine_bwd", combine_fn, combo, [], [(do_a, HW)], [(i, F32) for i in range(6)])
    pieces_a = []
    for gi, (window, dil) in enumerate(A_GROUPS):
        slab, cols = res['a_slabs'][gi]
        (dq, dkp, dko, dvp, dvo), dbias[gi], _ = band_bwd(
            tag + f"a{gi}_bwd", slab, dil, cols, tabs['bias'][gi], None, True, to_classes(a_cts[gi], dil),
            to_classes(a_cts[3 + gi], dil))
        group = [dq.astype(BF16), shift_add(tag + f"a{gi}_dk", dko, dkp, dil),
                 shift_add(tag + f"a{gi}_dv", dvo, dvp, dil)]
        pieces_a += [from_classes(t, dil) for t in group]
    g['dbias'] = dbias
    dproj = jnp.concatenate([dg0, dg1, dg2] + pieces_a + pieces_b + [dcq, dckv, dkr], axis=1)
    dw_in_p = matmul(tag + "dw_in", x, dproj, 'tn')
    g['w_in'] = _unproj_grad(dw_in_p, d)
    dx = matmul(tag + "dx", dproj, lw['w_in_p'], 'nt', add=dmix, scale=ALPHA)
    return dx, g


def kernel(x, rel_table, w_in, b_gate, sinks, q_norm_g, kv_norm_g, w_uq, w_ukv, w_branch, w_out, ln1_g, ln1_b, w_ffn_up, conv_w, conv_b, w_ffn_down, ln2_g, ln2_b, loss_target, m_rel_table, m_w_in, m_b_gate, m_sinks, m_q_norm_g, m_kv_norm_g, m_w_uq, m_w_ukv, m_w_branch, m_w_out, m_ln1_g, m_ln1_b, m_w_ffn_up, m_conv_w, m_conv_b, m_w_ffn_down, m_ln2_g, m_ln2_b, v_rel_table, v_w_in, v_b_gate, v_sinks, v_q_norm_g, v_kv_norm_g, v_w_uq, v_w_ukv, v_w_branch, v_w_out, v_ln1_g, v_ln1_b, v_w_ffn_up, v_conv_w, v_conv_b, v_w_ffn_down, v_ln2_g, v_ln2_b):
    args = locals()
    w = {n: args[n] for n in WEIGHTS}
    mom = {n: args["m_" + n] for n in WEIGHTS}
    var = {n: args["v_" + n] for n in WEIGHTS}
    s, d = x.shape[1], x.shape[2]
    xs = x.reshape(s, d)
    target = loss_target.reshape(s, d)

    big = [n for n in WEIGHTS if n in SHARDED]
    small_names = [n for n in WEIGHTS if n not in SHARDED]
    wire = lambda n: BF16 if n in GATHER_BF16 else F32
    gathered = exchange("gather_weights", [_rows(w[n]).astype(wire(n)) for n in big], scatter=False)
    full = {n: _unshard(t, w[n].shape, SHARDED[n]) for n, t in zip(big, gathered)}
    small = {n: w[n] for n in small_names}
    loss_part, grad_x, g_full, g_small = _local_grads(xs, target, full, small)
    grad_x = grad_x.reshape(x.shape)

    recv = exchange("exchange_grads", [_to_shards(g_full[n], SHARDED[n]).astype(wire(n)) for n in big], scatter=True)
    big_res = [{}, {}, {}, {}]
    for n, parts in zip(big, recv):
        outs = adamw_sum("adamw_" + n, parts, _rows(w[n]), _rows(mom[n]), _rows(var[n]))
        for k in range(4):
            big_res[k][n] = outs[k].reshape(w[n].shape)

    small_pack = _pack([g_small[n].reshape(w[n].shape) for n in small_names] + [loss_part[0, 0:1]], F32, 8)
    (small_all,) = exchange("gather_small_grads", [small_pack], scatter=False)
    pks = lambda tree: _pack([tree[n] for n in small_names] + [jnp.zeros((1,), F32)], F32, 8)
    small_out = adamw_sum("adamw_replicated", small_all, pks(w), pks(mom), pks(var))
    small_shapes = [w[n].shape for n in small_names] + [(1,)]
    small_res = [dict(zip(small_names + ['loss'], _unpack(o, small_shapes))) for o in small_out]

    loss = small_res[0]['loss'].reshape(())
    out = [loss, grad_x]
    for k in range(4):
        out += [big_res[k][n] if n in SHARDED else small_res[k][n] for n in WEIGHTS]
    return tuple(out)


def _local_grads(xs, target, full, small):
    s, d = xs.shape
    rel_table = small['rel_table']
    big = [n for n in WEIGHTS if n in SHARDED]
    small_names = [n for n in WEIGHTS if n not in SHARDED]

    cos, sin = _rope_tables(s)
    buckets = [_buckets(dil) for _, dil in A_GROUPS] + [_buckets(1)]
    def lookup(table, bucket):
        out = jnp.zeros((N_HEADS,) + bucket.shape, F32)
        for b in range(REL_BUCKETS):
            out = jnp.where((bucket == b)[None], table[b][:, None, None], out)
        return out

    bias = [lookup(rel_table[:, gi * N_HEADS:(gi + 1) * N_HEADS], buckets[gi]) for gi in range(4)]
    tabs = dict(cos=cos, sin=sin, bias=bias)

    act, saved, lws = xs, [], []
    for l in range(DEPTH):
        lw = _layer_weights(full, l, d)
        act, res = _layer_fwd(l, act, lw, small, tabs)
        saved.append(res)
        lws.append(lw)
    dy, loss_part = loss_head(act, target)

    grads = [None] * DEPTH
    for l in reversed(range(DEPTH)):
        dy, grads[l] = _layer_bwd(l, dy, saved[l], lws[l], small, tabs)
    grad_x = dy

    rel_cols = []
    for gi in range(4):
        both = grads[0]['dbias'][gi] + grads[1]['dbias'][gi] if DEPTH == 2 else grads[0]['dbias'][gi]
        rel_cols.append(rel_grad(f"rel_grad{gi}", both, buckets[gi])[:, :REL_BUCKETS].T)
    g_rel = jnp.concatenate(rel_cols, axis=1)

    def stacked(n):
        return jnp.stack([grads[l][n] for l in range(DEPTH)], axis=0)

    g_full = {n: stacked(n) for n in big}
    g_small = {'rel_table': g_rel}
    for n in small_names:
        if n != 'rel_table':
            g_small[n] = stacked(n)
    return loss_part, grad_x, g_full, g_small
```

```python
import functools
import math

import numpy as np
import jax
import jax.numpy as jnp
from jax import lax
from jax.experimental import pallas as pl
from jax.experimental.pallas import tpu as pltpu

F32 = jnp.float32
BF16 = jnp.bfloat16

N_DEV = 8
DEPTH = 2
HEAD_DIM = 64
BLOCK = 128
A_GROUPS = ((128, 1), (512, 4), (2048, 16))
N_HEADS = 8
HW = N_HEADS * HEAD_DIM
B_KV_HEADS = 2
C_Q_RANK = 256
C_KV_RANK = 128
C_NOPE = 64
C_ROPE = 32
ROPE_BASE = 10000.0
REL_BUCKETS = 32
REL_MAX_DIST = 2048
ALPHA = (2 * DEPTH) ** 0.25
LN_EPS = 1e-5
RMS_EPS = 1e-6
NEG = -1e30
ADAM_LR = 0.001
ADAM_B1 = 0.9
ADAM_B2 = 0.999
ADAM_EPS = 1e-08
ADAM_WD = 0.01
ADAM_STEP = 10

LANE = 128
TM = 256
VMEM_LIMIT = 56 * 1024 * 1024
MESH_ID = pl.DeviceIdType.MESH

WEIGHTS = ('rel_table', 'w_in', 'b_gate', 'sinks', 'q_norm_g', 'kv_norm_g', 'w_uq', 'w_ukv', 'w_branch',
           'w_out', 'ln1_g', 'ln1_b', 'w_ffn_up', 'conv_w', 'conv_b', 'w_ffn_down', 'ln2_g', 'ln2_b')
SHARDED = {'w_in': 2, 'w_uq': 2, 'w_ukv': 2, 'w_branch': 3, 'w_out': 1, 'w_ffn_up': 2, 'conv_w': 2,
           'w_ffn_down': 1}
GATHER_BF16 = ('w_in', 'w_uq', 'w_ukv', 'w_branch', 'w_out', 'w_ffn_up', 'w_ffn_down')


def _params(sem=None):
    return pltpu.CompilerParams(dimension_semantics=sem, vmem_limit_bytes=VMEM_LIMIT)


def _pick(n, target):
    if n <= target:
        return n
    best = None
    for t in range(LANE, target + 1, LANE):
        if n % t == 0:
            best = t
    assert best is not None, (n, target)
    return best


def _dot(a, b, ca, cb):
    return lax.dot_general(a.astype(BF16), b.astype(BF16), (((ca,), (cb,)), ((), ())),
                           preferred_element_type=F32)


@jax.custom_vjp
def mm(a, b):
    return _dot(a, b, 1, 0)


def _mm_fwd(a, b):
    return _dot(a, b, 1, 0), (a, b)


def _mm_bwd(res, g):
    a, b = res
    return _dot(g, b, 1, 1), _dot(a, g, 0, 0)


mm.defvjp(_mm_fwd, _mm_bwd)


@jax.custom_vjp
def mm_nt(a, b):
    return _dot(a, b, 1, 1)


def _mm_nt_fwd(a, b):
    return _dot(a, b, 1, 1), (a, b)


def _mm_nt_bwd(res, g):
    a, b = res
    return _dot(g, b, 1, 0), _dot(g, a, 0, 0)


mm_nt.defvjp(_mm_nt_fwd, _mm_nt_bwd)


def _split_impl(x, n):
    w = x.shape[-1] // n
    return tuple(x[:, i * w:(i + 1) * w] for i in range(n))


@functools.partial(jax.custom_vjp, nondiff_argnums=(1,))
def split_lanes(x, n):
    return _split_impl(x, n)


def _split_fwd(x, n):
    return _split_impl(x, n), None


def _split_bwd(n, _, gs):
    return (jnp.concatenate(gs, axis=-1),)


split_lanes.defvjp(_split_fwd, _split_bwd)


@jax.custom_vjp
def concat_lanes(xs):
    return jnp.concatenate(xs, axis=-1)


def _concat_fwd(xs):
    return jnp.concatenate(xs, axis=-1), len(xs)


def _concat_bwd(n, g):
    return (_split_impl(g, n),)


concat_lanes.defvjp(_concat_fwd, _concat_bwd)


@jax.custom_vjp
def concat_rows(a, b):
    return jnp.concatenate([a, b], axis=0)


def _crow_fwd(a, b):
    return jnp.concatenate([a, b], axis=0), a.shape[0]


def _crow_bwd(na, g):
    return g[:na], g[na:]


concat_rows.defvjp(_crow_fwd, _crow_bwd)


def _rot_half(x):
    n = x.shape[-1]
    lane = lax.broadcasted_iota(jnp.int32, (1, n), 1) % LANE
    lo = (lane >= C_NOPE) & (lane < C_NOPE + C_ROPE // 2)
    hi = (lane >= C_NOPE + C_ROPE // 2) & (lane < C_NOPE + C_ROPE)
    up = pltpu.roll(x, n - C_ROPE // 2, 1)
    dn = pltpu.roll(x, C_ROPE // 2, 1)
    return jnp.where(lo, -up, jnp.where(hi, dn, 0.0))


@jax.custom_vjp
def rope(x, c, s):
    return x * c + _rot_half(x) * s


def _rope_fwd(x, c, s):
    return x * c + _rot_half(x) * s, (c, s)


def _rope_bwd(res, g):
    c, s = res
    return g * c - _rot_half(g * s), jnp.zeros_like(c), jnp.zeros_like(s)


rope.defvjp(_rope_fwd, _rope_bwd)


def _sigmoid(x):
    return 0.5 * jnp.tanh(0.5 * x) + 0.5


def matmul(name, a, b, mode, out_dtype=F32, add=None, scale=1.0):
    if mode == 'nn':
        (m, k), n = a.shape, b.shape[1]
    elif mode == 'nt':
        (m, k), n = a.shape, b.shape[0]
    else:
        (k, m), n = a.shape, b.shape[1]
    tm, tn = _pick(m, 1024), _pick(n, 1024)
    tk = _pick(k, 1536)
    nk = k // tk
    a_spec = pl.BlockSpec((tk, tm), lambda i, j, kk: (kk, i)) if mode == 'tn' else \
        pl.BlockSpec((tm, tk), lambda i, j, kk: (i, kk))
    b_spec = pl.BlockSpec((tn, tk), lambda i, j, kk: (j, kk)) if mode == 'nt' else \
        pl.BlockSpec((tk, tn), lambda i, j, kk: (kk, j))
    o_spec = pl.BlockSpec((tm, tn), lambda i, j, kk: (i, j))
    ca = 0 if mode == 'tn' else 1
    cb = 1 if mode == 'nt' else 0
    has_add = add is not None

    def body(*refs):
        a_ref, b_ref = refs[:2]
        add_ref = refs[2] if has_add else None
        o_ref = refs[2 + int(has_add)]

        def finish(r):
            if has_add:
                r = r + scale * add_ref[...]
            o_ref[...] = r.astype(out_dtype)

        if nk == 1:
            finish(_dot(a_ref[...], b_ref[...], ca, cb))
            return
        acc_ref = refs[-1]
        kk = pl.program_id(2)

        @pl.when(kk == 0)
        def _():
            acc_ref[...] = jnp.zeros_like(acc_ref)

        acc_ref[...] += _dot(a_ref[...], b_ref[...], ca, cb)

        @pl.when(kk == nk - 1)
        def _():
            finish(acc_ref[...])

    ins = [a, b] + ([add] if has_add else [])
    specs = [a_spec, b_spec] + ([o_spec] if has_add else [])
    return pl.pallas_call(
        body, name=name, grid=(m // tm, n // tn, nk), in_specs=specs, out_specs=o_spec,
        out_shape=jax.ShapeDtypeStruct((m, n), out_dtype),
        scratch_shapes=[pltpu.VMEM((tm, tn), F32)] if nk > 1 else [],
        compiler_params=_params(("parallel", "parallel", "arbitrary")))(*ins)


def _row_spec(width, col_block=0, tm=TM):
    return pl.BlockSpec((tm, width), lambda i: (i, col_block))


def _full_spec(shape):
    nd = len(shape)
    return pl.BlockSpec(shape, lambda i: (0,) * nd)


def rows_fwd(name, fn, rows, params, outs, tm=TM):
    s = rows[0][0].shape[0]
    nr, npar = len(rows), len(params)

    def body(*refs):
        r = [refs[i][...].astype(F32) for i in range(nr)]
        p = [refs[nr + i][...] for i in range(npar)]
        res = fn(*r, *p)
        for o_ref, val in zip(refs[nr + npar:], res):
            o_ref[...] = val.astype(o_ref.dtype)

    return pl.pallas_call(
        body, name=name, grid=(s // tm,),
        in_specs=[_row_spec(w, cb, tm) for _, w, cb in rows] + [_full_spec(p.shape) for p in params],
        out_specs=[_row_spec(w, 0, tm) for w, _ in outs],
        out_shape=[jax.ShapeDtypeStruct((s, w), dt) for w, dt in outs],
        compiler_params=_params(("parallel",)))(*[a for a, _, _ in rows], *params)


def rows_bwd(name, fn, rows, params, cts, row_grads, tm=TM):
    s = rows[0][0].shape[0]
    nr, npar, nct, nrg = len(rows), len(params), len(cts), len(row_grads)

    def body(*refs):
        r = [refs[i][...].astype(F32) for i in range(nr)]
        p = [refs[nr + i][...].astype(F32) for i in range(npar)]
        g = tuple(refs[nr + npar + i][...].astype(F32) for i in range(nct))
        _, vjp = jax.vjp(lambda *a: tuple(fn(*a)), *r, *p)
        grads = vjp(g)
        outs = refs[nr + npar + nct:]
        for k, (idx, _) in enumerate(row_grads):
            outs[k][...] = grads[idx].astype(outs[k].dtype)

        @pl.when(pl.program_id(0) == 0)
        def _():
            for k in range(npar):
                outs[nrg + k][...] = jnp.zeros_like(outs[nrg + k])

        for k in range(npar):
            outs[nrg + k][...] += grads[nr + k]

    return pl.pallas_call(
        body, name=name, grid=(s // tm,),
        in_specs=[_row_spec(w, cb, tm) for _, w, cb in rows] + [_full_spec(p.shape) for p in params]
        + [_row_spec(w, 0, tm) for _, w in cts],
        out_specs=[_row_spec(rows[idx][1], 0, tm) for idx, _ in row_grads] + [_full_spec(p.shape) for p in params],
        out_shape=[jax.ShapeDtypeStruct((s, rows[idx][1]), dt) for idx, dt in row_grads]
        + [jax.ShapeDtypeStruct(p.shape, F32) for p in params],
        compiler_params=_params(("arbitrary",)))(*[a for a, _, _ in rows], *params, *[a for a, _ in cts])


def ln_fn(x, r, g, b):
    z = ALPHA * x + r
    mu = jnp.mean(z, axis=-1, keepdims=True)
    zc = z - mu
    var = jnp.mean(zc * zc, axis=-1, keepdims=True)
    return (zc * lax.rsqrt(var + LN_EPS) * g + b,)


def ln_twice(x, r, g, b):
    (y,) = ln_fn(x, r, g, b)
    return y, y


def _rms(x, g):
    return x * lax.rsqrt(jnp.mean(x * x, axis=-1, keepdims=True) + RMS_EPS) * g


def mla_q_fn(cq, c, s, g, wq):
    q = mm(_rms(cq, g), wq)
    return (rope(q, jnp.concatenate([c] * N_HEADS, axis=-1), jnp.concatenate([s] * N_HEADS, axis=-1)) * MLA_SCALE,)


@jax.custom_vjp
def tile_heads(x):
    return jnp.concatenate([x] * N_HEADS, axis=-1)


def _tile_fwd(x):
    return jnp.concatenate([x] * N_HEADS, axis=-1), None


def _tile_bwd(_, g):
    parts = _split_impl(g, N_HEADS)
    acc = parts[0]
    for p in parts[1:]:
        acc = acc + p
    return (acc,)


tile_heads.defvjp(_tile_fwd, _tile_bwd)


def mla_kv_fn(ckv, kr, c, s, g, wk, wv):
    n = _rms(ckv, g)
    return mm(n, wk) + tile_heads(rope(kr, c, s)), mm(n, wv)


def merge_fn(y0, y1, y2, g0, g1, g2, b0, b1, b2):
    return (_sigmoid(g0 + b0) * y0 + _sigmoid(g1 + b1) * y1 + _sigmoid(g2 + b2) * y2,)


def combine_fn(o1, o2, o3, l1, l2, l3):
    mx = lax.stop_gradient(jnp.maximum(jnp.maximum(l1, l2), l3))
    e1, e2, e3 = jnp.exp(l1 - mx), jnp.exp(l2 - mx), jnp.exp(l3 - mx)
    return ((e1 * o1 + e2 * o2 + e3 * o3) / (e1 + e2 + e3),)


def loss_head(y, t):
    s, d = y.shape

    def body(y_ref, t_ref, dy_ref, loss_ref):
        err = y_ref[...] - t_ref[...]
        dy_ref[...] = err * (1.0 / d)

        @pl.when(pl.program_id(0) == 0)
        def _():
            loss_ref[...] = jnp.zeros_like(loss_ref)

        loss_ref[...] += jnp.sum(err * err) * (0.5 / d)

    return pl.pallas_call(
        body, name="loss_head", grid=(s // TM,), in_specs=[_row_spec(d), _row_spec(d)],
        out_specs=[_row_spec(d), _full_spec((8, LANE))],
        out_shape=[jax.ShapeDtypeStruct((s, d), F32), jax.ShapeDtypeStruct((8, LANE), F32)],
        compiler_params=_params(("arbitrary",)))(y, t)


CONV_TN = 256
CONV_TM = 512


def _conv_taps(u_ext, w_ref, b_ref):
    d1, d2 = pltpu.roll(u_ext, 1, 0), pltpu.roll(u_ext, 2, 0)
    return w_ref[0:1, :] * d2 + w_ref[1:2, :] * d1 + w_ref[2:3, :] * u_ext + b_ref[...], d1, d2


def conv_glu_fwd(u, conv_w, conv_b):
    s, f2 = u.shape
    f = f2 // 2
    tn, tm = CONV_TN, min(CONV_TM, s)
    off = f // tn
    hb = tm // 8

    def body(ug, uv, hg, hv, wg, wv, bg, bv, o_ref):
        first = pl.program_id(0) == 0

        def conv(u_ref, h_ref, w_ref, b_ref):
            halo = jnp.where(first, 0.0, h_ref[...])
            ext = jnp.concatenate([halo, u_ref[...]], axis=0)
            return _conv_taps(ext, w_ref, b_ref)[0][8:]

        cg, cv = conv(ug, hg, wg, bg), conv(uv, hv, wv, bv)
        o_ref[...] = (cg * _sigmoid(cg) * cv).astype(o_ref.dtype)

    blk = lambda o: pl.BlockSpec((tm, tn), lambda i, j: (i, j + o))
    halo = lambda o: pl.BlockSpec((8, tn), lambda i, j: (jnp.maximum(i * hb - 1, 0), j + o))
    par = lambda r, o: pl.BlockSpec((r, tn), lambda i, j: (0, j + o))
    return pl.pallas_call(
        body, name="conv_glu_fwd", grid=(s // tm, off),
        in_specs=[blk(0), blk(off), halo(0), halo(off), par(3, 0), par(3, off), par(1, 0), par(1, off)],
        out_specs=pl.BlockSpec((tm, tn), lambda i, j: (i, j)),
        out_shape=jax.ShapeDtypeStruct((s, f), BF16),
        compiler_params=_params(("parallel", "parallel")))(u, u, u, u, conv_w, conv_w, conv_b, conv_b)


def conv_glu_bwd(u, dh, conv_w, conv_b):
    s, f2 = u.shape
    f = f2 // 2
    tn, tm = CONV_TN, min(CONV_TM, s)
    off = f // tn
    hb = tm // 8
    n_rows = s // tm

    def body(ug, uv, pg, pv, ng, nv, dh_ref, dhn_ref, wg, wv, bg, bv, dug, duv, dwg, dwv, dbg, dbv):
        i = pl.program_id(1)
        first, last = i == 0, i == n_rows - 1

        def ext(u_ref, p_ref, n_ref):
            return jnp.concatenate([jnp.where(first, 0.0, p_ref[...]), u_ref[...], n_ref[...]], axis=0)

        eg, ev = ext(ug, pg, ng), ext(uv, pv, nv)
        (cg, eg1, eg2), (cv, ev1, ev2) = _conv_taps(eg, wg, bg), _conv_taps(ev, wv, bv)
        dhe = jnp.concatenate([jnp.zeros((8, tn), F32), dh_ref[...], jnp.where(last, 0.0, dhn_ref[...])], axis=0)
        sg = _sigmoid(cg)
        dcg = dhe * cv * (sg * (1.0 + cg * (1.0 - sg)))
        dcv = dhe * (cg * sg)
        n = tm + 16

        @pl.when(i == 0)
        def _():
            for r in (dwg, dwv, dbg, dbv):
                r[...] = jnp.zeros_like(r)

        def back(dc, e, e1, e2, w_ref, du_ref, dw_ref, db_ref):
            du = w_ref[2:3, :] * dc + w_ref[1:2, :] * pltpu.roll(dc, n - 1, 0) + w_ref[0:1, :] * pltpu.roll(dc, n - 2, 0)
            du_ref[...] = du[8:8 + tm].astype(du_ref.dtype)
            own = dc[8:8 + tm]
            dw_ref[0:1, :] += jnp.sum(own * e2[8:8 + tm], axis=0, keepdims=True)
            dw_ref[1:2, :] += jnp.sum(own * e1[8:8 + tm], axis=0, keepdims=True)
            dw_ref[2:3, :] += jnp.sum(own * e[8:8 + tm], axis=0, keepdims=True)
            db_ref[...] += jnp.sum(own, axis=0, keepdims=True)

        back(dcg, eg, eg1, eg2, wg, dug, dwg, dbg)
        back(dcv, ev, ev1, ev2, wv, duv, dwv, dbv)

    blk = lambda o: pl.BlockSpec((tm, tn), lambda j, i: (i, j + o))
    prev = lambda o: pl.BlockSpec((8, tn), lambda j, i: (jnp.maximum(i * hb - 1, 0), j + o))
    nxt = lambda o: pl.BlockSpec((8, tn), lambda j, i: (jnp.minimum((i + 1) * hb, s // 8 - 1), j + o))
    par = lambda r, o: pl.BlockSpec((r, tn), lambda j, i: (0, j + o))
    return pl.pallas_call(
        body, name="conv_glu_bwd", grid=(off, n_rows),
        in_specs=[blk(0), blk(off), prev(0), prev(off), nxt(0), nxt(off), blk(0), nxt(0),
                  par(3, 0), par(3, off), par(1, 0), par(1, off)],
        out_specs=[blk(0), blk(0), par(3, 0), par(3, 0), par(1, 0), par(1, 0)],
        out_shape=[jax.ShapeDtypeStruct((s, f), BF16)] * 2 + [jax.ShapeDtypeStruct((3, f), F32)] * 2
        + [jax.ShapeDtypeStruct((1, f), F32)] * 2,
        compiler_params=_params(("parallel", "arbitrary")))(u, u, u, u, u, u, dh, dh, conv_w, conv_w, conv_b, conv_b)


def _band_fn(q, kp, kc, vp, vc, bias, sink, first, inclusive):
    k = concat_rows(kp, kc)
    v = concat_rows(vp, vc)
    qi = lax.broadcasted_iota(jnp.int32, (BLOCK, 2 * BLOCK), 0)
    ki = lax.broadcasted_iota(jnp.int32, (BLOCK, 2 * BLOCK), 1)
    step = BLOCK + qi - ki
    valid = (step >= 0) & ((step <= BLOCK) if inclusive else (step < BLOCK))
    valid = valid & (jnp.logical_not(first) | (ki >= BLOCK))
    lane = lax.broadcasted_iota(jnp.int32, (1, LANE), 1)
    scale = HEAD_DIM ** -0.5
    qs, ks, vs = split_lanes(q, 4), split_lanes(k, 4), split_lanes(v, 4)
    o_parts, l_parts = [], []
    for pair in range(4):
        o_acc, l_acc = None, None
        for sub in range(2):
            h = 2 * pair + sub
            mh = ((lane >= sub * HEAD_DIM) & (lane < (sub + 1) * HEAD_DIM)).astype(F32)
            logits = mm_nt(qs[pair] * mh, ks[pair]) * scale + bias[h]
            logits = jnp.where(valid, logits, NEG)
            m = lax.stop_gradient(jnp.max(logits, axis=-1, keepdims=True))
            if sink is not None:
                m = jnp.maximum(m, lax.stop_gradient(sink[h]))
            p = jnp.exp(logits - m)
            den = jnp.sum(p, axis=-1, keepdims=True)
            if sink is not None:
                den = den + jnp.exp(sink[h] - m)
            oh = mm(p / den, vs[pair] * mh)
            lh = (m + jnp.log(den)) * mh
            o_acc = oh if o_acc is None else o_acc + oh
            l_acc = lh if l_acc is None else l_acc + lh
        o_parts.append(o_acc)
        l_parts.append(l_acc)
    return concat_lanes(tuple(o_parts)), concat_lanes(tuple(l_parts))


def _band_specs(nb, cq, ck, cv):
    own = lambda n: jnp.minimum(n, nb - 1)
    prev = lambda n: jnp.maximum(own(n) - 1, 0)
    q = pl.BlockSpec((BLOCK, HW), lambda r, n: (r * nb + own(n), cq))
    kp = pl.BlockSpec((BLOCK, HW), lambda r, n: (r * nb + prev(n), ck))
    kc = pl.BlockSpec((BLOCK, HW), lambda r, n: (r * nb + own(n), ck))
    vp = pl.BlockSpec((BLOCK, HW), lambda r, n: (r * nb + prev(n), cv))
    vc = pl.BlockSpec((BLOCK, HW), lambda r, n: (r * nb + own(n), cv))
    return [q, kp, kc, vp, vc]


def to_classes(a, dil):
    if dil == 1:
        return a
    s, c = a.shape
    return a.reshape(s // dil, dil, c).transpose(1, 0, 2).reshape(s, c)


def from_classes(a, dil):
    if dil == 1:
        return a
    s, c = a.shape
    return a.reshape(dil, s // dil, c).transpose(1, 0, 2).reshape(s, c)


def _const_spec(shape):
    nd = len(shape)
    return pl.BlockSpec(shape, lambda r, n: (0,) * nd)


def band_fwd(name, qkv, dil, cols, bias, sink, inclusive):
    s = qkv.shape[0]
    nb = s // (BLOCK * dil)
    has_sink = sink is not None

    def body(*refs):
        q, kp, kc, vp, vc, b_ref = refs[:6]
        s_ref = refs[6] if has_sink else None
        o_ref, l_ref = refs[-2:]
        first = pl.program_id(1) == 0
        bias_l = tuple(b_ref[h] for h in range(N_HEADS))
        sink_l = tuple(s_ref[h] for h in range(N_HEADS)) if has_sink else None
        o, l = _band_fn(q[...], kp[...], kc[...], vp[...], vc[...], bias_l, sink_l, first, inclusive)
        o_ref[...] = o
        l_ref[...] = l

    out_spec = pl.BlockSpec((BLOCK, HW), lambda r, n: (r * nb + n, 0))
    ins = [qkv] * 5 + [bias] + ([sink] if has_sink else [])
    specs = _band_specs(nb, *cols) + [_const_spec(bias.shape)] + ([_const_spec(sink.shape)] if has_sink else [])
    return pl.pallas_call(
        body, name=name, grid=(dil, nb), in_specs=specs, out_specs=[out_spec, out_spec],
        out_shape=[jax.ShapeDtypeStruct((s, HW), F32)] * 2,
        compiler_params=_params(("parallel", "parallel")))(*ins)


def band_bwd(name, qkv, dil, cols, bias, sink, inclusive, do, dl):
    s = qkv.shape[0]
    nb = s // (BLOCK * dil)
    has_sink, has_dl = sink is not None, dl is not None
    n_in = 6 + int(has_sink) + 1 + int(has_dl)

    def body(*refs):
        q, kp, kc, vp, vc, b_ref = refs[:6]
        s_ref = refs[6] if has_sink else None
        do_ref = refs[6 + int(has_sink)]
        dl_ref = refs[7 + int(has_sink)] if has_dl else None
        dq_ref, dk_ref, dv_ref, db_ref = refs[n_in:n_in + 4]
        ds_ref = refs[n_in + 4] if has_sink else None
        ck, cv = refs[-2:]
        n = pl.program_id(1)

        @pl.when((pl.program_id(0) == 0) & (n == 0))
        def _():
            db_ref[...] = jnp.zeros_like(db_ref)
            if has_sink:
                ds_ref[...] = jnp.zeros_like(ds_ref)

        @pl.when(n < nb)
        def _():
            first = n == 0
            bias_l = tuple(b_ref[h] for h in range(N_HEADS))
            sink_l = tuple(s_ref[h] for h in range(N_HEADS)) if has_sink else ()

            def fn(q_, kp_, kc_, vp_, vc_, b_, s_):
                o, l = _band_fn(q_, kp_, kc_, vp_, vc_, b_, s_ if has_sink else None, first, inclusive)
                return (o, l) if has_dl else (o,)

            _, vjp = jax.vjp(fn, q[...], kp[...], kc[...], vp[...], vc[...], bias_l, sink_l)
            ct = (do_ref[...], dl_ref[...]) if has_dl else (do_ref[...],)
            g = vjp(ct)
            dq_ref[...] = g[0].astype(dq_ref.dtype)

            @pl.when(n > 0)
            def _():
                dk_ref[...] = (ck[...] + g[1]).astype(dk_ref.dtype)
                dv_ref[...] = (cv[...] + g[3]).astype(dv_ref.dtype)

            ck[...] = g[2]
            cv[...] = g[4]
            for h in range(N_HEADS):
                db_ref[h] += g[5][h]
                if has_sink:
                    ds_ref[h] += g[6][h]

        @pl.when(n == nb)
        def _():
            dk_ref[...] = ck[...].astype(dk_ref.dtype)
            dv_ref[...] = cv[...].astype(dv_ref.dtype)

    blk = pl.BlockSpec((BLOCK, HW), lambda r, n: (r * nb + jnp.minimum(n, nb - 1), 0))
    late = pl.BlockSpec((BLOCK, HW), lambda r, n: (r * nb + jnp.maximum(n - 1, 0), 0))
    ins = [qkv] * 5 + [bias] + ([sink] if has_sink else []) + [do] + ([dl] if has_dl else [])
    specs = _band_specs(nb, *cols) + [_const_spec(bias.shape)] \
        + ([_const_spec(sink.shape)] if has_sink else []) + [blk] * (1 + int(has_dl))
    out_shape = [jax.ShapeDtypeStruct((s, HW), BF16)] * 3 + [jax.ShapeDtypeStruct(bias.shape, F32)] \
        + ([jax.ShapeDtypeStruct(sink.shape, F32)] if has_sink else [])
    out_specs = [blk, late, late, _const_spec(bias.shape)] + ([_const_spec(sink.shape)] if has_sink else [])
    res = pl.pallas_call(
        body, name=name, grid=(dil, nb + 1), in_specs=specs, out_specs=out_specs, out_shape=out_shape,
        scratch_shapes=[pltpu.VMEM((BLOCK, HW), F32)] * 2,
        compiler_params=_params(("arbitrary", "arbitrary")))(*ins)
    return list(res[:3]), res[3], (res[4] if has_sink else None)


def rel_grad(name, dbias, bucket):
    def body(d_ref, b_ref, o_ref):
        lane = lax.broadcasted_iota(jnp.int32, (1, LANE), 1)
        bk = b_ref[...]
        for h in range(N_HEADS):
            d = d_ref[h]
            row = jnp.zeros((1, LANE), F32)
            for b in range(REL_BUCKETS):
                tot = jnp.sum(jnp.where(bk == b, d, 0.0))
                row = row + jnp.where(lane == b, tot, 0.0)
            o_ref[h:h + 1, :] = row

    return pl.pallas_call(
        body, name=name, out_shape=jax.ShapeDtypeStruct((N_HEADS, LANE), F32),
        in_specs=[pl.BlockSpec(memory_space=pltpu.VMEM)] * 2, out_specs=pl.BlockSpec(memory_space=pltpu.VMEM),
        compiler_params=_params())(dbias, bucket)


MLA_T = 512
MLA_HPS = 2
MLA_SCALE = (C_NOPE + C_ROPE) ** -0.5


def _diag_mask(t):
    r = lax.broadcasted_iota(jnp.int32, (t, t), 0)
    c = lax.broadcasted_iota(jnp.int32, (t, t), 1)
    return c <= r


def mla_fwd(q, k, v):
    s = q.shape[0]
    t = min(MLA_T, s)
    nq = s // t

    def body(q_ref, k_ref, v_ref, o_ref, l_ref, m_s, l_s, acc_s):
        i = pl.program_id(1)
        m_s[...] = jnp.full_like(m_s, NEG)
        l_s[...] = jnp.zeros_like(l_s)
        acc_s[...] = jnp.zeros_like(acc_s)

        def chunk(j, masked):
            rows = pl.ds(pl.multiple_of(j * t, t), t)
            for hh in range(MLA_HPS):
                ln = slice(hh * LANE, (hh + 1) * LANE)
                sc = _dot(q_ref[:, ln], k_ref[rows, ln], 1, 1)
                if masked:
                    sc = jnp.where(_diag_mask(t), sc, NEG)
                m_old = m_s[:, ln]
                m_new = jnp.maximum(m_old, jnp.max(sc, axis=-1, keepdims=True))
                a = jnp.exp(m_old - m_new)
                p = jnp.exp(sc - m_new[:, 0:1])
                l_s[:, ln] = a * l_s[:, ln] + jnp.sum(p, axis=-1, keepdims=True)
                acc_s[:, ln] = a * acc_s[:, ln] + _dot(p, v_ref[rows, ln], 1, 0)
                m_s[:, ln] = m_new

        def step(j, carry):
            chunk(j, False)
            return carry

        lax.fori_loop(0, i, step, 0)
        chunk(i, True)
        o_ref[...] = acc_s[...] / l_s[...]
        l_ref[...] = m_s[...] + jnp.log(l_s[...])

    w = MLA_HPS * LANE
    qs = pl.BlockSpec((t, w), lambda h, i: (i, h))
    ks = pl.BlockSpec((s, w), lambda h, i: (0, h))
    return pl.pallas_call(
        body, name="mla_fwd", grid=(N_HEADS // MLA_HPS, nq), in_specs=[qs, ks, ks], out_specs=[qs, qs],
        out_shape=[jax.ShapeDtypeStruct((s, N_HEADS * LANE), F32)] * 2,
        scratch_shapes=[pltpu.VMEM((t, w), F32)] * 3,
        compiler_params=_params(("parallel", "parallel")))(q, k, v)


def mla_delta_fn(do, o):
    parts = []
    for a, b in zip(_split_impl(do, N_HEADS), _split_impl(o, N_HEADS)):
        parts.append(jnp.sum(a * b, axis=-1, keepdims=True) + jnp.zeros_like(a))
    return jnp.concatenate(parts, axis=-1), do


def mla_bwd(q, k, v, do, lse, delta):
    s = q.shape[0]
    t = min(MLA_T, s)
    nq = s // t

    def body(q_ref, k_ref, v_ref, do_ref, l_ref, d_ref, dq_ref, dk_ref, dv_ref, dk_s, dv_s):
        j = pl.program_id(1)

        @pl.when(j == 0)
        def _():
            dq_ref[...] = jnp.zeros_like(dq_ref)

        dk_s[...] = jnp.zeros_like(dk_s)
        dv_s[...] = jnp.zeros_like(dv_s)
        kb, vb = k_ref[...], v_ref[...]

        def chunk(i, masked):
            rows = pl.ds(pl.multiple_of(i * t, t), t)
            qi, doi = q_ref[rows, :], do_ref[rows, :]
            sc = _dot(qi, kb, 1, 1)
            if masked:
                sc = jnp.where(_diag_mask(t), sc, NEG)
            p = jnp.exp(sc - l_ref[rows, :][:, 0:1])
            ds = p * (_dot(doi, vb, 1, 1) - d_ref[rows, :][:, 0:1])
            dv_s[...] += _dot(p, doi, 0, 0)
            dk_s[...] += _dot(ds, qi, 0, 0)
            dq_ref[rows, :] += _dot(ds, kb, 1, 0)

        def pair(p, carry):
            chunk(j + 1 + 2 * p, False)
            chunk(j + 2 + 2 * p, False)
            return carry

        chunk(j, True)
        n_off = nq - 1 - j
        lax.fori_loop(0, n_off // 2, pair, 0)

        @pl.when(n_off % 2 == 1)
        def _():
            chunk(nq - 1, False)

        dk_ref[...] = dk_s[...]
        dv_ref[...] = dv_s[...]

    ks = pl.BlockSpec((t, LANE), lambda h, j: (j, h))
    full = pl.BlockSpec((s, LANE), lambda h, j: (0, h))
    return pl.pallas_call(
        body, name="mla_bwd", grid=(N_HEADS, nq), in_specs=[full, ks, ks, full, full, full], out_specs=[full, ks, ks],
        out_shape=[jax.ShapeDtypeStruct((s, N_HEADS * LANE), F32)] * 3,
        scratch_shapes=[pltpu.VMEM((t, LANE), F32)] * 2,
        compiler_params=_params(("parallel", "arbitrary")))(q, k, v, do, lse, delta)


def _peer(k):
    x, y, c = lax.axis_index("x"), lax.axis_index("y"), lax.axis_index("c")
    px, py, pc = (x ^ ((k >> 2) & 1)), (y ^ ((k >> 1) & 1)), (c ^ (k & 1))
    return (px, py, pc), 4 * px + 2 * py + pc


def exchange(name, arrays, scatter):
    n = len(arrays)
    out_shape = [jax.ShapeDtypeStruct(a.shape if scatter else (N_DEV,) + a.shape, a.dtype) for a in arrays]

    def body(*refs):
        ins, outs = refs[:n], refs[n:2 * n]
        send_sems, recv_sems, local_sems = refs[2 * n:]
        me, me_idx = _peer(0)
        mine = [pltpu.make_async_copy(ins[a].at[me_idx] if scatter else ins[a], outs[a].at[me_idx], local_sems.at[a])
                for a in range(n)]
        for cp in mine:
            cp.start()
        sends, recvs = [], []
        for k in range(1, N_DEV):
            peer, peer_idx = _peer(k)
            for a in range(n):
                sem = (k - 1) * n + a
                src = ins[a].at[peer_idx] if scatter else ins[a]
                sends.append(pltpu.make_async_remote_copy(
                    src_ref=src, dst_ref=outs[a].at[me_idx], send_sem=send_sems.at[sem], recv_sem=recv_sems.at[sem],
                    device_id=peer, device_id_type=MESH_ID))
                recvs.append(pltpu.make_async_remote_copy(
                    src_ref=src, dst_ref=outs[a].at[peer_idx], send_sem=send_sems.at[sem], recv_sem=recv_sems.at[sem],
                    device_id=peer, device_id_type=MESH_ID))
        for cp in sends:
            cp.start()
        for cp in recvs:
            cp.wait_recv()
        for cp in sends:
            cp.wait_send()
        for cp in mine:
            cp.wait()

    n_sem = (N_DEV - 1) * n
    return pl.pallas_call(
        body, name=name, out_shape=out_shape,
        in_specs=[pl.BlockSpec(memory_space=pl.ANY)] * n, out_specs=[pl.BlockSpec(memory_space=pl.ANY)] * n,
        scratch_shapes=[pltpu.SemaphoreType.DMA((n_sem,)), pltpu.SemaphoreType.DMA((n_sem,)),
                        pltpu.SemaphoreType.DMA((n,))])(*arrays)


ADAMW_BLOCK_ELEMS = 128 * 1024


def adamw_sum(name, parts, w, m, v):
    r, c = w.shape
    fits = [t for t in range(16, r + 1, 16) if r % t == 0 and t * c <= ADAMW_BLOCK_ELEMS]
    tr = max(fits) if fits else r
    c1 = 1.0 - ADAM_B1 ** ADAM_STEP
    c2 = 1.0 - ADAM_B2 ** ADAM_STEP

    def body(p_ref, w_ref, m_ref, v_ref, g_out, d_out, m_out, v_out):
        g = p_ref[0].astype(F32)
        for d in range(1, N_DEV):
            g = g + p_ref[d].astype(F32)
        mn = ADAM_B1 * m_ref[...] + (1.0 - ADAM_B1) * g
        vn = ADAM_B2 * v_ref[...] + (1.0 - ADAM_B2) * (g * g)
        g_out[...] = g
        m_out[...] = mn
        v_out[...] = vn
        d_out[...] = -ADAM_LR * ((mn / c1) / (jnp.sqrt(vn / c2) + ADAM_EPS) + ADAM_WD * w_ref[...])

    row = pl.BlockSpec((tr, c), lambda i: (i, 0))
    return pl.pallas_call(
        body, name=name, grid=(r // tr,), in_specs=[pl.BlockSpec((N_DEV, tr, c), lambda i: (0, i, 0)), row, row, row],
        out_specs=[row] * 4, out_shape=[jax.ShapeDtypeStruct((r, c), F32)] * 4,
        compiler_params=_params(("parallel",)))(parts, w, m, v)


def _pack(arrays, dtype, row_mult):
    flat = jnp.concatenate([a.astype(dtype).reshape(-1) for a in arrays])
    n = flat.shape[0]
    quantum = row_mult * LANE
    total = -(-n // quantum) * quantum
    return jnp.pad(flat, (0, total - n)).reshape(total // LANE, LANE)


def _unpack(packed, shapes):
    flat = packed.reshape(-1)
    out, pos = [], 0
    for shp in shapes:
        n = int(np.prod(shp))
        out.append(flat[pos:pos + n].reshape(shp))
        pos += n
    return out


def _rows(a):
    return a.reshape(-1, a.shape[-1])


def _unshard(gathered, shard_shape, axis):
    t = jnp.moveaxis(gathered.reshape((N_DEV,) + tuple(shard_shape)), 0, axis)
    return t.reshape(tuple(shard_shape[:axis]) + (N_DEV * shard_shape[axis],) + tuple(shard_shape[axis + 1:]))


def _to_shards(full, axis):
    shp = full.shape
    t = full.reshape(shp[:axis] + (N_DEV, shp[axis] // N_DEV) + shp[axis + 1:])
    t = jnp.moveaxis(t, axis, 0)
    return t.reshape(N_DEV, -1, t.shape[-1])


A_COLS = len(A_GROUPS) * 3 * HW
KV_COLS = B_KV_HEADS * HEAD_DIM
KV_REP = N_HEADS // B_KV_HEADS
O_BQ = A_COLS
O_BK = O_BQ + HW
O_BV = O_BK + KV_COLS
O_CQ = O_BV + KV_COLS
O_CKV = O_CQ + C_Q_RANK
O_KR = O_CKV + C_KV_RANK
O_GATE = O_KR + C_ROPE


def _proj_weight(w_in):
    d = w_in.shape[0]

    def per_q_head(lo):
        t = w_in[:, lo:lo + KV_COLS].reshape(d, B_KV_HEADS, 1, HEAD_DIM)
        return jnp.broadcast_to(t, (d, B_KV_HEADS, KV_REP, HEAD_DIM)).reshape(d, HW)

    kr = jnp.pad(w_in[:, O_KR:O_GATE], ((0, 0), (C_NOPE, LANE - C_NOPE - C_ROPE)))
    return jnp.concatenate([w_in[:, O_GATE:], w_in[:, :O_BK], per_q_head(O_BK), per_q_head(O_BV),
                            w_in[:, O_CQ:O_KR], kr], axis=1)


def _unproj_grad(dw_p, d_model):
    d = dw_p.shape[0]
    g = 3 * d_model
    p_bk = g + O_BK
    p_bv = p_bk + HW
    p_cq = p_bv + HW
    p_kr = p_cq + C_Q_RANK + C_KV_RANK

    def sum_heads(lo):
        return dw_p[:, lo:lo + HW].reshape(d, B_KV_HEADS, KV_REP, HEAD_DIM).sum(axis=2).reshape(d, KV_COLS)

    return jnp.concatenate([dw_p[:, g:p_bk], sum_heads(p_bk), sum_heads(p_bv), dw_p[:, p_cq:p_kr],
                            dw_p[:, p_kr + C_NOPE:p_kr + C_NOPE + C_ROPE], dw_p[:, :g]], axis=1)


def _pad_heads(w, per_head, lo, hi):
    r = w.shape[0]
    t = w.reshape(r, N_HEADS, per_head)[:, :, lo:hi]
    return jnp.pad(t, ((0, 0), (0, 0), (0, LANE - (hi - lo)))).reshape(r, N_HEADS * LANE)


def _unpad_heads(w, width):
    r = w.shape[0]
    return w.reshape(r, N_HEADS, LANE)[:, :, :width]


def _t5_bucket(dist):
    n = jnp.maximum(dist, 0)
    max_exact = REL_BUCKETS // 2
    scaled = jnp.log(jnp.maximum(n, 1).astype(F32) / max_exact) / math.log(REL_MAX_DIST / max_exact)
    large = max_exact + (scaled * (REL_BUCKETS - max_exact)).astype(jnp.int32)
    return jnp.where(n < max_exact, n, jnp.minimum(large, REL_BUCKETS - 1))


def _buckets(dil):
    qi = jnp.arange(BLOCK)[:, None]
    ki = jnp.arange(2 * BLOCK)[None, :]
    return _t5_bucket((BLOCK + qi - ki) * dil).astype(jnp.int32)


def _rope_tables(s):
    pos = jnp.arange(s, dtype=F32)
    inv_freq = ROPE_BASE ** (-jnp.arange(0, C_ROPE, 2, dtype=F32) / C_ROPE)
    ang = pos[:, None] * inv_freq[None, :]
    cos, sin = jnp.cos(ang), jnp.sin(ang)
    ones = jnp.ones((s, C_NOPE), F32)
    tail = LANE - C_NOPE - C_ROPE
    c = jnp.concatenate([ones, cos, cos, jnp.ones((s, tail), F32)], axis=1)
    sn = jnp.concatenate([jnp.zeros((s, C_NOPE), F32), sin, sin, jnp.zeros((s, tail), F32)], axis=1)
    return c, sn


def _cols(d_model):
    g = 3 * d_model // HW
    a = [(g + 3 * i, g + 3 * i + 1, g + 3 * i + 2) for i in range(len(A_GROUPS))]
    b = (g + 9, g + 10, g + 11)
    cq_lane = (g + 12) * HW
    return a, b, cq_lane


def _layer_weights(full, l, d_model):
    w_in_p = _proj_weight(full['w_in'][l])
    per = C_NOPE + C_ROPE
    w_uq = full['w_uq'][l]
    w_ukv = full['w_ukv'][l]
    wb = full['w_branch'][l]
    wb2 = jnp.pad(wb[2].reshape(N_HEADS, HEAD_DIM, d_model), ((0, 0), (0, LANE - HEAD_DIM), (0, 0)))
    return dict(
        w_in_p=w_in_p, wq=_pad_heads(w_uq, per, 0, per), wk=_pad_heads(w_ukv, 2 * C_NOPE, 0, C_NOPE),
        wv=_pad_heads(w_ukv, 2 * C_NOPE, C_NOPE, 2 * C_NOPE), wb0=wb[0], wb1=wb[1],
        wb2=wb2.reshape(N_HEADS * LANE, d_model), w_out=full['w_out'][l], w_up=full['w_ffn_up'][l],
        w_down=full['w_ffn_down'][l], conv_w=full['conv_w'][l])


def _layer_fwd(l, x, x16, lw, small, tabs):
    s, d = x.shape
    a_cols, b_cols, cq_lane = _cols(d)
    tag = f"l{l}_"
    proj = matmul(tag + "proj", x16, lw['w_in_p'], 'nn')
    res = dict(x=x, x16=x16, proj=proj)
    outs, lses, slabs = [], [], []
    for gi, (window, dil) in enumerate(A_GROUPS):
        if dil == 1:
            slab, cols = proj, a_cols[gi]
        else:
            slab, cols = to_classes(proj[:, a_cols[gi][0] * HW:(a_cols[gi][2] + 1) * HW], dil), (0, 1, 2)
        o, lg = band_fwd(tag + f"a{gi}_fwd", slab, dil, cols, tabs['bias'][gi], None, True)
        outs.append(from_classes(o, dil))
        lses.append(from_classes(lg, dil))
        slabs.append((slab, cols))
    res['a_o'], res['a_l'], res['a_slabs'] = outs, lses, slabs
    (o_a,) = rows_fwd(tag + "a_combine", combine_fn, [(o, HW, 0) for o in outs] + [(lg, HW, 0) for lg in lses], [],
                      [(HW, BF16)])
    sink = small['sinks'][l].reshape(N_HEADS, 1, 1)
    o_b, _ = band_fwd(tag + "b_fwd", proj, 1, b_cols, tabs['bias'][3], sink, False)
    gq = small['q_norm_g'][l].reshape(1, -1)
    gkv = small['kv_norm_g'][l].reshape(1, -1)
    (qc,) = rows_fwd(tag + "c_q", mla_q_fn, [(proj, C_Q_RANK, cq_lane // C_Q_RANK), (tabs['cos'], LANE, 0),
                                            (tabs['sin'], LANE, 0)], [gq, lw['wq']], [(N_HEADS * LANE, BF16)])
    ckv_blk = (cq_lane + C_Q_RANK) // LANE
    kc, vc = rows_fwd(tag + "c_kv", mla_kv_fn, [(proj, LANE, ckv_blk), (proj, LANE, ckv_blk + 1), (tabs['cos'], LANE, 0),
                                                (tabs['sin'], LANE, 0)], [gkv, lw['wk'], lw['wv']],
                      [(N_HEADS * LANE, BF16)] * 2)
    o_c, lse_c = mla_fwd(qc, kc, vc)
    res.update(o_a=o_a, o_b=o_b, o_c=o_c, lse_c=lse_c, qc=qc, kc=kc, vc=vc)
    bg = small['b_gate'][l].reshape(3, 1, d)
    ys = [matmul(tag + f"branch{i}", o, lw[f'wb{i}'], 'nn') for i, o in enumerate((o_a, o_b, o_c))]
    (merged,) = rows_fwd(tag + "merge", merge_fn, [(y, d, 0) for y in ys] + [(proj, d, i) for i in range(3)],
                         [bg[0], bg[1], bg[2]], [(d, BF16)])
    mix = matmul(tag + "out_proj", merged, lw['w_out'], 'nn')
    ln1 = [small['ln1_g'][l].reshape(1, d), small['ln1_b'][l].reshape(1, d)]
    x1, x1_16 = rows_fwd(tag + "ln1", ln_twice, [(x, d, 0), (mix, d, 0)], ln1, [(d, F32), (d, BF16)])
    u = matmul(tag + "ffn_up", x1_16, lw['w_up'], 'nn')
    h = conv_glu_fwd(u, lw['conv_w'], small['conv_b'][l].reshape(1, -1))
    ff = matmul(tag + "ffn_down", h, lw['w_down'], 'nn')
    ln2 = [small['ln2_g'][l].reshape(1, d), small['ln2_b'][l].reshape(1, d)]
    x2, x2_16 = rows_fwd(tag + "ln2", ln_twice, [(x1, d, 0), (ff, d, 0)], ln2, [(d, F32), (d, BF16)])
    res.update(merged=merged, mix=mix, x1=x1, x1_16=x1_16, u=u, h=h, ff=ff, ys=ys)
    return x2, x2_16, res


def _layer_bwd(l, dy, res, lw, small, tabs):
    x, proj = res['x'], res['proj']
    s, d = x.shape
    a_cols, b_cols, cq_lane = _cols(d)
    tag = f"l{l}_"
    g = {}
    ln2 = [small['ln2_g'][l].reshape(1, d), small['ln2_b'][l].reshape(1, d)]
    dff, g['ln2_g'], g['ln2_b'] = rows_bwd(tag + "ln2_bwd", ln_fn, [(res['x1'], d, 0), (res['ff'], d, 0)], ln2,
                                           [(dy, d)], [(1, F32)])
    g['w_ffn_down'] = matmul(tag + "dw_down", res['h'], dff, 'tn')
    dh = matmul(tag + "dh", dff, lw['w_down'], 'nt')
    conv_b = small['conv_b'][l].reshape(1, -1)
    dug, duv, dwg, dwv, dbg, dbv = conv_glu_bwd(res['u'], dh, lw['conv_w'], conv_b)
    du = jnp.concatenate([dug, duv], axis=1)
    g['conv_w'] = jnp.concatenate([dwg, dwv], axis=1)
    g['conv_b'] = jnp.concatenate([dbg, dbv], axis=1).reshape(-1)
    g['w_ffn_up'] = matmul(tag + "dw_up", res['x1_16'], du, 'tn')
    dx1 = matmul(tag + "dx1", du, lw['w_up'], 'nt', add=dff, scale=ALPHA)
    ln1 = [small['ln1_g'][l].reshape(1, d), small['ln1_b'][l].reshape(1, d)]
    dmix, g['ln1_g'], g['ln1_b'] = rows_bwd(tag + "ln1_bwd", ln_fn, [(x, d, 0), (res['mix'], d, 0)], ln1,
                                            [(dx1, d)], [(1, F32)])
    g['w_out'] = matmul(tag + "dw_out", res['merged'], dmix, 'tn')
    dmerged = matmul(tag + "dmerged", dmix, lw['w_out'], 'nt')
    bg = small['b_gate'][l].reshape(3, 1, d)
    mrows = [(y, d, 0) for y in res['ys']] + [(proj, d, i) for i in range(3)]
    (dy0, dy1, dy2, dg0, dg1, dg2, db0, db1, db2) = rows_bwd(
        tag + "merge_bwd", merge_fn, mrows, [bg[0], bg[1], bg[2]], [(dmerged, d)], [(i, BF16) for i in range(6)])
    g['b_gate'] = jnp.concatenate([db0, db1, db2], axis=1).reshape(-1)
    branch_in = (res['o_a'], res['o_b'], res['o_c'])
    dwb = [matmul(tag + f"dw_branch{i}", o, dyi, 'tn') for i, (o, dyi) in enumerate(zip(branch_in, (dy0, dy1, dy2)))]
    dwb[2] = dwb[2].reshape(N_HEADS, LANE, d)[:, :HEAD_DIM].reshape(HW, d)
    g['w_branch'] = jnp.stack(dwb, axis=0)
    do_a, do_b, do_c = [matmul(tag + f"do_branch{i}", dyi, lw[f'wb{i}'], 'nt')
                        for i, dyi in enumerate((dy0, dy1, dy2))]
    cw = N_HEADS * LANE
    delta, do_c16 = rows_fwd(tag + "c_delta", mla_delta_fn, [(do_c, cw, 0), (res['o_c'], cw, 0)], [],
                             [(cw, F32), (cw, BF16)])
    dqc, dkc, dvc = mla_bwd(res['qc'], res['kc'], res['vc'], do_c16, res['lse_c'], delta)
    gq = small['q_norm_g'][l].reshape(1, -1)
    gkv = small['kv_norm_g'][l].reshape(1, -1)
    dcq, dgq, dwq = rows_bwd(tag + "c_q_bwd", mla_q_fn,
                             [(proj, C_Q_RANK, cq_lane // C_Q_RANK), (tabs['cos'], LANE, 0), (tabs['sin'], LANE, 0)],
                             [gq, lw['wq']], [(dqc, N_HEADS * LANE)], [(0, BF16)])
    ckv_blk = (cq_lane + C_Q_RANK) // LANE
    dckv, dkr, dgkv, dwk, dwv = rows_bwd(
        tag + "c_kv_bwd", mla_kv_fn,
        [(proj, LANE, ckv_blk), (proj, LANE, ckv_blk + 1), (tabs['cos'], LANE, 0), (tabs['sin'], LANE, 0)],
        [gkv, lw['wk'], lw['wv']], [(dkc, N_HEADS * LANE), (dvc, N_HEADS * LANE)], [(0, BF16), (1, BF16)])
    g['q_norm_g'], g['kv_norm_g'] = dgq.reshape(-1), dgkv.reshape(-1)
    per = C_NOPE + C_ROPE
    g['w_uq'] = _unpad_heads(dwq, per).reshape(C_Q_RANK, N_HEADS * per)
    g['w_ukv'] = jnp.concatenate([_unpad_heads(dwk, C_NOPE), _unpad_heads(dwv, C_NOPE)], axis=2).reshape(
        C_KV_RANK, N_HEADS * 2 * C_NOPE)
    sink = small['sinks'][l].reshape(N_HEADS, 1, 1)
    pieces_b, dbias_b, dsink = band_bwd(tag + "b_bwd", proj, 1, b_cols, tabs['bias'][3], sink, False, do_b, None)
    g['sinks'] = dsink.reshape(-1)
    dbias = [None] * 4
    dbias[3] = dbias_b
    combo = [(o, HW, 0) for o in res['a_o']] + [(lg, HW, 0) for lg in res['a_l']]
    a_cts = rows_bwd(tag + "a_combine_bwd", combine_fn, combo, [], [(do_a, HW)], [(i, F32) for i in range(6)])
    pieces_a = []
    for gi, (window, dil) in enumerate(A_GROUPS):
        slab, cols = res['a_slabs'][gi]
        group, dbias[gi], _ = band_bwd(
            tag + f"a{gi}_bwd", slab, dil, cols, tabs['bias'][gi], None, True, to_classes(a_cts[gi], dil),
            to_classes(a_cts[3 + gi], dil))
        pieces_a += [from_classes(t, dil) for t in group]
    g['dbias'] = dbias
    dproj = jnp.concatenate([dg0, dg1, dg2] + pieces_a + pieces_b + [dcq, dckv, dkr], axis=1)
    dw_in_p = matmul(tag + "dw_in", res['x16'], dproj, 'tn')
    g['w_in'] = _unproj_grad(dw_in_p, d)
    dx = matmul(tag + "dx", dproj, lw['w_in_p'], 'nt', add=dmix, scale=ALPHA)
    return dx, g


def kernel(x, rel_table, w_in, b_gate, sinks, q_norm_g, kv_norm_g, w_uq, w_ukv, w_branch, w_out, ln1_g, ln1_b, w_ffn_up, conv_w, conv_b, w_ffn_down, ln2_g, ln2_b, loss_target, m_rel_table, m_w_in, m_b_gate, m_sinks, m_q_norm_g, m_kv_norm_g, m_w_uq, m_w_ukv, m_w_branch, m_w_out, m_ln1_g, m_ln1_b, m_w_ffn_up, m_conv_w, m_conv_b, m_w_ffn_down, m_ln2_g, m_ln2_b, v_rel_table, v_w_in, v_b_gate, v_sinks, v_q_norm_g, v_kv_norm_g, v_w_uq, v_w_ukv, v_w_branch, v_w_out, v_ln1_g, v_ln1_b, v_w_ffn_up, v_conv_w, v_conv_b, v_w_ffn_down, v_ln2_g, v_ln2_b):
    args = locals()
    w = {n: args[n] for n in WEIGHTS}
    mom = {n: args["m_" + n] for n in WEIGHTS}
    var = {n: args["v_" + n] for n in WEIGHTS}
    s, d = x.shape[1], x.shape[2]
    xs = x.reshape(s, d)
    target = loss_target.reshape(s, d)

    big = [n for n in WEIGHTS if n in SHARDED]
    small_names = [n for n in WEIGHTS if n not in SHARDED]
    wire = lambda n: BF16 if n in GATHER_BF16 else F32
    gathered = exchange("gather_weights", [_rows(w[n]).astype(wire(n)) for n in big], scatter=False)
    full = {n: _unshard(t, w[n].shape, SHARDED[n]) for n, t in zip(big, gathered)}
    small = {n: w[n] for n in small_names}
    loss_part, grad_x, g_full, g_small = _local_grads(xs, target, full, small)
    grad_x = grad_x.reshape(x.shape)

    recv = exchange("exchange_grads", [_to_shards(g_full[n], SHARDED[n]).astype(wire(n)) for n in big], scatter=True)
    big_res = [{}, {}, {}, {}]
    for n, parts in zip(big, recv):
        outs = adamw_sum("adamw_" + n, parts, _rows(w[n]), _rows(mom[n]), _rows(var[n]))
        for k in range(4):
            big_res[k][n] = outs[k].reshape(w[n].shape)

    small_pack = _pack([g_small[n].reshape(w[n].shape) for n in small_names] + [loss_part[0, 0:1]], F32, 8)
    (small_all,) = exchange("gather_small_grads", [small_pack], scatter=False)
    pks = lambda tree: _pack([tree[n] for n in small_names] + [jnp.zeros((1,), F32)], F32, 8)
    small_out = adamw_sum("adamw_replicated", small_all, pks(w), pks(mom), pks(var))
    small_shapes = [w[n].shape for n in small_names] + [(1,)]
    small_res = [dict(zip(small_names + ['loss'], _unpack(o, small_shapes))) for o in small_out]

    loss = small_res[0]['loss'].reshape(())
    out = [loss, grad_x]
    for k in range(4):
        out += [big_res[k][n] if n in SHARDED else small_res[k][n] for n in WEIGHTS]
    return tuple(out)


def _local_grads(xs, target, full, small):
    s, d = xs.shape
    rel_table = small['rel_table']
    big = [n for n in WEIGHTS if n in SHARDED]
    small_names = [n for n in WEIGHTS if n not in SHARDED]

    cos, sin = _rope_tables(s)
    buckets = [_buckets(dil) for _, dil in A_GROUPS] + [_buckets(1)]
    def lookup(table, bucket):
        out = jnp.zeros((N_HEADS,) + bucket.shape, F32)
        for b in range(REL_BUCKETS):
            out = jnp.where((bucket == b)[None], table[b][:, None, None], out)
        return out

    bias = [lookup(rel_table[:, gi * N_HEADS:(gi + 1) * N_HEADS], buckets[gi]) for gi in range(4)]
    tabs = dict(cos=cos, sin=sin, bias=bias)

    act, act16, saved, lws = xs, xs.astype(BF16), [], []
    for l in range(DEPTH):
        lw = _layer_weights(full, l, d)
        act, act16, res = _layer_fwd(l, act, act16, lw, small, tabs)
        saved.append(res)
        lws.append(lw)
    dy, loss_part = loss_head(act, target)

    grads = [None] * DEPTH
    for l in reversed(range(DEPTH)):
        dy, grads[l] = _layer_bwd(l, dy, saved[l], lws[l], small, tabs)
    grad_x = dy

    rel_cols = []
    for gi in range(4):
        both = grads[0]['dbias'][gi] + grads[1]['dbias'][gi] if DEPTH == 2 else grads[0]['dbias'][gi]
        rel_cols.append(rel_grad(f"rel_grad{gi}", both, buckets[gi])[:, :REL_BUCKETS].T)
    g_rel = jnp.concatenate(rel_cols, axis=1)

    def stacked(n):
        return jnp.stack([grads[l][n] for l in range(DEPTH)], axis=0)

    g_full = {n: stacked(n) for n in big}
    g_small = {'rel_table': g_rel}
    for n in small_names:
        if n != 'rel_table':
            g_small[n] = stacked(n)
    return loss_part, grad_x, g_full, g_small
```

```python
import functools
import math

import numpy as np
import jax
import jax.numpy as jnp
from jax import lax
from jax.experimental import pallas as pl
from jax.experimental.pallas import tpu as pltpu

F32 = jnp.float32
BF16 = jnp.bfloat16

N_DEV = 8
DEPTH = 2
HEAD_DIM = 64
BLOCK = 128
A_GROUPS = ((128, 1), (512, 4), (2048, 16))
N_HEADS = 8
HW = N_HEADS * HEAD_DIM
B_KV_HEADS = 2
C_Q_RANK = 256
C_KV_RANK = 128
C_NOPE = 64
C_ROPE = 32
ROPE_BASE = 10000.0
REL_BUCKETS = 32
REL_MAX_DIST = 2048
ALPHA = (2 * DEPTH) ** 0.25
LN_EPS = 1e-5
RMS_EPS = 1e-6
NEG = -1e30
ADAM_LR = 0.001
ADAM_B1 = 0.9
ADAM_B2 = 0.999
ADAM_EPS = 1e-08
ADAM_WD = 0.01
ADAM_STEP = 10

LANE = 128
TM = 256
VMEM_LIMIT = 56 * 1024 * 1024
MESH_ID = pl.DeviceIdType.MESH

WEIGHTS = ('rel_table', 'w_in', 'b_gate', 'sinks', 'q_norm_g', 'kv_norm_g', 'w_uq', 'w_ukv', 'w_branch',
           'w_out', 'ln1_g', 'ln1_b', 'w_ffn_up', 'conv_w', 'conv_b', 'w_ffn_down', 'ln2_g', 'ln2_b')
SHARDED = {'w_in': 2, 'w_uq': 2, 'w_ukv': 2, 'w_branch': 3, 'w_out': 1, 'w_ffn_up': 2, 'conv_w': 2,
           'w_ffn_down': 1}
GATHER_BF16 = ('w_in', 'w_uq', 'w_ukv', 'w_branch', 'w_out', 'w_ffn_up', 'w_ffn_down')


def _params(sem=None):
    return pltpu.CompilerParams(dimension_semantics=sem, vmem_limit_bytes=VMEM_LIMIT)


def _pick(n, target):
    if n <= target:
        return n
    best = None
    for t in range(LANE, target + 1, LANE):
        if n % t == 0:
            best = t
    assert best is not None, (n, target)
    return best


def _dot(a, b, ca, cb):
    return lax.dot_general(a.astype(BF16), b.astype(BF16), (((ca,), (cb,)), ((), ())),
                           preferred_element_type=F32)


@jax.custom_vjp
def mm(a, b):
    return _dot(a, b, 1, 0)


def _mm_fwd(a, b):
    return _dot(a, b, 1, 0), (a, b)


def _mm_bwd(res, g):
    a, b = res
    return _dot(g, b, 1, 1), _dot(a, g, 0, 0)


mm.defvjp(_mm_fwd, _mm_bwd)


@jax.custom_vjp
def mm_nt(a, b):
    return _dot(a, b, 1, 1)


def _mm_nt_fwd(a, b):
    return _dot(a, b, 1, 1), (a, b)


def _mm_nt_bwd(res, g):
    a, b = res
    return _dot(g, b, 1, 0), _dot(g, a, 0, 0)


mm_nt.defvjp(_mm_nt_fwd, _mm_nt_bwd)


def _split_impl(x, n):
    w = x.shape[-1] // n
    return tuple(x[:, i * w:(i + 1) * w] for i in range(n))


@functools.partial(jax.custom_vjp, nondiff_argnums=(1,))
def split_lanes(x, n):
    return _split_impl(x, n)


def _split_fwd(x, n):
    return _split_impl(x, n), None


def _split_bwd(n, _, gs):
    return (jnp.concatenate(gs, axis=-1),)


split_lanes.defvjp(_split_fwd, _split_bwd)


@jax.custom_vjp
def concat_lanes(xs):
    return jnp.concatenate(xs, axis=-1)


def _concat_fwd(xs):
    return jnp.concatenate(xs, axis=-1), len(xs)


def _concat_bwd(n, g):
    return (_split_impl(g, n),)


concat_lanes.defvjp(_concat_fwd, _concat_bwd)


@jax.custom_vjp
def concat_rows(a, b):
    return jnp.concatenate([a, b], axis=0)


def _crow_fwd(a, b):
    return jnp.concatenate([a, b], axis=0), a.shape[0]


def _crow_bwd(na, g):
    return g[:na], g[na:]


concat_rows.defvjp(_crow_fwd, _crow_bwd)


def _rot_half(x):
    n = x.shape[-1]
    lane = lax.broadcasted_iota(jnp.int32, (1, n), 1) % LANE
    lo = (lane >= C_NOPE) & (lane < C_NOPE + C_ROPE // 2)
    hi = (lane >= C_NOPE + C_ROPE // 2) & (lane < C_NOPE + C_ROPE)
    up = pltpu.roll(x, n - C_ROPE // 2, 1)
    dn = pltpu.roll(x, C_ROPE // 2, 1)
    return jnp.where(lo, -up, jnp.where(hi, dn, 0.0))


@jax.custom_vjp
def rope(x, c, s):
    return x * c + _rot_half(x) * s


def _rope_fwd(x, c, s):
    return x * c + _rot_half(x) * s, (c, s)


def _rope_bwd(res, g):
    c, s = res
    return g * c - _rot_half(g * s), jnp.zeros_like(c), jnp.zeros_like(s)


rope.defvjp(_rope_fwd, _rope_bwd)


def _sigmoid(x):
    return 0.5 * jnp.tanh(0.5 * x) + 0.5


def matmul(name, a, b, mode, out_dtype=F32, add=None, scale=1.0):
    if mode == 'nn':
        (m, k), n = a.shape, b.shape[1]
    elif mode == 'nt':
        (m, k), n = a.shape, b.shape[0]
    else:
        (k, m), n = a.shape, b.shape[1]
    tm, tn = _pick(m, 1024), _pick(n, 1024)
    tk = _pick(k, 1536)
    nk = k // tk
    a_spec = pl.BlockSpec((tk, tm), lambda i, j, kk: (kk, i)) if mode == 'tn' else \
        pl.BlockSpec((tm, tk), lambda i, j, kk: (i, kk))
    b_spec = pl.BlockSpec((tn, tk), lambda i, j, kk: (j, kk)) if mode == 'nt' else \
        pl.BlockSpec((tk, tn), lambda i, j, kk: (kk, j))
    o_spec = pl.BlockSpec((tm, tn), lambda i, j, kk: (i, j))
    ca = 0 if mode == 'tn' else 1
    cb = 1 if mode == 'nt' else 0
    has_add = add is not None

    def body(*refs):
        a_ref, b_ref = refs[:2]
        add_ref = refs[2] if has_add else None
        o_ref = refs[2 + int(has_add)]

        def finish(r):
            if has_add:
                r = r + scale * add_ref[...]
            o_ref[...] = r.astype(out_dtype)

        if nk == 1:
            finish(_dot(a_ref[...], b_ref[...], ca, cb))
            return
        acc_ref = refs[-1]
        kk = pl.program_id(2)

        @pl.when(kk == 0)
        def _():
            acc_ref[...] = jnp.zeros_like(acc_ref)

        acc_ref[...] += _dot(a_ref[...], b_ref[...], ca, cb)

        @pl.when(kk == nk - 1)
        def _():
            finish(acc_ref[...])

    ins = [a, b] + ([add] if has_add else [])
    specs = [a_spec, b_spec] + ([o_spec] if has_add else [])
    return pl.pallas_call(
        body, name=name, grid=(m // tm, n // tn, nk), in_specs=specs, out_specs=o_spec,
        out_shape=jax.ShapeDtypeStruct((m, n), out_dtype),
        scratch_shapes=[pltpu.VMEM((tm, tn), F32)] if nk > 1 else [],
        compiler_params=_params(("parallel", "parallel", "arbitrary")))(*ins)


def _row_spec(width, col_block=0, tm=TM):
    return pl.BlockSpec((tm, width), lambda i: (i, col_block))


def _full_spec(shape):
    nd = len(shape)
    return pl.BlockSpec(shape, lambda i: (0,) * nd)


def rows_fwd(name, fn, rows, params, outs, tm=TM):
    s = rows[0][0].shape[0]
    nr, npar = len(rows), len(params)

    def body(*refs):
        r = [refs[i][...].astype(F32) for i in range(nr)]
        p = [refs[nr + i][...] for i in range(npar)]
        res = fn(*r, *p)
        for o_ref, val in zip(refs[nr + npar:], res):
            o_ref[...] = val.astype(o_ref.dtype)

    return pl.pallas_call(
        body, name=name, grid=(s // tm,),
        in_specs=[_row_spec(w, cb, tm) for _, w, cb in rows] + [_full_spec(p.shape) for p in params],
        out_specs=[_row_spec(w, 0, tm) for w, _ in outs],
        out_shape=[jax.ShapeDtypeStruct((s, w), dt) for w, dt in outs],
        compiler_params=_params(("parallel",)))(*[a for a, _, _ in rows], *params)


def rows_bwd(name, fn, rows, params, cts, row_grads, tm=TM):
    s = rows[0][0].shape[0]
    nr, npar, nct, nrg = len(rows), len(params), len(cts), len(row_grads)

    def body(*refs):
        r = [refs[i][...].astype(F32) for i in range(nr)]
        p = [refs[nr + i][...].astype(F32) for i in range(npar)]
        g = tuple(refs[nr + npar + i][...].astype(F32) for i in range(nct))
        _, vjp = jax.vjp(lambda *a: tuple(fn(*a)), *r, *p)
        grads = vjp(g)
        outs = refs[nr + npar + nct:]
        for k, (idx, _) in enumerate(row_grads):
            outs[k][...] = grads[idx].astype(outs[k].dtype)

        @pl.when(pl.program_id(0) == 0)
        def _():
            for k in range(npar):
                outs[nrg + k][...] = jnp.zeros_like(outs[nrg + k])

        for k in range(npar):
            outs[nrg + k][...] += grads[nr + k]

    return pl.pallas_call(
        body, name=name, grid=(s // tm,),
        in_specs=[_row_spec(w, cb, tm) for _, w, cb in rows] + [_full_spec(p.shape) for p in params]
        + [_row_spec(w, 0, tm) for _, w in cts],
        out_specs=[_row_spec(rows[idx][1], 0, tm) for idx, _ in row_grads] + [_full_spec(p.shape) for p in params],
        out_shape=[jax.ShapeDtypeStruct((s, rows[idx][1]), dt) for idx, dt in row_grads]
        + [jax.ShapeDtypeStruct(p.shape, F32) for p in params],
        compiler_params=_params(("arbitrary",)))(*[a for a, _, _ in rows], *params, *[a for a, _ in cts])


def ln_fn(x, r, g, b):
    z = ALPHA * x + r
    mu = jnp.mean(z, axis=-1, keepdims=True)
    zc = z - mu
    var = jnp.mean(zc * zc, axis=-1, keepdims=True)
    return (zc * lax.rsqrt(var + LN_EPS) * g + b,)


def ln_twice(x, r, g, b):
    (y,) = ln_fn(x, r, g, b)
    return y, y


def _rms(x, g):
    return x * lax.rsqrt(jnp.mean(x * x, axis=-1, keepdims=True) + RMS_EPS) * g


def mla_q_fn(cq, c, s, g, wq):
    q = mm(_rms(cq, g), wq)
    return (rope(q, jnp.concatenate([c] * N_HEADS, axis=-1), jnp.concatenate([s] * N_HEADS, axis=-1)) * MLA_SCALE,)


@jax.custom_vjp
def tile_heads(x):
    return jnp.concatenate([x] * N_HEADS, axis=-1)


def _tile_fwd(x):
    return jnp.concatenate([x] * N_HEADS, axis=-1), None


def _tile_bwd(_, g):
    parts = _split_impl(g, N_HEADS)
    acc = parts[0]
    for p in parts[1:]:
        acc = acc + p
    return (acc,)


tile_heads.defvjp(_tile_fwd, _tile_bwd)


def mla_kv_fn(ckv, kr, c, s, g, wk, wv):
    n = _rms(ckv, g)
    return mm(n, wk) + tile_heads(rope(kr, c, s)), mm(n, wv)


def merge_fn(y0, y1, y2, g0, g1, g2, b0, b1, b2):
    return (_sigmoid(g0 + b0) * y0 + _sigmoid(g1 + b1) * y1 + _sigmoid(g2 + b2) * y2,)


def combine_fn(o1, o2, o3, l1, l2, l3):
    mx = lax.stop_gradient(jnp.maximum(jnp.maximum(l1, l2), l3))
    e1, e2, e3 = jnp.exp(l1 - mx), jnp.exp(l2 - mx), jnp.exp(l3 - mx)
    return ((e1 * o1 + e2 * o2 + e3 * o3) / (e1 + e2 + e3),)


def loss_head(y, t):
    s, d = y.shape

    def body(y_ref, t_ref, dy_ref, loss_ref):
        err = y_ref[...] - t_ref[...]
        dy_ref[...] = err * (1.0 / d)

        @pl.when(pl.program_id(0) == 0)
        def _():
            loss_ref[...] = jnp.zeros_like(loss_ref)

        loss_ref[...] += jnp.sum(err * err) * (0.5 / d)

    return pl.pallas_call(
        body, name="loss_head", grid=(s // TM,), in_specs=[_row_spec(d), _row_spec(d)],
        out_specs=[_row_spec(d), _full_spec((8, LANE))],
        out_shape=[jax.ShapeDtypeStruct((s, d), F32), jax.ShapeDtypeStruct((8, LANE), F32)],
        compiler_params=_params(("arbitrary",)))(y, t)


CONV_TN = 256
CONV_TM = 512


def _conv_taps(u_ext, w_ref, b_ref):
    d1, d2 = pltpu.roll(u_ext, 1, 0), pltpu.roll(u_ext, 2, 0)
    return w_ref[0:1, :] * d2 + w_ref[1:2, :] * d1 + w_ref[2:3, :] * u_ext + b_ref[...], d1, d2


def conv_glu_fwd(u, conv_w, conv_b):
    s, f2 = u.shape
    f = f2 // 2
    tn, tm = CONV_TN, min(CONV_TM, s)
    off = f // tn
    hb = tm // 8

    def body(ug, uv, hg, hv, wg, wv, bg, bv, o_ref):
        first = pl.program_id(0) == 0

        def conv(u_ref, h_ref, w_ref, b_ref):
            halo = jnp.where(first, 0.0, h_ref[...])
            ext = jnp.concatenate([halo, u_ref[...]], axis=0)
            return _conv_taps(ext, w_ref, b_ref)[0][8:]

        cg, cv = conv(ug, hg, wg, bg), conv(uv, hv, wv, bv)
        o_ref[...] = (cg * _sigmoid(cg) * cv).astype(o_ref.dtype)

    blk = lambda o: pl.BlockSpec((tm, tn), lambda i, j: (i, j + o))
    halo = lambda o: pl.BlockSpec((8, tn), lambda i, j: (jnp.maximum(i * hb - 1, 0), j + o))
    par = lambda r, o: pl.BlockSpec((r, tn), lambda i, j: (0, j + o))
    return pl.pallas_call(
        body, name="conv_glu_fwd", grid=(s // tm, off),
        in_specs=[blk(0), blk(off), halo(0), halo(off), par(3, 0), par(3, off), par(1, 0), par(1, off)],
        out_specs=pl.BlockSpec((tm, tn), lambda i, j: (i, j)),
        out_shape=jax.ShapeDtypeStruct((s, f), BF16),
        compiler_params=_params(("parallel", "parallel")))(u, u, u, u, conv_w, conv_w, conv_b, conv_b)


def conv_glu_bwd(u, dh, conv_w, conv_b):
    s, f2 = u.shape
    f = f2 // 2
    tn, tm = CONV_TN, min(CONV_TM, s)
    off = f // tn
    hb = tm // 8
    n_rows = s // tm

    def body(ug, uv, pg, pv, ng, nv, dh_ref, dhn_ref, wg, wv, bg, bv, dug, duv, dwg, dwv, dbg, dbv):
        i = pl.program_id(1)
        first, last = i == 0, i == n_rows - 1

        def ext(u_ref, p_ref, n_ref):
            return jnp.concatenate([jnp.where(first, 0.0, p_ref[...]), u_ref[...], n_ref[...]], axis=0)

        eg, ev = ext(ug, pg, ng), ext(uv, pv, nv)
        (cg, eg1, eg2), (cv, ev1, ev2) = _conv_taps(eg, wg, bg), _conv_taps(ev, wv, bv)
        dhe = jnp.concatenate([jnp.zeros((8, tn), F32), dh_ref[...], jnp.where(last, 0.0, dhn_ref[...])], axis=0)
        sg = _sigmoid(cg)
        dcg = dhe * cv * (sg * (1.0 + cg * (1.0 - sg)))
        dcv = dhe * (cg * sg)
        n = tm + 16

        @pl.when(i == 0)
        def _():
            for r in (dwg, dwv, dbg, dbv):
                r[...] = jnp.zeros_like(r)

        def back(dc, e, e1, e2, w_ref, du_ref, dw_ref, db_ref):
            du = w_ref[2:3, :] * dc + w_ref[1:2, :] * pltpu.roll(dc, n - 1, 0) + w_ref[0:1, :] * pltpu.roll(dc, n - 2, 0)
            du_ref[...] = du[8:8 + tm].astype(du_ref.dtype)
            own = dc[8:8 + tm]
            dw_ref[0:1, :] += jnp.sum(own * e2[8:8 + tm], axis=0, keepdims=True)
            dw_ref[1:2, :] += jnp.sum(own * e1[8:8 + tm], axis=0, keepdims=True)
            dw_ref[2:3, :] += jnp.sum(own * e[8:8 + tm], axis=0, keepdims=True)
            db_ref[...] += jnp.sum(own, axis=0, keepdims=True)

        back(dcg, eg, eg1, eg2, wg, dug, dwg, dbg)
        back(dcv, ev, ev1, ev2, wv, duv, dwv, dbv)

    blk = lambda o: pl.BlockSpec((tm, tn), lambda j, i: (i, j + o))
    prev = lambda o: pl.BlockSpec((8, tn), lambda j, i: (jnp.maximum(i * hb - 1, 0), j + o))
    nxt = lambda o: pl.BlockSpec((8, tn), lambda j, i: (jnp.minimum((i + 1) * hb, s // 8 - 1), j + o))
    par = lambda r, o: pl.BlockSpec((r, tn), lambda j, i: (0, j + o))
    return pl.pallas_call(
        body, name="conv_glu_bwd", grid=(off, n_rows),
        in_specs=[blk(0), blk(off), prev(0), prev(off), nxt(0), nxt(off), blk(0), nxt(0),
                  par(3, 0), par(3, off), par(1, 0), par(1, off)],
        out_specs=[blk(0), blk(0), par(3, 0), par(3, 0), par(1, 0), par(1, 0)],
        out_shape=[jax.ShapeDtypeStruct((s, f), BF16)] * 2 + [jax.ShapeDtypeStruct((3, f), F32)] * 2
        + [jax.ShapeDtypeStruct((1, f), F32)] * 2,
        compiler_params=_params(("parallel", "arbitrary")))(u, u, u, u, u, u, dh, dh, conv_w, conv_w, conv_b, conv_b)


def _band_fn(q, kp, kc, vp, vc, bias, sink, first, inclusive):
    k = concat_rows(kp, kc)
    v = concat_rows(vp, vc)
    qi = lax.broadcasted_iota(jnp.int32, (BLOCK, 2 * BLOCK), 0)
    ki = lax.broadcasted_iota(jnp.int32, (BLOCK, 2 * BLOCK), 1)
    step = BLOCK + qi - ki
    valid = (step >= 0) & ((step <= BLOCK) if inclusive else (step < BLOCK))
    valid = valid & (jnp.logical_not(first) | (ki >= BLOCK))
    lane = lax.broadcasted_iota(jnp.int32, (1, LANE), 1)
    scale = HEAD_DIM ** -0.5
    qs, ks, vs = split_lanes(q, 4), split_lanes(k, 4), split_lanes(v, 4)
    o_parts, l_parts = [], []
    for pair in range(4):
        o_acc, l_acc = None, None
        for sub in range(2):
            h = 2 * pair + sub
            mh = ((lane >= sub * HEAD_DIM) & (lane < (sub + 1) * HEAD_DIM)).astype(F32)
            logits = mm_nt(qs[pair] * mh, ks[pair]) * scale + bias[h]
            logits = jnp.where(valid, logits, NEG)
            m = lax.stop_gradient(jnp.max(logits, axis=-1, keepdims=True))
            if sink is not None:
                m = jnp.maximum(m, lax.stop_gradient(sink[h]))
            p = jnp.exp(logits - m)
            den = jnp.sum(p, axis=-1, keepdims=True)
            if sink is not None:
                den = den + jnp.exp(sink[h] - m)
            oh = mm(p / den, vs[pair] * mh)
            lh = (m + jnp.log(den)) * mh
            o_acc = oh if o_acc is None else o_acc + oh
            l_acc = lh if l_acc is None else l_acc + lh
        o_parts.append(o_acc)
        l_parts.append(l_acc)
    return concat_lanes(tuple(o_parts)), concat_lanes(tuple(l_parts))


def _band_specs(nb, cq, ck, cv):
    own = lambda n: jnp.minimum(n, nb - 1)
    prev = lambda n: jnp.maximum(own(n) - 1, 0)
    q = pl.BlockSpec((BLOCK, HW), lambda r, n: (r * nb + own(n), cq))
    kp = pl.BlockSpec((BLOCK, HW), lambda r, n: (r * nb + prev(n), ck))
    kc = pl.BlockSpec((BLOCK, HW), lambda r, n: (r * nb + own(n), ck))
    vp = pl.BlockSpec((BLOCK, HW), lambda r, n: (r * nb + prev(n), cv))
    vc = pl.BlockSpec((BLOCK, HW), lambda r, n: (r * nb + own(n), cv))
    return [q, kp, kc, vp, vc]


def to_classes(a, dil):
    if dil == 1:
        return a
    s, c = a.shape
    return a.reshape(s // dil, dil, c).transpose(1, 0, 2).reshape(s, c)


def from_classes(a, dil):
    if dil == 1:
        return a
    s, c = a.shape
    return a.reshape(dil, s // dil, c).transpose(1, 0, 2).reshape(s, c)


def _const_spec(shape):
    nd = len(shape)
    return pl.BlockSpec(shape, lambda r, n: (0,) * nd)


def band_fwd(name, qkv, dil, cols, bias, sink, inclusive):
    s = qkv.shape[0]
    nb = s // (BLOCK * dil)
    has_sink = sink is not None

    def body(*refs):
        q, kp, kc, vp, vc, b_ref = refs[:6]
        s_ref = refs[6] if has_sink else None
        o_ref, l_ref = refs[-2:]
        first = pl.program_id(1) == 0
        bias_l = tuple(b_ref[h] for h in range(N_HEADS))
        sink_l = tuple(s_ref[h] for h in range(N_HEADS)) if has_sink else None
        o, l = _band_fn(q[...], kp[...], kc[...], vp[...], vc[...], bias_l, sink_l, first, inclusive)
        o_ref[...] = o
        l_ref[...] = l

    out_spec = pl.BlockSpec((BLOCK, HW), lambda r, n: (r * nb + n, 0))
    ins = [qkv] * 5 + [bias] + ([sink] if has_sink else [])
    specs = _band_specs(nb, *cols) + [_const_spec(bias.shape)] + ([_const_spec(sink.shape)] if has_sink else [])
    return pl.pallas_call(
        body, name=name, grid=(dil, nb), in_specs=specs, out_specs=[out_spec, out_spec],
        out_shape=[jax.ShapeDtypeStruct((s, HW), F32)] * 2,
        compiler_params=_params(("parallel", "parallel")))(*ins)


def band_bwd(name, qkv, dil, cols, bias, sink, inclusive, do, dl):
    s = qkv.shape[0]
    nb = s // (BLOCK * dil)
    has_sink, has_dl = sink is not None, dl is not None
    n_in = 6 + int(has_sink) + 1 + int(has_dl)

    def body(*refs):
        q, kp, kc, vp, vc, b_ref = refs[:6]
        s_ref = refs[6] if has_sink else None
        do_ref = refs[6 + int(has_sink)]
        dl_ref = refs[7 + int(has_sink)] if has_dl else None
        dq_ref, dk_ref, dv_ref, db_ref = refs[n_in:n_in + 4]
        ds_ref = refs[n_in + 4] if has_sink else None
        ck, cv = refs[-2:]
        n = pl.program_id(1)

        @pl.when((pl.program_id(0) == 0) & (n == 0))
        def _():
            db_ref[...] = jnp.zeros_like(db_ref)
            if has_sink:
                ds_ref[...] = jnp.zeros_like(ds_ref)

        @pl.when(n < nb)
        def _():
            first = n == 0
            bias_l = tuple(b_ref[h] for h in range(N_HEADS))
            sink_l = tuple(s_ref[h] for h in range(N_HEADS)) if has_sink else ()

            def fn(q_, kp_, kc_, vp_, vc_, b_, s_):
                o, l = _band_fn(q_, kp_, kc_, vp_, vc_, b_, s_ if has_sink else None, first, inclusive)
                return (o, l) if has_dl else (o,)

            _, vjp = jax.vjp(fn, q[...], kp[...], kc[...], vp[...], vc[...], bias_l, sink_l)
            ct = (do_ref[...], dl_ref[...]) if has_dl else (do_ref[...],)
            g = vjp(ct)
            dq_ref[...] = g[0].astype(dq_ref.dtype)

            @pl.when(n > 0)
            def _():
                dk_ref[...] = (ck[...] + g[1]).astype(dk_ref.dtype)
                dv_ref[...] = (cv[...] + g[3]).astype(dv_ref.dtype)

            ck[...] = g[2]
            cv[...] = g[4]
            for h in range(N_HEADS):
                db_ref[h] += g[5][h]
                if has_sink:
                    ds_ref[h] += g[6][h]

        @pl.when(n == nb)
        def _():
            dk_ref[...] = ck[...].astype(dk_ref.dtype)
            dv_ref[...] = cv[...].astype(dv_ref.dtype)

    blk = pl.BlockSpec((BLOCK, HW), lambda r, n: (r * nb + jnp.minimum(n, nb - 1), 0))
    late = pl.BlockSpec((BLOCK, HW), lambda r, n: (r * nb + jnp.maximum(n - 1, 0), 0))
    ins = [qkv] * 5 + [bias] + ([sink] if has_sink else []) + [do] + ([dl] if has_dl else [])
    specs = _band_specs(nb, *cols) + [_const_spec(bias.shape)] \
        + ([_const_spec(sink.shape)] if has_sink else []) + [blk] * (1 + int(has_dl))
    out_shape = [jax.ShapeDtypeStruct((s, HW), BF16)] * 3 + [jax.ShapeDtypeStruct(bias.shape, F32)] \
        + ([jax.ShapeDtypeStruct(sink.shape, F32)] if has_sink else [])
    out_specs = [blk, late, late, _const_spec(bias.shape)] + ([_const_spec(sink.shape)] if has_sink else [])
    res = pl.pallas_call(
        body, name=name, grid=(dil, nb + 1), in_specs=specs, out_specs=out_specs, out_shape=out_shape,
        scratch_shapes=[pltpu.VMEM((BLOCK, HW), F32)] * 2,
        compiler_params=_params(("arbitrary", "arbitrary")))(*ins)
    return list(res[:3]), res[3], (res[4] if has_sink else None)


def rel_grad(name, dbias, bucket):
    def body(d_ref, b_ref, o_ref):
        lane = lax.broadcasted_iota(jnp.int32, (1, LANE), 1)
        bk = b_ref[...]
        for h in range(N_HEADS):
            d = d_ref[h]
            row = jnp.zeros((1, LANE), F32)
            for b in range(REL_BUCKETS):
                tot = jnp.sum(jnp.where(bk == b, d, 0.0))
                row = row + jnp.where(lane == b, tot, 0.0)
            o_ref[h:h + 1, :] = row

    return pl.pallas_call(
        body, name=name, out_shape=jax.ShapeDtypeStruct((N_HEADS, LANE), F32),
        in_specs=[pl.BlockSpec(memory_space=pltpu.VMEM)] * 2, out_specs=pl.BlockSpec(memory_space=pltpu.VMEM),
        compiler_params=_params())(dbias, bucket)


MLA_T = 512
MLA_HPS = 2
MLA_SCALE = (C_NOPE + C_ROPE) ** -0.5


def _diag_mask(t):
    r = lax.broadcasted_iota(jnp.int32, (t, t), 0)
    c = lax.broadcasted_iota(jnp.int32, (t, t), 1)
    return c <= r


def _ride_along(comm, grid):
    if comm is None:
        return [], [], [], [], lambda *_: None
    arrays, scatter = comm
    out_shape, specs, sems = _push_shapes(arrays, scatter)

    def hook(ins, outs, sem_refs):
        ids = [pl.program_id(a) for a in range(len(grid))]
        first, last = ids[0] == 0, ids[0] == grid[0] - 1
        for a in range(1, len(grid)):
            first, last = first & (ids[a] == 0), last & (ids[a] == grid[a] - 1)

        @pl.when(first)
        def _():
            _pushes(ins, outs, sem_refs, scatter).start()

        @pl.when(last)
        def _():
            _pushes(ins, outs, sem_refs, scatter).wait()

    return list(arrays), specs, out_shape, sems, hook


def mla_fwd(name, q, k, v, comm=None):
    s = q.shape[0]
    t = min(MLA_T, s)
    nq = s // t
    grid = (N_HEADS // MLA_HPS, nq)
    c_in, c_specs, c_shapes, c_sems, c_hook = _ride_along(comm, grid)
    nc = len(c_in)

    def body(*refs):
        q_ref, k_ref, v_ref = refs[:3]
        o_ref, l_ref = refs[3 + nc:5 + nc]
        m_s, l_s, acc_s = refs[5 + 2 * nc:8 + 2 * nc]
        c_hook(refs[3:3 + nc], refs[5 + nc:5 + 2 * nc], refs[8 + 2 * nc:])
        i = pl.program_id(1)
        m_s[...] = jnp.full_like(m_s, NEG)
        l_s[...] = jnp.zeros_like(l_s)
        acc_s[...] = jnp.zeros_like(acc_s)

        def chunk(j, masked):
            rows = pl.ds(pl.multiple_of(j * t, t), t)
            for hh in range(MLA_HPS):
                ln = slice(hh * LANE, (hh + 1) * LANE)
                sc = _dot(q_ref[:, ln], k_ref[rows, ln], 1, 1)
                if masked:
                    sc = jnp.where(_diag_mask(t), sc, NEG)
                m_old = m_s[:, ln]
                m_new = jnp.maximum(m_old, jnp.max(sc, axis=-1, keepdims=True))
                a = jnp.exp(m_old - m_new)
                p = jnp.exp(sc - m_new[:, 0:1])
                l_s[:, ln] = a * l_s[:, ln] + jnp.sum(p, axis=-1, keepdims=True)
                acc_s[:, ln] = a * acc_s[:, ln] + _dot(p, v_ref[rows, ln], 1, 0)
                m_s[:, ln] = m_new

        def step(j, carry):
            chunk(j, False)
            return carry

        lax.fori_loop(0, i, step, 0)
        chunk(i, True)
        o_ref[...] = acc_s[...] / l_s[...]
        l_ref[...] = m_s[...] + jnp.log(l_s[...])

    w = MLA_HPS * LANE
    qs = pl.BlockSpec((t, w), lambda h, i: (i, h))
    ks = pl.BlockSpec((s, w), lambda h, i: (0, h))
    res = pl.pallas_call(
        body, name=name, grid=grid, in_specs=[qs, ks, ks] + c_specs, out_specs=[qs, qs] + c_specs,
        out_shape=[jax.ShapeDtypeStruct((s, N_HEADS * LANE), F32)] * 2 + c_shapes,
        scratch_shapes=[pltpu.VMEM((t, w), F32)] * 3 + c_sems,
        compiler_params=_params(("arbitrary", "arbitrary")))(q, k, v, *c_in)
    return res[0], res[1], list(res[2:])


def mla_delta_fn(do, o):
    parts = []
    for a, b in zip(_split_impl(do, N_HEADS), _split_impl(o, N_HEADS)):
        parts.append(jnp.sum(a * b, axis=-1, keepdims=True) + jnp.zeros_like(a))
    return jnp.concatenate(parts, axis=-1), do


def mla_bwd(name, q, k, v, do, lse, delta, comm=None):
    s = q.shape[0]
    t = min(MLA_T, s)
    nq = s // t
    grid = (N_HEADS, nq)
    c_in, c_specs, c_shapes, c_sems, c_hook = _ride_along(comm, grid)
    nc = len(c_in)

    def body(*refs):
        q_ref, k_ref, v_ref, do_ref, l_ref, d_ref = refs[:6]
        dq_ref, dk_ref, dv_ref = refs[6 + nc:9 + nc]
        dk_s, dv_s = refs[9 + 2 * nc:11 + 2 * nc]
        c_hook(refs[6:6 + nc], refs[9 + nc:9 + 2 * nc], refs[11 + 2 * nc:])
        j = pl.program_id(1)

        @pl.when(j == 0)
        def _():
            dq_ref[...] = jnp.zeros_like(dq_ref)

        dk_s[...] = jnp.zeros_like(dk_s)
        dv_s[...] = jnp.zeros_like(dv_s)
        kb, vb = k_ref[...], v_ref[...]

        def chunk(i, masked):
            rows = pl.ds(pl.multiple_of(i * t, t), t)
            qi, doi = q_ref[rows, :], do_ref[rows, :]
            sc = _dot(qi, kb, 1, 1)
            if masked:
                sc = jnp.where(_diag_mask(t), sc, NEG)
            p = jnp.exp(sc - l_ref[rows, :][:, 0:1])
            ds = p * (_dot(doi, vb, 1, 1) - d_ref[rows, :][:, 0:1])
            dv_s[...] += _dot(p, doi, 0, 0)
            dk_s[...] += _dot(ds, qi, 0, 0)
            dq_ref[rows, :] += _dot(ds, kb, 1, 0)

        def pair(p, carry):
            chunk(j + 1 + 2 * p, False)
            chunk(j + 2 + 2 * p, False)
            return carry

        chunk(j, True)
        n_off = nq - 1 - j
        lax.fori_loop(0, n_off // 2, pair, 0)

        @pl.when(n_off % 2 == 1)
        def _():
            chunk(nq - 1, False)

        dk_ref[...] = dk_s[...]
        dv_ref[...] = dv_s[...]

    ks = pl.BlockSpec((t, LANE), lambda h, j: (j, h))
    full = pl.BlockSpec((s, LANE), lambda h, j: (0, h))
    res = pl.pallas_call(
        body, name=name, grid=grid, in_specs=[full, ks, ks, full, full, full] + c_specs,
        out_specs=[full, ks, ks] + c_specs,
        out_shape=[jax.ShapeDtypeStruct((s, N_HEADS * LANE), F32)] * 3 + c_shapes,
        scratch_shapes=[pltpu.VMEM((t, LANE), F32)] * 2 + c_sems,
        compiler_params=_params(("arbitrary", "arbitrary")))(q, k, v, do, lse, delta, *c_in)
    return res[0], res[1], res[2], list(res[3:])


def _peer(k):
    x, y, c = lax.axis_index("x"), lax.axis_index("y"), lax.axis_index("c")
    px, py, pc = (x ^ ((k >> 2) & 1)), (y ^ ((k >> 1) & 1)), (c ^ (k & 1))
    return (px, py, pc), 4 * px + 2 * py + pc


def exchange(name, arrays, scatter):
    n = len(arrays)

    def body(*refs):
        push = _pushes(refs[:n], refs[n:2 * n], refs[2 * n:], scatter)
        push.start()
        push.wait()

    out_shape, specs, sems = _push_shapes(arrays, scatter)
    return pl.pallas_call(body, name=name, out_shape=out_shape, in_specs=specs, out_specs=specs,
                          scratch_shapes=sems)(*arrays)


def _push_shapes(arrays, scatter):
    n = len(arrays)
    out_shape = [jax.ShapeDtypeStruct(a.shape if scatter else (N_DEV,) + a.shape, a.dtype) for a in arrays]
    n_sem = (N_DEV - 1) * n
    sems = [pltpu.SemaphoreType.DMA((n_sem,)), pltpu.SemaphoreType.DMA((n_sem,)), pltpu.SemaphoreType.DMA((n,))]
    return out_shape, [pl.BlockSpec(memory_space=pl.ANY)] * n, sems


class _pushes:
    def __init__(self, ins, outs, sems, scatter):
        self.ins, self.outs, self.sems, self.scatter = ins, outs, sems, scatter

    def _local(self):
        _, me_idx = _peer(0)
        return [pltpu.make_async_copy(self.ins[a].at[me_idx] if self.scatter else self.ins[a],
                                      self.outs[a].at[me_idx], self.sems[2].at[a]) for a in range(len(self.ins))]

    def _remote(self, incoming):
        send_sems, recv_sems, _ = self.sems
        n = len(self.ins)
        _, me_idx = _peer(0)
        copies = []
        for k in range(1, N_DEV):
            peer, peer_idx = _peer(k)
            for a in range(n):
                sem = (k - 1) * n + a
                copies.append(pltpu.make_async_remote_copy(
                    src_ref=self.ins[a].at[peer_idx] if self.scatter else self.ins[a],
                    dst_ref=self.outs[a].at[peer_idx if incoming else me_idx],
                    send_sem=send_sems.at[sem], recv_sem=recv_sems.at[sem], device_id=peer, device_id_type=MESH_ID))
        return copies

    def start(self):
        for cp in self._local() + self._remote(incoming=False):
            cp.start()

    def wait(self):
        for cp in self._remote(incoming=True):
            cp.wait_recv()
        for cp in self._remote(incoming=False):
            cp.wait_send()
        for cp in self._local():
            cp.wait()


ADAMW_BLOCK_ELEMS = 128 * 1024


def adamw_sum(name, parts, w, m, v):
    nl, r, c = w.shape
    fits = [t for t in range(16, r + 1, 16) if r % t == 0 and t * c <= ADAMW_BLOCK_ELEMS]
    tr = max(fits) if fits else r
    c1 = 1.0 - ADAM_B1 ** ADAM_STEP
    c2 = 1.0 - ADAM_B2 ** ADAM_STEP

    def body(*refs):
        p_refs = refs[:nl]
        w_ref, m_ref, v_ref, g_out, d_out, m_out, v_out = refs[nl:]
        for layer in range(nl):
            @pl.when(pl.program_id(0) == layer)
            def _():
                g = p_refs[layer][0].astype(F32)
                for d in range(1, N_DEV):
                    g = g + p_refs[layer][d].astype(F32)
                mn = ADAM_B1 * m_ref[...] + (1.0 - ADAM_B1) * g
                vn = ADAM_B2 * v_ref[...] + (1.0 - ADAM_B2) * (g * g)
                g_out[...] = g
                m_out[...] = mn
                v_out[...] = vn
                d_out[...] = -ADAM_LR * ((mn / c1) / (jnp.sqrt(vn / c2) + ADAM_EPS) + ADAM_WD * w_ref[...])

    row = pl.BlockSpec((None, tr, c), lambda l, i: (l, i, 0))
    return pl.pallas_call(
        body, name=name, grid=(nl, r // tr),
        in_specs=[pl.BlockSpec((N_DEV, tr, c), lambda l, i: (0, i, 0))] * nl + [row, row, row],
        out_specs=[row] * 4, out_shape=[jax.ShapeDtypeStruct((nl, r, c), F32)] * 4,
        compiler_params=_params(("parallel", "parallel")))(*parts, w, m, v)


def _pack(arrays, dtype, row_mult):
    flat = jnp.concatenate([a.astype(dtype).reshape(-1) for a in arrays])
    n = flat.shape[0]
    quantum = row_mult * LANE
    total = -(-n // quantum) * quantum
    return jnp.pad(flat, (0, total - n)).reshape(total // LANE, LANE)


def _unpack(packed, shapes):
    flat = packed.reshape(-1)
    out, pos = [], 0
    for shp in shapes:
        n = int(np.prod(shp))
        out.append(flat[pos:pos + n].reshape(shp))
        pos += n
    return out


def _rows(a):
    return a.reshape(-1, a.shape[-1])


def _unshard(gathered, shard_shape, axis):
    t = jnp.moveaxis(gathered.reshape((N_DEV,) + tuple(shard_shape)), 0, axis)
    return t.reshape(tuple(shard_shape[:axis]) + (N_DEV * shard_shape[axis],) + tuple(shard_shape[axis + 1:]))


def _to_shards(full, axis):
    shp = full.shape
    t = full.reshape(shp[:axis] + (N_DEV, shp[axis] // N_DEV) + shp[axis + 1:])
    t = jnp.moveaxis(t, axis, 0)
    return t.reshape(N_DEV, -1, t.shape[-1])


A_COLS = len(A_GROUPS) * 3 * HW
KV_COLS = B_KV_HEADS * HEAD_DIM
KV_REP = N_HEADS // B_KV_HEADS
O_BQ = A_COLS
O_BK = O_BQ + HW
O_BV = O_BK + KV_COLS
O_CQ = O_BV + KV_COLS
O_CKV = O_CQ + C_Q_RANK
O_KR = O_CKV + C_KV_RANK
O_GATE = O_KR + C_ROPE


def _proj_weight(w_in):
    d = w_in.shape[0]

    def per_q_head(lo):
        t = w_in[:, lo:lo + KV_COLS].reshape(d, B_KV_HEADS, 1, HEAD_DIM)
        return jnp.broadcast_to(t, (d, B_KV_HEADS, KV_REP, HEAD_DIM)).reshape(d, HW)

    kr = jnp.pad(w_in[:, O_KR:O_GATE], ((0, 0), (C_NOPE, LANE - C_NOPE - C_ROPE)))
    return jnp.concatenate([w_in[:, O_GATE:], w_in[:, :O_BK], per_q_head(O_BK), per_q_head(O_BV),
                            w_in[:, O_CQ:O_KR], kr], axis=1)


def _unproj_grad(dw_p, d_model):
    d = dw_p.shape[0]
    g = 3 * d_model
    p_bk = g + O_BK
    p_bv = p_bk + HW
    p_cq = p_bv + HW
    p_kr = p_cq + C_Q_RANK + C_KV_RANK

    def sum_heads(lo):
        return dw_p[:, lo:lo + HW].reshape(d, B_KV_HEADS, KV_REP, HEAD_DIM).sum(axis=2).reshape(d, KV_COLS)

    return jnp.concatenate([dw_p[:, g:p_bk], sum_heads(p_bk), sum_heads(p_bv), dw_p[:, p_cq:p_kr],
                            dw_p[:, p_kr + C_NOPE:p_kr + C_NOPE + C_ROPE], dw_p[:, :g]], axis=1)


def _pad_heads(w, per_head, lo, hi):
    r = w.shape[0]
    t = w.reshape(r, N_HEADS, per_head)[:, :, lo:hi]
    return jnp.pad(t, ((0, 0), (0, 0), (0, LANE - (hi - lo)))).reshape(r, N_HEADS * LANE)


def _unpad_heads(w, width):
    r = w.shape[0]
    return w.reshape(r, N_HEADS, LANE)[:, :, :width]


def _t5_bucket(dist):
    n = jnp.maximum(dist, 0)
    max_exact = REL_BUCKETS // 2
    scaled = jnp.log(jnp.maximum(n, 1).astype(F32) / max_exact) / math.log(REL_MAX_DIST / max_exact)
    large = max_exact + (scaled * (REL_BUCKETS - max_exact)).astype(jnp.int32)
    return jnp.where(n < max_exact, n, jnp.minimum(large, REL_BUCKETS - 1))


def _buckets(dil):
    qi = jnp.arange(BLOCK)[:, None]
    ki = jnp.arange(2 * BLOCK)[None, :]
    return _t5_bucket((BLOCK + qi - ki) * dil).astype(jnp.int32)


def _rope_tables(s):
    pos = jnp.arange(s, dtype=F32)
    inv_freq = ROPE_BASE ** (-jnp.arange(0, C_ROPE, 2, dtype=F32) / C_ROPE)
    ang = pos[:, None] * inv_freq[None, :]
    cos, sin = jnp.cos(ang), jnp.sin(ang)
    ones = jnp.ones((s, C_NOPE), F32)
    tail = LANE - C_NOPE - C_ROPE
    c = jnp.concatenate([ones, cos, cos, jnp.ones((s, tail), F32)], axis=1)
    sn = jnp.concatenate([jnp.zeros((s, C_NOPE), F32), sin, sin, jnp.zeros((s, tail), F32)], axis=1)
    return c, sn


def _cols(d_model):
    g = 3 * d_model // HW
    a = [(g + 3 * i, g + 3 * i + 1, g + 3 * i + 2) for i in range(len(A_GROUPS))]
    b = (g + 9, g + 10, g + 11)
    cq_lane = (g + 12) * HW
    return a, b, cq_lane


def _layer_weights(full, d_model):
    w_in_p = _proj_weight(full['w_in'])
    per = C_NOPE + C_ROPE
    w_uq = full['w_uq']
    w_ukv = full['w_ukv']
    wb = full['w_branch']
    wb2 = jnp.pad(wb[2].reshape(N_HEADS, HEAD_DIM, d_model), ((0, 0), (0, LANE - HEAD_DIM), (0, 0)))
    return dict(
        w_in_p=w_in_p, wq=_pad_heads(w_uq, per, 0, per), wk=_pad_heads(w_ukv, 2 * C_NOPE, 0, C_NOPE),
        wv=_pad_heads(w_ukv, 2 * C_NOPE, C_NOPE, 2 * C_NOPE), wb0=wb[0], wb1=wb[1],
        wb2=wb2.reshape(N_HEADS * LANE, d_model), w_out=full['w_out'], w_up=full['w_ffn_up'],
        w_down=full['w_ffn_down'], conv_w=full['conv_w'])


def _layer_fwd(l, x, x16, lw, small, tabs, comm=None):
    s, d = x.shape
    a_cols, b_cols, cq_lane = _cols(d)
    tag = f"l{l}_"
    proj = matmul(tag + "proj", x16, lw['w_in_p'], 'nn')
    res = dict(x=x, x16=x16, proj=proj)
    outs, lses, slabs = [], [], []
    for gi, (window, dil) in enumerate(A_GROUPS):
        if dil == 1:
            slab, cols = proj, a_cols[gi]
        else:
            slab, cols = to_classes(proj[:, a_cols[gi][0] * HW:(a_cols[gi][2] + 1) * HW], dil), (0, 1, 2)
        o, lg = band_fwd(tag + f"a{gi}_fwd", slab, dil, cols, tabs['bias'][gi], None, True)
        outs.append(from_classes(o, dil))
        lses.append(from_classes(lg, dil))
        slabs.append((slab, cols))
    res['a_o'], res['a_l'], res['a_slabs'] = outs, lses, slabs
    (o_a,) = rows_fwd(tag + "a_combine", combine_fn, [(o, HW, 0) for o in outs] + [(lg, HW, 0) for lg in lses], [],
                      [(HW, BF16)])
    sink = small['sinks'][l].reshape(N_HEADS, 1, 1)
    o_b, _ = band_fwd(tag + "b_fwd", proj, 1, b_cols, tabs['bias'][3], sink, False)
    gq = small['q_norm_g'][l].reshape(1, -1)
    gkv = small['kv_norm_g'][l].reshape(1, -1)
    (qc,) = rows_fwd(tag + "c_q", mla_q_fn, [(proj, C_Q_RANK, cq_lane // C_Q_RANK), (tabs['cos'], LANE, 0),
                                            (tabs['sin'], LANE, 0)], [gq, lw['wq']], [(N_HEADS * LANE, BF16)])
    ckv_blk = (cq_lane + C_Q_RANK) // LANE
    kc, vc = rows_fwd(tag + "c_kv", mla_kv_fn, [(proj, LANE, ckv_blk), (proj, LANE, ckv_blk + 1), (tabs['cos'], LANE, 0),
                                                (tabs['sin'], LANE, 0)], [gkv, lw['wk'], lw['wv']],
                      [(N_HEADS * LANE, BF16)] * 2)
    o_c, lse_c, res['comm'] = mla_fwd(tag + "mla_fwd", qc, kc, vc, comm)
    res.update(o_a=o_a, o_b=o_b, o_c=o_c, lse_c=lse_c, qc=qc, kc=kc, vc=vc)
    bg = small['b_gate'][l].reshape(3, 1, d)
    ys = [matmul(tag + f"branch{i}", o, lw[f'wb{i}'], 'nn') for i, o in enumerate((o_a, o_b, o_c))]
    (merged,) = rows_fwd(tag + "merge", merge_fn, [(y, d, 0) for y in ys] + [(proj, d, i) for i in range(3)],
                         [bg[0], bg[1], bg[2]], [(d, BF16)])
    mix = matmul(tag + "out_proj", merged, lw['w_out'], 'nn')
    ln1 = [small['ln1_g'][l].reshape(1, d), small['ln1_b'][l].reshape(1, d)]
    x1, x1_16 = rows_fwd(tag + "ln1", ln_twice, [(x, d, 0), (mix, d, 0)], ln1, [(d, F32), (d, BF16)])
    u = matmul(tag + "ffn_up", x1_16, lw['w_up'], 'nn')
    h = conv_glu_fwd(u, lw['conv_w'], small['conv_b'][l].reshape(1, -1))
    ff = matmul(tag + "ffn_down", h, lw['w_down'], 'nn')
    ln2 = [small['ln2_g'][l].reshape(1, d), small['ln2_b'][l].reshape(1, d)]
    x2, x2_16 = rows_fwd(tag + "ln2", ln_twice, [(x1, d, 0), (ff, d, 0)], ln2, [(d, F32), (d, BF16)])
    res.update(merged=merged, mix=mix, x1=x1, x1_16=x1_16, u=u, h=h, ff=ff, ys=ys)
    return x2, x2_16, res


def _layer_bwd(l, dy, res, lw, small, tabs, comm=None):
    x, proj = res['x'], res['proj']
    s, d = x.shape
    a_cols, b_cols, cq_lane = _cols(d)
    tag = f"l{l}_"
    g = {}
    ln2 = [small['ln2_g'][l].reshape(1, d), small['ln2_b'][l].reshape(1, d)]
    dff, g['ln2_g'], g['ln2_b'] = rows_bwd(tag + "ln2_bwd", ln_fn, [(res['x1'], d, 0), (res['ff'], d, 0)], ln2,
                                           [(dy, d)], [(1, F32)])
    g['w_ffn_down'] = matmul(tag + "dw_down", res['h'], dff, 'tn')
    dh = matmul(tag + "dh", dff, lw['w_down'], 'nt')
    conv_b = small['conv_b'][l].reshape(1, -1)
    dug, duv, dwg, dwv, dbg, dbv = conv_glu_bwd(res['u'], dh, lw['conv_w'], conv_b)
    du = jnp.concatenate([dug, duv], axis=1)
    g['conv_w'] = jnp.concatenate([dwg, dwv], axis=1)
    g['conv_b'] = jnp.concatenate([dbg, dbv], axis=1).reshape(-1)
    g['w_ffn_up'] = matmul(tag + "dw_up", res['x1_16'], du, 'tn')
    dx1 = matmul(tag + "dx1", du, lw['w_up'], 'nt', add=dff, scale=ALPHA)
    ln1 = [small['ln1_g'][l].reshape(1, d), small['ln1_b'][l].reshape(1, d)]
    dmix, g['ln1_g'], g['ln1_b'] = rows_bwd(tag + "ln1_bwd", ln_fn, [(x, d, 0), (res['mix'], d, 0)], ln1,
                                            [(dx1, d)], [(1, F32)])
    g['w_out'] = matmul(tag + "dw_out", res['merged'], dmix, 'tn')
    dmerged = matmul(tag + "dmerged", dmix, lw['w_out'], 'nt')
    bg = small['b_gate'][l].reshape(3, 1, d)
    mrows = [(y, d, 0) for y in res['ys']] + [(proj, d, i) for i in range(3)]
    (dy0, dy1, dy2, dg0, dg1, dg2, db0, db1, db2) = rows_bwd(
        tag + "merge_bwd", merge_fn, mrows, [bg[0], bg[1], bg[2]], [(dmerged, d)], [(i, BF16) for i in range(6)])
    g['b_gate'] = jnp.concatenate([db0, db1, db2], axis=1).reshape(-1)
    branch_in = (res['o_a'], res['o_b'], res['o_c'])
    dwb = [matmul(tag + f"dw_branch{i}", o, dyi, 'tn') for i, (o, dyi) in enumerate(zip(branch_in, (dy0, dy1, dy2)))]
    dwb[2] = dwb[2].reshape(N_HEADS, LANE, d)[:, :HEAD_DIM].reshape(HW, d)
    g['w_branch'] = jnp.stack(dwb, axis=0)
    do_a, do_b, do_c = [matmul(tag + f"do_branch{i}", dyi, lw[f'wb{i}'], 'nt')
                        for i, dyi in enumerate((dy0, dy1, dy2))]
    cw = N_HEADS * LANE
    delta, do_c16 = rows_fwd(tag + "c_delta", mla_delta_fn, [(do_c, cw, 0), (res['o_c'], cw, 0)], [],
                             [(cw, F32), (cw, BF16)])
    dqc, dkc, dvc, g['comm'] = mla_bwd(tag + "mla_bwd", res['qc'], res['kc'], res['vc'], do_c16, res['lse_c'], delta,
                                       comm)
    gq = small['q_norm_g'][l].reshape(1, -1)
    gkv = small['kv_norm_g'][l].reshape(1, -1)
    dcq, dgq, dwq = rows_bwd(tag + "c_q_bwd", mla_q_fn,
                             [(proj, C_Q_RANK, cq_lane // C_Q_RANK), (tabs['cos'], LANE, 0), (tabs['sin'], LANE, 0)],
                             [gq, lw['wq']], [(dqc, N_HEADS * LANE)], [(0, BF16)])
    ckv_blk = (cq_lane + C_Q_RANK) // LANE
    dckv, dkr, dgkv, dwk, dwv = rows_bwd(
        tag + "c_kv_bwd", mla_kv_fn,
        [(proj, LANE, ckv_blk), (proj, LANE, ckv_blk + 1), (tabs['cos'], LANE, 0), (tabs['sin'], LANE, 0)],
        [gkv, lw['wk'], lw['wv']], [(dkc, N_HEADS * LANE), (dvc, N_HEADS * LANE)], [(0, BF16), (1, BF16)])
    g['q_norm_g'], g['kv_norm_g'] = dgq.reshape(-1), dgkv.reshape(-1)
    per = C_NOPE + C_ROPE
    g['w_uq'] = _unpad_heads(dwq, per).reshape(C_Q_RANK, N_HEADS * per)
    g['w_ukv'] = jnp.concatenate([_unpad_heads(dwk, C_NOPE), _unpad_heads(dwv, C_NOPE)], axis=2).reshape(
        C_KV_RANK, N_HEADS * 2 * C_NOPE)
    sink = small['sinks'][l].reshape(N_HEADS, 1, 1)
    pieces_b, dbias_b, dsink = band_bwd(tag + "b_bwd", proj, 1, b_cols, tabs['bias'][3], sink, False, do_b, None)
    g['sinks'] = dsink.reshape(-1)
    dbias = [None] * 4
    dbias[3] = dbias_b
    combo = [(o, HW, 0) for o in res['a_o']] + [(lg, HW, 0) for lg in res['a_l']]
    a_cts = rows_bwd(tag + "a_combine_bwd", combine_fn, combo, [], [(do_a, HW)], [(i, F32) for i in range(6)])
    pieces_a = []
    for gi, (window, dil) in enumerate(A_GROUPS):
        slab, cols = res['a_slabs'][gi]
        group, dbias[gi], _ = band_bwd(
            tag + f"a{gi}_bwd", slab, dil, cols, tabs['bias'][gi], None, True, to_classes(a_cts[gi], dil),
            to_classes(a_cts[3 + gi], dil))
        pieces_a += [from_classes(t, dil) for t in group]
    g['dbias'] = dbias
    dproj = jnp.concatenate([dg0, dg1, dg2] + pieces_a + pieces_b + [dcq, dckv, dkr], axis=1)
    dw_in_p = matmul(tag + "dw_in", res['x16'], dproj, 'tn')
    g['w_in'] = _unproj_grad(dw_in_p, d)
    dx = matmul(tag + "dx", dproj, lw['w_in_p'], 'nt', add=dmix, scale=ALPHA)
    return dx, g


def kernel(x, rel_table, w_in, b_gate, sinks, q_norm_g, kv_norm_g, w_uq, w_ukv, w_branch, w_out, ln1_g, ln1_b, w_ffn_up, conv_w, conv_b, w_ffn_down, ln2_g, ln2_b, loss_target, m_rel_table, m_w_in, m_b_gate, m_sinks, m_q_norm_g, m_kv_norm_g, m_w_uq, m_w_ukv, m_w_branch, m_w_out, m_ln1_g, m_ln1_b, m_w_ffn_up, m_conv_w, m_conv_b, m_w_ffn_down, m_ln2_g, m_ln2_b, v_rel_table, v_w_in, v_b_gate, v_sinks, v_q_norm_g, v_kv_norm_g, v_w_uq, v_w_ukv, v_w_branch, v_w_out, v_ln1_g, v_ln1_b, v_w_ffn_up, v_conv_w, v_conv_b, v_w_ffn_down, v_ln2_g, v_ln2_b):
    args = locals()
    w = {n: args[n] for n in WEIGHTS}
    mom = {n: args["m_" + n] for n in WEIGHTS}
    var = {n: args["v_" + n] for n in WEIGHTS}
    s, d = x.shape[1], x.shape[2]
    xs = x.reshape(s, d)
    target = loss_target.reshape(s, d)

    big = [n for n in WEIGHTS if n in SHARDED]
    small_names = [n for n in WEIGHTS if n not in SHARDED]
    assert DEPTH == 2
    wire = lambda n: BF16 if n in GATHER_BF16 else F32
    shards = [[_rows(w[n][l]).astype(wire(n)) for n in big] for l in range(DEPTH)]

    def to_full(gathered):
        return {n: _unshard(t, w[n].shape[1:], SHARDED[n] - 1) for n, t in zip(big, gathered)}

    def grad_shards(g):
        return [_to_shards(g[n], SHARDED[n] - 1).astype(wire(n)) for n in big]

    full0 = to_full(exchange("gather_weights0", shards[0], scatter=False))
    small = {n: w[n] for n in small_names}
    loss_part, grad_x, grads, g_rel, recv1 = _local_grads(
        xs, target, [full0, None], small, fused=(shards[1], to_full, grad_shards))
    grad_x = grad_x.reshape(x.shape)
    recv0 = exchange("exchange_grads0", grad_shards(grads[0]), scatter=True)

    big_res = [{}, {}, {}, {}]
    per_layer = lambda t: t.reshape((DEPTH, -1, t.shape[-1]))
    for i, n in enumerate(big):
        outs = adamw_sum("adamw_" + n, [recv0[i], recv1[i]], per_layer(w[n]), per_layer(mom[n]), per_layer(var[n]))
        for k in range(4):
            big_res[k][n] = outs[k].reshape(w[n].shape)

    g_small = {'rel_table': g_rel}
    for n in small_names:
        if n != 'rel_table':
            g_small[n] = jnp.stack([grads[l][n] for l in range(DEPTH)], axis=0)
    small_pack = _pack([g_small[n].reshape(w[n].shape) for n in small_names] + [loss_part[0, 0:1]], F32, 8)
    (small_all,) = exchange("gather_small_grads", [small_pack], scatter=False)
    pks = lambda tree: _pack([tree[n] for n in small_names] + [jnp.zeros((1,), F32)], F32, 8)[None]
    small_out = adamw_sum("adamw_replicated", [small_all], pks(w), pks(mom), pks(var))
    small_shapes = [w[n].shape for n in small_names] + [(1,)]
    small_res = [dict(zip(small_names + ['loss'], _unpack(o, small_shapes))) for o in small_out]

    loss = small_res[0]['loss'].reshape(())
    out = [loss, grad_x]
    for k in range(4):
        out += [big_res[k][n] if n in SHARDED else small_res[k][n] for n in WEIGHTS]
    return tuple(out)


def _local_grads(xs, target, fulls, small, fused=None):
    s, d = xs.shape
    rel_table = small['rel_table']
    fulls = list(fulls)

    cos, sin = _rope_tables(s)
    buckets = [_buckets(dil) for _, dil in A_GROUPS] + [_buckets(1)]
    def lookup(table, bucket):
        out = jnp.zeros((N_HEADS,) + bucket.shape, F32)
        for b in range(REL_BUCKETS):
            out = jnp.where((bucket == b)[None], table[b][:, None, None], out)
        return out

    bias = [lookup(rel_table[:, gi * N_HEADS:(gi + 1) * N_HEADS], buckets[gi]) for gi in range(4)]
    tabs = dict(cos=cos, sin=sin, bias=bias)

    act, act16, saved, lws = xs, xs.astype(BF16), [], []
    for l in range(DEPTH):
        lw = _layer_weights(fulls[l], d)
        comm = (fused[0], False) if fused is not None and l == 0 else None
        act, act16, res = _layer_fwd(l, act, act16, lw, small, tabs, comm)
        if comm is not None:
            fulls[1] = fused[1](res['comm'])
        saved.append(res)
        lws.append(lw)
    dy, loss_part = loss_head(act, target)

    grads, received = [None] * DEPTH, None
    for l in reversed(range(DEPTH)):
        comm = (fused[2](grads[1]), True) if fused is not None and l == 0 else None
        dy, grads[l] = _layer_bwd(l, dy, saved[l], lws[l], small, tabs, comm)
        if comm is not None:
            received = grads[l]['comm']
    grad_x = dy

    rel_cols = []
    for gi in range(4):
        both = grads[0]['dbias'][gi] + grads[1]['dbias'][gi] if DEPTH == 2 else grads[0]['dbias'][gi]
        rel_cols.append(rel_grad(f"rel_grad{gi}", both, buckets[gi])[:, :REL_BUCKETS].T)
    g_rel = jnp.concatenate(rel_cols, axis=1)
    return loss_part, grad_x, grads, g_rel, received
```

```python
import functools
import math

import numpy as np
import jax
import jax.numpy as jnp
from jax import lax
from jax.experimental import pallas as pl
from jax.experimental.pallas import tpu as pltpu

F32 = jnp.float32
BF16 = jnp.bfloat16

N_DEV = 8
DEPTH = 2
HEAD_DIM = 64
BLOCK = 128
A_GROUPS = ((128, 1), (512, 4), (2048, 16))
N_HEADS = 8
HW = N_HEADS * HEAD_DIM
B_KV_HEADS = 2
C_Q_RANK = 256
C_KV_RANK = 128
C_NOPE = 64
C_ROPE = 32
ROPE_BASE = 10000.0
REL_BUCKETS = 32
REL_MAX_DIST = 2048
ALPHA = (2 * DEPTH) ** 0.25
LN_EPS = 1e-5
RMS_EPS = 1e-6
NEG = -1e30
ADAM_LR = 0.001
ADAM_B1 = 0.9
ADAM_B2 = 0.999
ADAM_EPS = 1e-08
ADAM_WD = 0.01
ADAM_STEP = 10

LANE = 128
TM = 256
VMEM_LIMIT = 56 * 1024 * 1024
MESH_ID = pl.DeviceIdType.MESH

WEIGHTS = ('rel_table', 'w_in', 'b_gate', 'sinks', 'q_norm_g', 'kv_norm_g', 'w_uq', 'w_ukv', 'w_branch',
           'w_out', 'ln1_g', 'ln1_b', 'w_ffn_up', 'conv_w', 'conv_b', 'w_ffn_down', 'ln2_g', 'ln2_b')
SHARDED = {'w_in': 2, 'w_uq': 2, 'w_ukv': 2, 'w_branch': 3, 'w_out': 1, 'w_ffn_up': 2, 'conv_w': 2,
           'w_ffn_down': 1}
GATHER_BF16 = ('w_in', 'w_uq', 'w_ukv', 'w_branch', 'w_out', 'w_ffn_up', 'w_ffn_down')


def _params(sem=None):
    return pltpu.CompilerParams(dimension_semantics=sem, vmem_limit_bytes=VMEM_LIMIT)


def _pick(n, target):
    if n <= target:
        return n
    best = None
    for t in range(LANE, target + 1, LANE):
        if n % t == 0:
            best = t
    assert best is not None, (n, target)
    return best


def _dot(a, b, ca, cb):
    return lax.dot_general(a.astype(BF16), b.astype(BF16), (((ca,), (cb,)), ((), ())),
                           preferred_element_type=F32)


@jax.custom_vjp
def mm(a, b):
    return _dot(a, b, 1, 0)


def _mm_fwd(a, b):
    return _dot(a, b, 1, 0), (a, b)


def _mm_bwd(res, g):
    a, b = res
    return _dot(g, b, 1, 1), _dot(a, g, 0, 0)


mm.defvjp(_mm_fwd, _mm_bwd)


@jax.custom_vjp
def mm_nt(a, b):
    return _dot(a, b, 1, 1)


def _mm_nt_fwd(a, b):
    return _dot(a, b, 1, 1), (a, b)


def _mm_nt_bwd(res, g):
    a, b = res
    return _dot(g, b, 1, 0), _dot(g, a, 0, 0)


mm_nt.defvjp(_mm_nt_fwd, _mm_nt_bwd)


def _split_impl(x, n):
    w = x.shape[-1] // n
    return tuple(x[:, i * w:(i + 1) * w] for i in range(n))


@functools.partial(jax.custom_vjp, nondiff_argnums=(1,))
def split_lanes(x, n):
    return _split_impl(x, n)


def _split_fwd(x, n):
    return _split_impl(x, n), None


def _split_bwd(n, _, gs):
    return (jnp.concatenate(gs, axis=-1),)


split_lanes.defvjp(_split_fwd, _split_bwd)


@jax.custom_vjp
def concat_lanes(xs):
    return jnp.concatenate(xs, axis=-1)


def _concat_fwd(xs):
    return jnp.concatenate(xs, axis=-1), len(xs)


def _concat_bwd(n, g):
    return (_split_impl(g, n),)


concat_lanes.defvjp(_concat_fwd, _concat_bwd)


@jax.custom_vjp
def concat_rows(a, b):
    return jnp.concatenate([a, b], axis=0)


def _crow_fwd(a, b):
    return jnp.concatenate([a, b], axis=0), a.shape[0]


def _crow_bwd(na, g):
    return g[:na], g[na:]


concat_rows.defvjp(_crow_fwd, _crow_bwd)


def _rot_half(x):
    n = x.shape[-1]
    lane = lax.broadcasted_iota(jnp.int32, (1, n), 1) % LANE
    lo = (lane >= C_NOPE) & (lane < C_NOPE + C_ROPE // 2)
    hi = (lane >= C_NOPE + C_ROPE // 2) & (lane < C_NOPE + C_ROPE)
    up = pltpu.roll(x, n - C_ROPE // 2, 1)
    dn = pltpu.roll(x, C_ROPE // 2, 1)
    return jnp.where(lo, -up, jnp.where(hi, dn, 0.0))


@jax.custom_vjp
def rope(x, c, s):
    return x * c + _rot_half(x) * s


def _rope_fwd(x, c, s):
    return x * c + _rot_half(x) * s, (c, s)


def _rope_bwd(res, g):
    c, s = res
    return g * c - _rot_half(g * s), jnp.zeros_like(c), jnp.zeros_like(s)


rope.defvjp(_rope_fwd, _rope_bwd)


def _sigmoid(x):
    return 0.5 * jnp.tanh(0.5 * x) + 0.5


MATMUL_VMEM = 44 * 1024 * 1024


def _matmul_tiles(m, n, k, a_bytes, b_bytes, o_bytes, has_add):
    tm, tn, tk = _pick(m, 1024), _pick(n, 2560), _pick(k, 2560)

    def need(tn_, tk_):
        acc = tm * tn_ * 4 if k // tk_ > 1 else 0
        return 2 * (tm * tk_ * a_bytes + tk_ * tn_ * b_bytes + tm * tn_ * o_bytes) + acc + 2 * tm * tn_ * 4 * int(has_add)

    while need(tn, tk) > MATMUL_VMEM:
        if tk >= tn and tk > LANE:
            tk = _pick(k, tk - LANE)
        else:
            tn = _pick(n, tn - LANE)
    return tm, tn, tk


def matmul(name, a, b, mode, out_dtype=F32, add=None, scale=1.0, comm=None):
    if mode == 'nn':
        (m, k), n = a.shape, b.shape[1]
    elif mode == 'nt':
        (m, k), n = a.shape, b.shape[0]
    else:
        (k, m), n = a.shape, b.shape[1]
    tm, tn, tk = _matmul_tiles(m, n, k, a.dtype.itemsize, b.dtype.itemsize, jnp.dtype(out_dtype).itemsize,
                               add is not None)
    nk = k // tk
    grid = (m // tm, n // tn, nk)
    c_in, c_specs, c_shapes, c_sems, c_hook = _ride_along(comm, grid)
    nc = len(c_in)
    a_spec = pl.BlockSpec((tk, tm), lambda i, j, kk: (kk, i)) if mode == 'tn' else \
        pl.BlockSpec((tm, tk), lambda i, j, kk: (i, kk))
    b_spec = pl.BlockSpec((tn, tk), lambda i, j, kk: (j, kk)) if mode == 'nt' else \
        pl.BlockSpec((tk, tn), lambda i, j, kk: (kk, j))
    o_spec = pl.BlockSpec((tm, tn), lambda i, j, kk: (i, j))
    ca = 0 if mode == 'tn' else 1
    cb = 1 if mode == 'nt' else 0
    has_add = add is not None

    n_in = 2 + int(has_add)

    def body(*refs):
        a_ref, b_ref = refs[:2]
        add_ref = refs[2] if has_add else None
        o_ref = refs[n_in + nc]
        c_hook(refs[n_in:n_in + nc], refs[n_in + nc + 1:n_in + 2 * nc + 1], refs[len(refs) - 3:])

        def finish(r):
            if has_add:
                r = r + scale * add_ref[...]
            o_ref[...] = r.astype(out_dtype)

        if nk == 1:
            finish(_dot(a_ref[...], b_ref[...], ca, cb))
            return
        acc_ref = refs[n_in + 2 * nc + 1]
        kk = pl.program_id(2)

        @pl.when(kk == 0)
        def _():
            acc_ref[...] = jnp.zeros_like(acc_ref)

        acc_ref[...] += _dot(a_ref[...], b_ref[...], ca, cb)

        @pl.when(kk == nk - 1)
        def _():
            finish(acc_ref[...])

    ins = [a, b] + ([add] if has_add else []) + c_in
    specs = [a_spec, b_spec] + ([o_spec] if has_add else []) + c_specs
    res = pl.pallas_call(
        body, name=name, grid=grid, in_specs=specs, out_specs=[o_spec] + c_specs,
        out_shape=[jax.ShapeDtypeStruct((m, n), out_dtype)] + c_shapes,
        scratch_shapes=([pltpu.VMEM((tm, tn), F32)] if nk > 1 else []) + c_sems,
        compiler_params=_params(("arbitrary",) * 3 if comm is not None else ("parallel", "parallel", "arbitrary")))(*ins)
    return res[0] if comm is None else (res[0], list(res[1:]))


def _row_spec(width, col_block=0, tm=TM):
    return pl.BlockSpec((tm, width), lambda i: (i, col_block))


def _full_spec(shape):
    nd = len(shape)
    return pl.BlockSpec(shape, lambda i: (0,) * nd)


def rows_fwd(name, fn, rows, params, outs, tm=TM):
    s = rows[0][0].shape[0]
    nr, npar = len(rows), len(params)

    def body(*refs):
        r = [refs[i][...].astype(F32) for i in range(nr)]
        p = [refs[nr + i][...] for i in range(npar)]
        res = fn(*r, *p)
        for o_ref, val in zip(refs[nr + npar:], res):
            o_ref[...] = val.astype(o_ref.dtype)

    return pl.pallas_call(
        body, name=name, grid=(s // tm,),
        in_specs=[_row_spec(w, cb, tm) for _, w, cb in rows] + [_full_spec(p.shape) for p in params],
        out_specs=[_row_spec(w, 0, tm) for w, _ in outs],
        out_shape=[jax.ShapeDtypeStruct((s, w), dt) for w, dt in outs],
        compiler_params=_params(("parallel",)))(*[a for a, _, _ in rows], *params)


def rows_bwd(name, fn, rows, params, cts, row_grads, tm=TM):
    s = rows[0][0].shape[0]
    nr, npar, nct, nrg = len(rows), len(params), len(cts), len(row_grads)

    def body(*refs):
        r = [refs[i][...].astype(F32) for i in range(nr)]
        p = [refs[nr + i][...].astype(F32) for i in range(npar)]
        g = tuple(refs[nr + npar + i][...].astype(F32) for i in range(nct))
        _, vjp = jax.vjp(lambda *a: tuple(fn(*a)), *r, *p)
        grads = vjp(g)
        outs = refs[nr + npar + nct:]
        for k, (idx, _) in enumerate(row_grads):
            outs[k][...] = grads[idx].astype(outs[k].dtype)

        @pl.when(pl.program_id(0) == 0)
        def _():
            for k in range(npar):
                outs[nrg + k][...] = jnp.zeros_like(outs[nrg + k])

        for k in range(npar):
            outs[nrg + k][...] += grads[nr + k]

    return pl.pallas_call(
        body, name=name, grid=(s // tm,),
        in_specs=[_row_spec(w, cb, tm) for _, w, cb in rows] + [_full_spec(p.shape) for p in params]
        + [_row_spec(w, 0, tm) for _, w in cts],
        out_specs=[_row_spec(rows[idx][1], 0, tm) for idx, _ in row_grads] + [_full_spec(p.shape) for p in params],
        out_shape=[jax.ShapeDtypeStruct((s, rows[idx][1]), dt) for idx, dt in row_grads]
        + [jax.ShapeDtypeStruct(p.shape, F32) for p in params],
        compiler_params=_params(("arbitrary",)))(*[a for a, _, _ in rows], *params, *[a for a, _ in cts])


def ln_fn(x, r, g, b):
    z = ALPHA * x + r
    mu = jnp.mean(z, axis=-1, keepdims=True)
    zc = z - mu
    var = jnp.mean(zc * zc, axis=-1, keepdims=True)
    return (zc * lax.rsqrt(var + LN_EPS) * g + b,)


def ln_twice(x, r, g, b):
    (y,) = ln_fn(x, r, g, b)
    return y, y


def _rms(x, g):
    return x * lax.rsqrt(jnp.mean(x * x, axis=-1, keepdims=True) + RMS_EPS) * g


def mla_q_fn(cq, c, s, g, wq):
    q = mm(_rms(cq, g), wq)
    return (rope(q, jnp.concatenate([c] * N_HEADS, axis=-1), jnp.concatenate([s] * N_HEADS, axis=-1)) * MLA_SCALE,)


@jax.custom_vjp
def tile_heads(x):
    return jnp.concatenate([x] * N_HEADS, axis=-1)


def _tile_fwd(x):
    return jnp.concatenate([x] * N_HEADS, axis=-1), None


def _tile_bwd(_, g):
    parts = _split_impl(g, N_HEADS)
    acc = parts[0]
    for p in parts[1:]:
        acc = acc + p
    return (acc,)


tile_heads.defvjp(_tile_fwd, _tile_bwd)


def mla_kv_fn(ckv, kr, c, s, g, wk, wv):
    n = _rms(ckv, g)
    return mm(n, wk) + tile_heads(rope(kr, c, s)), mm(n, wv)


def merge_fn(y0, y1, y2, g0, g1, g2, b0, b1, b2):
    return (_sigmoid(g0 + b0) * y0 + _sigmoid(g1 + b1) * y1 + _sigmoid(g2 + b2) * y2,)


def combine_fn(o1, o2, o3, l1, l2, l3):
    mx = lax.stop_gradient(jnp.maximum(jnp.maximum(l1, l2), l3))
    e1, e2, e3 = jnp.exp(l1 - mx), jnp.exp(l2 - mx), jnp.exp(l3 - mx)
    return ((e1 * o1 + e2 * o2 + e3 * o3) / (e1 + e2 + e3),)


def loss_head(y, t):
    s, d = y.shape

    def body(y_ref, t_ref, dy_ref, loss_ref):
        err = y_ref[...] - t_ref[...]
        dy_ref[...] = err * (1.0 / d)

        @pl.when(pl.program_id(0) == 0)
        def _():
            loss_ref[...] = jnp.zeros_like(loss_ref)

        loss_ref[...] += jnp.sum(err * err) * (0.5 / d)

    return pl.pallas_call(
        body, name="loss_head", grid=(s // TM,), in_specs=[_row_spec(d), _row_spec(d)],
        out_specs=[_row_spec(d), _full_spec((8, LANE))],
        out_shape=[jax.ShapeDtypeStruct((s, d), F32), jax.ShapeDtypeStruct((8, LANE), F32)],
        compiler_params=_params(("arbitrary",)))(y, t)


CONV_TN = 256
CONV_TM = 512


def _conv_taps(u_ext, w_ref, b_ref):
    d1, d2 = pltpu.roll(u_ext, 1, 0), pltpu.roll(u_ext, 2, 0)
    return w_ref[0:1, :] * d2 + w_ref[1:2, :] * d1 + w_ref[2:3, :] * u_ext + b_ref[...], d1, d2


def conv_glu_fwd(u, conv_w, conv_b):
    s, f2 = u.shape
    f = f2 // 2
    tn, tm = CONV_TN, min(CONV_TM, s)
    off = f // tn
    hb = tm // 8

    def body(ug, uv, hg, hv, wg, wv, bg, bv, o_ref):
        first = pl.program_id(0) == 0

        def conv(u_ref, h_ref, w_ref, b_ref):
            halo = jnp.where(first, 0.0, h_ref[...])
            ext = jnp.concatenate([halo, u_ref[...]], axis=0)
            return _conv_taps(ext, w_ref, b_ref)[0][8:]

        cg, cv = conv(ug, hg, wg, bg), conv(uv, hv, wv, bv)
        o_ref[...] = (cg * _sigmoid(cg) * cv).astype(o_ref.dtype)

    blk = lambda o: pl.BlockSpec((tm, tn), lambda i, j: (i, j + o))
    halo = lambda o: pl.BlockSpec((8, tn), lambda i, j: (jnp.maximum(i * hb - 1, 0), j + o))
    par = lambda r, o: pl.BlockSpec((r, tn), lambda i, j: (0, j + o))
    return pl.pallas_call(
        body, name="conv_glu_fwd", grid=(s // tm, off),
        in_specs=[blk(0), blk(off), halo(0), halo(off), par(3, 0), par(3, off), par(1, 0), par(1, off)],
        out_specs=pl.BlockSpec((tm, tn), lambda i, j: (i, j)),
        out_shape=jax.ShapeDtypeStruct((s, f), BF16),
        compiler_params=_params(("parallel", "parallel")))(u, u, u, u, conv_w, conv_w, conv_b, conv_b)


def conv_glu_bwd(u, dh, conv_w, conv_b):
    s, f2 = u.shape
    f = f2 // 2
    tn, tm = CONV_TN, min(CONV_TM, s)
    off = f // tn
    hb = tm // 8
    n_rows = s // tm

    def body(ug, uv, pg, pv, ng, nv, dh_ref, dhn_ref, wg, wv, bg, bv, dug, duv, dwg, dwv, dbg, dbv):
        i = pl.program_id(1)
        first, last = i == 0, i == n_rows - 1

        def ext(u_ref, p_ref, n_ref):
            return jnp.concatenate([jnp.where(first, 0.0, p_ref[...]), u_ref[...], n_ref[...]], axis=0)

        eg, ev = ext(ug, pg, ng), ext(uv, pv, nv)
        (cg, eg1, eg2), (cv, ev1, ev2) = _conv_taps(eg, wg, bg), _conv_taps(ev, wv, bv)
        dhe = jnp.concatenate([jnp.zeros((8, tn), F32), dh_ref[...], jnp.where(last, 0.0, dhn_ref[...])], axis=0)
        sg = _sigmoid(cg)
        dcg = dhe * cv * (sg * (1.0 + cg * (1.0 - sg)))
        dcv = dhe * (cg * sg)
        n = tm + 16

        @pl.when(i == 0)
        def _():
            for r in (dwg, dwv, dbg, dbv):
                r[...] = jnp.zeros_like(r)

        def back(dc, e, e1, e2, w_ref, du_ref, dw_ref, db_ref):
            du = w_ref[2:3, :] * dc + w_ref[1:2, :] * pltpu.roll(dc, n - 1, 0) + w_ref[0:1, :] * pltpu.roll(dc, n - 2, 0)
            du_ref[...] = du[8:8 + tm].astype(du_ref.dtype)
            own = dc[8:8 + tm]
            dw_ref[0:1, :] += jnp.sum(own * e2[8:8 + tm], axis=0, keepdims=True)
            dw_ref[1:2, :] += jnp.sum(own * e1[8:8 + tm], axis=0, keepdims=True)
            dw_ref[2:3, :] += jnp.sum(own * e[8:8 + tm], axis=0, keepdims=True)
            db_ref[...] += jnp.sum(own, axis=0, keepdims=True)

        back(dcg, eg, eg1, eg2, wg, dug, dwg, dbg)
        back(dcv, ev, ev1, ev2, wv, duv, dwv, dbv)

    blk = lambda o: pl.BlockSpec((tm, tn), lambda j, i: (i, j + o))
    prev = lambda o: pl.BlockSpec((8, tn), lambda j, i: (jnp.maximum(i * hb - 1, 0), j + o))
    nxt = lambda o: pl.BlockSpec((8, tn), lambda j, i: (jnp.minimum((i + 1) * hb, s // 8 - 1), j + o))
    par = lambda r, o: pl.BlockSpec((r, tn), lambda j, i: (0, j + o))
    return pl.pallas_call(
        body, name="conv_glu_bwd", grid=(off, n_rows),
        in_specs=[blk(0), blk(off), prev(0), prev(off), nxt(0), nxt(off), blk(0), nxt(0),
                  par(3, 0), par(3, off), par(1, 0), par(1, off)],
        out_specs=[blk(0), blk(0), par(3, 0), par(3, 0), par(1, 0), par(1, 0)],
        out_shape=[jax.ShapeDtypeStruct((s, f), BF16)] * 2 + [jax.ShapeDtypeStruct((3, f), F32)] * 2
        + [jax.ShapeDtypeStruct((1, f), F32)] * 2,
        compiler_params=_params(("parallel", "arbitrary")))(u, u, u, u, u, u, dh, dh, conv_w, conv_w, conv_b, conv_b)


def _band_fn(q, kp, kc, vp, vc, bias, sink, first, inclusive):
    k = concat_rows(kp, kc)
    v = concat_rows(vp, vc)
    qi = lax.broadcasted_iota(jnp.int32, (BLOCK, 2 * BLOCK), 0)
    ki = lax.broadcasted_iota(jnp.int32, (BLOCK, 2 * BLOCK), 1)
    step = BLOCK + qi - ki
    valid = (step >= 0) & ((step <= BLOCK) if inclusive else (step < BLOCK))
    valid = valid & (jnp.logical_not(first) | (ki >= BLOCK))
    lane = lax.broadcasted_iota(jnp.int32, (1, LANE), 1)
    scale = HEAD_DIM ** -0.5
    qs, ks, vs = split_lanes(q, 4), split_lanes(k, 4), split_lanes(v, 4)
    o_parts, l_parts = [], []
    for pair in range(4):
        o_acc, l_acc = None, None
        for sub in range(2):
            h = 2 * pair + sub
            mh = ((lane >= sub * HEAD_DIM) & (lane < (sub + 1) * HEAD_DIM)).astype(F32)
            logits = mm_nt(qs[pair] * mh, ks[pair]) * scale + bias[h]
            logits = jnp.where(valid, logits, NEG)
            m = lax.stop_gradient(jnp.max(logits, axis=-1, keepdims=True))
            if sink is not None:
                m = jnp.maximum(m, lax.stop_gradient(sink[h]))
            p = jnp.exp(logits - m)
            den = jnp.sum(p, axis=-1, keepdims=True)
            if sink is not None:
                den = den + jnp.exp(sink[h] - m)
            oh = mm(p / den, vs[pair] * mh)
            lh = (m + jnp.log(den)) * mh
            o_acc = oh if o_acc is None else o_acc + oh
            l_acc = lh if l_acc is None else l_acc + lh
        o_parts.append(o_acc)
        l_parts.append(l_acc)
    return concat_lanes(tuple(o_parts)), concat_lanes(tuple(l_parts))


def _band_specs(nb, cq, ck, cv):
    own = lambda n: jnp.minimum(n, nb - 1)
    prev = lambda n: jnp.maximum(own(n) - 1, 0)
    q = pl.BlockSpec((BLOCK, HW), lambda r, n: (r * nb + own(n), cq))
    kp = pl.BlockSpec((BLOCK, HW), lambda r, n: (r * nb + prev(n), ck))
    kc = pl.BlockSpec((BLOCK, HW), lambda r, n: (r * nb + own(n), ck))
    vp = pl.BlockSpec((BLOCK, HW), lambda r, n: (r * nb + prev(n), cv))
    vc = pl.BlockSpec((BLOCK, HW), lambda r, n: (r * nb + own(n), cv))
    return [q, kp, kc, vp, vc]


def to_classes(a, dil):
    if dil == 1:
        return a
    s, c = a.shape
    return a.reshape(s // dil, dil, c).transpose(1, 0, 2).reshape(s, c)


def from_classes(a, dil):
    if dil == 1:
        return a
    s, c = a.shape
    return a.reshape(dil, s // dil, c).transpose(1, 0, 2).reshape(s, c)


def _const_spec(shape):
    nd = len(shape)
    return pl.BlockSpec(shape, lambda r, n: (0,) * nd)


def band_fwd(name, qkv, dil, cols, bias, sink, inclusive):
    s = qkv.shape[0]
    nb = s // (BLOCK * dil)
    has_sink = sink is not None

    def body(*refs):
        q, kp, kc, vp, vc, b_ref = refs[:6]
        s_ref = refs[6] if has_sink else None
        o_ref, l_ref = refs[-2:]
        first = pl.program_id(1) == 0
        bias_l = tuple(b_ref[h] for h in range(N_HEADS))
        sink_l = tuple(s_ref[h] for h in range(N_HEADS)) if has_sink else None
        o, l = _band_fn(q[...], kp[...], kc[...], vp[...], vc[...], bias_l, sink_l, first, inclusive)
        o_ref[...] = o
        l_ref[...] = l

    out_spec = pl.BlockSpec((BLOCK, HW), lambda r, n: (r * nb + n, 0))
    ins = [qkv] * 5 + [bias] + ([sink] if has_sink else [])
    specs = _band_specs(nb, *cols) + [_const_spec(bias.shape)] + ([_const_spec(sink.shape)] if has_sink else [])
    return pl.pallas_call(
        body, name=name, grid=(dil, nb), in_specs=specs, out_specs=[out_spec, out_spec],
        out_shape=[jax.ShapeDtypeStruct((s, HW), F32)] * 2,
        compiler_params=_params(("parallel", "parallel")))(*ins)


def band_bwd(name, qkv, dil, cols, bias, sink, inclusive, do, dl):
    s = qkv.shape[0]
    nb = s // (BLOCK * dil)
    has_sink, has_dl = sink is not None, dl is not None
    n_in = 6 + int(has_sink) + 1 + int(has_dl)

    def body(*refs):
        q, kp, kc, vp, vc, b_ref = refs[:6]
        s_ref = refs[6] if has_sink else None
        do_ref = refs[6 + int(has_sink)]
        dl_ref = refs[7 + int(has_sink)] if has_dl else None
        dq_ref, dk_ref, dv_ref, db_ref = refs[n_in:n_in + 4]
        ds_ref = refs[n_in + 4] if has_sink else None
        ck, cv = refs[-2:]
        n = pl.program_id(1)

        @pl.when((pl.program_id(0) == 0) & (n == 0))
        def _():
            db_ref[...] = jnp.zeros_like(db_ref)
            if has_sink:
                ds_ref[...] = jnp.zeros_like(ds_ref)

        @pl.when(n < nb)
        def _():
            first = n == 0
            bias_l = tuple(b_ref[h] for h in range(N_HEADS))
            sink_l = tuple(s_ref[h] for h in range(N_HEADS)) if has_sink else ()

            def fn(q_, kp_, kc_, vp_, vc_, b_, s_):
                o, l = _band_fn(q_, kp_, kc_, vp_, vc_, b_, s_ if has_sink else None, first, inclusive)
                return (o, l) if has_dl else (o,)

            _, vjp = jax.vjp(fn, q[...], kp[...], kc[...], vp[...], vc[...], bias_l, sink_l)
            ct = (do_ref[...], dl_ref[...]) if has_dl else (do_ref[...],)
            g = vjp(ct)
            dq_ref[...] = g[0].astype(dq_ref.dtype)

            @pl.when(n > 0)
            def _():
                dk_ref[...] = (ck[...] + g[1]).astype(dk_ref.dtype)
                dv_ref[...] = (cv[...] + g[3]).astype(dv_ref.dtype)

            ck[...] = g[2]
            cv[...] = g[4]
            for h in range(N_HEADS):
                db_ref[h] += g[5][h]
                if has_sink:
                    ds_ref[h] += g[6][h]

        @pl.when(n == nb)
        def _():
            dk_ref[...] = ck[...].astype(dk_ref.dtype)
            dv_ref[...] = cv[...].astype(dv_ref.dtype)

    blk = pl.BlockSpec((BLOCK, HW), lambda r, n: (r * nb + jnp.minimum(n, nb - 1), 0))
    late = pl.BlockSpec((BLOCK, HW), lambda r, n: (r * nb + jnp.maximum(n - 1, 0), 0))
    ins = [qkv] * 5 + [bias] + ([sink] if has_sink else []) + [do] + ([dl] if has_dl else [])
    specs = _band_specs(nb, *cols) + [_const_spec(bias.shape)] \
        + ([_const_spec(sink.shape)] if has_sink else []) + [blk] * (1 + int(has_dl))
    out_shape = [jax.ShapeDtypeStruct((s, HW), BF16)] * 3 + [jax.ShapeDtypeStruct(bias.shape, F32)] \
        + ([jax.ShapeDtypeStruct(sink.shape, F32)] if has_sink else [])
    out_specs = [blk, late, late, _const_spec(bias.shape)] + ([_const_spec(sink.shape)] if has_sink else [])
    res = pl.pallas_call(
        body, name=name, grid=(dil, nb + 1), in_specs=specs, out_specs=out_specs, out_shape=out_shape,
        scratch_shapes=[pltpu.VMEM((BLOCK, HW), F32)] * 2,
        compiler_params=_params(("arbitrary", "arbitrary")))(*ins)
    return list(res[:3]), res[3], (res[4] if has_sink else None)


def rel_grad(name, dbias, bucket):
    def body(d_ref, b_ref, o_ref):
        lane = lax.broadcasted_iota(jnp.int32, (1, LANE), 1)
        bk = b_ref[...]
        for h in range(N_HEADS):
            d = d_ref[h]
            row = jnp.zeros((1, LANE), F32)
            for b in range(REL_BUCKETS):
                tot = jnp.sum(jnp.where(bk == b, d, 0.0))
                row = row + jnp.where(lane == b, tot, 0.0)
            o_ref[h:h + 1, :] = row

    return pl.pallas_call(
        body, name=name, out_shape=jax.ShapeDtypeStruct((N_HEADS, LANE), F32),
        in_specs=[pl.BlockSpec(memory_space=pltpu.VMEM)] * 2, out_specs=pl.BlockSpec(memory_space=pltpu.VMEM),
        compiler_params=_params())(dbias, bucket)


MLA_T = 512
MLA_HPS = 2
MLA_SCALE = (C_NOPE + C_ROPE) ** -0.5


def _diag_mask(t):
    r = lax.broadcasted_iota(jnp.int32, (t, t), 0)
    c = lax.broadcasted_iota(jnp.int32, (t, t), 1)
    return c <= r


def _ride_along(comm, grid):
    if comm is None:
        return [], [], [], [], lambda *_: None
    arrays, scatter = comm
    out_shape, specs, sems = _push_shapes(arrays, scatter)

    def hook(ins, outs, sem_refs):
        ids = [pl.program_id(a) for a in range(len(grid))]
        first, last = ids[0] == 0, ids[0] == grid[0] - 1
        for a in range(1, len(grid)):
            first, last = first & (ids[a] == 0), last & (ids[a] == grid[a] - 1)

        @pl.when(first)
        def _():
            _pushes(ins, outs, sem_refs, scatter).start()

        @pl.when(last)
        def _():
            _pushes(ins, outs, sem_refs, scatter).wait()

    return list(arrays), specs, out_shape, sems, hook


def mla_fwd(name, q, k, v, comm=None):
    s = q.shape[0]
    t = min(MLA_T, s)
    nq = s // t
    grid = (N_HEADS // MLA_HPS, nq)
    c_in, c_specs, c_shapes, c_sems, c_hook = _ride_along(comm, grid)
    nc = len(c_in)

    def body(*refs):
        q_ref, k_ref, v_ref = refs[:3]
        o_ref, l_ref = refs[3 + nc:5 + nc]
        m_s, l_s, acc_s = refs[5 + 2 * nc:8 + 2 * nc]
        c_hook(refs[3:3 + nc], refs[5 + nc:5 + 2 * nc], refs[8 + 2 * nc:])
        i = pl.program_id(1)
        m_s[...] = jnp.full_like(m_s, NEG)
        l_s[...] = jnp.zeros_like(l_s)
        acc_s[...] = jnp.zeros_like(acc_s)

        def chunk(j, masked):
            rows = pl.ds(pl.multiple_of(j * t, t), t)
            for hh in range(MLA_HPS):
                ln = slice(hh * LANE, (hh + 1) * LANE)
                sc = _dot(q_ref[:, ln], k_ref[rows, ln], 1, 1)
                if masked:
                    sc = jnp.where(_diag_mask(t), sc, NEG)
                m_old = m_s[:, ln]
                m_new = jnp.maximum(m_old, jnp.max(sc, axis=-1, keepdims=True))
                a = jnp.exp(m_old - m_new)
                p = jnp.exp(sc - m_new[:, 0:1])
                l_s[:, ln] = a * l_s[:, ln] + jnp.sum(p, axis=-1, keepdims=True)
                acc_s[:, ln] = a * acc_s[:, ln] + _dot(p, v_ref[rows, ln], 1, 0)
                m_s[:, ln] = m_new

        def step(j, carry):
            chunk(j, False)
            return carry

        lax.fori_loop(0, i, step, 0)
        chunk(i, True)
        o_ref[...] = acc_s[...] / l_s[...]
        l_ref[...] = m_s[...] + jnp.log(l_s[...])

    w = MLA_HPS * LANE
    qs = pl.BlockSpec((t, w), lambda h, i: (i, h))
    ks = pl.BlockSpec((s, w), lambda h, i: (0, h))
    res = pl.pallas_call(
        body, name=name, grid=grid, in_specs=[qs, ks, ks] + c_specs, out_specs=[qs, qs] + c_specs,
        out_shape=[jax.ShapeDtypeStruct((s, N_HEADS * LANE), F32)] * 2 + c_shapes,
        scratch_shapes=[pltpu.VMEM((t, w), F32)] * 3 + c_sems,
        compiler_params=_params(("arbitrary", "arbitrary")))(q, k, v, *c_in)
    return res[0], res[1], list(res[2:])


def mla_delta_fn(do, o):
    parts = []
    for a, b in zip(_split_impl(do, N_HEADS), _split_impl(o, N_HEADS)):
        parts.append(jnp.sum(a * b, axis=-1, keepdims=True) + jnp.zeros_like(a))
    return jnp.concatenate(parts, axis=-1), do


def mla_bwd(name, q, k, v, do, lse, delta, comm=None):
    s = q.shape[0]
    t = min(MLA_T, s)
    nq = s // t
    grid = (N_HEADS, nq)
    c_in, c_specs, c_shapes, c_sems, c_hook = _ride_along(comm, grid)
    nc = len(c_in)

    def body(*refs):
        q_ref, k_ref, v_ref, do_ref, l_ref, d_ref = refs[:6]
        dq_ref, dk_ref, dv_ref = refs[6 + nc:9 + nc]
        dk_s, dv_s = refs[9 + 2 * nc:11 + 2 * nc]
        c_hook(refs[6:6 + nc], refs[9 + nc:9 + 2 * nc], refs[11 + 2 * nc:])
        j = pl.program_id(1)

        @pl.when(j == 0)
        def _():
            dq_ref[...] = jnp.zeros_like(dq_ref)

        dk_s[...] = jnp.zeros_like(dk_s)
        dv_s[...] = jnp.zeros_like(dv_s)
        kb, vb = k_ref[...], v_ref[...]

        def chunk(i, masked):
            rows = pl.ds(pl.multiple_of(i * t, t), t)
            qi, doi = q_ref[rows, :], do_ref[rows, :]
            sc = _dot(qi, kb, 1, 1)
            if masked:
                sc = jnp.where(_diag_mask(t), sc, NEG)
            p = jnp.exp(sc - l_ref[rows, :][:, 0:1])
            ds = p * (_dot(doi, vb, 1, 1) - d_ref[rows, :][:, 0:1])
            dv_s[...] += _dot(p, doi, 0, 0)
            dk_s[...] += _dot(ds, qi, 0, 0)
            dq_ref[rows, :] += _dot(ds, kb, 1, 0)

        def pair(p, carry):
            chunk(j + 1 + 2 * p, False)
            chunk(j + 2 + 2 * p, False)
            return carry

        chunk(j, True)
        n_off = nq - 1 - j
        lax.fori_loop(0, n_off // 2, pair, 0)

        @pl.when(n_off % 2 == 1)
        def _():
            chunk(nq - 1, False)

        dk_ref[...] = dk_s[...]
        dv_ref[...] = dv_s[...]

    ks = pl.BlockSpec((t, LANE), lambda h, j: (j, h))
    full = pl.BlockSpec((s, LANE), lambda h, j: (0, h))
    res = pl.pallas_call(
        body, name=name, grid=grid, in_specs=[full, ks, ks, full, full, full] + c_specs,
        out_specs=[full, ks, ks] + c_specs,
        out_shape=[jax.ShapeDtypeStruct((s, N_HEADS * LANE), F32)] * 3 + c_shapes,
        scratch_shapes=[pltpu.VMEM((t, LANE), F32)] * 2 + c_sems,
        compiler_params=_params(("arbitrary", "arbitrary")))(q, k, v, do, lse, delta, *c_in)
    return res[0], res[1], res[2], list(res[3:])


def _peer(k):
    x, y, c = lax.axis_index("x"), lax.axis_index("y"), lax.axis_index("c")
    px, py, pc = (x ^ ((k >> 2) & 1)), (y ^ ((k >> 1) & 1)), (c ^ (k & 1))
    return (px, py, pc), 4 * px + 2 * py + pc


def exchange(name, arrays, scatter):
    n = len(arrays)

    def body(*refs):
        push = _pushes(refs[:n], refs[n:2 * n], refs[2 * n:], scatter)
        push.start()
        push.wait()

    out_shape, specs, sems = _push_shapes(arrays, scatter)
    return pl.pallas_call(body, name=name, out_shape=out_shape, in_specs=specs, out_specs=specs,
                          scratch_shapes=sems)(*arrays)


def _push_shapes(arrays, scatter):
    n = len(arrays)
    out_shape = [jax.ShapeDtypeStruct(a.shape if scatter else (N_DEV,) + a.shape, a.dtype) for a in arrays]
    n_sem = (N_DEV - 1) * n
    sems = [pltpu.SemaphoreType.DMA((n_sem,)), pltpu.SemaphoreType.DMA((n_sem,)), pltpu.SemaphoreType.DMA((n,))]
    return out_shape, [pl.BlockSpec(memory_space=pl.ANY)] * n, sems


class _pushes:
    def __init__(self, ins, outs, sems, scatter):
        self.ins, self.outs, self.sems, self.scatter = ins, outs, sems, scatter

    def _local(self):
        _, me_idx = _peer(0)
        return [pltpu.make_async_copy(self.ins[a].at[me_idx] if self.scatter else self.ins[a],
                                      self.outs[a].at[me_idx], self.sems[2].at[a]) for a in range(len(self.ins))]

    def _remote(self, incoming):
        send_sems, recv_sems, _ = self.sems
        n = len(self.ins)
        _, me_idx = _peer(0)
        copies = []
        for k in range(1, N_DEV):
            peer, peer_idx = _peer(k)
            for a in range(n):
                sem = (k - 1) * n + a
                copies.append(pltpu.make_async_remote_copy(
                    src_ref=self.ins[a].at[peer_idx] if self.scatter else self.ins[a],
                    dst_ref=self.outs[a].at[peer_idx if incoming else me_idx],
                    send_sem=send_sems.at[sem], recv_sem=recv_sems.at[sem], device_id=peer, device_id_type=MESH_ID))
        return copies

    def start(self):
        for cp in self._local() + self._remote(incoming=False):
            cp.start()

    def wait(self):
        for cp in self._remote(incoming=True):
            cp.wait_recv()
        for cp in self._remote(incoming=False):
            cp.wait_send()
        for cp in self._local():
            cp.wait()


ADAMW_BLOCK_ELEMS = 128 * 1024


def adamw_sum(name, parts, w, m, v):
    nl, r, c = w.shape
    fits = [t for t in range(16, r + 1, 16) if r % t == 0 and t * c <= ADAMW_BLOCK_ELEMS]
    tr = max(fits) if fits else r
    c1 = 1.0 - ADAM_B1 ** ADAM_STEP
    c2 = 1.0 - ADAM_B2 ** ADAM_STEP

    def body(*refs):
        p_refs = refs[:nl]
        w_ref, m_ref, v_ref, g_out, d_out, m_out, v_out = refs[nl:]
        for layer in range(nl):
            @pl.when(pl.program_id(0) == layer)
            def _():
                g = p_refs[layer][0].astype(F32)
                for d in range(1, N_DEV):
                    g = g + p_refs[layer][d].astype(F32)
                mn = ADAM_B1 * m_ref[...] + (1.0 - ADAM_B1) * g
                vn = ADAM_B2 * v_ref[...] + (1.0 - ADAM_B2) * (g * g)
                g_out[...] = g
                m_out[...] = mn
                v_out[...] = vn
                d_out[...] = -ADAM_LR * ((mn / c1) / (jnp.sqrt(vn / c2) + ADAM_EPS) + ADAM_WD * w_ref[...])

    row = pl.BlockSpec((None, tr, c), lambda l, i: (l, i, 0))
    return pl.pallas_call(
        body, name=name, grid=(nl, r // tr),
        in_specs=[pl.BlockSpec((N_DEV, tr, c), lambda l, i: (0, i, 0))] * nl + [row, row, row],
        out_specs=[row] * 4, out_shape=[jax.ShapeDtypeStruct((nl, r, c), F32)] * 4,
        compiler_params=_params(("parallel", "parallel")))(*parts, w, m, v)


def _pack(arrays, dtype, row_mult):
    flat = jnp.concatenate([a.astype(dtype).reshape(-1) for a in arrays])
    n = flat.shape[0]
    quantum = row_mult * LANE
    total = -(-n // quantum) * quantum
    return jnp.pad(flat, (0, total - n)).reshape(total // LANE, LANE)


def _unpack(packed, shapes):
    flat = packed.reshape(-1)
    out, pos = [], 0
    for shp in shapes:
        n = int(np.prod(shp))
        out.append(flat[pos:pos + n].reshape(shp))
        pos += n
    return out


def _rows(a):
    return a.reshape(-1, a.shape[-1])


def _unshard(gathered, shard_shape, axis):
    t = jnp.moveaxis(gathered.reshape((N_DEV,) + tuple(shard_shape)), 0, axis)
    return t.reshape(tuple(shard_shape[:axis]) + (N_DEV * shard_shape[axis],) + tuple(shard_shape[axis + 1:]))


def _to_shards(full, axis):
    shp = full.shape
    t = full.reshape(shp[:axis] + (N_DEV, shp[axis] // N_DEV) + shp[axis + 1:])
    t = jnp.moveaxis(t, axis, 0)
    return t.reshape(N_DEV, -1, t.shape[-1])


A_COLS = len(A_GROUPS) * 3 * HW
KV_COLS = B_KV_HEADS * HEAD_DIM
KV_REP = N_HEADS // B_KV_HEADS
O_BQ = A_COLS
O_BK = O_BQ + HW
O_BV = O_BK + KV_COLS
O_CQ = O_BV + KV_COLS
O_CKV = O_CQ + C_Q_RANK
O_KR = O_CKV + C_KV_RANK
O_GATE = O_KR + C_ROPE


def _proj_weight(w_in):
    d = w_in.shape[0]

    def per_q_head(lo):
        t = w_in[:, lo:lo + KV_COLS].reshape(d, B_KV_HEADS, 1, HEAD_DIM)
        return jnp.broadcast_to(t, (d, B_KV_HEADS, KV_REP, HEAD_DIM)).reshape(d, HW)

    kr = jnp.pad(w_in[:, O_KR:O_GATE], ((0, 0), (C_NOPE, LANE - C_NOPE - C_ROPE)))
    return jnp.concatenate([w_in[:, O_GATE:], w_in[:, :O_BK], per_q_head(O_BK), per_q_head(O_BV),
                            w_in[:, O_CQ:O_KR], kr], axis=1)


def _unproj_grad(dw_p, d_model):
    d = dw_p.shape[0]
    g = 3 * d_model
    p_bk = g + O_BK
    p_bv = p_bk + HW
    p_cq = p_bv + HW
    p_kr = p_cq + C_Q_RANK + C_KV_RANK

    def sum_heads(lo):
        return dw_p[:, lo:lo + HW].reshape(d, B_KV_HEADS, KV_REP, HEAD_DIM).sum(axis=2).reshape(d, KV_COLS)

    return jnp.concatenate([dw_p[:, g:p_bk], sum_heads(p_bk), sum_heads(p_bv), dw_p[:, p_cq:p_kr],
                            dw_p[:, p_kr + C_NOPE:p_kr + C_NOPE + C_ROPE], dw_p[:, :g]], axis=1)


def _pad_heads(w, per_head, lo, hi):
    r = w.shape[0]
    t = w.reshape(r, N_HEADS, per_head)[:, :, lo:hi]
    return jnp.pad(t, ((0, 0), (0, 0), (0, LANE - (hi - lo)))).reshape(r, N_HEADS * LANE)


def _unpad_heads(w, width):
    r = w.shape[0]
    return w.reshape(r, N_HEADS, LANE)[:, :, :width]


def _t5_bucket(dist):
    n = jnp.maximum(dist, 0)
    max_exact = REL_BUCKETS // 2
    scaled = jnp.log(jnp.maximum(n, 1).astype(F32) / max_exact) / math.log(REL_MAX_DIST / max_exact)
    large = max_exact + (scaled * (REL_BUCKETS - max_exact)).astype(jnp.int32)
    return jnp.where(n < max_exact, n, jnp.minimum(large, REL_BUCKETS - 1))


def _buckets(dil):
    qi = jnp.arange(BLOCK)[:, None]
    ki = jnp.arange(2 * BLOCK)[None, :]
    return _t5_bucket((BLOCK + qi - ki) * dil).astype(jnp.int32)


def _rope_tables(s):
    pos = jnp.arange(s, dtype=F32)
    inv_freq = ROPE_BASE ** (-jnp.arange(0, C_ROPE, 2, dtype=F32) / C_ROPE)
    ang = pos[:, None] * inv_freq[None, :]
    cos, sin = jnp.cos(ang), jnp.sin(ang)
    ones = jnp.ones((s, C_NOPE), F32)
    tail = LANE - C_NOPE - C_ROPE
    c = jnp.concatenate([ones, cos, cos, jnp.ones((s, tail), F32)], axis=1)
    sn = jnp.concatenate([jnp.zeros((s, C_NOPE), F32), sin, sin, jnp.zeros((s, tail), F32)], axis=1)
    return c, sn


def _cols(d_model):
    g = 3 * d_model // HW
    a = [(g + 3 * i, g + 3 * i + 1, g + 3 * i + 2) for i in range(len(A_GROUPS))]
    b = (g + 9, g + 10, g + 11)
    cq_lane = (g + 12) * HW
    return a, b, cq_lane


def _layer_weights(full, d_model):
    lw = {}
    if 'w_in' in full:
        lw['w_in_p'] = _proj_weight(full['w_in'])
    if 'w_uq' in full:
        per = C_NOPE + C_ROPE
        w_uq, w_ukv, wb = full['w_uq'], full['w_ukv'], full['w_branch']
        wb2 = jnp.pad(wb[2].reshape(N_HEADS, HEAD_DIM, d_model), ((0, 0), (0, LANE - HEAD_DIM), (0, 0)))
        lw.update(
            wq=_pad_heads(w_uq, per, 0, per), wk=_pad_heads(w_ukv, 2 * C_NOPE, 0, C_NOPE),
            wv=_pad_heads(w_ukv, 2 * C_NOPE, C_NOPE, 2 * C_NOPE), wb0=wb[0], wb1=wb[1],
            wb2=wb2.reshape(N_HEADS * LANE, d_model), w_out=full['w_out'], w_up=full['w_ffn_up'],
            w_down=full['w_ffn_down'], conv_w=full['conv_w'])
    return lw


def _layer_fwd(l, x, x16, lw, small, tabs, comm=None, proj_comm=None):
    s, d = x.shape
    a_cols, b_cols, cq_lane = _cols(d)
    tag = f"l{l}_"
    if proj_comm is None:
        proj = matmul(tag + "proj", x16, lw['w_in_p'], 'nn')
    else:
        proj, got = matmul(tag + "proj", x16, lw['w_in_p'], 'nn', comm=proj_comm[0])
        lw = dict(lw, **_layer_weights(proj_comm[1](got), d))
    res = dict(x=x, x16=x16, proj=proj, lw=lw)
    outs, lses, slabs = [], [], []
    for gi, (window, dil) in enumerate(A_GROUPS):
        if dil == 1:
            slab, cols = proj, a_cols[gi]
        else:
            slab, cols = to_classes(proj[:, a_cols[gi][0] * HW:(a_cols[gi][2] + 1) * HW], dil), (0, 1, 2)
        o, lg = band_fwd(tag + f"a{gi}_fwd", slab, dil, cols, tabs['bias'][gi], None, True)
        outs.append(from_classes(o, dil))
        lses.append(from_classes(lg, dil))
        slabs.append((slab, cols))
    res['a_o'], res['a_l'], res['a_slabs'] = outs, lses, slabs
    (o_a,) = rows_fwd(tag + "a_combine", combine_fn, [(o, HW, 0) for o in outs] + [(lg, HW, 0) for lg in lses], [],
                      [(HW, BF16)])
    sink = small['sinks'][l].reshape(N_HEADS, 1, 1)
    o_b, _ = band_fwd(tag + "b_fwd", proj, 1, b_cols, tabs['bias'][3], sink, False)
    gq = small['q_norm_g'][l].reshape(1, -1)
    gkv = small['kv_norm_g'][l].reshape(1, -1)
    (qc,) = rows_fwd(tag + "c_q", mla_q_fn, [(proj, C_Q_RANK, cq_lane // C_Q_RANK), (tabs['cos'], LANE, 0),
                                            (tabs['sin'], LANE, 0)], [gq, lw['wq']], [(N_HEADS * LANE, BF16)])
    ckv_blk = (cq_lane + C_Q_RANK) // LANE
    kc, vc = rows_fwd(tag + "c_kv", mla_kv_fn, [(proj, LANE, ckv_blk), (proj, LANE, ckv_blk + 1), (tabs['cos'], LANE, 0),
                                                (tabs['sin'], LANE, 0)], [gkv, lw['wk'], lw['wv']],
                      [(N_HEADS * LANE, BF16)] * 2)
    o_c, lse_c, res['comm'] = mla_fwd(tag + "mla_fwd", qc, kc, vc, comm)
    res.update(o_a=o_a, o_b=o_b, o_c=o_c, lse_c=lse_c, qc=qc, kc=kc, vc=vc)
    bg = small['b_gate'][l].reshape(3, 1, d)
    ys = [matmul(tag + f"branch{i}", o, lw[f'wb{i}'], 'nn') for i, o in enumerate((o_a, o_b, o_c))]
    (merged,) = rows_fwd(tag + "merge", merge_fn, [(y, d, 0) for y in ys] + [(proj, d, i) for i in range(3)],
                         [bg[0], bg[1], bg[2]], [(d, BF16)])
    mix = matmul(tag + "out_proj", merged, lw['w_out'], 'nn')
    ln1 = [small['ln1_g'][l].reshape(1, d), small['ln1_b'][l].reshape(1, d)]
    x1, x1_16 = rows_fwd(tag + "ln1", ln_twice, [(x, d, 0), (mix, d, 0)], ln1, [(d, F32), (d, BF16)])
    u = matmul(tag + "ffn_up", x1_16, lw['w_up'], 'nn')
    h = conv_glu_fwd(u, lw['conv_w'], small['conv_b'][l].reshape(1, -1))
    ff = matmul(tag + "ffn_down", h, lw['w_down'], 'nn')
    ln2 = [small['ln2_g'][l].reshape(1, d), small['ln2_b'][l].reshape(1, d)]
    x2, x2_16 = rows_fwd(tag + "ln2", ln_twice, [(x1, d, 0), (ff, d, 0)], ln2, [(d, F32), (d, BF16)])
    res.update(merged=merged, mix=mix, x1=x1, x1_16=x1_16, u=u, h=h, ff=ff, ys=ys)
    return x2, x2_16, res


def _layer_bwd(l, dy, res, lw, small, tabs, comm=None, dw_comm=None):
    x, proj = res['x'], res['proj']
    s, d = x.shape
    a_cols, b_cols, cq_lane = _cols(d)
    tag = f"l{l}_"
    g = {}
    ln2 = [small['ln2_g'][l].reshape(1, d), small['ln2_b'][l].reshape(1, d)]
    dff, g['ln2_g'], g['ln2_b'] = rows_bwd(tag + "ln2_bwd", ln_fn, [(res['x1'], d, 0), (res['ff'], d, 0)], ln2,
                                           [(dy, d)], [(1, F32)])
    g['w_ffn_down'] = matmul(tag + "dw_down", res['h'], dff, 'tn')
    dh = matmul(tag + "dh", dff, lw['w_down'], 'nt')
    conv_b = small['conv_b'][l].reshape(1, -1)
    dug, duv, dwg, dwv, dbg, dbv = conv_glu_bwd(res['u'], dh, lw['conv_w'], conv_b)
    du = jnp.concatenate([dug, duv], axis=1)
    g['conv_w'] = jnp.concatenate([dwg, dwv], axis=1)
    g['conv_b'] = jnp.concatenate([dbg, dbv], axis=1).reshape(-1)
    g['w_ffn_up'] = matmul(tag + "dw_up", res['x1_16'], du, 'tn')
    dx1 = matmul(tag + "dx1", du, lw['w_up'], 'nt', add=dff, scale=ALPHA)
    ln1 = [small['ln1_g'][l].reshape(1, d), small['ln1_b'][l].reshape(1, d)]
    dmix, g['ln1_g'], g['ln1_b'] = rows_bwd(tag + "ln1_bwd", ln_fn, [(x, d, 0), (res['mix'], d, 0)], ln1,
                                            [(dx1, d)], [(1, F32)])
    g['w_out'] = matmul(tag + "dw_out", res['merged'], dmix, 'tn')
    dmerged = matmul(tag + "dmerged", dmix, lw['w_out'], 'nt')
    bg = small['b_gate'][l].reshape(3, 1, d)
    mrows = [(y, d, 0) for y in res['ys']] + [(proj, d, i) for i in range(3)]
    (dy0, dy1, dy2, dg0, dg1, dg2, db0, db1, db2) = rows_bwd(
        tag + "merge_bwd", merge_fn, mrows, [bg[0], bg[1], bg[2]], [(dmerged, d)], [(i, BF16) for i in range(6)])
    g['b_gate'] = jnp.concatenate([db0, db1, db2], axis=1).reshape(-1)
    branch_in = (res['o_a'], res['o_b'], res['o_c'])
    dwb = [matmul(tag + f"dw_branch{i}", o, dyi, 'tn') for i, (o, dyi) in enumerate(zip(branch_in, (dy0, dy1, dy2)))]
    dwb[2] = dwb[2].reshape(N_HEADS, LANE, d)[:, :HEAD_DIM].reshape(HW, d)
    g['w_branch'] = jnp.stack(dwb, axis=0)
    do_a, do_b, do_c = [matmul(tag + f"do_branch{i}", dyi, lw[f'wb{i}'], 'nt')
                        for i, dyi in enumerate((dy0, dy1, dy2))]
    cw = N_HEADS * LANE
    delta, do_c16 = rows_fwd(tag + "c_delta", mla_delta_fn, [(do_c, cw, 0), (res['o_c'], cw, 0)], [],
                             [(cw, F32), (cw, BF16)])
    dqc, dkc, dvc, g['comm'] = mla_bwd(tag + "mla_bwd", res['qc'], res['kc'], res['vc'], do_c16, res['lse_c'], delta,
                                       comm)
    gq = small['q_norm_g'][l].reshape(1, -1)
    gkv = small['kv_norm_g'][l].reshape(1, -1)
    dcq, dgq, dwq = rows_bwd(tag + "c_q_bwd", mla_q_fn,
                             [(proj, C_Q_RANK, cq_lane // C_Q_RANK), (tabs['cos'], LANE, 0), (tabs['sin'], LANE, 0)],
                             [gq, lw['wq']], [(dqc, N_HEADS * LANE)], [(0, BF16)])
    ckv_blk = (cq_lane + C_Q_RANK) // LANE
    dckv, dkr, dgkv, dwk, dwv = rows_bwd(
        tag + "c_kv_bwd", mla_kv_fn,
        [(proj, LANE, ckv_blk), (proj, LANE, ckv_blk + 1), (tabs['cos'], LANE, 0), (tabs['sin'], LANE, 0)],
        [gkv, lw['wk'], lw['wv']], [(dkc, N_HEADS * LANE), (dvc, N_HEADS * LANE)], [(0, BF16), (1, BF16)])
    g['q_norm_g'], g['kv_norm_g'] = dgq.reshape(-1), dgkv.reshape(-1)
    per = C_NOPE + C_ROPE
    g['w_uq'] = _unpad_heads(dwq, per).reshape(C_Q_RANK, N_HEADS * per)
    g['w_ukv'] = jnp.concatenate([_unpad_heads(dwk, C_NOPE), _unpad_heads(dwv, C_NOPE)], axis=2).reshape(
        C_KV_RANK, N_HEADS * 2 * C_NOPE)
    sink = small['sinks'][l].reshape(N_HEADS, 1, 1)
    pieces_b, dbias_b, dsink = band_bwd(tag + "b_bwd", proj, 1, b_cols, tabs['bias'][3], sink, False, do_b, None)
    g['sinks'] = dsink.reshape(-1)
    dbias = [None] * 4
    dbias[3] = dbias_b
    combo = [(o, HW, 0) for o in res['a_o']] + [(lg, HW, 0) for lg in res['a_l']]
    a_cts = rows_bwd(tag + "a_combine_bwd", combine_fn, combo, [], [(do_a, HW)], [(i, F32) for i in range(6)])
    pieces_a = []
    for gi, (window, dil) in enumerate(A_GROUPS):
        slab, cols = res['a_slabs'][gi]
        group, dbias[gi], _ = band_bwd(
            tag + f"a{gi}_bwd", slab, dil, cols, tabs['bias'][gi], None, True, to_classes(a_cts[gi], dil),
            to_classes(a_cts[3 + gi], dil))
        pieces_a += [from_classes(t, dil) for t in group]
    g['dbias'] = dbias
    dproj = jnp.concatenate([dg0, dg1, dg2] + pieces_a + pieces_b + [dcq, dckv, dkr], axis=1)
    if dw_comm is None:
        dw_in_p = matmul(tag + "dw_in", res['x16'], dproj, 'tn')
    else:
        dw_in_p, g['dw_comm'] = matmul(tag + "dw_in", res['x16'], dproj, 'tn', comm=dw_comm(g))
    g['w_in'] = _unproj_grad(dw_in_p, d)
    dx = matmul(tag + "dx", dproj, lw['w_in_p'], 'nt', add=dmix, scale=ALPHA)
    return dx, g


def kernel(x, rel_table, w_in, b_gate, sinks, q_norm_g, kv_norm_g, w_uq, w_ukv, w_branch, w_out, ln1_g, ln1_b, w_ffn_up, conv_w, conv_b, w_ffn_down, ln2_g, ln2_b, loss_target, m_rel_table, m_w_in, m_b_gate, m_sinks, m_q_norm_g, m_kv_norm_g, m_w_uq, m_w_ukv, m_w_branch, m_w_out, m_ln1_g, m_ln1_b, m_w_ffn_up, m_conv_w, m_conv_b, m_w_ffn_down, m_ln2_g, m_ln2_b, v_rel_table, v_w_in, v_b_gate, v_sinks, v_q_norm_g, v_kv_norm_g, v_w_uq, v_w_ukv, v_w_branch, v_w_out, v_ln1_g, v_ln1_b, v_w_ffn_up, v_conv_w, v_conv_b, v_w_ffn_down, v_ln2_g, v_ln2_b):
    args = locals()
    w = {n: args[n] for n in WEIGHTS}
    mom = {n: args["m_" + n] for n in WEIGHTS}
    var = {n: args["v_" + n] for n in WEIGHTS}
    s, d = x.shape[1], x.shape[2]
    xs = x.reshape(s, d)
    target = loss_target.reshape(s, d)

    big = [n for n in WEIGHTS if n in SHARDED]
    small_names = [n for n in WEIGHTS if n not in SHARDED]
    assert DEPTH == 2
    wire = lambda n: BF16 if n in GATHER_BF16 else F32
    rest = [n for n in big if n != 'w_in']

    def shards(l, names):
        return [_rows(w[n][l]).astype(wire(n)) for n in names]

    def to_full(names, gathered):
        return {n: _unshard(t, w[n].shape[1:], SHARDED[n] - 1) for n, t in zip(names, gathered)}

    def grad_shards(names, g):
        return [_to_shards(g[n], SHARDED[n] - 1).astype(wire(n)) for n in names]

    w_in0 = to_full(['w_in'], exchange("gather_w_in0", shards(0, ['w_in']), scatter=False))
    fused = dict(proj0=((shards(0, rest), False), lambda got: to_full(rest, got)),
                 fwd0=((shards(1, big), False), lambda got: to_full(big, got)),
                 bwd0=lambda g1: (grad_shards(big, g1), True),
                 dw0=lambda g0: (grad_shards(rest, g0), True))
    small = {n: w[n] for n in small_names}
    loss_part, grad_x, grads, g_rel, recv1, recv0_rest = _local_grads(xs, target, [w_in0, None], small, fused)
    grad_x = grad_x.reshape(x.shape)
    recv0 = dict(zip(rest, recv0_rest))
    (recv0['w_in'],) = exchange("exchange_dw_in0", grad_shards(['w_in'], grads[0]), scatter=True)

    big_res = [{}, {}, {}, {}]
    per_layer = lambda t: t.reshape((DEPTH, -1, t.shape[-1]))
    for i, n in enumerate(big):
        outs = adamw_sum("adamw_" + n, [recv0[n], recv1[i]], per_layer(w[n]), per_layer(mom[n]), per_layer(var[n]))
        for k in range(4):
            big_res[k][n] = outs[k].reshape(w[n].shape)

    g_small = {'rel_table': g_rel}
    for n in small_names:
        if n != 'rel_table':
            g_small[n] = jnp.stack([grads[l][n] for l in range(DEPTH)], axis=0)
    small_pack = _pack([g_small[n].reshape(w[n].shape) for n in small_names] + [loss_part[0, 0:1]], F32, 8)
    (small_all,) = exchange("gather_small_grads", [small_pack], scatter=False)
    pks = lambda tree: _pack([tree[n] for n in small_names] + [jnp.zeros((1,), F32)], F32, 8)[None]
    small_out = adamw_sum("adamw_replicated", [small_all], pks(w), pks(mom), pks(var))
    small_shapes = [w[n].shape for n in small_names] + [(1,)]
    small_res = [dict(zip(small_names + ['loss'], _unpack(o, small_shapes))) for o in small_out]

    loss = small_res[0]['loss'].reshape(())
    out = [loss, grad_x]
    for k in range(4):
        out += [big_res[k][n] if n in SHARDED else small_res[k][n] for n in WEIGHTS]
    return tuple(out)


def _local_grads(xs, target, fulls, small, fused=None):
    s, d = xs.shape
    rel_table = small['rel_table']
    fulls = list(fulls)

    cos, sin = _rope_tables(s)
    buckets = [_buckets(dil) for _, dil in A_GROUPS] + [_buckets(1)]
    def lookup(table, bucket):
        out = jnp.zeros((N_HEADS,) + bucket.shape, F32)
        for b in range(REL_BUCKETS):
            out = jnp.where((bucket == b)[None], table[b][:, None, None], out)
        return out

    bias = [lookup(rel_table[:, gi * N_HEADS:(gi + 1) * N_HEADS], buckets[gi]) for gi in range(4)]
    tabs = dict(cos=cos, sin=sin, bias=bias)

    act, act16, saved = xs, xs.astype(BF16), []
    for l in range(DEPTH):
        lw = _layer_weights(fulls[l], d)
        ride = fused is not None and l == 0
        act, act16, res = _layer_fwd(l, act, act16, lw, small, tabs, fused['fwd0'][0] if ride else None,
                                     fused['proj0'] if ride else None)
        if ride:
            fulls[1] = fused['fwd0'][1](res['comm'])
        saved.append(res)
    dy, loss_part = loss_head(act, target)

    grads, recv1, recv0_rest = [None] * DEPTH, None, None
    for l in reversed(range(DEPTH)):
        ride = fused is not None and l == 0
        dy, grads[l] = _layer_bwd(l, dy, saved[l], saved[l]['lw'], small, tabs,
                                  fused['bwd0'](grads[1]) if ride else None, fused['dw0'] if ride else None)
        if ride:
            recv1, recv0_rest = grads[l]['comm'], grads[l]['dw_comm']
    grad_x = dy

    rel_cols = []
    for gi in range(4):
        both = grads[0]['dbias'][gi] + grads[1]['dbias'][gi] if DEPTH == 2 else grads[0]['dbias'][gi]
        rel_cols.append(rel_grad(f"rel_grad{gi}", both, buckets[gi])[:, :REL_BUCKETS].T)
    g_rel = jnp.concatenate(rel_cols, axis=1)
    return loss_part, grad_x, grads, g_rel, recv1, recv0_rest
```

```python
import functools
import math

import numpy as np
import jax
import jax.numpy as jnp
from jax import lax
from jax.experimental import pallas as pl
from jax.experimental.pallas import tpu as pltpu

F32 = jnp.float32
BF16 = jnp.bfloat16

N_DEV = 8
DEPTH = 2
HEAD_DIM = 64
BLOCK = 128
A_GROUPS = ((128, 1), (512, 4), (2048, 16))
N_HEADS = 8
HW = N_HEADS * HEAD_DIM
B_KV_HEADS = 2
C_Q_RANK = 256
C_KV_RANK = 128
C_NOPE = 64
C_ROPE = 32
ROPE_BASE = 10000.0
REL_BUCKETS = 32
REL_MAX_DIST = 2048
ALPHA = (2 * DEPTH) ** 0.25
LN_EPS = 1e-5
RMS_EPS = 1e-6
NEG = -1e30
ADAM_LR = 0.001
ADAM_B1 = 0.9
ADAM_B2 = 0.999
ADAM_EPS = 1e-08
ADAM_WD = 0.01
ADAM_STEP = 10

LANE = 128
TM = 256
VMEM_LIMIT = 56 * 1024 * 1024
MESH_ID = pl.DeviceIdType.MESH

WEIGHTS = ('rel_table', 'w_in', 'b_gate', 'sinks', 'q_norm_g', 'kv_norm_g', 'w_uq', 'w_ukv', 'w_branch',
           'w_out', 'ln1_g', 'ln1_b', 'w_ffn_up', 'conv_w', 'conv_b', 'w_ffn_down', 'ln2_g', 'ln2_b')
SHARDED = {'w_in': 2, 'w_uq': 2, 'w_ukv': 2, 'w_branch': 3, 'w_out': 1, 'w_ffn_up': 2, 'conv_w': 2,
           'w_ffn_down': 1}
GATHER_BF16 = ('w_in', 'w_uq', 'w_ukv', 'w_branch', 'w_out', 'w_ffn_up', 'w_ffn_down')


def _params(sem=None):
    return pltpu.CompilerParams(dimension_semantics=sem, vmem_limit_bytes=VMEM_LIMIT)


def _pick(n, target):
    if n <= target:
        return n
    best = None
    for t in range(LANE, target + 1, LANE):
        if n % t == 0:
            best = t
    assert best is not None, (n, target)
    return best


def _dot(a, b, ca, cb):
    return lax.dot_general(a.astype(BF16), b.astype(BF16), (((ca,), (cb,)), ((), ())),
                           preferred_element_type=F32)


@jax.custom_vjp
def mm(a, b):
    return _dot(a, b, 1, 0)


def _mm_fwd(a, b):
    return _dot(a, b, 1, 0), (a, b)


def _mm_bwd(res, g):
    a, b = res
    return _dot(g, b, 1, 1), _dot(a, g, 0, 0)


mm.defvjp(_mm_fwd, _mm_bwd)


@jax.custom_vjp
def mm_nt(a, b):
    return _dot(a, b, 1, 1)


def _mm_nt_fwd(a, b):
    return _dot(a, b, 1, 1), (a, b)


def _mm_nt_bwd(res, g):
    a, b = res
    return _dot(g, b, 1, 0), _dot(g, a, 0, 0)


mm_nt.defvjp(_mm_nt_fwd, _mm_nt_bwd)


def _split_impl(x, n):
    w = x.shape[-1] // n
    return tuple(x[:, i * w:(i + 1) * w] for i in range(n))


@functools.partial(jax.custom_vjp, nondiff_argnums=(1,))
def split_lanes(x, n):
    return _split_impl(x, n)


def _split_fwd(x, n):
    return _split_impl(x, n), None


def _split_bwd(n, _, gs):
    return (jnp.concatenate(gs, axis=-1),)


split_lanes.defvjp(_split_fwd, _split_bwd)


@jax.custom_vjp
def concat_lanes(xs):
    return jnp.concatenate(xs, axis=-1)


def _concat_fwd(xs):
    return jnp.concatenate(xs, axis=-1), len(xs)


def _concat_bwd(n, g):
    return (_split_impl(g, n),)


concat_lanes.defvjp(_concat_fwd, _concat_bwd)


@jax.custom_vjp
def concat_rows(a, b):
    return jnp.concatenate([a, b], axis=0)


def _crow_fwd(a, b):
    return jnp.concatenate([a, b], axis=0), a.shape[0]


def _crow_bwd(na, g):
    return g[:na], g[na:]


concat_rows.defvjp(_crow_fwd, _crow_bwd)


def _rot_half(x):
    n = x.shape[-1]
    lane = lax.broadcasted_iota(jnp.int32, (1, n), 1) % LANE
    lo = (lane >= C_NOPE) & (lane < C_NOPE + C_ROPE // 2)
    hi = (lane >= C_NOPE + C_ROPE // 2) & (lane < C_NOPE + C_ROPE)
    up = pltpu.roll(x, n - C_ROPE // 2, 1)
    dn = pltpu.roll(x, C_ROPE // 2, 1)
    return jnp.where(lo, -up, jnp.where(hi, dn, 0.0))


@jax.custom_vjp
def rope(x, c, s):
    return x * c + _rot_half(x) * s


def _rope_fwd(x, c, s):
    return x * c + _rot_half(x) * s, (c, s)


def _rope_bwd(res, g):
    c, s = res
    return g * c - _rot_half(g * s), jnp.zeros_like(c), jnp.zeros_like(s)


rope.defvjp(_rope_fwd, _rope_bwd)


def _sigmoid(x):
    return 0.5 * jnp.tanh(0.5 * x) + 0.5


MATMUL_VMEM = 44 * 1024 * 1024


def _matmul_tiles(m, n, k, a_bytes, b_bytes, o_bytes, has_add):
    tm, tn, tk = _pick(m, 1024), _pick(n, 2560), _pick(k, 2560)

    def need(tn_, tk_):
        acc = tm * tn_ * 4 if k // tk_ > 1 else 0
        return 2 * (tm * tk_ * a_bytes + tk_ * tn_ * b_bytes + tm * tn_ * o_bytes) + acc + 2 * tm * tn_ * 4 * int(has_add)

    while need(tn, tk) > MATMUL_VMEM:
        if tk >= tn and tk > LANE:
            tk = _pick(k, tk - LANE)
        else:
            tn = _pick(n, tn - LANE)
    return tm, tn, tk


def matmul(name, a, b, mode, out_dtype=F32, add=None, scale=1.0, comm=None):
    if mode == 'nn':
        (m, k), n = a.shape, b.shape[1]
    elif mode == 'nt':
        (m, k), n = a.shape, b.shape[0]
    else:
        (k, m), n = a.shape, b.shape[1]
    tm, tn, tk = _matmul_tiles(m, n, k, a.dtype.itemsize, b.dtype.itemsize, jnp.dtype(out_dtype).itemsize,
                               add is not None)
    nk = k // tk
    grid = (m // tm, n // tn, nk)
    c_in, c_specs, c_shapes, c_sems, c_hook = _ride_along(comm, grid)
    nc = len(c_in)
    a_spec = pl.BlockSpec((tk, tm), lambda i, j, kk: (kk, i)) if mode == 'tn' else \
        pl.BlockSpec((tm, tk), lambda i, j, kk: (i, kk))
    b_spec = pl.BlockSpec((tn, tk), lambda i, j, kk: (j, kk)) if mode == 'nt' else \
        pl.BlockSpec((tk, tn), lambda i, j, kk: (kk, j))
    o_spec = pl.BlockSpec((tm, tn), lambda i, j, kk: (i, j))
    ca = 0 if mode == 'tn' else 1
    cb = 1 if mode == 'nt' else 0
    has_add = add is not None

    n_in = 2 + int(has_add)

    def body(*refs):
        a_ref, b_ref = refs[:2]
        add_ref = refs[2] if has_add else None
        o_ref = refs[n_in + nc]
        c_hook(refs[n_in:n_in + nc], refs[n_in + nc + 1:n_in + 2 * nc + 1], refs[len(refs) - 3:])

        def finish(r):
            if has_add:
                r = r + scale * add_ref[...]
            o_ref[...] = r.astype(out_dtype)

        if nk == 1:
            finish(_dot(a_ref[...], b_ref[...], ca, cb))
            return
        acc_ref = refs[n_in + 2 * nc + 1]
        kk = pl.program_id(2)

        @pl.when(kk == 0)
        def _():
            acc_ref[...] = jnp.zeros_like(acc_ref)

        acc_ref[...] += _dot(a_ref[...], b_ref[...], ca, cb)

        @pl.when(kk == nk - 1)
        def _():
            finish(acc_ref[...])

    ins = [a, b] + ([add] if has_add else []) + c_in
    specs = [a_spec, b_spec] + ([o_spec] if has_add else []) + c_specs
    res = pl.pallas_call(
        body, name=name, grid=grid, in_specs=specs, out_specs=[o_spec] + c_specs,
        out_shape=[jax.ShapeDtypeStruct((m, n), out_dtype)] + c_shapes,
        scratch_shapes=([pltpu.VMEM((tm, tn), F32)] if nk > 1 else []) + c_sems,
        compiler_params=_params(("arbitrary",) * 3 if comm is not None else ("parallel", "parallel", "arbitrary")))(*ins)
    return res[0] if comm is None else (res[0], list(res[1:]))


def _row_spec(width, col_block=0, tm=TM):
    return pl.BlockSpec((tm, width), lambda i: (i, col_block))


def _full_spec(shape):
    nd = len(shape)
    return pl.BlockSpec(shape, lambda i: (0,) * nd)


def rows_fwd(name, fn, rows, params, outs, tm=TM):
    s = rows[0][0].shape[0]
    nr, npar = len(rows), len(params)

    def body(*refs):
        r = [refs[i][...].astype(F32) for i in range(nr)]
        p = [refs[nr + i][...] for i in range(npar)]
        res = fn(*r, *p)
        for o_ref, val in zip(refs[nr + npar:], res):
            o_ref[...] = val.astype(o_ref.dtype)

    return pl.pallas_call(
        body, name=name, grid=(s // tm,),
        in_specs=[_row_spec(w, cb, tm) for _, w, cb in rows] + [_full_spec(p.shape) for p in params],
        out_specs=[_row_spec(w, 0, tm) for w, _ in outs],
        out_shape=[jax.ShapeDtypeStruct((s, w), dt) for w, dt in outs],
        compiler_params=_params(("parallel",)))(*[a for a, _, _ in rows], *params)


def rows_bwd(name, fn, rows, params, cts, row_grads, tm=TM):
    s = rows[0][0].shape[0]
    nr, npar, nct, nrg = len(rows), len(params), len(cts), len(row_grads)

    def body(*refs):
        r = [refs[i][...].astype(F32) for i in range(nr)]
        p = [refs[nr + i][...].astype(F32) for i in range(npar)]
        g = tuple(refs[nr + npar + i][...].astype(F32) for i in range(nct))
        _, vjp = jax.vjp(lambda *a: tuple(fn(*a)), *r, *p)
        grads = vjp(g)
        outs = refs[nr + npar + nct:]
        for k, (idx, _) in enumerate(row_grads):
            outs[k][...] = grads[idx].astype(outs[k].dtype)

        @pl.when(pl.program_id(0) == 0)
        def _():
            for k in range(npar):
                outs[nrg + k][...] = jnp.zeros_like(outs[nrg + k])

        for k in range(npar):
            outs[nrg + k][...] += grads[nr + k]

    return pl.pallas_call(
        body, name=name, grid=(s // tm,),
        in_specs=[_row_spec(w, cb, tm) for _, w, cb in rows] + [_full_spec(p.shape) for p in params]
        + [_row_spec(w, 0, tm) for _, w in cts],
        out_specs=[_row_spec(rows[idx][1], 0, tm) for idx, _ in row_grads] + [_full_spec(p.shape) for p in params],
        out_shape=[jax.ShapeDtypeStruct((s, rows[idx][1]), dt) for idx, dt in row_grads]
        + [jax.ShapeDtypeStruct(p.shape, F32) for p in params],
        compiler_params=_params(("arbitrary",)))(*[a for a, _, _ in rows], *params, *[a for a, _ in cts])


def ln_fn(x, r, g, b):
    z = ALPHA * x + r
    mu = jnp.mean(z, axis=-1, keepdims=True)
    zc = z - mu
    var = jnp.mean(zc * zc, axis=-1, keepdims=True)
    return (zc * lax.rsqrt(var + LN_EPS) * g + b,)


def ln_twice(x, r, g, b):
    (y,) = ln_fn(x, r, g, b)
    return y, y


def _rms(x, g):
    return x * lax.rsqrt(jnp.mean(x * x, axis=-1, keepdims=True) + RMS_EPS) * g


def mla_q_fn(cq, c, s, g, wq):
    q = mm(_rms(cq, g), wq)
    return (rope(q, jnp.concatenate([c] * N_HEADS, axis=-1), jnp.concatenate([s] * N_HEADS, axis=-1)) * MLA_SCALE,)


@jax.custom_vjp
def tile_heads(x):
    return jnp.concatenate([x] * N_HEADS, axis=-1)


def _tile_fwd(x):
    return jnp.concatenate([x] * N_HEADS, axis=-1), None


def _tile_bwd(_, g):
    parts = _split_impl(g, N_HEADS)
    acc = parts[0]
    for p in parts[1:]:
        acc = acc + p
    return (acc,)


tile_heads.defvjp(_tile_fwd, _tile_bwd)


def mla_kv_fn(ckv, kr, c, s, g, wk, wv):
    n = _rms(ckv, g)
    lane = lax.broadcasted_iota(jnp.int32, (1, N_HEADS * LANE), 1) % LANE
    one = (lane == MLA_ONE).astype(F32)
    return mm(n, wk) + tile_heads(rope(kr, c, s)), mm(n, wv) + one


def merge_fn(y0, y1, y2, g0, g1, g2, b0, b1, b2):
    return (_sigmoid(g0 + b0) * y0 + _sigmoid(g1 + b1) * y1 + _sigmoid(g2 + b2) * y2,)


def combine_fn(o1, o2, o3, l1, l2, l3):
    mx = lax.stop_gradient(jnp.maximum(jnp.maximum(l1, l2), l3))
    e1, e2, e3 = jnp.exp(l1 - mx), jnp.exp(l2 - mx), jnp.exp(l3 - mx)
    return ((e1 * o1 + e2 * o2 + e3 * o3) / (e1 + e2 + e3),)


def loss_head(y, t):
    s, d = y.shape

    def body(y_ref, t_ref, dy_ref, loss_ref):
        err = y_ref[...] - t_ref[...]
        dy_ref[...] = err * (1.0 / d)

        @pl.when(pl.program_id(0) == 0)
        def _():
            loss_ref[...] = jnp.zeros_like(loss_ref)

        loss_ref[...] += jnp.sum(err * err) * (0.5 / d)

    return pl.pallas_call(
        body, name="loss_head", grid=(s // TM,), in_specs=[_row_spec(d), _row_spec(d)],
        out_specs=[_row_spec(d), _full_spec((8, LANE))],
        out_shape=[jax.ShapeDtypeStruct((s, d), F32), jax.ShapeDtypeStruct((8, LANE), F32)],
        compiler_params=_params(("arbitrary",)))(y, t)


CONV_TN = 256
CONV_TM = 512


def _conv_taps(u_ext, w_ref, b_ref):
    d1, d2 = pltpu.roll(u_ext, 1, 0), pltpu.roll(u_ext, 2, 0)
    return w_ref[0:1, :] * d2 + w_ref[1:2, :] * d1 + w_ref[2:3, :] * u_ext + b_ref[...], d1, d2


def conv_glu_fwd(u, conv_w, conv_b):
    s, f2 = u.shape
    f = f2 // 2
    tn, tm = CONV_TN, min(CONV_TM, s)
    off = f // tn
    hb = tm // 8

    def body(ug, uv, hg, hv, wg, wv, bg, bv, o_ref):
        first = pl.program_id(0) == 0

        def conv(u_ref, h_ref, w_ref, b_ref):
            halo = jnp.where(first, 0.0, h_ref[...])
            ext = jnp.concatenate([halo, u_ref[...]], axis=0)
            return _conv_taps(ext, w_ref, b_ref)[0][8:]

        cg, cv = conv(ug, hg, wg, bg), conv(uv, hv, wv, bv)
        o_ref[...] = (cg * _sigmoid(cg) * cv).astype(o_ref.dtype)

    blk = lambda o: pl.BlockSpec((tm, tn), lambda i, j: (i, j + o))
    halo = lambda o: pl.BlockSpec((8, tn), lambda i, j: (jnp.maximum(i * hb - 1, 0), j + o))
    par = lambda r, o: pl.BlockSpec((r, tn), lambda i, j: (0, j + o))
    return pl.pallas_call(
        body, name="conv_glu_fwd", grid=(s // tm, off),
        in_specs=[blk(0), blk(off), halo(0), halo(off), par(3, 0), par(3, off), par(1, 0), par(1, off)],
        out_specs=pl.BlockSpec((tm, tn), lambda i, j: (i, j)),
        out_shape=jax.ShapeDtypeStruct((s, f), BF16),
        compiler_params=_params(("parallel", "parallel")))(u, u, u, u, conv_w, conv_w, conv_b, conv_b)


def conv_glu_bwd(u, dh, conv_w, conv_b):
    s, f2 = u.shape
    f = f2 // 2
    tn, tm = CONV_TN, min(CONV_TM, s)
    off = f // tn
    hb = tm // 8
    n_rows = s // tm

    def body(ug, uv, pg, pv, ng, nv, dh_ref, dhn_ref, wg, wv, bg, bv, dug, duv, dwg, dwv, dbg, dbv):
        i = pl.program_id(1)
        first, last = i == 0, i == n_rows - 1

        def ext(u_ref, p_ref, n_ref):
            return jnp.concatenate([jnp.where(first, 0.0, p_ref[...]), u_ref[...], n_ref[...]], axis=0)

        eg, ev = ext(ug, pg, ng), ext(uv, pv, nv)
        (cg, eg1, eg2), (cv, ev1, ev2) = _conv_taps(eg, wg, bg), _conv_taps(ev, wv, bv)
        dhe = jnp.concatenate([jnp.zeros((8, tn), F32), dh_ref[...], jnp.where(last, 0.0, dhn_ref[...])], axis=0)
        sg = _sigmoid(cg)
        dcg = dhe * cv * (sg * (1.0 + cg * (1.0 - sg)))
        dcv = dhe * (cg * sg)
        n = tm + 16

        @pl.when(i == 0)
        def _():
            for r in (dwg, dwv, dbg, dbv):
                r[...] = jnp.zeros_like(r)

        def back(dc, e, e1, e2, w_ref, du_ref, dw_ref, db_ref):
            du = w_ref[2:3, :] * dc + w_ref[1:2, :] * pltpu.roll(dc, n - 1, 0) + w_ref[0:1, :] * pltpu.roll(dc, n - 2, 0)
            du_ref[...] = du[8:8 + tm].astype(du_ref.dtype)
            own = dc[8:8 + tm]
            dw_ref[0:1, :] += jnp.sum(own * e2[8:8 + tm], axis=0, keepdims=True)
            dw_ref[1:2, :] += jnp.sum(own * e1[8:8 + tm], axis=0, keepdims=True)
            dw_ref[2:3, :] += jnp.sum(own * e[8:8 + tm], axis=0, keepdims=True)
            db_ref[...] += jnp.sum(own, axis=0, keepdims=True)

        back(dcg, eg, eg1, eg2, wg, dug, dwg, dbg)
        back(dcv, ev, ev1, ev2, wv, duv, dwv, dbv)

    blk = lambda o: pl.BlockSpec((tm, tn), lambda j, i: (i, j + o))
    prev = lambda o: pl.BlockSpec((8, tn), lambda j, i: (jnp.maximum(i * hb - 1, 0), j + o))
    nxt = lambda o: pl.BlockSpec((8, tn), lambda j, i: (jnp.minimum((i + 1) * hb, s // 8 - 1), j + o))
    par = lambda r, o: pl.BlockSpec((r, tn), lambda j, i: (0, j + o))
    return pl.pallas_call(
        body, name="conv_glu_bwd", grid=(off, n_rows),
        in_specs=[blk(0), blk(off), prev(0), prev(off), nxt(0), nxt(off), blk(0), nxt(0),
                  par(3, 0), par(3, off), par(1, 0), par(1, off)],
        out_specs=[blk(0), blk(0), par(3, 0), par(3, 0), par(1, 0), par(1, 0)],
        out_shape=[jax.ShapeDtypeStruct((s, f), BF16)] * 2 + [jax.ShapeDtypeStruct((3, f), F32)] * 2
        + [jax.ShapeDtypeStruct((1, f), F32)] * 2,
        compiler_params=_params(("parallel", "arbitrary")))(u, u, u, u, u, u, dh, dh, conv_w, conv_w, conv_b, conv_b)


def _band_fn(q, kp, kc, vp, vc, bias, sink, first, inclusive):
    k = concat_rows(kp, kc)
    v = concat_rows(vp, vc)
    qi = lax.broadcasted_iota(jnp.int32, (BLOCK, 2 * BLOCK), 0)
    ki = lax.broadcasted_iota(jnp.int32, (BLOCK, 2 * BLOCK), 1)
    step = BLOCK + qi - ki
    valid = (step >= 0) & ((step <= BLOCK) if inclusive else (step < BLOCK))
    valid = valid & (jnp.logical_not(first) | (ki >= BLOCK))
    lane = lax.broadcasted_iota(jnp.int32, (1, LANE), 1)
    scale = HEAD_DIM ** -0.5
    qs, ks, vs = split_lanes(q, 4), split_lanes(k, 4), split_lanes(v, 4)
    o_parts, l_parts = [], []
    for pair in range(4):
        o_acc, l_acc = None, None
        for sub in range(2):
            h = 2 * pair + sub
            mh = ((lane >= sub * HEAD_DIM) & (lane < (sub + 1) * HEAD_DIM)).astype(F32)
            logits = mm_nt(qs[pair] * mh, ks[pair]) * scale + bias[h]
            logits = jnp.where(valid, logits, NEG)
            m = lax.stop_gradient(jnp.max(logits, axis=-1, keepdims=True))
            if sink is not None:
                m = jnp.maximum(m, lax.stop_gradient(sink[h]))
            p = jnp.exp(logits - m)
            den = jnp.sum(p, axis=-1, keepdims=True)
            if sink is not None:
                den = den + jnp.exp(sink[h] - m)
            oh = mm(p * (1.0 / den), vs[pair] * mh)
            lh = (m + jnp.log(den)) * mh
            o_acc = oh if o_acc is None else o_acc + oh
            l_acc = lh if l_acc is None else l_acc + lh
        o_parts.append(o_acc)
        l_parts.append(l_acc)
    return concat_lanes(tuple(o_parts)), concat_lanes(tuple(l_parts))


def _band_specs(nb, cq, ck, cv):
    own = lambda n: jnp.minimum(n, nb - 1)
    prev = lambda n: jnp.maximum(own(n) - 1, 0)
    q = pl.BlockSpec((BLOCK, HW), lambda r, n: (r * nb + own(n), cq))
    kp = pl.BlockSpec((BLOCK, HW), lambda r, n: (r * nb + prev(n), ck))
    kc = pl.BlockSpec((BLOCK, HW), lambda r, n: (r * nb + own(n), ck))
    vp = pl.BlockSpec((BLOCK, HW), lambda r, n: (r * nb + prev(n), cv))
    vc = pl.BlockSpec((BLOCK, HW), lambda r, n: (r * nb + own(n), cv))
    return [q, kp, kc, vp, vc]


def to_classes(a, dil):
    if dil == 1:
        return a
    s, c = a.shape
    return a.reshape(s // dil, dil, c).transpose(1, 0, 2).reshape(s, c)


def from_classes(a, dil):
    if dil == 1:
        return a
    s, c = a.shape
    return a.reshape(dil, s // dil, c).transpose(1, 0, 2).reshape(s, c)


def _const_spec(shape):
    nd = len(shape)
    return pl.BlockSpec(shape, lambda r, n: (0,) * nd)


def band_fwd(name, qkv, dil, cols, bias, sink, inclusive):
    s = qkv.shape[0]
    nb = s // (BLOCK * dil)
    has_sink = sink is not None

    def body(*refs):
        q, kp, kc, vp, vc, b_ref = refs[:6]
        s_ref = refs[6] if has_sink else None
        o_ref, l_ref = refs[-2:]
        first = pl.program_id(1) == 0
        bias_l = tuple(b_ref[h] for h in range(N_HEADS))
        sink_l = tuple(s_ref[h] for h in range(N_HEADS)) if has_sink else None
        o, l = _band_fn(q[...], kp[...], kc[...], vp[...], vc[...], bias_l, sink_l, first, inclusive)
        o_ref[...] = o
        l_ref[...] = l

    out_spec = pl.BlockSpec((BLOCK, HW), lambda r, n: (r * nb + n, 0))
    ins = [qkv] * 5 + [bias] + ([sink] if has_sink else [])
    specs = _band_specs(nb, *cols) + [_const_spec(bias.shape)] + ([_const_spec(sink.shape)] if has_sink else [])
    return pl.pallas_call(
        body, name=name, grid=(dil, nb), in_specs=specs, out_specs=[out_spec, out_spec],
        out_shape=[jax.ShapeDtypeStruct((s, HW), F32)] * 2,
        compiler_params=_params(("parallel", "parallel")))(*ins)


def band_bwd(name, qkv, dil, cols, bias, sink, inclusive, do, dl):
    s = qkv.shape[0]
    nb = s // (BLOCK * dil)
    has_sink, has_dl = sink is not None, dl is not None
    n_in = 6 + int(has_sink) + 1 + int(has_dl)

    def body(*refs):
        q, kp, kc, vp, vc, b_ref = refs[:6]
        s_ref = refs[6] if has_sink else None
        do_ref = refs[6 + int(has_sink)]
        dl_ref = refs[7 + int(has_sink)] if has_dl else None
        dq_ref, dk_ref, dv_ref, db_ref = refs[n_in:n_in + 4]
        ds_ref = refs[n_in + 4] if has_sink else None
        ck, cv = refs[-2:]
        n = pl.program_id(1)

        @pl.when((pl.program_id(0) == 0) & (n == 0))
        def _():
            db_ref[...] = jnp.zeros_like(db_ref)
            if has_sink:
                ds_ref[...] = jnp.zeros_like(ds_ref)

        @pl.when(n < nb)
        def _():
            first = n == 0
            bias_l = tuple(b_ref[h] for h in range(N_HEADS))
            sink_l = tuple(s_ref[h] for h in range(N_HEADS)) if has_sink else ()

            def fn(q_, kp_, kc_, vp_, vc_, b_, s_):
                o, l = _band_fn(q_, kp_, kc_, vp_, vc_, b_, s_ if has_sink else None, first, inclusive)
                return (o, l) if has_dl else (o,)

            _, vjp = jax.vjp(fn, q[...], kp[...], kc[...], vp[...], vc[...], bias_l, sink_l)
            ct = (do_ref[...], dl_ref[...]) if has_dl else (do_ref[...],)
            g = vjp(ct)
            dq_ref[...] = g[0].astype(dq_ref.dtype)

            @pl.when(n > 0)
            def _():
                dk_ref[...] = (ck[...] + g[1]).astype(dk_ref.dtype)
                dv_ref[...] = (cv[...] + g[3]).astype(dv_ref.dtype)

            ck[...] = g[2]
            cv[...] = g[4]
            for h in range(N_HEADS):
                db_ref[h] += g[5][h]
                if has_sink:
                    ds_ref[h] += g[6][h]

        @pl.when(n == nb)
        def _():
            dk_ref[...] = ck[...].astype(dk_ref.dtype)
            dv_ref[...] = cv[...].astype(dv_ref.dtype)

    blk = pl.BlockSpec((BLOCK, HW), lambda r, n: (r * nb + jnp.minimum(n, nb - 1), 0))
    late = pl.BlockSpec((BLOCK, HW), lambda r, n: (r * nb + jnp.maximum(n - 1, 0), 0))
    ins = [qkv] * 5 + [bias] + ([sink] if has_sink else []) + [do] + ([dl] if has_dl else [])
    specs = _band_specs(nb, *cols) + [_const_spec(bias.shape)] \
        + ([_const_spec(sink.shape)] if has_sink else []) + [blk] * (1 + int(has_dl))
    out_shape = [jax.ShapeDtypeStruct((s, HW), BF16)] * 3 + [jax.ShapeDtypeStruct(bias.shape, F32)] \
        + ([jax.ShapeDtypeStruct(sink.shape, F32)] if has_sink else [])
    out_specs = [blk, late, late, _const_spec(bias.shape)] + ([_const_spec(sink.shape)] if has_sink else [])
    res = pl.pallas_call(
        body, name=name, grid=(dil, nb + 1), in_specs=specs, out_specs=out_specs, out_shape=out_shape,
        scratch_shapes=[pltpu.VMEM((BLOCK, HW), F32)] * 2,
        compiler_params=_params(("arbitrary", "arbitrary")))(*ins)
    return list(res[:3]), res[3], (res[4] if has_sink else None)


def rel_grad(name, dbias, bucket):
    def body(d_ref, b_ref, o_ref):
        lane = lax.broadcasted_iota(jnp.int32, (1, LANE), 1)
        bk = b_ref[...]
        for h in range(N_HEADS):
            d = d_ref[h]
            row = jnp.zeros((1, LANE), F32)
            for b in range(REL_BUCKETS):
                tot = jnp.sum(jnp.where(bk == b, d, 0.0))
                row = row + jnp.where(lane == b, tot, 0.0)
            o_ref[h:h + 1, :] = row

    return pl.pallas_call(
        body, name=name, out_shape=jax.ShapeDtypeStruct((N_HEADS, LANE), F32),
        in_specs=[pl.BlockSpec(memory_space=pltpu.VMEM)] * 2, out_specs=pl.BlockSpec(memory_space=pltpu.VMEM),
        compiler_params=_params())(dbias, bucket)


MLA_T = 512
MLA_HPS = 2
MLA_SCALE = (C_NOPE + C_ROPE) ** -0.5


MLA_ONE = C_NOPE


def _lane_tiles(x, n):
    return jnp.concatenate([x] * n, axis=-1)


def _diag_mask(t):
    r = lax.broadcasted_iota(jnp.int32, (t, t), 0)
    c = lax.broadcasted_iota(jnp.int32, (t, t), 1)
    return c <= r


def _ride_along(comm, grid):
    if comm is None:
        return [], [], [], [], lambda *_: None
    arrays, scatter = comm
    out_shape, specs, sems = _push_shapes(arrays, scatter)

    def hook(ins, outs, sem_refs):
        ids = [pl.program_id(a) for a in range(len(grid))]
        first, last = ids[0] == 0, ids[0] == grid[0] - 1
        for a in range(1, len(grid)):
            first, last = first & (ids[a] == 0), last & (ids[a] == grid[a] - 1)

        @pl.when(first)
        def _():
            _pushes(ins, outs, sem_refs, scatter).start()

        @pl.when(last)
        def _():
            _pushes(ins, outs, sem_refs, scatter).wait()

    return list(arrays), specs, out_shape, sems, hook


def mla_fwd(name, q, k, v, comm=None):
    s = q.shape[0]
    t = min(MLA_T, s)
    nq = s // t
    grid = (N_HEADS // MLA_HPS, nq)
    c_in, c_specs, c_shapes, c_sems, c_hook = _ride_along(comm, grid)
    nc = len(c_in)

    def body(*refs):
        q_ref, k_ref, v_ref = refs[:3]
        o_ref, l_ref = refs[3 + nc:5 + nc]
        m_s, acc_s = refs[5 + 2 * nc:7 + 2 * nc]
        c_hook(refs[3:3 + nc], refs[5 + nc:5 + 2 * nc], refs[7 + 2 * nc:])
        i = pl.program_id(1)
        m_s[...] = jnp.full_like(m_s, NEG)
        acc_s[...] = jnp.zeros_like(acc_s)

        def chunk(j, masked):
            rows = pl.ds(pl.multiple_of(j * t, t), t)
            for hh in range(MLA_HPS):
                ln = slice(hh * LANE, (hh + 1) * LANE)
                sc = _dot(q_ref[:, ln], k_ref[rows, ln], 1, 1)
                if masked:
                    sc = jnp.where(_diag_mask(t), sc, NEG)
                m_old = m_s[:, ln]
                m_new = jnp.maximum(m_old, jnp.max(sc, axis=-1, keepdims=True))
                a = jnp.exp(m_old - m_new)
                p = jnp.exp(sc - _lane_tiles(m_new, t // LANE))
                acc_s[:, ln] = a * acc_s[:, ln] + _dot(p, v_ref[rows, ln], 1, 0)
                m_s[:, ln] = m_new

        def step(j, carry):
            chunk(j, False)
            return carry

        lax.fori_loop(0, i, step, 0)
        chunk(i, True)
        for hh in range(MLA_HPS):
            ln = slice(hh * LANE, (hh + 1) * LANE)
            acc = acc_s[:, ln]
            den = acc[:, MLA_ONE:MLA_ONE + 1]
            o_ref[:, ln] = acc / den
            l_ref[:, ln] = m_s[:, ln] + jnp.log(den)

    w = MLA_HPS * LANE
    qs = pl.BlockSpec((t, w), lambda h, i: (i, h))
    ks = pl.BlockSpec((s, w), lambda h, i: (0, h))
    res = pl.pallas_call(
        body, name=name, grid=grid, in_specs=[qs, ks, ks] + c_specs, out_specs=[qs, qs] + c_specs,
        out_shape=[jax.ShapeDtypeStruct((s, N_HEADS * LANE), F32)] * 2 + c_shapes,
        scratch_shapes=[pltpu.VMEM((t, w), F32)] * 2 + c_sems,
        compiler_params=_params(("arbitrary", "arbitrary")))(q, k, v, *c_in)
    return res[0], res[1], list(res[2:])


def mla_delta_fn(do, o):
    parts = []
    for a, b in zip(_split_impl(do, N_HEADS), _split_impl(o, N_HEADS)):
        parts.append(jnp.sum(a * b, axis=-1, keepdims=True) + jnp.zeros_like(a))
    return jnp.concatenate(parts, axis=-1), do


def mla_bwd(name, q, k, v, do, lse, delta, comm=None):
    s = q.shape[0]
    t = min(MLA_T, s)
    nq = s // t
    grid = (N_HEADS, nq)
    c_in, c_specs, c_shapes, c_sems, c_hook = _ride_along(comm, grid)
    nc = len(c_in)

    def body(*refs):
        q_ref, k_ref, v_ref, do_ref, l_ref, d_ref = refs[:6]
        dq_ref, dk_ref, dv_ref = refs[6 + nc:9 + nc]
        dk_s, dv_s = refs[9 + 2 * nc:11 + 2 * nc]
        c_hook(refs[6:6 + nc], refs[9 + nc:9 + 2 * nc], refs[11 + 2 * nc:])
        j = pl.program_id(1)

        @pl.when(j == 0)
        def _():
            dq_ref[...] = jnp.zeros_like(dq_ref)

        dk_s[...] = jnp.zeros_like(dk_s)
        dv_s[...] = jnp.zeros_like(dv_s)
        kb, vb = k_ref[...], v_ref[...]

        def chunk(i, masked):
            rows = pl.ds(pl.multiple_of(i * t, t), t)
            qi, doi = q_ref[rows, :], do_ref[rows, :]
            sc = _dot(qi, kb, 1, 1)
            if masked:
                sc = jnp.where(_diag_mask(t), sc, NEG)
            p = jnp.exp(sc - _lane_tiles(l_ref[rows, :], t // LANE))
            ds = p * (_dot(doi, vb, 1, 1) - _lane_tiles(d_ref[rows, :], t // LANE))
            dv_s[...] += _dot(p, doi, 0, 0)
            dk_s[...] += _dot(ds, qi, 0, 0)
            dq_ref[rows, :] += _dot(ds, kb, 1, 0)

        def pair(p, carry):
            chunk(j + 1 + 2 * p, False)
            chunk(j + 2 + 2 * p, False)
            return carry

        chunk(j, True)
        n_off = nq - 1 - j
        lax.fori_loop(0, n_off // 2, pair, 0)

        @pl.when(n_off % 2 == 1)
        def _():
            chunk(nq - 1, False)

        dk_ref[...] = dk_s[...]
        dv_ref[...] = dv_s[...]

    ks = pl.BlockSpec((t, LANE), lambda h, j: (j, h))
    full = pl.BlockSpec((s, LANE), lambda h, j: (0, h))
    res = pl.pallas_call(
        body, name=name, grid=grid, in_specs=[full, ks, ks, full, full, full] + c_specs,
        out_specs=[full, ks, ks] + c_specs,
        out_shape=[jax.ShapeDtypeStruct((s, N_HEADS * LANE), F32)] * 3 + c_shapes,
        scratch_shapes=[pltpu.VMEM((t, LANE), F32)] * 2 + c_sems,
        compiler_params=_params(("arbitrary", "arbitrary")))(q, k, v, do, lse, delta, *c_in)
    return res[0], res[1], res[2], list(res[3:])


def _peer(k):
    x, y, c = lax.axis_index("x"), lax.axis_index("y"), lax.axis_index("c")
    px, py, pc = (x ^ ((k >> 2) & 1)), (y ^ ((k >> 1) & 1)), (c ^ (k & 1))
    return (px, py, pc), 4 * px + 2 * py + pc


def exchange(name, arrays, scatter):
    n = len(arrays)

    def body(*refs):
        push = _pushes(refs[:n], refs[n:2 * n], refs[2 * n:], scatter)
        push.start()
        push.wait()

    out_shape, specs, sems = _push_shapes(arrays, scatter)
    return pl.pallas_call(body, name=name, out_shape=out_shape, in_specs=specs, out_specs=specs,
                          scratch_shapes=sems)(*arrays)


def _push_shapes(arrays, scatter):
    n = len(arrays)
    out_shape = [jax.ShapeDtypeStruct(a.shape if scatter else (N_DEV,) + a.shape, a.dtype) for a in arrays]
    n_sem = (N_DEV - 1) * n
    sems = [pltpu.SemaphoreType.DMA((n_sem,)), pltpu.SemaphoreType.DMA((n_sem,)), pltpu.SemaphoreType.DMA((n,))]
    return out_shape, [pl.BlockSpec(memory_space=pl.ANY)] * n, sems


class _pushes:
    def __init__(self, ins, outs, sems, scatter):
        self.ins, self.outs, self.sems, self.scatter = ins, outs, sems, scatter

    def _local(self):
        _, me_idx = _peer(0)
        return [pltpu.make_async_copy(self.ins[a].at[me_idx] if self.scatter else self.ins[a],
                                      self.outs[a].at[me_idx], self.sems[2].at[a]) for a in range(len(self.ins))]

    def _remote(self, incoming):
        send_sems, recv_sems, _ = self.sems
        n = len(self.ins)
        _, me_idx = _peer(0)
        copies = []
        for k in range(1, N_DEV):
            peer, peer_idx = _peer(k)
            for a in range(n):
                sem = (k - 1) * n + a
                copies.append(pltpu.make_async_remote_copy(
                    src_ref=self.ins[a].at[peer_idx] if self.scatter else self.ins[a],
                    dst_ref=self.outs[a].at[peer_idx if incoming else me_idx],
                    send_sem=send_sems.at[sem], recv_sem=recv_sems.at[sem], device_id=peer, device_id_type=MESH_ID))
        return copies

    def start(self):
        for cp in self._local() + self._remote(incoming=False):
            cp.start()

    def wait(self):
        for cp in self._remote(incoming=True):
            cp.wait_recv()
        for cp in self._remote(incoming=False):
            cp.wait_send()
        for cp in self._local():
            cp.wait()


ADAMW_BLOCK_ELEMS = 128 * 1024


def adamw_sum(name, parts, w, m, v):
    nl, r, c = w.shape
    fits = [t for t in range(16, r + 1, 16) if r % t == 0 and t * c <= ADAMW_BLOCK_ELEMS]
    tr = max(fits) if fits else r
    c1 = 1.0 - ADAM_B1 ** ADAM_STEP
    c2 = 1.0 - ADAM_B2 ** ADAM_STEP

    def body(*refs):
        p_refs = refs[:nl]
        w_ref, m_ref, v_ref, g_out, d_out, m_out, v_out = refs[nl:]
        for layer in range(nl):
            @pl.when(pl.program_id(0) == layer)
            def _():
                g = p_refs[layer][0].astype(F32)
                for d in range(1, N_DEV):
                    g = g + p_refs[layer][d].astype(F32)
                mn = ADAM_B1 * m_ref[...] + (1.0 - ADAM_B1) * g
                vn = ADAM_B2 * v_ref[...] + (1.0 - ADAM_B2) * (g * g)
                g_out[...] = g
                m_out[...] = mn
                v_out[...] = vn
                d_out[...] = -ADAM_LR * ((mn / c1) / (jnp.sqrt(vn / c2) + ADAM_EPS) + ADAM_WD * w_ref[...])

    row = pl.BlockSpec((None, tr, c), lambda l, i: (l, i, 0))
    return pl.pallas_call(
        body, name=name, grid=(nl, r // tr),
        in_specs=[pl.BlockSpec((N_DEV, tr, c), lambda l, i: (0, i, 0))] * nl + [row, row, row],
        out_specs=[row] * 4, out_shape=[jax.ShapeDtypeStruct((nl, r, c), F32)] * 4,
        compiler_params=_params(("parallel", "parallel")))(*parts, w, m, v)


def _pack(arrays, dtype, row_mult):
    flat = jnp.concatenate([a.astype(dtype).reshape(-1) for a in arrays])
    n = flat.shape[0]
    quantum = row_mult * LANE
    total = -(-n // quantum) * quantum
    return jnp.pad(flat, (0, total - n)).reshape(total // LANE, LANE)


def _unpack(packed, shapes):
    flat = packed.reshape(-1)
    out, pos = [], 0
    for shp in shapes:
        n = int(np.prod(shp))
        out.append(flat[pos:pos + n].reshape(shp))
        pos += n
    return out


def _rows(a):
    return a.reshape(-1, a.shape[-1])


def _unshard(gathered, shard_shape, axis):
    t = jnp.moveaxis(gathered.reshape((N_DEV,) + tuple(shard_shape)), 0, axis)
    return t.reshape(tuple(shard_shape[:axis]) + (N_DEV * shard_shape[axis],) + tuple(shard_shape[axis + 1:]))


def _to_shards(full, axis):
    shp = full.shape
    t = full.reshape(shp[:axis] + (N_DEV, shp[axis] // N_DEV) + shp[axis + 1:])
    t = jnp.moveaxis(t, axis, 0)
    return t.reshape(N_DEV, -1, t.shape[-1])


A_COLS = len(A_GROUPS) * 3 * HW
KV_COLS = B_KV_HEADS * HEAD_DIM
KV_REP = N_HEADS // B_KV_HEADS
O_BQ = A_COLS
O_BK = O_BQ + HW
O_BV = O_BK + KV_COLS
O_CQ = O_BV + KV_COLS
O_CKV = O_CQ + C_Q_RANK
O_KR = O_CKV + C_KV_RANK
O_GATE = O_KR + C_ROPE


def _proj_weight(w_in):
    d = w_in.shape[0]

    def per_q_head(lo):
        t = w_in[:, lo:lo + KV_COLS].reshape(d, B_KV_HEADS, 1, HEAD_DIM)
        return jnp.broadcast_to(t, (d, B_KV_HEADS, KV_REP, HEAD_DIM)).reshape(d, HW)

    kr = jnp.pad(w_in[:, O_KR:O_GATE], ((0, 0), (C_NOPE, LANE - C_NOPE - C_ROPE)))
    return jnp.concatenate([w_in[:, O_GATE:], w_in[:, :O_BK], per_q_head(O_BK), per_q_head(O_BV),
                            w_in[:, O_CQ:O_KR], kr], axis=1)


def _unproj_grad(dw_p, d_model):
    d = dw_p.shape[0]
    g = 3 * d_model
    p_bk = g + O_BK
    p_bv = p_bk + HW
    p_cq = p_bv + HW
    p_kr = p_cq + C_Q_RANK + C_KV_RANK

    def sum_heads(lo):
        return dw_p[:, lo:lo + HW].reshape(d, B_KV_HEADS, KV_REP, HEAD_DIM).sum(axis=2).reshape(d, KV_COLS)

    return jnp.concatenate([dw_p[:, g:p_bk], sum_heads(p_bk), sum_heads(p_bv), dw_p[:, p_cq:p_kr],
                            dw_p[:, p_kr + C_NOPE:p_kr + C_NOPE + C_ROPE], dw_p[:, :g]], axis=1)


def _pad_heads(w, per_head, lo, hi):
    r = w.shape[0]
    t = w.reshape(r, N_HEADS, per_head)[:, :, lo:hi]
    return jnp.pad(t, ((0, 0), (0, 0), (0, LANE - (hi - lo)))).reshape(r, N_HEADS * LANE)


def _unpad_heads(w, width):
    r = w.shape[0]
    return w.reshape(r, N_HEADS, LANE)[:, :, :width]


def _t5_bucket(dist):
    n = jnp.maximum(dist, 0)
    max_exact = REL_BUCKETS // 2
    scaled = jnp.log(jnp.maximum(n, 1).astype(F32) / max_exact) / math.log(REL_MAX_DIST / max_exact)
    large = max_exact + (scaled * (REL_BUCKETS - max_exact)).astype(jnp.int32)
    return jnp.where(n < max_exact, n, jnp.minimum(large, REL_BUCKETS - 1))


def _buckets(dil):
    qi = jnp.arange(BLOCK)[:, None]
    ki = jnp.arange(2 * BLOCK)[None, :]
    return _t5_bucket((BLOCK + qi - ki) * dil).astype(jnp.int32)


def _rope_tables(s):
    pos = jnp.arange(s, dtype=F32)
    inv_freq = ROPE_BASE ** (-jnp.arange(0, C_ROPE, 2, dtype=F32) / C_ROPE)
    ang = pos[:, None] * inv_freq[None, :]
    cos, sin = jnp.cos(ang), jnp.sin(ang)
    ones = jnp.ones((s, C_NOPE), F32)
    tail = LANE - C_NOPE - C_ROPE
    c = jnp.concatenate([ones, cos, cos, jnp.ones((s, tail), F32)], axis=1)
    sn = jnp.concatenate([jnp.zeros((s, C_NOPE), F32), sin, sin, jnp.zeros((s, tail), F32)], axis=1)
    return c, sn


def _cols(d_model):
    g = 3 * d_model // HW
    a = [(g + 3 * i, g + 3 * i + 1, g + 3 * i + 2) for i in range(len(A_GROUPS))]
    b = (g + 9, g + 10, g + 11)
    cq_lane = (g + 12) * HW
    return a, b, cq_lane


def _layer_weights(full, d_model):
    lw = {}
    if 'w_in' in full:
        lw['w_in_p'] = _proj_weight(full['w_in'])
    if 'w_uq' in full:
        per = C_NOPE + C_ROPE
        w_uq, w_ukv, wb = full['w_uq'], full['w_ukv'], full['w_branch']
        wb2 = jnp.pad(wb[2].reshape(N_HEADS, HEAD_DIM, d_model), ((0, 0), (0, LANE - HEAD_DIM), (0, 0)))
        lw.update(
            wq=_pad_heads(w_uq, per, 0, per), wk=_pad_heads(w_ukv, 2 * C_NOPE, 0, C_NOPE),
            wv=_pad_heads(w_ukv, 2 * C_NOPE, C_NOPE, 2 * C_NOPE), wb0=wb[0], wb1=wb[1],
            wb2=wb2.reshape(N_HEADS * LANE, d_model), w_out=full['w_out'], w_up=full['w_ffn_up'],
            w_down=full['w_ffn_down'], conv_w=full['conv_w'])
    return lw


def _layer_fwd(l, x, x16, lw, small, tabs, comm=None, proj_comm=None):
    s, d = x.shape
    a_cols, b_cols, cq_lane = _cols(d)
    tag = f"l{l}_"
    if proj_comm is None:
        proj = matmul(tag + "proj", x16, lw['w_in_p'], 'nn')
    else:
        proj, got = matmul(tag + "proj", x16, lw['w_in_p'], 'nn', comm=proj_comm[0])
        lw = dict(lw, **_layer_weights(proj_comm[1](got), d))
    res = dict(x=x, x16=x16, proj=proj, lw=lw)
    outs, lses, slabs = [], [], []
    for gi, (window, dil) in enumerate(A_GROUPS):
        if dil == 1:
            slab, cols = proj, a_cols[gi]
        else:
            slab, cols = to_classes(proj[:, a_cols[gi][0] * HW:(a_cols[gi][2] + 1) * HW], dil), (0, 1, 2)
        o, lg = band_fwd(tag + f"a{gi}_fwd", slab, dil, cols, tabs['bias'][gi], None, True)
        outs.append(from_classes(o, dil))
        lses.append(from_classes(lg, dil))
        slabs.append((slab, cols))
    res['a_o'], res['a_l'], res['a_slabs'] = outs, lses, slabs
    (o_a,) = rows_fwd(tag + "a_combine", combine_fn, [(o, HW, 0) for o in outs] + [(lg, HW, 0) for lg in lses], [],
                      [(HW, BF16)])
    sink = small['sinks'][l].reshape(N_HEADS, 1, 1)
    o_b, _ = band_fwd(tag + "b_fwd", proj, 1, b_cols, tabs['bias'][3], sink, False)
    gq = small['q_norm_g'][l].reshape(1, -1)
    gkv = small['kv_norm_g'][l].reshape(1, -1)
    (qc,) = rows_fwd(tag + "c_q", mla_q_fn, [(proj, C_Q_RANK, cq_lane // C_Q_RANK), (tabs['cos'], LANE, 0),
                                            (tabs['sin'], LANE, 0)], [gq, lw['wq']], [(N_HEADS * LANE, BF16)])
    ckv_blk = (cq_lane + C_Q_RANK) // LANE
    kc, vc = rows_fwd(tag + "c_kv", mla_kv_fn, [(proj, LANE, ckv_blk), (proj, LANE, ckv_blk + 1), (tabs['cos'], LANE, 0),
                                                (tabs['sin'], LANE, 0)], [gkv, lw['wk'], lw['wv']],
                      [(N_HEADS * LANE, BF16)] * 2)
    o_c, lse_c, res['comm'] = mla_fwd(tag + "mla_fwd", qc, kc, vc, comm)
    res.update(o_a=o_a, o_b=o_b, o_c=o_c, lse_c=lse_c, qc=qc, kc=kc, vc=vc)
    bg = small['b_gate'][l].reshape(3, 1, d)
    ys = [matmul(tag + f"branch{i}", o, lw[f'wb{i}'], 'nn') for i, o in enumerate((o_a, o_b, o_c))]
    (merged,) = rows_fwd(tag + "merge", merge_fn, [(y, d, 0) for y in ys] + [(proj, d, i) for i in range(3)],
                         [bg[0], bg[1], bg[2]], [(d, BF16)])
    mix = matmul(tag + "out_proj", merged, lw['w_out'], 'nn')
    ln1 = [small['ln1_g'][l].reshape(1, d), small['ln1_b'][l].reshape(1, d)]
    x1, x1_16 = rows_fwd(tag + "ln1", ln_twice, [(x, d, 0), (mix, d, 0)], ln1, [(d, F32), (d, BF16)])
    u = matmul(tag + "ffn_up", x1_16, lw['w_up'], 'nn')
    h = conv_glu_fwd(u, lw['conv_w'], small['conv_b'][l].reshape(1, -1))
    ff = matmul(tag + "ffn_down", h, lw['w_down'], 'nn')
    ln2 = [small['ln2_g'][l].reshape(1, d), small['ln2_b'][l].reshape(1, d)]
    x2, x2_16 = rows_fwd(tag + "ln2", ln_twice, [(x1, d, 0), (ff, d, 0)], ln2, [(d, F32), (d, BF16)])
    res.update(merged=merged, mix=mix, x1=x1, x1_16=x1_16, u=u, h=h, ff=ff, ys=ys)
    return x2, x2_16, res


def _layer_bwd(l, dy, res, lw, small, tabs, comm=None, dw_comm=None):
    x, proj = res['x'], res['proj']
    s, d = x.shape
    a_cols, b_cols, cq_lane = _cols(d)
    tag = f"l{l}_"
    g = {}
    ln2 = [small['ln2_g'][l].reshape(1, d), small['ln2_b'][l].reshape(1, d)]
    dff, g['ln2_g'], g['ln2_b'] = rows_bwd(tag + "ln2_bwd", ln_fn, [(res['x1'], d, 0), (res['ff'], d, 0)], ln2,
                                           [(dy, d)], [(1, F32)])
    g['w_ffn_down'] = matmul(tag + "dw_down", res['h'], dff, 'tn')
    dh = matmul(tag + "dh", dff, lw['w_down'], 'nt')
    conv_b = small['conv_b'][l].reshape(1, -1)
    dug, duv, dwg, dwv, dbg, dbv = conv_glu_bwd(res['u'], dh, lw['conv_w'], conv_b)
    du = jnp.concatenate([dug, duv], axis=1)
    g['conv_w'] = jnp.concatenate([dwg, dwv], axis=1)
    g['conv_b'] = jnp.concatenate([dbg, dbv], axis=1).reshape(-1)
    g['w_ffn_up'] = matmul(tag + "dw_up", res['x1_16'], du, 'tn')
    dx1 = matmul(tag + "dx1", du, lw['w_up'], 'nt', add=dff, scale=ALPHA)
    ln1 = [small['ln1_g'][l].reshape(1, d), small['ln1_b'][l].reshape(1, d)]
    dmix, g['ln1_g'], g['ln1_b'] = rows_bwd(tag + "ln1_bwd", ln_fn, [(x, d, 0), (res['mix'], d, 0)], ln1,
                                            [(dx1, d)], [(1, F32)])
    g['w_out'] = matmul(tag + "dw_out", res['merged'], dmix, 'tn')
    dmerged = matmul(tag + "dmerged", dmix, lw['w_out'], 'nt')
    bg = small['b_gate'][l].reshape(3, 1, d)
    mrows = [(y, d, 0) for y in res['ys']] + [(proj, d, i) for i in range(3)]
    (dy0, dy1, dy2, dg0, dg1, dg2, db0, db1, db2) = rows_bwd(
        tag + "merge_bwd", merge_fn, mrows, [bg[0], bg[1], bg[2]], [(dmerged, d)], [(i, BF16) for i in range(6)])
    g['b_gate'] = jnp.concatenate([db0, db1, db2], axis=1).reshape(-1)
    branch_in = (res['o_a'], res['o_b'], res['o_c'])
    dwb = [matmul(tag + f"dw_branch{i}", o, dyi, 'tn') for i, (o, dyi) in enumerate(zip(branch_in, (dy0, dy1, dy2)))]
    dwb[2] = dwb[2].reshape(N_HEADS, LANE, d)[:, :HEAD_DIM].reshape(HW, d)
    g['w_branch'] = jnp.stack(dwb, axis=0)
    do_a, do_b, do_c = [matmul(tag + f"do_branch{i}", dyi, lw[f'wb{i}'], 'nt')
                        for i, dyi in enumerate((dy0, dy1, dy2))]
    cw = N_HEADS * LANE
    delta, do_c16 = rows_fwd(tag + "c_delta", mla_delta_fn, [(do_c, cw, 0), (res['o_c'], cw, 0)], [],
                             [(cw, F32), (cw, BF16)])
    dqc, dkc, dvc, g['comm'] = mla_bwd(tag + "mla_bwd", res['qc'], res['kc'], res['vc'], do_c16, res['lse_c'], delta,
                                       comm)
    gq = small['q_norm_g'][l].reshape(1, -1)
    gkv = small['kv_norm_g'][l].reshape(1, -1)
    dcq, dgq, dwq = rows_bwd(tag + "c_q_bwd", mla_q_fn,
                             [(proj, C_Q_RANK, cq_lane // C_Q_RANK), (tabs['cos'], LANE, 0), (tabs['sin'], LANE, 0)],
                             [gq, lw['wq']], [(dqc, N_HEADS * LANE)], [(0, BF16)])
    ckv_blk = (cq_lane + C_Q_RANK) // LANE
    dckv, dkr, dgkv, dwk, dwv = rows_bwd(
        tag + "c_kv_bwd", mla_kv_fn,
        [(proj, LANE, ckv_blk), (proj, LANE, ckv_blk + 1), (tabs['cos'], LANE, 0), (tabs['sin'], LANE, 0)],
        [gkv, lw['wk'], lw['wv']], [(dkc, N_HEADS * LANE), (dvc, N_HEADS * LANE)], [(0, BF16), (1, BF16)])
    g['q_norm_g'], g['kv_norm_g'] = dgq.reshape(-1), dgkv.reshape(-1)
    per = C_NOPE + C_ROPE
    g['w_uq'] = _unpad_heads(dwq, per).reshape(C_Q_RANK, N_HEADS * per)
    g['w_ukv'] = jnp.concatenate([_unpad_heads(dwk, C_NOPE), _unpad_heads(dwv, C_NOPE)], axis=2).reshape(
        C_KV_RANK, N_HEADS * 2 * C_NOPE)
    sink = small['sinks'][l].reshape(N_HEADS, 1, 1)
    pieces_b, dbias_b, dsink = band_bwd(tag + "b_bwd", proj, 1, b_cols, tabs['bias'][3], sink, False, do_b, None)
    g['sinks'] = dsink.reshape(-1)
    dbias = [None] * 4
    dbias[3] = dbias_b
    combo = [(o, HW, 0) for o in res['a_o']] + [(lg, HW, 0) for lg in res['a_l']]
    a_cts = rows_bwd(tag + "a_combine_bwd", combine_fn, combo, [], [(do_a, HW)], [(i, F32) for i in range(6)])
    pieces_a = []
    for gi, (window, dil) in enumerate(A_GROUPS):
        slab, cols = res['a_slabs'][gi]
        group, dbias[gi], _ = band_bwd(
            tag + f"a{gi}_bwd", slab, dil, cols, tabs['bias'][gi], None, True, to_classes(a_cts[gi], dil),
            to_classes(a_cts[3 + gi], dil))
        pieces_a += [from_classes(t, dil) for t in group]
    g['dbias'] = dbias
    dproj = jnp.concatenate([dg0, dg1, dg2] + pieces_a + pieces_b + [dcq, dckv, dkr], axis=1)
    if dw_comm is None:
        dw_in_p = matmul(tag + "dw_in", res['x16'], dproj, 'tn')
    else:
        dw_in_p, g['dw_comm'] = matmul(tag + "dw_in", res['x16'], dproj, 'tn', comm=dw_comm(g))
    g['w_in'] = _unproj_grad(dw_in_p, d)
    dx = matmul(tag + "dx", dproj, lw['w_in_p'], 'nt', add=dmix, scale=ALPHA)
    return dx, g


def kernel(x, rel_table, w_in, b_gate, sinks, q_norm_g, kv_norm_g, w_uq, w_ukv, w_branch, w_out, ln1_g, ln1_b, w_ffn_up, conv_w, conv_b, w_ffn_down, ln2_g, ln2_b, loss_target, m_rel_table, m_w_in, m_b_gate, m_sinks, m_q_norm_g, m_kv_norm_g, m_w_uq, m_w_ukv, m_w_branch, m_w_out, m_ln1_g, m_ln1_b, m_w_ffn_up, m_conv_w, m_conv_b, m_w_ffn_down, m_ln2_g, m_ln2_b, v_rel_table, v_w_in, v_b_gate, v_sinks, v_q_norm_g, v_kv_norm_g, v_w_uq, v_w_ukv, v_w_branch, v_w_out, v_ln1_g, v_ln1_b, v_w_ffn_up, v_conv_w, v_conv_b, v_w_ffn_down, v_ln2_g, v_ln2_b):
    args = locals()
    w = {n: args[n] for n in WEIGHTS}
    mom = {n: args["m_" + n] for n in WEIGHTS}
    var = {n: args["v_" + n] for n in WEIGHTS}
    s, d = x.shape[1], x.shape[2]
    xs = x.reshape(s, d)
    target = loss_target.reshape(s, d)

    big = [n for n in WEIGHTS if n in SHARDED]
    small_names = [n for n in WEIGHTS if n not in SHARDED]
    assert DEPTH == 2
    wire = lambda n: BF16 if n in GATHER_BF16 else F32
    rest = [n for n in big if n != 'w_in']

    def shards(l, names):
        return [_rows(w[n][l]).astype(wire(n)) for n in names]

    def to_full(names, gathered):
        return {n: _unshard(t, w[n].shape[1:], SHARDED[n] - 1) for n, t in zip(names, gathered)}

    def grad_shards(names, g):
        return [_to_shards(g[n], SHARDED[n] - 1).astype(wire(n)) for n in names]

    w_in0 = to_full(['w_in'], exchange("gather_w_in0", shards(0, ['w_in']), scatter=False))
    fused = dict(proj0=((shards(0, rest), False), lambda got: to_full(rest, got)),
                 fwd0=((shards(1, big), False), lambda got: to_full(big, got)),
                 bwd0=lambda g1: (grad_shards(big, g1), True),
                 dw0=lambda g0: (grad_shards(rest, g0), True))
    small = {n: w[n] for n in small_names}
    loss_part, grad_x, grads, g_rel, recv1, recv0_rest = _local_grads(xs, target, [w_in0, None], small, fused)
    grad_x = grad_x.reshape(x.shape)
    recv0 = dict(zip(rest, recv0_rest))
    (recv0['w_in'],) = exchange("exchange_dw_in0", grad_shards(['w_in'], grads[0]), scatter=True)

    big_res = [{}, {}, {}, {}]
    per_layer = lambda t: t.reshape((DEPTH, -1, t.shape[-1]))
    for i, n in enumerate(big):
        outs = adamw_sum("adamw_" + n, [recv0[n], recv1[i]], per_layer(w[n]), per_layer(mom[n]), per_layer(var[n]))
        for k in range(4):
            big_res[k][n] = outs[k].reshape(w[n].shape)

    g_small = {'rel_table': g_rel}
    for n in small_names:
        if n != 'rel_table':
            g_small[n] = jnp.stack([grads[l][n] for l in range(DEPTH)], axis=0)
    small_pack = _pack([g_small[n].reshape(w[n].shape) for n in small_names] + [loss_part[0, 0:1]], F32, 8)
    (small_all,) = exchange("gather_small_grads", [small_pack], scatter=False)
    pks = lambda tree: _pack([tree[n] for n in small_names] + [jnp.zeros((1,), F32)], F32, 8)[None]
    small_out = adamw_sum("adamw_replicated", [small_all], pks(w), pks(mom), pks(var))
    small_shapes = [w[n].shape for n in small_names] + [(1,)]
    small_res = [dict(zip(small_names + ['loss'], _unpack(o, small_shapes))) for o in small_out]

    loss = small_res[0]['loss'].reshape(())
    out = [loss, grad_x]
    for k in range(4):
        out += [big_res[k][n] if n in SHARDED else small_res[k][n] for n in WEIGHTS]
    return tuple(out)


def _local_grads(xs, target, fulls, small, fused=None):
    s, d = xs.shape
    rel_table = small['rel_table']
    fulls = list(fulls)

    cos, sin = _rope_tables(s)
    buckets = [_buckets(dil) for _, dil in A_GROUPS] + [_buckets(1)]
    def lookup(table, bucket):
        out = jnp.zeros((N_HEADS,) + bucket.shape, F32)
        for b in range(REL_BUCKETS):
            out = jnp.where((bucket == b)[None], table[b][:, None, None], out)
        return out

    bias = [lookup(rel_table[:, gi * N_HEADS:(gi + 1) * N_HEADS], buckets[gi]) for gi in range(4)]
    tabs = dict(cos=cos, sin=sin, bias=bias)

    act, act16, saved = xs, xs.astype(BF16), []
    for l in range(DEPTH):
        lw = _layer_weights(fulls[l], d)
        ride = fused is not None and l == 0
        act, act16, res = _layer_fwd(l, act, act16, lw, small, tabs, fused['fwd0'][0] if ride else None,
                                     fused['proj0'] if ride else None)
        if ride:
            fulls[1] = fused['fwd0'][1](res['comm'])
        saved.append(res)
    dy, loss_part = loss_head(act, target)

    grads, recv1, recv0_rest = [None] * DEPTH, None, None
    for l in reversed(range(DEPTH)):
        ride = fused is not None and l == 0
        dy, grads[l] = _layer_bwd(l, dy, saved[l], saved[l]['lw'], small, tabs,
                                  fused['bwd0'](grads[1]) if ride else None, fused['dw0'] if ride else None)
        if ride:
            recv1, recv0_rest = grads[l]['comm'], grads[l]['dw_comm']
    grad_x = dy

    rel_cols = []
    for gi in range(4):
        both = grads[0]['dbias'][gi] + grads[1]['dbias'][gi] if DEPTH == 2 else grads[0]['dbias'][gi]
        rel_cols.append(rel_grad(f"rel_grad{gi}", both, buckets[gi])[:, :REL_BUCKETS].T)
    g_rel = jnp.concatenate(rel_cols, axis=1)
    return loss_part, grad_x, grads, g_rel, recv1, recv0_rest
```

```python
import functools
import math

import numpy as np
import jax
import jax.numpy as jnp
from jax import lax
from jax.experimental import pallas as pl
from jax.experimental.pallas import tpu as pltpu

F32 = jnp.float32
BF16 = jnp.bfloat16

N_DEV = 8
DEPTH = 2
HEAD_DIM = 64
BLOCK = 128
A_GROUPS = ((128, 1), (512, 4), (2048, 16))
N_HEADS = 8
HW = N_HEADS * HEAD_DIM
B_KV_HEADS = 2
C_Q_RANK = 256
C_KV_RANK = 128
C_NOPE = 64
C_ROPE = 32
ROPE_BASE = 10000.0
REL_BUCKETS = 32
REL_MAX_DIST = 2048
ALPHA = (2 * DEPTH) ** 0.25
LN_EPS = 1e-5
RMS_EPS = 1e-6
NEG = -1e30
ADAM_LR = 0.001
ADAM_B1 = 0.9
ADAM_B2 = 0.999
ADAM_EPS = 1e-08
ADAM_WD = 0.01
ADAM_STEP = 10

LANE = 128
TM = 256
VMEM_LIMIT = 56 * 1024 * 1024
MESH_ID = pl.DeviceIdType.MESH

WEIGHTS = ('rel_table', 'w_in', 'b_gate', 'sinks', 'q_norm_g', 'kv_norm_g', 'w_uq', 'w_ukv', 'w_branch',
           'w_out', 'ln1_g', 'ln1_b', 'w_ffn_up', 'conv_w', 'conv_b', 'w_ffn_down', 'ln2_g', 'ln2_b')
SHARDED = {'w_in': 2, 'w_uq': 2, 'w_ukv': 2, 'w_branch': 3, 'w_out': 1, 'w_ffn_up': 2, 'conv_w': 2,
           'w_ffn_down': 1}
GATHER_BF16 = ('w_in', 'w_uq', 'w_ukv', 'w_branch', 'w_out', 'w_ffn_up', 'w_ffn_down')


def _params(sem=None):
    return pltpu.CompilerParams(dimension_semantics=sem, vmem_limit_bytes=VMEM_LIMIT)


def _pick(n, target):
    if n <= target:
        return n
    best = None
    for t in range(LANE, target + 1, LANE):
        if n % t == 0:
            best = t
    assert best is not None, (n, target)
    return best


def _dot(a, b, ca, cb):
    return lax.dot_general(a.astype(BF16), b.astype(BF16), (((ca,), (cb,)), ((), ())),
                           preferred_element_type=F32)


@jax.custom_vjp
def mm(a, b):
    return _dot(a, b, 1, 0)


def _mm_fwd(a, b):
    return _dot(a, b, 1, 0), (a, b)


def _mm_bwd(res, g):
    a, b = res
    return _dot(g, b, 1, 1), _dot(a, g, 0, 0)


mm.defvjp(_mm_fwd, _mm_bwd)


@jax.custom_vjp
def mm_nt(a, b):
    return _dot(a, b, 1, 1)


def _mm_nt_fwd(a, b):
    return _dot(a, b, 1, 1), (a, b)


def _mm_nt_bwd(res, g):
    a, b = res
    return _dot(g, b, 1, 0), _dot(g, a, 0, 0)


mm_nt.defvjp(_mm_nt_fwd, _mm_nt_bwd)


def _split_impl(x, n):
    w = x.shape[-1] // n
    return tuple(x[:, i * w:(i + 1) * w] for i in range(n))


@functools.partial(jax.custom_vjp, nondiff_argnums=(1,))
def split_lanes(x, n):
    return _split_impl(x, n)


def _split_fwd(x, n):
    return _split_impl(x, n), None


def _split_bwd(n, _, gs):
    return (jnp.concatenate(gs, axis=-1),)


split_lanes.defvjp(_split_fwd, _split_bwd)


@jax.custom_vjp
def concat_lanes(xs):
    return jnp.concatenate(xs, axis=-1)


def _concat_fwd(xs):
    return jnp.concatenate(xs, axis=-1), len(xs)


def _concat_bwd(n, g):
    return (_split_impl(g, n),)


concat_lanes.defvjp(_concat_fwd, _concat_bwd)


@jax.custom_vjp
def concat_rows(a, b):
    return jnp.concatenate([a, b], axis=0)


def _crow_fwd(a, b):
    return jnp.concatenate([a, b], axis=0), a.shape[0]


def _crow_bwd(na, g):
    return g[:na], g[na:]


concat_rows.defvjp(_crow_fwd, _crow_bwd)


def _rot_half(x):
    n = x.shape[-1]
    lane = lax.broadcasted_iota(jnp.int32, (1, n), 1) % LANE
    lo = (lane >= C_NOPE) & (lane < C_NOPE + C_ROPE // 2)
    hi = (lane >= C_NOPE + C_ROPE // 2) & (lane < C_NOPE + C_ROPE)
    up = pltpu.roll(x, n - C_ROPE // 2, 1)
    dn = pltpu.roll(x, C_ROPE // 2, 1)
    return jnp.where(lo, -up, jnp.where(hi, dn, 0.0))


@jax.custom_vjp
def rope(x, c, s):
    return x * c + _rot_half(x) * s


def _rope_fwd(x, c, s):
    return x * c + _rot_half(x) * s, (c, s)


def _rope_bwd(res, g):
    c, s = res
    return g * c - _rot_half(g * s), jnp.zeros_like(c), jnp.zeros_like(s)


rope.defvjp(_rope_fwd, _rope_bwd)


def _sigmoid(x):
    return 0.5 * jnp.tanh(0.5 * x) + 0.5


MATMUL_VMEM = 44 * 1024 * 1024


def _matmul_tiles(m, n, k, a_bytes, b_bytes, o_bytes, has_add):
    tm, tn, tk = _pick(m, 1024), _pick(n, 2560), _pick(k, 2560)

    def need(tn_, tk_):
        acc = tm * tn_ * 4 if k // tk_ > 1 else 0
        return 2 * (tm * tk_ * a_bytes + tk_ * tn_ * b_bytes + tm * tn_ * o_bytes) + acc + 2 * tm * tn_ * 4 * int(has_add)

    while need(tn, tk) > MATMUL_VMEM:
        if tk >= tn and tk > LANE:
            tk = _pick(k, tk - LANE)
        else:
            tn = _pick(n, tn - LANE)
    return tm, tn, tk


def matmul(name, a, b, mode, out_dtype=F32, add=None, scale=1.0, comm=None):
    if mode == 'nn':
        (m, k), n = a.shape, b.shape[1]
    elif mode == 'nt':
        (m, k), n = a.shape, b.shape[0]
    else:
        (k, m), n = a.shape, b.shape[1]
    tm, tn, tk = _matmul_tiles(m, n, k, a.dtype.itemsize, b.dtype.itemsize, jnp.dtype(out_dtype).itemsize,
                               add is not None)
    nk = k // tk
    grid = (m // tm, n // tn, nk)
    c_in, c_specs, c_shapes, c_sems, c_hook = _ride_along(comm, grid)
    nc = len(c_in)
    a_spec = pl.BlockSpec((tk, tm), lambda i, j, kk: (kk, i)) if mode == 'tn' else \
        pl.BlockSpec((tm, tk), lambda i, j, kk: (i, kk))
    b_spec = pl.BlockSpec((tn, tk), lambda i, j, kk: (j, kk)) if mode == 'nt' else \
        pl.BlockSpec((tk, tn), lambda i, j, kk: (kk, j))
    o_spec = pl.BlockSpec((tm, tn), lambda i, j, kk: (i, j))
    ca = 0 if mode == 'tn' else 1
    cb = 1 if mode == 'nt' else 0
    has_add = add is not None

    n_in = 2 + int(has_add)

    def body(*refs):
        a_ref, b_ref = refs[:2]
        add_ref = refs[2] if has_add else None
        o_ref = refs[n_in + nc]
        c_hook(refs[n_in:n_in + nc], refs[n_in + nc + 1:n_in + 2 * nc + 1], refs[len(refs) - 3:])

        def finish(r):
            if has_add:
                r = r + scale * add_ref[...]
            o_ref[...] = r.astype(out_dtype)

        if nk == 1:
            finish(_dot(a_ref[...], b_ref[...], ca, cb))
            return
        acc_ref = refs[n_in + 2 * nc + 1]
        kk = pl.program_id(2)

        @pl.when(kk == 0)
        def _():
            acc_ref[...] = jnp.zeros_like(acc_ref)

        acc_ref[...] += _dot(a_ref[...], b_ref[...], ca, cb)

        @pl.when(kk == nk - 1)
        def _():
            finish(acc_ref[...])

    ins = [a, b] + ([add] if has_add else []) + c_in
    specs = [a_spec, b_spec] + ([o_spec] if has_add else []) + c_specs
    res = pl.pallas_call(
        body, name=name, grid=grid, in_specs=specs, out_specs=[o_spec] + c_specs,
        out_shape=[jax.ShapeDtypeStruct((m, n), out_dtype)] + c_shapes,
        scratch_shapes=([pltpu.VMEM((tm, tn), F32)] if nk > 1 else []) + c_sems,
        compiler_params=_params(("arbitrary",) * 3 if comm is not None else ("parallel", "parallel", "arbitrary")))(*ins)
    return res[0] if comm is None else (res[0], list(res[1:]))


def _row_spec(width, col_block=0, tm=TM):
    return pl.BlockSpec((tm, width), lambda i: (i, col_block))


def _full_spec(shape):
    nd = len(shape)
    return pl.BlockSpec(shape, lambda i: (0,) * nd)


def rows_fwd(name, fn, rows, params, outs, tm=TM):
    s = rows[0][0].shape[0]
    nr, npar = len(rows), len(params)

    def body(*refs):
        r = [refs[i][...].astype(F32) for i in range(nr)]
        p = [refs[nr + i][...] for i in range(npar)]
        res = fn(*r, *p)
        for o_ref, val in zip(refs[nr + npar:], res):
            o_ref[...] = val.astype(o_ref.dtype)

    return pl.pallas_call(
        body, name=name, grid=(s // tm,),
        in_specs=[_row_spec(w, cb, tm) for _, w, cb in rows] + [_full_spec(p.shape) for p in params],
        out_specs=[_row_spec(w, 0, tm) for w, _ in outs],
        out_shape=[jax.ShapeDtypeStruct((s, w), dt) for w, dt in outs],
        compiler_params=_params(("parallel",)))(*[a for a, _, _ in rows], *params)


def rows_bwd(name, fn, rows, params, cts, row_grads, tm=TM):
    s = rows[0][0].shape[0]
    nr, npar, nct, nrg = len(rows), len(params), len(cts), len(row_grads)

    def body(*refs):
        r = [refs[i][...].astype(F32) for i in range(nr)]
        p = [refs[nr + i][...].astype(F32) for i in range(npar)]
        g = tuple(refs[nr + npar + i][...].astype(F32) for i in range(nct))
        _, vjp = jax.vjp(lambda *a: tuple(fn(*a)), *r, *p)
        grads = vjp(g)
        outs = refs[nr + npar + nct:]
        for k, (idx, _) in enumerate(row_grads):
            outs[k][...] = grads[idx].astype(outs[k].dtype)

        @pl.when(pl.program_id(0) == 0)
        def _():
            for k in range(npar):
                outs[nrg + k][...] = jnp.zeros_like(outs[nrg + k])

        for k in range(npar):
            outs[nrg + k][...] += grads[nr + k]

    return pl.pallas_call(
        body, name=name, grid=(s // tm,),
        in_specs=[_row_spec(w, cb, tm) for _, w, cb in rows] + [_full_spec(p.shape) for p in params]
        + [_row_spec(w, 0, tm) for _, w in cts],
        out_specs=[_row_spec(rows[idx][1], 0, tm) for idx, _ in row_grads] + [_full_spec(p.shape) for p in params],
        out_shape=[jax.ShapeDtypeStruct((s, rows[idx][1]), dt) for idx, dt in row_grads]
        + [jax.ShapeDtypeStruct(p.shape, F32) for p in params],
        compiler_params=_params(("arbitrary",)))(*[a for a, _, _ in rows], *params, *[a for a, _ in cts])


def ln_fn(x, r, g, b):
    z = ALPHA * x + r
    mu = jnp.mean(z, axis=-1, keepdims=True)
    zc = z - mu
    var = jnp.mean(zc * zc, axis=-1, keepdims=True)
    return (zc * lax.rsqrt(var + LN_EPS) * g + b,)


def ln_twice(x, r, g, b):
    (y,) = ln_fn(x, r, g, b)
    return y, y


def _rms(x, g):
    return x * lax.rsqrt(jnp.mean(x * x, axis=-1, keepdims=True) + RMS_EPS) * g


def mla_q_fn(cq, c, s, g, wq):
    q = mm(_rms(cq, g), wq)
    return (rope(q, jnp.concatenate([c] * N_HEADS, axis=-1), jnp.concatenate([s] * N_HEADS, axis=-1)) * MLA_SCALE,)


@jax.custom_vjp
def tile_heads(x):
    return jnp.concatenate([x] * N_HEADS, axis=-1)


def _tile_fwd(x):
    return jnp.concatenate([x] * N_HEADS, axis=-1), None


def _tile_bwd(_, g):
    parts = _split_impl(g, N_HEADS)
    acc = parts[0]
    for p in parts[1:]:
        acc = acc + p
    return (acc,)


tile_heads.defvjp(_tile_fwd, _tile_bwd)


def mla_kv_fn(ckv, kr, c, s, g, wk, wv):
    n = _rms(ckv, g)
    lane = lax.broadcasted_iota(jnp.int32, (1, N_HEADS * LANE), 1) % LANE
    one = (lane == MLA_ONE).astype(F32)
    return mm(n, wk) + tile_heads(rope(kr, c, s)), mm(n, wv) + one


def merge_fn(y0, y1, y2, g0, g1, g2, b0, b1, b2):
    return (_sigmoid(g0 + b0) * y0 + _sigmoid(g1 + b1) * y1 + _sigmoid(g2 + b2) * y2,)


def combine_fn(o1, o2, o3, l1, l2, l3):
    mx = lax.stop_gradient(jnp.maximum(jnp.maximum(l1, l2), l3))
    e1, e2, e3 = jnp.exp(l1 - mx), jnp.exp(l2 - mx), jnp.exp(l3 - mx)
    return ((e1 * o1 + e2 * o2 + e3 * o3) / (e1 + e2 + e3),)


def loss_head(y, t):
    s, d = y.shape

    def body(y_ref, t_ref, dy_ref, loss_ref):
        err = y_ref[...] - t_ref[...]
        dy_ref[...] = err * (1.0 / d)

        @pl.when(pl.program_id(0) == 0)
        def _():
            loss_ref[...] = jnp.zeros_like(loss_ref)

        loss_ref[...] += jnp.sum(err * err) * (0.5 / d)

    return pl.pallas_call(
        body, name="loss_head", grid=(s // TM,), in_specs=[_row_spec(d), _row_spec(d)],
        out_specs=[_row_spec(d), _full_spec((8, LANE))],
        out_shape=[jax.ShapeDtypeStruct((s, d), F32), jax.ShapeDtypeStruct((8, LANE), F32)],
        compiler_params=_params(("arbitrary",)))(y, t)


CONV_TN = 256
CONV_TM = 512


def _conv_taps(u_ext, w_ref, b_ref):
    d1, d2 = pltpu.roll(u_ext, 1, 0), pltpu.roll(u_ext, 2, 0)
    return w_ref[0:1, :] * d2 + w_ref[1:2, :] * d1 + w_ref[2:3, :] * u_ext + b_ref[...], d1, d2


def conv_glu_fwd(u, conv_w, conv_b):
    s, f2 = u.shape
    f = f2 // 2
    tn, tm = CONV_TN, min(CONV_TM, s)
    off = f // tn
    hb = tm // 8

    def body(ug, uv, hg, hv, wg, wv, bg, bv, o_ref):
        first = pl.program_id(0) == 0

        def conv(u_ref, h_ref, w_ref, b_ref):
            halo = jnp.where(first, 0.0, h_ref[...])
            ext = jnp.concatenate([halo, u_ref[...]], axis=0)
            return _conv_taps(ext, w_ref, b_ref)[0][8:]

        cg, cv = conv(ug, hg, wg, bg), conv(uv, hv, wv, bv)
        o_ref[...] = (cg * _sigmoid(cg) * cv).astype(o_ref.dtype)

    blk = lambda o: pl.BlockSpec((tm, tn), lambda i, j: (i, j + o))
    halo = lambda o: pl.BlockSpec((8, tn), lambda i, j: (jnp.maximum(i * hb - 1, 0), j + o))
    par = lambda r, o: pl.BlockSpec((r, tn), lambda i, j: (0, j + o))
    return pl.pallas_call(
        body, name="conv_glu_fwd", grid=(s // tm, off),
        in_specs=[blk(0), blk(off), halo(0), halo(off), par(3, 0), par(3, off), par(1, 0), par(1, off)],
        out_specs=pl.BlockSpec((tm, tn), lambda i, j: (i, j)),
        out_shape=jax.ShapeDtypeStruct((s, f), BF16),
        compiler_params=_params(("parallel", "parallel")))(u, u, u, u, conv_w, conv_w, conv_b, conv_b)


def conv_glu_bwd(u, dh, conv_w, conv_b):
    s, f2 = u.shape
    f = f2 // 2
    tn, tm = CONV_TN, min(CONV_TM, s)
    off = f // tn
    hb = tm // 8
    n_rows = s // tm

    def body(ug, uv, pg, pv, ng, nv, dh_ref, dhn_ref, wg, wv, bg, bv, dug, duv, dwg, dwv, dbg, dbv):
        i = pl.program_id(1)
        first, last = i == 0, i == n_rows - 1

        def ext(u_ref, p_ref, n_ref):
            return jnp.concatenate([jnp.where(first, 0.0, p_ref[...]), u_ref[...], n_ref[...]], axis=0)

        eg, ev = ext(ug, pg, ng), ext(uv, pv, nv)
        (cg, eg1, eg2), (cv, ev1, ev2) = _conv_taps(eg, wg, bg), _conv_taps(ev, wv, bv)
        dhe = jnp.concatenate([jnp.zeros((8, tn), F32), dh_ref[...], jnp.where(last, 0.0, dhn_ref[...])], axis=0)
        sg = _sigmoid(cg)
        dcg = dhe * cv * (sg * (1.0 + cg * (1.0 - sg)))
        dcv = dhe * (cg * sg)
        n = tm + 16

        @pl.when(i == 0)
        def _():
            for r in (dwg, dwv, dbg, dbv):
                r[...] = jnp.zeros_like(r)

        def back(dc, e, e1, e2, w_ref, du_ref, dw_ref, db_ref):
            du = w_ref[2:3, :] * dc + w_ref[1:2, :] * pltpu.roll(dc, n - 1, 0) + w_ref[0:1, :] * pltpu.roll(dc, n - 2, 0)
            du_ref[...] = du[8:8 + tm].astype(du_ref.dtype)
            own = dc[8:8 + tm]
            dw_ref[0:1, :] += jnp.sum(own * e2[8:8 + tm], axis=0, keepdims=True)
            dw_ref[1:2, :] += jnp.sum(own * e1[8:8 + tm], axis=0, keepdims=True)
            dw_ref[2:3, :] += jnp.sum(own * e[8:8 + tm], axis=0, keepdims=True)
            db_ref[...] += jnp.sum(own, axis=0, keepdims=True)

        back(dcg, eg, eg1, eg2, wg, dug, dwg, dbg)
        back(dcv, ev, ev1, ev2, wv, duv, dwv, dbv)

    blk = lambda o: pl.BlockSpec((tm, tn), lambda j, i: (i, j + o))
    prev = lambda o: pl.BlockSpec((8, tn), lambda j, i: (jnp.maximum(i * hb - 1, 0), j + o))
    nxt = lambda o: pl.BlockSpec((8, tn), lambda j, i: (jnp.minimum((i + 1) * hb, s // 8 - 1), j + o))
    par = lambda r, o: pl.BlockSpec((r, tn), lambda j, i: (0, j + o))
    return pl.pallas_call(
        body, name="conv_glu_bwd", grid=(off, n_rows),
        in_specs=[blk(0), blk(off), prev(0), prev(off), nxt(0), nxt(off), blk(0), nxt(0),
                  par(3, 0), par(3, off), par(1, 0), par(1, off)],
        out_specs=[blk(0), blk(0), par(3, 0), par(3, 0), par(1, 0), par(1, 0)],
        out_shape=[jax.ShapeDtypeStruct((s, f), BF16)] * 2 + [jax.ShapeDtypeStruct((3, f), F32)] * 2
        + [jax.ShapeDtypeStruct((1, f), F32)] * 2,
        compiler_params=_params(("parallel", "arbitrary")))(u, u, u, u, u, u, dh, dh, conv_w, conv_w, conv_b, conv_b)


def _band_fn(q, kp, kc, vp, vc, bias, sink, first, inclusive):
    k = concat_rows(kp, kc)
    v = concat_rows(vp, vc)
    qi = lax.broadcasted_iota(jnp.int32, (BLOCK, 2 * BLOCK), 0)
    ki = lax.broadcasted_iota(jnp.int32, (BLOCK, 2 * BLOCK), 1)
    step = BLOCK + qi - ki
    valid = (step >= 0) & ((step <= BLOCK) if inclusive else (step < BLOCK))
    valid = valid & (jnp.logical_not(first) | (ki >= BLOCK))
    lane = lax.broadcasted_iota(jnp.int32, (1, LANE), 1)
    scale = HEAD_DIM ** -0.5
    qs, ks, vs = split_lanes(q, 4), split_lanes(k, 4), split_lanes(v, 4)
    o_parts, l_parts = [], []
    for pair in range(4):
        o_acc, l_acc = None, None
        for sub in range(2):
            h = 2 * pair + sub
            mh = ((lane >= sub * HEAD_DIM) & (lane < (sub + 1) * HEAD_DIM)).astype(F32)
            logits = mm_nt(qs[pair] * mh, ks[pair]) * scale + bias[h]
            logits = jnp.where(valid, logits, NEG)
            m = lax.stop_gradient(jnp.max(logits, axis=-1, keepdims=True))
            if sink is not None:
                m = jnp.maximum(m, lax.stop_gradient(sink[h]))
            p = jnp.exp(logits - m)
            den = jnp.sum(p, axis=-1, keepdims=True)
            if sink is not None:
                den = den + jnp.exp(sink[h] - m)
            oh = mm(p * (1.0 / den), vs[pair] * mh)
            lh = (m + jnp.log(den)) * mh
            o_acc = oh if o_acc is None else o_acc + oh
            l_acc = lh if l_acc is None else l_acc + lh
        o_parts.append(o_acc)
        l_parts.append(l_acc)
    return concat_lanes(tuple(o_parts)), concat_lanes(tuple(l_parts))


def _band_specs(nb, cq, ck, cv):
    own = lambda n: jnp.minimum(n, nb - 1)
    prev = lambda n: jnp.maximum(own(n) - 1, 0)
    q = pl.BlockSpec((BLOCK, HW), lambda r, n: (r * nb + own(n), cq))
    kp = pl.BlockSpec((BLOCK, HW), lambda r, n: (r * nb + prev(n), ck))
    kc = pl.BlockSpec((BLOCK, HW), lambda r, n: (r * nb + own(n), ck))
    vp = pl.BlockSpec((BLOCK, HW), lambda r, n: (r * nb + prev(n), cv))
    vc = pl.BlockSpec((BLOCK, HW), lambda r, n: (r * nb + own(n), cv))
    return [q, kp, kc, vp, vc]


def to_classes(a, dil):
    if dil == 1:
        return a
    s, c = a.shape
    return a.reshape(s // dil, dil, c).transpose(1, 0, 2).reshape(s, c)


def from_classes(a, dil):
    if dil == 1:
        return a
    s, c = a.shape
    return a.reshape(dil, s // dil, c).transpose(1, 0, 2).reshape(s, c)


def _const_spec(shape):
    nd = len(shape)
    return pl.BlockSpec(shape, lambda r, n: (0,) * nd)


def band_fwd(name, qkv, dil, cols, bias, sink, inclusive):
    s = qkv.shape[0]
    nb = s // (BLOCK * dil)
    has_sink = sink is not None

    def body(*refs):
        q, kp, kc, vp, vc, b_ref = refs[:6]
        s_ref = refs[6] if has_sink else None
        o_ref, l_ref = refs[-2:]
        first = pl.program_id(1) == 0
        bias_l = tuple(b_ref[h] for h in range(N_HEADS))
        sink_l = tuple(s_ref[h] for h in range(N_HEADS)) if has_sink else None
        o, l = _band_fn(q[...], kp[...], kc[...], vp[...], vc[...], bias_l, sink_l, first, inclusive)
        o_ref[...] = o
        l_ref[...] = l

    out_spec = pl.BlockSpec((BLOCK, HW), lambda r, n: (r * nb + n, 0))
    ins = [qkv] * 5 + [bias] + ([sink] if has_sink else [])
    specs = _band_specs(nb, *cols) + [_const_spec(bias.shape)] + ([_const_spec(sink.shape)] if has_sink else [])
    return pl.pallas_call(
        body, name=name, grid=(dil, nb), in_specs=specs, out_specs=[out_spec, out_spec],
        out_shape=[jax.ShapeDtypeStruct((s, HW), F32)] * 2,
        compiler_params=_params(("parallel", "parallel")))(*ins)


def band_bwd(name, qkv, dil, cols, bias, sink, inclusive, do, dl):
    s = qkv.shape[0]
    nb = s // (BLOCK * dil)
    has_sink, has_dl = sink is not None, dl is not None
    n_in = 6 + int(has_sink) + 1 + int(has_dl)

    def body(*refs):
        q, kp, kc, vp, vc, b_ref = refs[:6]
        s_ref = refs[6] if has_sink else None
        do_ref = refs[6 + int(has_sink)]
        dl_ref = refs[7 + int(has_sink)] if has_dl else None
        out_ref, db_ref = refs[n_in:n_in + 2]
        ds_ref = refs[n_in + 2] if has_sink else None
        cq, ck, cv = refs[-3:]
        n = pl.program_id(1)

        def emit(dq, dk, dv):
            out_ref[:, 0:HW] = dq.astype(out_ref.dtype)
            out_ref[:, HW:2 * HW] = dk.astype(out_ref.dtype)
            out_ref[:, 2 * HW:3 * HW] = dv.astype(out_ref.dtype)

        @pl.when((pl.program_id(0) == 0) & (n == 0))
        def _():
            db_ref[...] = jnp.zeros_like(db_ref)
            if has_sink:
                ds_ref[...] = jnp.zeros_like(ds_ref)

        @pl.when(n < nb)
        def _():
            first = n == 0
            bias_l = tuple(b_ref[h] for h in range(N_HEADS))
            sink_l = tuple(s_ref[h] for h in range(N_HEADS)) if has_sink else ()

            def fn(q_, kp_, kc_, vp_, vc_, b_, s_):
                o, l = _band_fn(q_, kp_, kc_, vp_, vc_, b_, s_ if has_sink else None, first, inclusive)
                return (o, l) if has_dl else (o,)

            blocks = [r[...].astype(F32) for r in (q, kp, kc, vp, vc)]
            _, vjp = jax.vjp(fn, *blocks, bias_l, sink_l)
            ct = (do_ref[...].astype(F32), dl_ref[...]) if has_dl else (do_ref[...].astype(F32),)
            g = vjp(ct)

            @pl.when(n > 0)
            def _():
                emit(cq[...], ck[...] + g[1], cv[...] + g[3])

            cq[...] = g[0]
            ck[...] = g[2]
            cv[...] = g[4]
            for h in range(N_HEADS):
                db_ref[h] += g[5][h]
                if has_sink:
                    ds_ref[h] += g[6][h]

        @pl.when(n == nb)
        def _():
            emit(cq[...], ck[...], cv[...])

    blk = pl.BlockSpec((BLOCK, HW), lambda r, n: (r * nb + jnp.minimum(n, nb - 1), 0))
    late = pl.BlockSpec((BLOCK, 3 * HW), lambda r, n: (r * nb + jnp.maximum(n - 1, 0), 0))
    ins = [qkv] * 5 + [bias] + ([sink] if has_sink else []) + [do] + ([dl] if has_dl else [])
    specs = _band_specs(nb, *cols) + [_const_spec(bias.shape)] \
        + ([_const_spec(sink.shape)] if has_sink else []) + [blk] * (1 + int(has_dl))
    out_shape = [jax.ShapeDtypeStruct((s, 3 * HW), BF16), jax.ShapeDtypeStruct(bias.shape, F32)] \
        + ([jax.ShapeDtypeStruct(sink.shape, F32)] if has_sink else [])
    out_specs = [late, _const_spec(bias.shape)] + ([_const_spec(sink.shape)] if has_sink else [])
    res = pl.pallas_call(
        body, name=name, grid=(dil, nb + 1), in_specs=specs, out_specs=out_specs, out_shape=out_shape,
        scratch_shapes=[pltpu.VMEM((BLOCK, HW), F32)] * 3,
        compiler_params=_params(("arbitrary", "arbitrary")))(*ins)
    return res[0], res[1], (res[2] if has_sink else None)


def rel_grad(name, dbias, bucket):
    def body(d_ref, b_ref, o_ref):
        lane = lax.broadcasted_iota(jnp.int32, (1, LANE), 1)
        bk = b_ref[...]
        for h in range(N_HEADS):
            d = d_ref[h]
            row = jnp.zeros((1, LANE), F32)
            for b in range(REL_BUCKETS):
                tot = jnp.sum(jnp.where(bk == b, d, 0.0))
                row = row + jnp.where(lane == b, tot, 0.0)
            o_ref[h:h + 1, :] = row

    return pl.pallas_call(
        body, name=name, out_shape=jax.ShapeDtypeStruct((N_HEADS, LANE), F32),
        in_specs=[pl.BlockSpec(memory_space=pltpu.VMEM)] * 2, out_specs=pl.BlockSpec(memory_space=pltpu.VMEM),
        compiler_params=_params())(dbias, bucket)


MLA_T = 512
MLA_HPS = 2
MLA_SCALE = (C_NOPE + C_ROPE) ** -0.5


MLA_ONE = C_NOPE


def _lane_tiles(x, n):
    return jnp.concatenate([x] * n, axis=-1)


def _diag_mask(t):
    r = lax.broadcasted_iota(jnp.int32, (t, t), 0)
    c = lax.broadcasted_iota(jnp.int32, (t, t), 1)
    return c <= r


def _ride_along(comm, grid):
    if comm is None:
        return [], [], [], [], lambda *_: None
    arrays, scatter = comm
    out_shape, specs, sems = _push_shapes(arrays, scatter)

    def hook(ins, outs, sem_refs):
        ids = [pl.program_id(a) for a in range(len(grid))]
        first, last = ids[0] == 0, ids[0] == grid[0] - 1
        for a in range(1, len(grid)):
            first, last = first & (ids[a] == 0), last & (ids[a] == grid[a] - 1)

        @pl.when(first)
        def _():
            _pushes(ins, outs, sem_refs, scatter).start()

        @pl.when(last)
        def _():
            _pushes(ins, outs, sem_refs, scatter).wait()

    return list(arrays), specs, out_shape, sems, hook


def mla_fwd(name, q, k, v, comm=None):
    s = q.shape[0]
    t = min(MLA_T, s)
    nq = s // t
    grid = (N_HEADS // MLA_HPS, nq)
    c_in, c_specs, c_shapes, c_sems, c_hook = _ride_along(comm, grid)
    nc = len(c_in)

    def body(*refs):
        q_ref, k_ref, v_ref = refs[:3]
        o_ref, l_ref = refs[3 + nc:5 + nc]
        m_s, acc_s = refs[5 + 2 * nc:7 + 2 * nc]
        c_hook(refs[3:3 + nc], refs[5 + nc:5 + 2 * nc], refs[7 + 2 * nc:])
        i = pl.program_id(1)
        m_s[...] = jnp.full_like(m_s, NEG)
        acc_s[...] = jnp.zeros_like(acc_s)

        def chunk(j, masked):
            rows = pl.ds(pl.multiple_of(j * t, t), t)
            for hh in range(MLA_HPS):
                ln = slice(hh * LANE, (hh + 1) * LANE)
                sc = _dot(q_ref[:, ln], k_ref[rows, ln], 1, 1)
                if masked:
                    sc = jnp.where(_diag_mask(t), sc, NEG)
                m_old = m_s[:, ln]
                m_new = jnp.maximum(m_old, jnp.max(sc, axis=-1, keepdims=True))
                a = jnp.exp(m_old - m_new)
                p = jnp.exp(sc - _lane_tiles(m_new, t // LANE))
                acc_s[:, ln] = a * acc_s[:, ln] + _dot(p, v_ref[rows, ln], 1, 0)
                m_s[:, ln] = m_new

        def step(j, carry):
            chunk(j, False)
            return carry

        lax.fori_loop(0, i, step, 0)
        chunk(i, True)
        for hh in range(MLA_HPS):
            ln = slice(hh * LANE, (hh + 1) * LANE)
            acc = acc_s[:, ln]
            den = acc[:, MLA_ONE:MLA_ONE + 1]
            o_ref[:, ln] = acc / den
            l_ref[:, ln] = m_s[:, ln] + jnp.log(den)

    w = MLA_HPS * LANE
    qs = pl.BlockSpec((t, w), lambda h, i: (i, h))
    ks = pl.BlockSpec((s, w), lambda h, i: (0, h))
    res = pl.pallas_call(
        body, name=name, grid=grid, in_specs=[qs, ks, ks] + c_specs, out_specs=[qs, qs] + c_specs,
        out_shape=[jax.ShapeDtypeStruct((s, N_HEADS * LANE), F32)] * 2 + c_shapes,
        scratch_shapes=[pltpu.VMEM((t, w), F32)] * 2 + c_sems,
        compiler_params=_params(("arbitrary", "arbitrary")))(q, k, v, *c_in)
    return res[0], res[1], list(res[2:])


def mla_delta_fn(do, o):
    parts = []
    for a, b in zip(_split_impl(do, N_HEADS), _split_impl(o, N_HEADS)):
        parts.append(jnp.sum(a * b, axis=-1, keepdims=True) + jnp.zeros_like(a))
    return jnp.concatenate(parts, axis=-1), do


def mla_bwd(name, q, k, v, do, lse, delta, comm=None):
    s = q.shape[0]
    t = min(MLA_T, s)
    nq = s // t
    grid = (N_HEADS, nq)
    c_in, c_specs, c_shapes, c_sems, c_hook = _ride_along(comm, grid)
    nc = len(c_in)

    def body(*refs):
        q_ref, k_ref, v_ref, do_ref, l_ref, d_ref = refs[:6]
        dq_ref, dk_ref, dv_ref = refs[6 + nc:9 + nc]
        dk_s, dv_s = refs[9 + 2 * nc:11 + 2 * nc]
        c_hook(refs[6:6 + nc], refs[9 + nc:9 + 2 * nc], refs[11 + 2 * nc:])
        j = pl.program_id(1)

        @pl.when(j == 0)
        def _():
            dq_ref[...] = jnp.zeros_like(dq_ref)

        dk_s[...] = jnp.zeros_like(dk_s)
        dv_s[...] = jnp.zeros_like(dv_s)
        kb, vb = k_ref[...], v_ref[...]

        def chunk(i, masked):
            rows = pl.ds(pl.multiple_of(i * t, t), t)
            qi, doi = q_ref[rows, :], do_ref[rows, :]
            sc = _dot(qi, kb, 1, 1)
            if masked:
                sc = jnp.where(_diag_mask(t), sc, NEG)
            p = jnp.exp(sc - _lane_tiles(l_ref[rows, :], t // LANE))
            ds = p * (_dot(doi, vb, 1, 1) - _lane_tiles(d_ref[rows, :], t // LANE))
            dv_s[...] += _dot(p, doi, 0, 0)
            dk_s[...] += _dot(ds, qi, 0, 0)
            dq_ref[rows, :] += _dot(ds, kb, 1, 0)

        def pair(p, carry):
            chunk(j + 1 + 2 * p, False)
            chunk(j + 2 + 2 * p, False)
            return carry

        chunk(j, True)
        n_off = nq - 1 - j
        lax.fori_loop(0, n_off // 2, pair, 0)

        @pl.when(n_off % 2 == 1)
        def _():
            chunk(nq - 1, False)

        dk_ref[...] = dk_s[...]
        dv_ref[...] = dv_s[...]

    ks = pl.BlockSpec((t, LANE), lambda h, j: (j, h))
    full = pl.BlockSpec((s, LANE), lambda h, j: (0, h))
    res = pl.pallas_call(
        body, name=name, grid=grid, in_specs=[full, ks, ks, full, full, full] + c_specs,
        out_specs=[full, ks, ks] + c_specs,
        out_shape=[jax.ShapeDtypeStruct((s, N_HEADS * LANE), F32)] * 3 + c_shapes,
        scratch_shapes=[pltpu.VMEM((t, LANE), F32)] * 2 + c_sems,
        compiler_params=_params(("arbitrary", "arbitrary")))(q, k, v, do, lse, delta, *c_in)
    return res[0], res[1], res[2], list(res[3:])


def _peer(k):
    x, y, c = lax.axis_index("x"), lax.axis_index("y"), lax.axis_index("c")
    px, py, pc = (x ^ ((k >> 2) & 1)), (y ^ ((k >> 1) & 1)), (c ^ (k & 1))
    return (px, py, pc), 4 * px + 2 * py + pc


def exchange(name, arrays, scatter):
    n = len(arrays)

    def body(*refs):
        push = _pushes(refs[:n], refs[n:2 * n], refs[2 * n:], scatter)
        push.start()
        push.wait()

    out_shape, specs, sems = _push_shapes(arrays, scatter)
    return pl.pallas_call(body, name=name, out_shape=out_shape, in_specs=specs, out_specs=specs,
                          scratch_shapes=sems)(*arrays)


def _push_shapes(arrays, scatter):
    n = len(arrays)
    out_shape = [jax.ShapeDtypeStruct(a.shape if scatter else (N_DEV,) + a.shape, a.dtype) for a in arrays]
    n_sem = (N_DEV - 1) * n
    sems = [pltpu.SemaphoreType.DMA((n_sem,)), pltpu.SemaphoreType.DMA((n_sem,)), pltpu.SemaphoreType.DMA((n,))]
    return out_shape, [pl.BlockSpec(memory_space=pl.ANY)] * n, sems


class _pushes:
    def __init__(self, ins, outs, sems, scatter):
        self.ins, self.outs, self.sems, self.scatter = ins, outs, sems, scatter

    def _local(self):
        _, me_idx = _peer(0)
        return [pltpu.make_async_copy(self.ins[a].at[me_idx] if self.scatter else self.ins[a],
                                      self.outs[a].at[me_idx], self.sems[2].at[a]) for a in range(len(self.ins))]

    def _remote(self, incoming):
        send_sems, recv_sems, _ = self.sems
        n = len(self.ins)
        _, me_idx = _peer(0)
        copies = []
        for k in range(1, N_DEV):
            peer, peer_idx = _peer(k)
            for a in range(n):
                sem = (k - 1) * n + a
                copies.append(pltpu.make_async_remote_copy(
                    src_ref=self.ins[a].at[peer_idx] if self.scatter else self.ins[a],
                    dst_ref=self.outs[a].at[peer_idx if incoming else me_idx],
                    send_sem=send_sems.at[sem], recv_sem=recv_sems.at[sem], device_id=peer, device_id_type=MESH_ID))
        return copies

    def start(self):
        for cp in self._local() + self._remote(incoming=False):
            cp.start()

    def wait(self):
        for cp in self._remote(incoming=True):
            cp.wait_recv()
        for cp in self._remote(incoming=False):
            cp.wait_send()
        for cp in self._local():
            cp.wait()


ADAMW_BLOCK_ELEMS = 128 * 1024


def adamw_sum(name, parts, w, m, v):
    nl, r, c = w.shape
    fits = [t for t in range(16, r + 1, 16) if r % t == 0 and t * c <= ADAMW_BLOCK_ELEMS]
    tr = max(fits) if fits else r
    c1 = 1.0 - ADAM_B1 ** ADAM_STEP
    c2 = 1.0 - ADAM_B2 ** ADAM_STEP

    def body(*refs):
        p_refs = refs[:nl]
        w_ref, m_ref, v_ref, g_out, d_out, m_out, v_out = refs[nl:]
        for layer in range(nl):
            @pl.when(pl.program_id(0) == layer)
            def _():
                g = p_refs[layer][0].astype(F32)
                for d in range(1, N_DEV):
                    g = g + p_refs[layer][d].astype(F32)
                mn = ADAM_B1 * m_ref[...] + (1.0 - ADAM_B1) * g
                vn = ADAM_B2 * v_ref[...] + (1.0 - ADAM_B2) * (g * g)
                g_out[...] = g
                m_out[...] = mn
                v_out[...] = vn
                d_out[...] = -ADAM_LR * ((mn / c1) / (jnp.sqrt(vn / c2) + ADAM_EPS) + ADAM_WD * w_ref[...])

    row = pl.BlockSpec((None, tr, c), lambda l, i: (l, i, 0))
    return pl.pallas_call(
        body, name=name, grid=(nl, r // tr),
        in_specs=[pl.BlockSpec((N_DEV, tr, c), lambda l, i: (0, i, 0))] * nl + [row, row, row],
        out_specs=[row] * 4, out_shape=[jax.ShapeDtypeStruct((nl, r, c), F32)] * 4,
        compiler_params=_params(("parallel", "parallel")))(*parts, w, m, v)


def _pack(arrays, dtype, row_mult):
    flat = jnp.concatenate([a.astype(dtype).reshape(-1) for a in arrays])
    n = flat.shape[0]
    quantum = row_mult * LANE
    total = -(-n // quantum) * quantum
    return jnp.pad(flat, (0, total - n)).reshape(total // LANE, LANE)


def _unpack(packed, shapes):
    flat = packed.reshape(-1)
    out, pos = [], 0
    for shp in shapes:
        n = int(np.prod(shp))
        out.append(flat[pos:pos + n].reshape(shp))
        pos += n
    return out


def _rows(a):
    return a.reshape(-1, a.shape[-1])


def _unshard(gathered, shard_shape, axis):
    t = jnp.moveaxis(gathered.reshape((N_DEV,) + tuple(shard_shape)), 0, axis)
    return t.reshape(tuple(shard_shape[:axis]) + (N_DEV * shard_shape[axis],) + tuple(shard_shape[axis + 1:]))


def _to_shards(full, axis):
    shp = full.shape
    t = full.reshape(shp[:axis] + (N_DEV, shp[axis] // N_DEV) + shp[axis + 1:])
    t = jnp.moveaxis(t, axis, 0)
    return t.reshape(N_DEV, -1, t.shape[-1])


A_COLS = len(A_GROUPS) * 3 * HW
KV_COLS = B_KV_HEADS * HEAD_DIM
KV_REP = N_HEADS // B_KV_HEADS
O_BQ = A_COLS
O_BK = O_BQ + HW
O_BV = O_BK + KV_COLS
O_CQ = O_BV + KV_COLS
O_CKV = O_CQ + C_Q_RANK
O_KR = O_CKV + C_KV_RANK
O_GATE = O_KR + C_ROPE


def _proj_weight(w_in):
    d = w_in.shape[0]

    def per_q_head(lo):
        t = w_in[:, lo:lo + KV_COLS].reshape(d, B_KV_HEADS, 1, HEAD_DIM)
        return jnp.broadcast_to(t, (d, B_KV_HEADS, KV_REP, HEAD_DIM)).reshape(d, HW)

    kr = jnp.pad(w_in[:, O_KR:O_GATE], ((0, 0), (C_NOPE, LANE - C_NOPE - C_ROPE)))
    return jnp.concatenate([w_in[:, O_GATE:], w_in[:, :O_BK], per_q_head(O_BK), per_q_head(O_BV),
                            w_in[:, O_CQ:O_KR], kr], axis=1)


def _unproj_grad(dw_p, d_model):
    d = dw_p.shape[0]
    g = 3 * d_model
    p_bk = g + O_BK
    p_bv = p_bk + HW
    p_cq = p_bv + HW
    p_kr = p_cq + C_Q_RANK + C_KV_RANK

    def sum_heads(lo):
        return dw_p[:, lo:lo + HW].reshape(d, B_KV_HEADS, KV_REP, HEAD_DIM).sum(axis=2).reshape(d, KV_COLS)

    return jnp.concatenate([dw_p[:, g:p_bk], sum_heads(p_bk), sum_heads(p_bv), dw_p[:, p_cq:p_kr],
                            dw_p[:, p_kr + C_NOPE:p_kr + C_NOPE + C_ROPE], dw_p[:, :g]], axis=1)


def _pad_heads(w, per_head, lo, hi):
    r = w.shape[0]
    t = w.reshape(r, N_HEADS, per_head)[:, :, lo:hi]
    return jnp.pad(t, ((0, 0), (0, 0), (0, LANE - (hi - lo)))).reshape(r, N_HEADS * LANE)


def _unpad_heads(w, width):
    r = w.shape[0]
    return w.reshape(r, N_HEADS, LANE)[:, :, :width]


def _t5_bucket(dist):
    n = jnp.maximum(dist, 0)
    max_exact = REL_BUCKETS // 2
    scaled = jnp.log(jnp.maximum(n, 1).astype(F32) / max_exact) / math.log(REL_MAX_DIST / max_exact)
    large = max_exact + (scaled * (REL_BUCKETS - max_exact)).astype(jnp.int32)
    return jnp.where(n < max_exact, n, jnp.minimum(large, REL_BUCKETS - 1))


def _buckets(dil):
    qi = jnp.arange(BLOCK)[:, None]
    ki = jnp.arange(2 * BLOCK)[None, :]
    return _t5_bucket((BLOCK + qi - ki) * dil).astype(jnp.int32)


def _rope_tables(s):
    pos = jnp.arange(s, dtype=F32)
    inv_freq = ROPE_BASE ** (-jnp.arange(0, C_ROPE, 2, dtype=F32) / C_ROPE)
    ang = pos[:, None] * inv_freq[None, :]
    cos, sin = jnp.cos(ang), jnp.sin(ang)
    ones = jnp.ones((s, C_NOPE), F32)
    tail = LANE - C_NOPE - C_ROPE
    c = jnp.concatenate([ones, cos, cos, jnp.ones((s, tail), F32)], axis=1)
    sn = jnp.concatenate([jnp.zeros((s, C_NOPE), F32), sin, sin, jnp.zeros((s, tail), F32)], axis=1)
    return c, sn


def _cols(d_model):
    g = 3 * d_model // HW
    a = [(g + 3 * i, g + 3 * i + 1, g + 3 * i + 2) for i in range(len(A_GROUPS))]
    b = (g + 9, g + 10, g + 11)
    cq_lane = (g + 12) * HW
    return a, b, cq_lane


def _layer_weights(full, d_model):
    lw = {}
    if 'w_in' in full:
        lw['w_in_p'] = _proj_weight(full['w_in'])
    if 'w_uq' in full:
        per = C_NOPE + C_ROPE
        w_uq, w_ukv, wb = full['w_uq'], full['w_ukv'], full['w_branch']
        wb2 = jnp.pad(wb[2].reshape(N_HEADS, HEAD_DIM, d_model), ((0, 0), (0, LANE - HEAD_DIM), (0, 0)))
        lw.update(
            wq=_pad_heads(w_uq, per, 0, per), wk=_pad_heads(w_ukv, 2 * C_NOPE, 0, C_NOPE),
            wv=_pad_heads(w_ukv, 2 * C_NOPE, C_NOPE, 2 * C_NOPE), wb0=wb[0], wb1=wb[1],
            wb2=wb2.reshape(N_HEADS * LANE, d_model), w_out=full['w_out'], w_up=full['w_ffn_up'],
            w_down=full['w_ffn_down'], conv_w=full['conv_w'])
    return lw


def _layer_fwd(l, x, x16, lw, small, tabs, comm=None, proj_comm=None):
    s, d = x.shape
    a_cols, b_cols, cq_lane = _cols(d)
    tag = f"l{l}_"
    if proj_comm is None:
        proj = matmul(tag + "proj", x16, lw['w_in_p'], 'nn')
    else:
        proj, got = matmul(tag + "proj", x16, lw['w_in_p'], 'nn', comm=proj_comm[0])
        lw = dict(lw, **_layer_weights(proj_comm[1](got), d))
    res = dict(x=x, x16=x16, proj=proj, lw=lw)
    outs, lses, slabs = [], [], []
    for gi, (window, dil) in enumerate(A_GROUPS):
        if dil == 1:
            slab, cols = proj, a_cols[gi]
        else:
            slab = to_classes(proj[:, a_cols[gi][0] * HW:(a_cols[gi][2] + 1) * HW].astype(BF16), dil)
            cols = (0, 1, 2)
        o, lg = band_fwd(tag + f"a{gi}_fwd", slab, dil, cols, tabs['bias'][gi], None, True)
        outs.append(from_classes(o, dil))
        lses.append(from_classes(lg, dil))
        slabs.append((slab, cols))
    res['a_o'], res['a_l'], res['a_slabs'] = outs, lses, slabs
    (o_a,) = rows_fwd(tag + "a_combine", combine_fn, [(o, HW, 0) for o in outs] + [(lg, HW, 0) for lg in lses], [],
                      [(HW, BF16)])
    sink = small['sinks'][l].reshape(N_HEADS, 1, 1)
    o_b, _ = band_fwd(tag + "b_fwd", proj, 1, b_cols, tabs['bias'][3], sink, False)
    gq = small['q_norm_g'][l].reshape(1, -1)
    gkv = small['kv_norm_g'][l].reshape(1, -1)
    (qc,) = rows_fwd(tag + "c_q", mla_q_fn, [(proj, C_Q_RANK, cq_lane // C_Q_RANK), (tabs['cos'], LANE, 0),
                                            (tabs['sin'], LANE, 0)], [gq, lw['wq']], [(N_HEADS * LANE, BF16)])
    ckv_blk = (cq_lane + C_Q_RANK) // LANE
    kc, vc = rows_fwd(tag + "c_kv", mla_kv_fn, [(proj, LANE, ckv_blk), (proj, LANE, ckv_blk + 1), (tabs['cos'], LANE, 0),
                                                (tabs['sin'], LANE, 0)], [gkv, lw['wk'], lw['wv']],
                      [(N_HEADS * LANE, BF16)] * 2)
    o_c, lse_c, res['comm'] = mla_fwd(tag + "mla_fwd", qc, kc, vc, comm)
    res.update(o_a=o_a, o_b=o_b, o_c=o_c, lse_c=lse_c, qc=qc, kc=kc, vc=vc)
    bg = small['b_gate'][l].reshape(3, 1, d)
    ys = [matmul(tag + f"branch{i}", o, lw[f'wb{i}'], 'nn') for i, o in enumerate((o_a, o_b, o_c))]
    (merged,) = rows_fwd(tag + "merge", merge_fn, [(y, d, 0) for y in ys] + [(proj, d, i) for i in range(3)],
                         [bg[0], bg[1], bg[2]], [(d, BF16)])
    mix = matmul(tag + "out_proj", merged, lw['w_out'], 'nn')
    ln1 = [small['ln1_g'][l].reshape(1, d), small['ln1_b'][l].reshape(1, d)]
    x1, x1_16 = rows_fwd(tag + "ln1", ln_twice, [(x, d, 0), (mix, d, 0)], ln1, [(d, F32), (d, BF16)])
    u = matmul(tag + "ffn_up", x1_16, lw['w_up'], 'nn')
    h = conv_glu_fwd(u, lw['conv_w'], small['conv_b'][l].reshape(1, -1))
    ff = matmul(tag + "ffn_down", h, lw['w_down'], 'nn')
    ln2 = [small['ln2_g'][l].reshape(1, d), small['ln2_b'][l].reshape(1, d)]
    x2, x2_16 = rows_fwd(tag + "ln2", ln_twice, [(x1, d, 0), (ff, d, 0)], ln2, [(d, F32), (d, BF16)])
    res.update(merged=merged, mix=mix, x1=x1, x1_16=x1_16, u=u, h=h, ff=ff, ys=ys)
    return x2, x2_16, res


def _layer_bwd(l, dy, res, lw, small, tabs, comm=None, dw_comm=None):
    x, proj = res['x'], res['proj']
    s, d = x.shape
    a_cols, b_cols, cq_lane = _cols(d)
    tag = f"l{l}_"
    g = {}
    ln2 = [small['ln2_g'][l].reshape(1, d), small['ln2_b'][l].reshape(1, d)]
    dff, g['ln2_g'], g['ln2_b'] = rows_bwd(tag + "ln2_bwd", ln_fn, [(res['x1'], d, 0), (res['ff'], d, 0)], ln2,
                                           [(dy, d)], [(1, F32)])
    g['w_ffn_down'] = matmul(tag + "dw_down", res['h'], dff, 'tn')
    dh = matmul(tag + "dh", dff, lw['w_down'], 'nt')
    conv_b = small['conv_b'][l].reshape(1, -1)
    dug, duv, dwg, dwv, dbg, dbv = conv_glu_bwd(res['u'], dh, lw['conv_w'], conv_b)
    du = jnp.concatenate([dug, duv], axis=1)
    g['conv_w'] = jnp.concatenate([dwg, dwv], axis=1)
    g['conv_b'] = jnp.concatenate([dbg, dbv], axis=1).reshape(-1)
    g['w_ffn_up'] = matmul(tag + "dw_up", res['x1_16'], du, 'tn')
    dx1 = matmul(tag + "dx1", du, lw['w_up'], 'nt', add=dff, scale=ALPHA)
    ln1 = [small['ln1_g'][l].reshape(1, d), small['ln1_b'][l].reshape(1, d)]
    dmix, g['ln1_g'], g['ln1_b'] = rows_bwd(tag + "ln1_bwd", ln_fn, [(x, d, 0), (res['mix'], d, 0)], ln1,
                                            [(dx1, d)], [(1, F32)])
    g['w_out'] = matmul(tag + "dw_out", res['merged'], dmix, 'tn')
    dmerged = matmul(tag + "dmerged", dmix, lw['w_out'], 'nt')
    bg = small['b_gate'][l].reshape(3, 1, d)
    mrows = [(y, d, 0) for y in res['ys']] + [(proj, d, i) for i in range(3)]
    (dy0, dy1, dy2, dg0, dg1, dg2, db0, db1, db2) = rows_bwd(
        tag + "merge_bwd", merge_fn, mrows, [bg[0], bg[1], bg[2]], [(dmerged, d)], [(i, BF16) for i in range(6)])
    g['b_gate'] = jnp.concatenate([db0, db1, db2], axis=1).reshape(-1)
    branch_in = (res['o_a'], res['o_b'], res['o_c'])
    dwb = [matmul(tag + f"dw_branch{i}", o, dyi, 'tn') for i, (o, dyi) in enumerate(zip(branch_in, (dy0, dy1, dy2)))]
    dwb[2] = dwb[2].reshape(N_HEADS, LANE, d)[:, :HEAD_DIM].reshape(HW, d)
    g['w_branch'] = jnp.stack(dwb, axis=0)
    do_a, do_b, do_c = [matmul(tag + f"do_branch{i}", dyi, lw[f'wb{i}'], 'nt')
                        for i, dyi in enumerate((dy0, dy1, dy2))]
    cw = N_HEADS * LANE
    delta, do_c16 = rows_fwd(tag + "c_delta", mla_delta_fn, [(do_c, cw, 0), (res['o_c'], cw, 0)], [],
                             [(cw, F32), (cw, BF16)])
    dqc, dkc, dvc, g['comm'] = mla_bwd(tag + "mla_bwd", res['qc'], res['kc'], res['vc'], do_c16, res['lse_c'], delta,
                                       comm)
    gq = small['q_norm_g'][l].reshape(1, -1)
    gkv = small['kv_norm_g'][l].reshape(1, -1)
    dcq, dgq, dwq = rows_bwd(tag + "c_q_bwd", mla_q_fn,
                             [(proj, C_Q_RANK, cq_lane // C_Q_RANK), (tabs['cos'], LANE, 0), (tabs['sin'], LANE, 0)],
                             [gq, lw['wq']], [(dqc, N_HEADS * LANE)], [(0, BF16)])
    ckv_blk = (cq_lane + C_Q_RANK) // LANE
    dckv, dkr, dgkv, dwk, dwv = rows_bwd(
        tag + "c_kv_bwd", mla_kv_fn,
        [(proj, LANE, ckv_blk), (proj, LANE, ckv_blk + 1), (tabs['cos'], LANE, 0), (tabs['sin'], LANE, 0)],
        [gkv, lw['wk'], lw['wv']], [(dkc, N_HEADS * LANE), (dvc, N_HEADS * LANE)], [(0, BF16), (1, BF16)])
    g['q_norm_g'], g['kv_norm_g'] = dgq.reshape(-1), dgkv.reshape(-1)
    per = C_NOPE + C_ROPE
    g['w_uq'] = _unpad_heads(dwq, per).reshape(C_Q_RANK, N_HEADS * per)
    g['w_ukv'] = jnp.concatenate([_unpad_heads(dwk, C_NOPE), _unpad_heads(dwv, C_NOPE)], axis=2).reshape(
        C_KV_RANK, N_HEADS * 2 * C_NOPE)
    sink = small['sinks'][l].reshape(N_HEADS, 1, 1)
    piece_b, dbias_b, dsink = band_bwd(tag + "b_bwd", proj, 1, b_cols, tabs['bias'][3], sink, False, do_b, None)
    pieces_b = [piece_b]
    g['sinks'] = dsink.reshape(-1)
    dbias = [None] * 4
    dbias[3] = dbias_b
    combo = [(o, HW, 0) for o in res['a_o']] + [(lg, HW, 0) for lg in res['a_l']]
    a_cts = rows_bwd(tag + "a_combine_bwd", combine_fn, combo, [], [(do_a, HW)],
                     [(i, BF16) for i in range(3)] + [(i, F32) for i in range(3, 6)])
    pieces_a = []
    for gi, (window, dil) in enumerate(A_GROUPS):
        slab, cols = res['a_slabs'][gi]
        group, dbias[gi], _ = band_bwd(
            tag + f"a{gi}_bwd", slab, dil, cols, tabs['bias'][gi], None, True, to_classes(a_cts[gi], dil),
            to_classes(a_cts[3 + gi], dil))
        pieces_a.append(from_classes(group, dil))
    g['dbias'] = dbias
    dproj = jnp.concatenate([dg0, dg1, dg2] + pieces_a + pieces_b + [dcq, dckv, dkr], axis=1)
    if dw_comm is None:
        dw_in_p = matmul(tag + "dw_in", res['x16'], dproj, 'tn')
    else:
        dw_in_p, g['dw_comm'] = matmul(tag + "dw_in", res['x16'], dproj, 'tn', comm=dw_comm(g))
    g['w_in'] = _unproj_grad(dw_in_p, d)
    dx = matmul(tag + "dx", dproj, lw['w_in_p'], 'nt', add=dmix, scale=ALPHA)
    return dx, g


def kernel(x, rel_table, w_in, b_gate, sinks, q_norm_g, kv_norm_g, w_uq, w_ukv, w_branch, w_out, ln1_g, ln1_b, w_ffn_up, conv_w, conv_b, w_ffn_down, ln2_g, ln2_b, loss_target, m_rel_table, m_w_in, m_b_gate, m_sinks, m_q_norm_g, m_kv_norm_g, m_w_uq, m_w_ukv, m_w_branch, m_w_out, m_ln1_g, m_ln1_b, m_w_ffn_up, m_conv_w, m_conv_b, m_w_ffn_down, m_ln2_g, m_ln2_b, v_rel_table, v_w_in, v_b_gate, v_sinks, v_q_norm_g, v_kv_norm_g, v_w_uq, v_w_ukv, v_w_branch, v_w_out, v_ln1_g, v_ln1_b, v_w_ffn_up, v_conv_w, v_conv_b, v_w_ffn_down, v_ln2_g, v_ln2_b):
    args = locals()
    w = {n: args[n] for n in WEIGHTS}
    mom = {n: args["m_" + n] for n in WEIGHTS}
    var = {n: args["v_" + n] for n in WEIGHTS}
    s, d = x.shape[1], x.shape[2]
    xs = x.reshape(s, d)
    target = loss_target.reshape(s, d)

    big = [n for n in WEIGHTS if n in SHARDED]
    small_names = [n for n in WEIGHTS if n not in SHARDED]
    assert DEPTH == 2
    wire = lambda n: BF16 if n in GATHER_BF16 else F32
    rest = [n for n in big if n != 'w_in']

    def shards(l, names):
        return [_rows(w[n][l]).astype(wire(n)) for n in names]

    def to_full(names, gathered):
        return {n: _unshard(t, w[n].shape[1:], SHARDED[n] - 1) for n, t in zip(names, gathered)}

    def grad_shards(names, g):
        return [_to_shards(g[n], SHARDED[n] - 1).astype(wire(n)) for n in names]

    w_in0 = to_full(['w_in'], exchange("gather_w_in0", shards(0, ['w_in']), scatter=False))
    fused = dict(proj0=((shards(0, rest), False), lambda got: to_full(rest, got)),
                 fwd0=((shards(1, big), False), lambda got: to_full(big, got)),
                 bwd0=lambda g1: (grad_shards(big, g1), True),
                 dw0=lambda g0: (grad_shards(rest, g0), True))
    small = {n: w[n] for n in small_names}
    loss_part, grad_x, grads, g_rel, recv1, recv0_rest = _local_grads(xs, target, [w_in0, None], small, fused)
    grad_x = grad_x.reshape(x.shape)
    recv0 = dict(zip(rest, recv0_rest))
    (recv0['w_in'],) = exchange("exchange_dw_in0", grad_shards(['w_in'], grads[0]), scatter=True)

    big_res = [{}, {}, {}, {}]
    per_layer = lambda t: t.reshape((DEPTH, -1, t.shape[-1]))
    for i, n in enumerate(big):
        outs = adamw_sum("adamw_" + n, [recv0[n], recv1[i]], per_layer(w[n]), per_layer(mom[n]), per_layer(var[n]))
        for k in range(4):
            big_res[k][n] = outs[k].reshape(w[n].shape)

    g_small = {'rel_table': g_rel}
    for n in small_names:
        if n != 'rel_table':
            g_small[n] = jnp.stack([grads[l][n] for l in range(DEPTH)], axis=0)
    small_pack = _pack([g_small[n].reshape(w[n].shape) for n in small_names] + [loss_part[0, 0:1]], F32, 8)
    (small_all,) = exchange("gather_small_grads", [small_pack], scatter=False)
    pks = lambda tree: _pack([tree[n] for n in small_names] + [jnp.zeros((1,), F32)], F32, 8)[None]
    small_out = adamw_sum("adamw_replicated", [small_all], pks(w), pks(mom), pks(var))
    small_shapes = [w[n].shape for n in small_names] + [(1,)]
    small_res = [dict(zip(small_names + ['loss'], _unpack(o, small_shapes))) for o in small_out]

    loss = small_res[0]['loss'].reshape(())
    out = [loss, grad_x]
    for k in range(4):
        out += [big_res[k][n] if n in SHARDED else small_res[k][n] for n in WEIGHTS]
    return tuple(out)


def _local_grads(xs, target, fulls, small, fused=None):
    s, d = xs.shape
    rel_table = small['rel_table']
    fulls = list(fulls)

    cos, sin = _rope_tables(s)
    buckets = [_buckets(dil) for _, dil in A_GROUPS] + [_buckets(1)]
    def lookup(table, bucket):
        out = jnp.zeros((N_HEADS,) + bucket.shape, F32)
        for b in range(REL_BUCKETS):
            out = jnp.where((bucket == b)[None], table[b][:, None, None], out)
        return out

    bias = [lookup(rel_table[:, gi * N_HEADS:(gi + 1) * N_HEADS], buckets[gi]) for gi in range(4)]
    tabs = dict(cos=cos, sin=sin, bias=bias)

    act, act16, saved = xs, xs.astype(BF16), []
    for l in range(DEPTH):
        lw = _layer_weights(fulls[l], d)
        ride = fused is not None and l == 0
        act, act16, res = _layer_fwd(l, act, act16, lw, small, tabs, fused['fwd0'][0] if ride else None,
                                     fused['proj0'] if ride else None)
        if ride:
            fulls[1] = fused['fwd0'][1](res['comm'])
        saved.append(res)
    dy, loss_part = loss_head(act, target)

    grads, recv1, recv0_rest = [None] * DEPTH, None, None
    for l in reversed(range(DEPTH)):
        ride = fused is not None and l == 0
        dy, grads[l] = _layer_bwd(l, dy, saved[l], saved[l]['lw'], small, tabs,
                                  fused['bwd0'](grads[1]) if ride else None, fused['dw0'] if ride else None)
        if ride:
            recv1, recv0_rest = grads[l]['comm'], grads[l]['dw_comm']
    grad_x = dy

    rel_cols = []
    for gi in range(4):
        both = grads[0]['dbias'][gi] + grads[1]['dbias'][gi] if DEPTH == 2 else grads[0]['dbias'][gi]
        rel_cols.append(rel_grad(f"rel_grad{gi}", both, buckets[gi])[:, :REL_BUCKETS].T)
    g_rel = jnp.concatenate(rel_cols, axis=1)
    return loss_part, grad_x, grads, g_rel, recv1, recv0_rest
```

```python
import functools
import math

import numpy as np
import jax
import jax.numpy as jnp
from jax import lax
from jax.experimental import pallas as pl
from jax.experimental.pallas import tpu as pltpu

F32 = jnp.float32
BF16 = jnp.bfloat16

N_DEV = 8
DEPTH = 2
HEAD_DIM = 64
BLOCK = 128
A_GROUPS = ((128, 1), (512, 4), (2048, 16))
N_HEADS = 8
HW = N_HEADS * HEAD_DIM
B_KV_HEADS = 2
C_Q_RANK = 256
C_KV_RANK = 128
C_NOPE = 64
C_ROPE = 32
ROPE_BASE = 10000.0
REL_BUCKETS = 32
REL_MAX_DIST = 2048
ALPHA = (2 * DEPTH) ** 0.25
LN_EPS = 1e-5
RMS_EPS = 1e-6
NEG = -1e30
ADAM_LR = 0.001
ADAM_B1 = 0.9
ADAM_B2 = 0.999
ADAM_EPS = 1e-08
ADAM_WD = 0.01
ADAM_STEP = 10

LANE = 128
TM = 256
VMEM_LIMIT = 56 * 1024 * 1024
MESH_ID = pl.DeviceIdType.MESH

WEIGHTS = ('rel_table', 'w_in', 'b_gate', 'sinks', 'q_norm_g', 'kv_norm_g', 'w_uq', 'w_ukv', 'w_branch',
           'w_out', 'ln1_g', 'ln1_b', 'w_ffn_up', 'conv_w', 'conv_b', 'w_ffn_down', 'ln2_g', 'ln2_b')
SHARDED = {'w_in': 2, 'w_uq': 2, 'w_ukv': 2, 'w_branch': 3, 'w_out': 1, 'w_ffn_up': 2, 'conv_w': 2,
           'w_ffn_down': 1}
GATHER_BF16 = ('w_in', 'w_uq', 'w_ukv', 'w_branch', 'w_out', 'w_ffn_up', 'w_ffn_down')


def _params(sem=None):
    return pltpu.CompilerParams(dimension_semantics=sem, vmem_limit_bytes=VMEM_LIMIT)


def _pick(n, target):
    if n <= target:
        return n
    best = None
    for t in range(LANE, target + 1, LANE):
        if n % t == 0:
            best = t
    assert best is not None, (n, target)
    return best


def _dot(a, b, ca, cb):
    return lax.dot_general(a.astype(BF16), b.astype(BF16), (((ca,), (cb,)), ((), ())),
                           preferred_element_type=F32)


@jax.custom_vjp
def mm(a, b):
    return _dot(a, b, 1, 0)


def _mm_fwd(a, b):
    return _dot(a, b, 1, 0), (a, b)


def _mm_bwd(res, g):
    a, b = res
    return _dot(g, b, 1, 1), _dot(a, g, 0, 0)


mm.defvjp(_mm_fwd, _mm_bwd)


@jax.custom_vjp
def mm_nt(a, b):
    return _dot(a, b, 1, 1)


def _mm_nt_fwd(a, b):
    return _dot(a, b, 1, 1), (a, b)


def _mm_nt_bwd(res, g):
    a, b = res
    return _dot(g, b, 1, 0), _dot(g, a, 0, 0)


mm_nt.defvjp(_mm_nt_fwd, _mm_nt_bwd)


def _split_impl(x, n):
    w = x.shape[-1] // n
    return tuple(x[:, i * w:(i + 1) * w] for i in range(n))


@functools.partial(jax.custom_vjp, nondiff_argnums=(1,))
def split_lanes(x, n):
    return _split_impl(x, n)


def _split_fwd(x, n):
    return _split_impl(x, n), None


def _split_bwd(n, _, gs):
    return (jnp.concatenate(gs, axis=-1),)


split_lanes.defvjp(_split_fwd, _split_bwd)


@jax.custom_vjp
def concat_lanes(xs):
    return jnp.concatenate(xs, axis=-1)


def _concat_fwd(xs):
    return jnp.concatenate(xs, axis=-1), len(xs)


def _concat_bwd(n, g):
    return (_split_impl(g, n),)


concat_lanes.defvjp(_concat_fwd, _concat_bwd)


@jax.custom_vjp
def concat_rows(a, b):
    return jnp.concatenate([a, b], axis=0)


def _crow_fwd(a, b):
    return jnp.concatenate([a, b], axis=0), a.shape[0]


def _crow_bwd(na, g):
    return g[:na], g[na:]


concat_rows.defvjp(_crow_fwd, _crow_bwd)


def _rot_half(x):
    n = x.shape[-1]
    lane = lax.broadcasted_iota(jnp.int32, (1, n), 1) % LANE
    lo = (lane >= C_NOPE) & (lane < C_NOPE + C_ROPE // 2)
    hi = (lane >= C_NOPE + C_ROPE // 2) & (lane < C_NOPE + C_ROPE)
    up = pltpu.roll(x, n - C_ROPE // 2, 1)
    dn = pltpu.roll(x, C_ROPE // 2, 1)
    return jnp.where(lo, -up, jnp.where(hi, dn, 0.0))


@jax.custom_vjp
def rope(x, c, s):
    return x * c + _rot_half(x) * s


def _rope_fwd(x, c, s):
    return x * c + _rot_half(x) * s, (c, s)


def _rope_bwd(res, g):
    c, s = res
    return g * c - _rot_half(g * s), jnp.zeros_like(c), jnp.zeros_like(s)


rope.defvjp(_rope_fwd, _rope_bwd)


def _sigmoid(x):
    return 0.5 * jnp.tanh(0.5 * x) + 0.5


MATMUL_VMEM = 44 * 1024 * 1024


def _matmul_tiles(m, n, k, a_bytes, b_bytes, o_bytes, has_add):
    tm, tn, tk = _pick(m, 1024), _pick(n, 2560), _pick(k, 2560)

    def need(tn_, tk_):
        acc = tm * tn_ * 4 if k // tk_ > 1 else 0
        return 2 * (tm * tk_ * a_bytes + tk_ * tn_ * b_bytes + tm * tn_ * o_bytes) + acc + 2 * tm * tn_ * 4 * int(has_add)

    while need(tn, tk) > MATMUL_VMEM:
        if tk >= tn and tk > LANE:
            tk = _pick(k, tk - LANE)
        else:
            tn = _pick(n, tn - LANE)
    return tm, tn, tk


def matmul(name, a, b, mode, out_dtype=F32, add=None, scale=1.0, comm=None):
    if mode == 'nn':
        (m, k), n = a.shape, b.shape[1]
    elif mode == 'nt':
        (m, k), n = a.shape, b.shape[0]
    else:
        (k, m), n = a.shape, b.shape[1]
    tm, tn, tk = _matmul_tiles(m, n, k, a.dtype.itemsize, b.dtype.itemsize, jnp.dtype(out_dtype).itemsize,
                               add is not None)
    nk = k // tk
    grid = (m // tm, n // tn, nk)
    c_in, c_specs, c_shapes, c_sems, c_hook = _ride_along(comm, grid)
    nc = len(c_in)
    a_spec = pl.BlockSpec((tk, tm), lambda i, j, kk: (kk, i)) if mode == 'tn' else \
        pl.BlockSpec((tm, tk), lambda i, j, kk: (i, kk))
    b_spec = pl.BlockSpec((tn, tk), lambda i, j, kk: (j, kk)) if mode == 'nt' else \
        pl.BlockSpec((tk, tn), lambda i, j, kk: (kk, j))
    o_spec = pl.BlockSpec((tm, tn), lambda i, j, kk: (i, j))
    ca = 0 if mode == 'tn' else 1
    cb = 1 if mode == 'nt' else 0
    has_add = add is not None

    n_in = 2 + int(has_add)

    def body(*refs):
        a_ref, b_ref = refs[:2]
        add_ref = refs[2] if has_add else None
        o_ref = refs[n_in + nc]
        c_hook(refs[n_in:n_in + nc], refs[n_in + nc + 1:n_in + 2 * nc + 1], refs[len(refs) - 3:])

        def finish(r):
            if has_add:
                r = r + scale * add_ref[...]
            o_ref[...] = r.astype(out_dtype)

        if nk == 1:
            finish(_dot(a_ref[...], b_ref[...], ca, cb))
            return
        acc_ref = refs[n_in + 2 * nc + 1]
        kk = pl.program_id(2)

        @pl.when(kk == 0)
        def _():
            acc_ref[...] = jnp.zeros_like(acc_ref)

        acc_ref[...] += _dot(a_ref[...], b_ref[...], ca, cb)

        @pl.when(kk == nk - 1)
        def _():
            finish(acc_ref[...])

    ins = [a, b] + ([add] if has_add else []) + c_in
    specs = [a_spec, b_spec] + ([o_spec] if has_add else []) + c_specs
    res = pl.pallas_call(
        body, name=name, grid=grid, in_specs=specs, out_specs=[o_spec] + c_specs,
        out_shape=[jax.ShapeDtypeStruct((m, n), out_dtype)] + c_shapes,
        scratch_shapes=([pltpu.VMEM((tm, tn), F32)] if nk > 1 else []) + c_sems,
        compiler_params=_params(("arbitrary",) * 3 if comm is not None else ("parallel", "parallel", "arbitrary")))(*ins)
    return res[0] if comm is None else (res[0], list(res[1:]))


def _row_spec(width, col_block=0, tm=TM):
    return pl.BlockSpec((tm, width), lambda i: (i, col_block))


def _full_spec(shape):
    nd = len(shape)
    return pl.BlockSpec(shape, lambda i: (0,) * nd)


def rows_fwd(name, fn, rows, params, outs, tm=TM):
    s = rows[0][0].shape[0]
    nr, npar = len(rows), len(params)

    def body(*refs):
        r = [refs[i][...].astype(F32) for i in range(nr)]
        p = [refs[nr + i][...] for i in range(npar)]
        res = fn(*r, *p)
        for o_ref, val in zip(refs[nr + npar:], res):
            o_ref[...] = val.astype(o_ref.dtype)

    return pl.pallas_call(
        body, name=name, grid=(s // tm,),
        in_specs=[_row_spec(w, cb, tm) for _, w, cb in rows] + [_full_spec(p.shape) for p in params],
        out_specs=[_row_spec(w, 0, tm) for w, _ in outs],
        out_shape=[jax.ShapeDtypeStruct((s, w), dt) for w, dt in outs],
        compiler_params=_params(("parallel",)))(*[a for a, _, _ in rows], *params)


def rows_bwd(name, fn, rows, params, cts, row_grads, tm=TM):
    s = rows[0][0].shape[0]
    nr, npar, nct, nrg = len(rows), len(params), len(cts), len(row_grads)

    def body(*refs):
        r = [refs[i][...].astype(F32) for i in range(nr)]
        p = [refs[nr + i][...].astype(F32) for i in range(npar)]
        g = tuple(refs[nr + npar + i][...].astype(F32) for i in range(nct))
        _, vjp = jax.vjp(lambda *a: tuple(fn(*a)), *r, *p)
        grads = vjp(g)
        outs = refs[nr + npar + nct:]
        for k, (idx, _) in enumerate(row_grads):
            outs[k][...] = grads[idx].astype(outs[k].dtype)

        @pl.when(pl.program_id(0) == 0)
        def _():
            for k in range(npar):
                outs[nrg + k][...] = jnp.zeros_like(outs[nrg + k])

        for k in range(npar):
            outs[nrg + k][...] += grads[nr + k]

    return pl.pallas_call(
        body, name=name, grid=(s // tm,),
        in_specs=[_row_spec(w, cb, tm) for _, w, cb in rows] + [_full_spec(p.shape) for p in params]
        + [_row_spec(w, 0, tm) for _, w in cts],
        out_specs=[_row_spec(rows[idx][1], 0, tm) for idx, _ in row_grads] + [_full_spec(p.shape) for p in params],
        out_shape=[jax.ShapeDtypeStruct((s, rows[idx][1]), dt) for idx, dt in row_grads]
        + [jax.ShapeDtypeStruct(p.shape, F32) for p in params],
        compiler_params=_params(("arbitrary",)))(*[a for a, _, _ in rows], *params, *[a for a, _ in cts])


def ln_fn(x, r, g, b):
    z = ALPHA * x + r
    mu = jnp.mean(z, axis=-1, keepdims=True)
    zc = z - mu
    var = jnp.mean(zc * zc, axis=-1, keepdims=True)
    return (zc * lax.rsqrt(var + LN_EPS) * g + b,)


def ln_twice(x, r, g, b):
    (y,) = ln_fn(x, r, g, b)
    return y, y


def _rms(x, g):
    return x * lax.rsqrt(jnp.mean(x * x, axis=-1, keepdims=True) + RMS_EPS) * g


def mla_q_fn(cq, c, s, g, wq):
    q = mm(_rms(cq, g), wq)
    return (rope(q, jnp.concatenate([c] * N_HEADS, axis=-1), jnp.concatenate([s] * N_HEADS, axis=-1)) * MLA_SCALE,)


@jax.custom_vjp
def tile_heads(x):
    return jnp.concatenate([x] * N_HEADS, axis=-1)


def _tile_fwd(x):
    return jnp.concatenate([x] * N_HEADS, axis=-1), None


def _tile_bwd(_, g):
    parts = _split_impl(g, N_HEADS)
    acc = parts[0]
    for p in parts[1:]:
        acc = acc + p
    return (acc,)


tile_heads.defvjp(_tile_fwd, _tile_bwd)


def mla_kv_fn(ckv, kr, c, s, g, wk, wv):
    n = _rms(ckv, g)
    lane = lax.broadcasted_iota(jnp.int32, (1, N_HEADS * LANE), 1) % LANE
    one = (lane == MLA_ONE).astype(F32)
    return mm(n, wk) + tile_heads(rope(kr, c, s)), mm(n, wv) + one


def merge_fn(y0, y1, y2, g0, g1, g2, b0, b1, b2):
    return (_sigmoid(g0 + b0) * y0 + _sigmoid(g1 + b1) * y1 + _sigmoid(g2 + b2) * y2,)


def combine_fn(o1, o2, o3, l1, l2, l3):
    mx = lax.stop_gradient(jnp.maximum(jnp.maximum(l1, l2), l3))
    e1, e2, e3 = jnp.exp(l1 - mx), jnp.exp(l2 - mx), jnp.exp(l3 - mx)
    return ((e1 * o1 + e2 * o2 + e3 * o3) / (e1 + e2 + e3),)


def loss_head(y, t):
    s, d = y.shape

    def body(y_ref, t_ref, dy_ref, loss_ref):
        err = y_ref[...] - t_ref[...]
        dy_ref[...] = err * (1.0 / d)

        @pl.when(pl.program_id(0) == 0)
        def _():
            loss_ref[...] = jnp.zeros_like(loss_ref)

        loss_ref[...] += jnp.sum(err * err) * (0.5 / d)

    return pl.pallas_call(
        body, name="loss_head", grid=(s // TM,), in_specs=[_row_spec(d), _row_spec(d)],
        out_specs=[_row_spec(d), _full_spec((8, LANE))],
        out_shape=[jax.ShapeDtypeStruct((s, d), F32), jax.ShapeDtypeStruct((8, LANE), F32)],
        compiler_params=_params(("arbitrary",)))(y, t)


CONV_TN = 256
CONV_TM = 512


def _conv_taps(u_ext, w_ref, b_ref):
    d1, d2 = pltpu.roll(u_ext, 1, 0), pltpu.roll(u_ext, 2, 0)
    return w_ref[0:1, :] * d2 + w_ref[1:2, :] * d1 + w_ref[2:3, :] * u_ext + b_ref[...], d1, d2


def conv_glu_fwd(u, conv_w, conv_b):
    s, f2 = u.shape
    f = f2 // 2
    tn, tm = CONV_TN, min(CONV_TM, s)
    off = f // tn
    hb = tm // 8

    def body(ug, uv, hg, hv, wg, wv, bg, bv, o_ref):
        first = pl.program_id(0) == 0

        def conv(u_ref, h_ref, w_ref, b_ref):
            halo = jnp.where(first, 0.0, h_ref[...])
            ext = jnp.concatenate([halo, u_ref[...]], axis=0)
            return _conv_taps(ext, w_ref, b_ref)[0][8:]

        cg, cv = conv(ug, hg, wg, bg), conv(uv, hv, wv, bv)
        o_ref[...] = (cg * _sigmoid(cg) * cv).astype(o_ref.dtype)

    blk = lambda o: pl.BlockSpec((tm, tn), lambda i, j: (i, j + o))
    halo = lambda o: pl.BlockSpec((8, tn), lambda i, j: (jnp.maximum(i * hb - 1, 0), j + o))
    par = lambda r, o: pl.BlockSpec((r, tn), lambda i, j: (0, j + o))
    return pl.pallas_call(
        body, name="conv_glu_fwd", grid=(s // tm, off),
        in_specs=[blk(0), blk(off), halo(0), halo(off), par(3, 0), par(3, off), par(1, 0), par(1, off)],
        out_specs=pl.BlockSpec((tm, tn), lambda i, j: (i, j)),
        out_shape=jax.ShapeDtypeStruct((s, f), BF16),
        compiler_params=_params(("parallel", "parallel")))(u, u, u, u, conv_w, conv_w, conv_b, conv_b)


def conv_glu_bwd(u, dh, conv_w, conv_b):
    s, f2 = u.shape
    f = f2 // 2
    tn, tm = CONV_TN, min(CONV_TM, s)
    off = f // tn
    hb = tm // 8
    n_rows = s // tm

    def body(ug, uv, pg, pv, ng, nv, dh_ref, dhn_ref, wg, wv, bg, bv, dug, duv, dwg, dwv, dbg, dbv):
        i = pl.program_id(1)
        first, last = i == 0, i == n_rows - 1

        def ext(u_ref, p_ref, n_ref):
            return jnp.concatenate([jnp.where(first, 0.0, p_ref[...]), u_ref[...], n_ref[...]], axis=0)

        eg, ev = ext(ug, pg, ng), ext(uv, pv, nv)
        (cg, eg1, eg2), (cv, ev1, ev2) = _conv_taps(eg, wg, bg), _conv_taps(ev, wv, bv)
        dhe = jnp.concatenate([jnp.zeros((8, tn), F32), dh_ref[...], jnp.where(last, 0.0, dhn_ref[...])], axis=0)
        sg = _sigmoid(cg)
        dcg = dhe * cv * (sg * (1.0 + cg * (1.0 - sg)))
        dcv = dhe * (cg * sg)
        n = tm + 16

        @pl.when(i == 0)
        def _():
            for r in (dwg, dwv, dbg, dbv):
                r[...] = jnp.zeros_like(r)

        def back(dc, e, e1, e2, w_ref, du_ref, dw_ref, db_ref):
            du = w_ref[2:3, :] * dc + w_ref[1:2, :] * pltpu.roll(dc, n - 1, 0) + w_ref[0:1, :] * pltpu.roll(dc, n - 2, 0)
            du_ref[...] = du[8:8 + tm].astype(du_ref.dtype)
            own = dc[8:8 + tm]
            dw_ref[0:1, :] += jnp.sum(own * e2[8:8 + tm], axis=0, keepdims=True)
            dw_ref[1:2, :] += jnp.sum(own * e1[8:8 + tm], axis=0, keepdims=True)
            dw_ref[2:3, :] += jnp.sum(own * e[8:8 + tm], axis=0, keepdims=True)
            db_ref[...] += jnp.sum(own, axis=0, keepdims=True)

        back(dcg, eg, eg1, eg2, wg, dug, dwg, dbg)
        back(dcv, ev, ev1, ev2, wv, duv, dwv, dbv)

    blk = lambda o: pl.BlockSpec((tm, tn), lambda j, i: (i, j + o))
    prev = lambda o: pl.BlockSpec((8, tn), lambda j, i: (jnp.maximum(i * hb - 1, 0), j + o))
    nxt = lambda o: pl.BlockSpec((8, tn), lambda j, i: (jnp.minimum((i + 1) * hb, s // 8 - 1), j + o))
    par = lambda r, o: pl.BlockSpec((r, tn), lambda j, i: (0, j + o))
    return pl.pallas_call(
        body, name="conv_glu_bwd", grid=(off, n_rows),
        in_specs=[blk(0), blk(off), prev(0), prev(off), nxt(0), nxt(off), blk(0), nxt(0),
                  par(3, 0), par(3, off), par(1, 0), par(1, off)],
        out_specs=[blk(0), blk(0), par(3, 0), par(3, 0), par(1, 0), par(1, 0)],
        out_shape=[jax.ShapeDtypeStruct((s, f), BF16)] * 2 + [jax.ShapeDtypeStruct((3, f), F32)] * 2
        + [jax.ShapeDtypeStruct((1, f), F32)] * 2,
        compiler_params=_params(("parallel", "arbitrary")))(u, u, u, u, u, u, dh, dh, conv_w, conv_w, conv_b, conv_b)


def _band_fn(q, kp, kc, vp, vc, bias, sink, first, inclusive):
    k = concat_rows(kp, kc)
    v = concat_rows(vp, vc)
    qi = lax.broadcasted_iota(jnp.int32, (BLOCK, 2 * BLOCK), 0)
    ki = lax.broadcasted_iota(jnp.int32, (BLOCK, 2 * BLOCK), 1)
    step = BLOCK + qi - ki
    valid = (step >= 0) & ((step <= BLOCK) if inclusive else (step < BLOCK))
    valid = valid & (jnp.logical_not(first) | (ki >= BLOCK))
    lane = lax.broadcasted_iota(jnp.int32, (1, LANE), 1)
    scale = HEAD_DIM ** -0.5
    qs, ks, vs = split_lanes(q, 4), split_lanes(k, 4), split_lanes(v, 4)
    o_parts, l_parts = [], []
    for pair in range(4):
        o_acc, l_acc = None, None
        for sub in range(2):
            h = 2 * pair + sub
            mh = ((lane >= sub * HEAD_DIM) & (lane < (sub + 1) * HEAD_DIM)).astype(F32)
            logits = mm_nt(qs[pair] * mh, ks[pair]) * scale + bias[h]
            logits = jnp.where(valid, logits, NEG)
            m = lax.stop_gradient(jnp.max(logits, axis=-1, keepdims=True))
            if sink is not None:
                m = jnp.maximum(m, lax.stop_gradient(sink[h]))
            p = jnp.exp(logits - m)
            den = jnp.sum(p, axis=-1, keepdims=True)
            if sink is not None:
                den = den + jnp.exp(sink[h] - m)
            oh = mm(p * (1.0 / den), vs[pair] * mh)
            lh = (m + jnp.log(den)) * mh
            o_acc = oh if o_acc is None else o_acc + oh
            l_acc = lh if l_acc is None else l_acc + lh
        o_parts.append(o_acc)
        l_parts.append(l_acc)
    return concat_lanes(tuple(o_parts)), concat_lanes(tuple(l_parts))


def _band_specs(nb, cq, ck, cv):
    own = lambda n: jnp.minimum(n, nb - 1)
    prev = lambda n: jnp.maximum(own(n) - 1, 0)
    q = pl.BlockSpec((BLOCK, HW), lambda r, n: (r * nb + own(n), cq))
    kp = pl.BlockSpec((BLOCK, HW), lambda r, n: (r * nb + prev(n), ck))
    kc = pl.BlockSpec((BLOCK, HW), lambda r, n: (r * nb + own(n), ck))
    vp = pl.BlockSpec((BLOCK, HW), lambda r, n: (r * nb + prev(n), cv))
    vc = pl.BlockSpec((BLOCK, HW), lambda r, n: (r * nb + own(n), cv))
    return [q, kp, kc, vp, vc]


def to_classes(a, dil):
    if dil == 1:
        return a
    s, c = a.shape
    return a.reshape(s // dil, dil, c).transpose(1, 0, 2).reshape(s, c)


def from_classes(a, dil):
    if dil == 1:
        return a
    s, c = a.shape
    return a.reshape(dil, s // dil, c).transpose(1, 0, 2).reshape(s, c)


def _const_spec(shape):
    nd = len(shape)
    return pl.BlockSpec(shape, lambda r, n: (0,) * nd)


def band_fwd(name, qkv, dil, cols, bias, sink, inclusive):
    s = qkv.shape[0]
    nb = s // (BLOCK * dil)
    has_sink = sink is not None

    def body(*refs):
        q, kp, kc, vp, vc, b_ref = refs[:6]
        s_ref = refs[6] if has_sink else None
        o_ref, l_ref = refs[-2:]
        first = pl.program_id(1) == 0
        bias_l = tuple(b_ref[h] for h in range(N_HEADS))
        sink_l = tuple(s_ref[h] for h in range(N_HEADS)) if has_sink else None
        o, l = _band_fn(q[...], kp[...], kc[...], vp[...], vc[...], bias_l, sink_l, first, inclusive)
        o_ref[...] = o
        l_ref[...] = l

    out_spec = pl.BlockSpec((BLOCK, HW), lambda r, n: (r * nb + n, 0))
    ins = [qkv] * 5 + [bias] + ([sink] if has_sink else [])
    specs = _band_specs(nb, *cols) + [_const_spec(bias.shape)] + ([_const_spec(sink.shape)] if has_sink else [])
    return pl.pallas_call(
        body, name=name, grid=(dil, nb), in_specs=specs, out_specs=[out_spec, out_spec],
        out_shape=[jax.ShapeDtypeStruct((s, HW), F32)] * 2,
        compiler_params=_params(("parallel", "parallel")))(*ins)


def band_bwd(name, qkv, dil, cols, bias, sink, inclusive, do, dl):
    s = qkv.shape[0]
    nb = s // (BLOCK * dil)
    has_sink, has_dl = sink is not None, dl is not None
    n_in = 6 + int(has_sink) + 1 + int(has_dl)

    def body(*refs):
        q, kp, kc, vp, vc, b_ref = refs[:6]
        s_ref = refs[6] if has_sink else None
        do_ref = refs[6 + int(has_sink)]
        dl_ref = refs[7 + int(has_sink)] if has_dl else None
        out_ref, db_ref = refs[n_in:n_in + 2]
        ds_ref = refs[n_in + 2] if has_sink else None
        cq, ck, cv = refs[-3:]
        n = pl.program_id(1)

        def emit(dq, dk, dv):
            out_ref[:, 0:HW] = dq.astype(out_ref.dtype)
            out_ref[:, HW:2 * HW] = dk.astype(out_ref.dtype)
            out_ref[:, 2 * HW:3 * HW] = dv.astype(out_ref.dtype)

        @pl.when((pl.program_id(0) == 0) & (n == 0))
        def _():
            db_ref[...] = jnp.zeros_like(db_ref)
            if has_sink:
                ds_ref[...] = jnp.zeros_like(ds_ref)

        @pl.when(n < nb)
        def _():
            first = n == 0
            bias_l = tuple(b_ref[h] for h in range(N_HEADS))
            sink_l = tuple(s_ref[h] for h in range(N_HEADS)) if has_sink else ()

            def fn(q_, kp_, kc_, vp_, vc_, b_, s_):
                o, l = _band_fn(q_, kp_, kc_, vp_, vc_, b_, s_ if has_sink else None, first, inclusive)
                return (o, l) if has_dl else (o,)

            blocks = [r[...].astype(F32) for r in (q, kp, kc, vp, vc)]
            _, vjp = jax.vjp(fn, *blocks, bias_l, sink_l)
            ct = (do_ref[...].astype(F32), dl_ref[...]) if has_dl else (do_ref[...].astype(F32),)
            g = vjp(ct)

            @pl.when(n > 0)
            def _():
                emit(cq[...], ck[...] + g[1], cv[...] + g[3])

            cq[...] = g[0]
            ck[...] = g[2]
            cv[...] = g[4]
            for h in range(N_HEADS):
                db_ref[h] += g[5][h]
                if has_sink:
                    ds_ref[h] += g[6][h]

        @pl.when(n == nb)
        def _():
            emit(cq[...], ck[...], cv[...])

    blk = pl.BlockSpec((BLOCK, HW), lambda r, n: (r * nb + jnp.minimum(n, nb - 1), 0))
    late = pl.BlockSpec((BLOCK, 3 * HW), lambda r, n: (r * nb + jnp.maximum(n - 1, 0), 0))
    ins = [qkv] * 5 + [bias] + ([sink] if has_sink else []) + [do] + ([dl] if has_dl else [])
    specs = _band_specs(nb, *cols) + [_const_spec(bias.shape)] \
        + ([_const_spec(sink.shape)] if has_sink else []) + [blk] * (1 + int(has_dl))
    out_shape = [jax.ShapeDtypeStruct((s, 3 * HW), BF16), jax.ShapeDtypeStruct(bias.shape, F32)] \
        + ([jax.ShapeDtypeStruct(sink.shape, F32)] if has_sink else [])
    out_specs = [late, _const_spec(bias.shape)] + ([_const_spec(sink.shape)] if has_sink else [])
    res = pl.pallas_call(
        body, name=name, grid=(dil, nb + 1), in_specs=specs, out_specs=out_specs, out_shape=out_shape,
        scratch_shapes=[pltpu.VMEM((BLOCK, HW), F32)] * 3,
        compiler_params=_params(("arbitrary", "arbitrary")))(*ins)
    return res[0], res[1], (res[2] if has_sink else None)


def rel_grad(name, dbias, bucket):
    def body(d_ref, b_ref, o_ref):
        lane = lax.broadcasted_iota(jnp.int32, (1, LANE), 1)
        bk = b_ref[...]
        for h in range(N_HEADS):
            d = d_ref[h]
            row = jnp.zeros((1, LANE), F32)
            for b in range(REL_BUCKETS):
                tot = jnp.sum(jnp.where(bk == b, d, 0.0))
                row = row + jnp.where(lane == b, tot, 0.0)
            o_ref[h:h + 1, :] = row

    return pl.pallas_call(
        body, name=name, out_shape=jax.ShapeDtypeStruct((N_HEADS, LANE), F32),
        in_specs=[pl.BlockSpec(memory_space=pltpu.VMEM)] * 2, out_specs=pl.BlockSpec(memory_space=pltpu.VMEM),
        compiler_params=_params())(dbias, bucket)


MLA_T = 512
MLA_HPS = 2
MLA_SCALE = (C_NOPE + C_ROPE) ** -0.5


MLA_ONE = C_NOPE


def _lane_tiles(x, n):
    return jnp.concatenate([x] * n, axis=-1)


def _diag_mask(t):
    r = lax.broadcasted_iota(jnp.int32, (t, t), 0)
    c = lax.broadcasted_iota(jnp.int32, (t, t), 1)
    return c <= r


def _ride_along(comm, grid):
    if comm is None:
        return [], [], [], [], lambda *_: None
    arrays, scatter = comm
    out_shape, specs, sems = _push_shapes(arrays, scatter)

    def hook(ins, outs, sem_refs):
        ids = [pl.program_id(a) for a in range(len(grid))]
        first, last = ids[0] == 0, ids[0] == grid[0] - 1
        for a in range(1, len(grid)):
            first, last = first & (ids[a] == 0), last & (ids[a] == grid[a] - 1)

        @pl.when(first)
        def _():
            _pushes(ins, outs, sem_refs, scatter).start()

        @pl.when(last)
        def _():
            _pushes(ins, outs, sem_refs, scatter).wait()

    return list(arrays), specs, out_shape, sems, hook


def mla_fwd(name, q, k, v, comm=None):
    s = q.shape[0]
    t = min(MLA_T, s)
    nq = s // t
    grid = (N_HEADS // MLA_HPS, nq)
    c_in, c_specs, c_shapes, c_sems, c_hook = _ride_along(comm, grid)
    nc = len(c_in)

    def body(*refs):
        q_ref, k_ref, v_ref = refs[:3]
        o_ref, l_ref = refs[3 + nc:5 + nc]
        m_s, acc_s = refs[5 + 2 * nc:7 + 2 * nc]
        c_hook(refs[3:3 + nc], refs[5 + nc:5 + 2 * nc], refs[7 + 2 * nc:])
        i = pl.program_id(1)
        m_s[...] = jnp.full_like(m_s, NEG)
        acc_s[...] = jnp.zeros_like(acc_s)

        def chunk(j, masked):
            rows = pl.ds(pl.multiple_of(j * t, t), t)
            for hh in range(MLA_HPS):
                ln = slice(hh * LANE, (hh + 1) * LANE)
                sc = _dot(q_ref[:, ln], k_ref[rows, ln], 1, 1)
                if masked:
                    sc = jnp.where(_diag_mask(t), sc, NEG)
                m_old = m_s[:, ln]
                m_new = jnp.maximum(m_old, jnp.max(sc, axis=-1, keepdims=True))
                a = jnp.exp(m_old - m_new)
                p = jnp.exp(sc - _lane_tiles(m_new, t // LANE))
                acc_s[:, ln] = a * acc_s[:, ln] + _dot(p, v_ref[rows, ln], 1, 0)
                m_s[:, ln] = m_new

        def step(j, carry):
            chunk(j, False)
            return carry

        lax.fori_loop(0, i, step, 0)
        chunk(i, True)
        for hh in range(MLA_HPS):
            ln = slice(hh * LANE, (hh + 1) * LANE)
            acc = acc_s[:, ln]
            den = acc[:, MLA_ONE:MLA_ONE + 1]
            o_ref[:, ln] = acc / den
            l_ref[:, ln] = m_s[:, ln] + jnp.log(den)

    w = MLA_HPS * LANE
    qs = pl.BlockSpec((t, w), lambda h, i: (i, h))
    ks = pl.BlockSpec((s, w), lambda h, i: (0, h))
    res = pl.pallas_call(
        body, name=name, grid=grid, in_specs=[qs, ks, ks] + c_specs, out_specs=[qs, qs] + c_specs,
        out_shape=[jax.ShapeDtypeStruct((s, N_HEADS * LANE), F32)] * 2 + c_shapes,
        scratch_shapes=[pltpu.VMEM((t, w), F32)] * 2 + c_sems,
        compiler_params=_params(("arbitrary", "arbitrary")))(q, k, v, *c_in)
    return res[0], res[1], list(res[2:])


def mla_delta_fn(do, o):
    parts = []
    for a, b in zip(_split_impl(do, N_HEADS), _split_impl(o, N_HEADS)):
        parts.append(jnp.sum(a * b, axis=-1, keepdims=True) + jnp.zeros_like(a))
    return jnp.concatenate(parts, axis=-1), do


def mla_bwd(name, q, k, v, do, lse, delta, comm=None):
    s = q.shape[0]
    t = min(MLA_T, s)
    nq = s // t
    grid = (N_HEADS, nq)
    c_in, c_specs, c_shapes, c_sems, c_hook = _ride_along(comm, grid)
    nc = len(c_in)

    def body(*refs):
        q_ref, k_ref, v_ref, do_ref, l_ref, d_ref = refs[:6]
        dq_ref, dk_ref, dv_ref = refs[6 + nc:9 + nc]
        dk_s, dv_s = refs[9 + 2 * nc:11 + 2 * nc]
        c_hook(refs[6:6 + nc], refs[9 + nc:9 + 2 * nc], refs[11 + 2 * nc:])
        j = pl.program_id(1)

        @pl.when(j == 0)
        def _():
            dq_ref[...] = jnp.zeros_like(dq_ref)

        dk_s[...] = jnp.zeros_like(dk_s)
        dv_s[...] = jnp.zeros_like(dv_s)
        kb, vb = k_ref[...], v_ref[...]

        def chunk(i, masked):
            rows = pl.ds(pl.multiple_of(i * t, t), t)
            qi, doi = q_ref[rows, :], do_ref[rows, :]
            sc = _dot(qi, kb, 1, 1)
            if masked:
                sc = jnp.where(_diag_mask(t), sc, NEG)
            p = jnp.exp(sc - _lane_tiles(l_ref[rows, :], t // LANE))
            ds = p * (_dot(doi, vb, 1, 1) - _lane_tiles(d_ref[rows, :], t // LANE))
            dv_s[...] += _dot(p, doi, 0, 0)
            dk_s[...] += _dot(ds, qi, 0, 0)
            dq_ref[rows, :] += _dot(ds, kb, 1, 0)

        def pair(p, carry):
            chunk(j + 1 + 2 * p, False)
            chunk(j + 2 + 2 * p, False)
            return carry

        chunk(j, True)
        n_off = nq - 1 - j
        lax.fori_loop(0, n_off // 2, pair, 0)

        @pl.when(n_off % 2 == 1)
        def _():
            chunk(nq - 1, False)

        dk_ref[...] = dk_s[...]
        dv_ref[...] = dv_s[...]

    ks = pl.BlockSpec((t, LANE), lambda h, j: (j, h))
    full = pl.BlockSpec((s, LANE), lambda h, j: (0, h))
    res = pl.pallas_call(
        body, name=name, grid=grid, in_specs=[full, ks, ks, full, full, full] + c_specs,
        out_specs=[full, ks, ks] + c_specs,
        out_shape=[jax.ShapeDtypeStruct((s, N_HEADS * LANE), F32)] * 3 + c_shapes,
        scratch_shapes=[pltpu.VMEM((t, LANE), F32)] * 2 + c_sems,
        compiler_params=_params(("arbitrary", "arbitrary")))(q, k, v, do, lse, delta, *c_in)
    return res[0], res[1], res[2], list(res[3:])


def _peer(k):
    x, y, c = lax.axis_index("x"), lax.axis_index("y"), lax.axis_index("c")
    px, py, pc = (x ^ ((k >> 2) & 1)), (y ^ ((k >> 1) & 1)), (c ^ (k & 1))
    return (px, py, pc), 4 * px + 2 * py + pc


def exchange(name, arrays, scatter):
    n = len(arrays)

    def body(*refs):
        push = _pushes(refs[:n], refs[n:2 * n], refs[2 * n:], scatter)
        push.start()
        push.wait()

    out_shape, specs, sems = _push_shapes(arrays, scatter)
    return pl.pallas_call(body, name=name, out_shape=out_shape, in_specs=specs, out_specs=specs,
                          scratch_shapes=sems)(*arrays)


def _push_shapes(arrays, scatter):
    n = len(arrays)
    out_shape = [jax.ShapeDtypeStruct(a.shape if scatter else (N_DEV,) + a.shape, a.dtype) for a in arrays]
    n_sem = (N_DEV - 1) * n
    sems = [pltpu.SemaphoreType.DMA((n_sem,)), pltpu.SemaphoreType.DMA((n_sem,)), pltpu.SemaphoreType.DMA((n,))]
    return out_shape, [pl.BlockSpec(memory_space=pl.ANY)] * n, sems


class _pushes:
    def __init__(self, ins, outs, sems, scatter):
        self.ins, self.outs, self.sems, self.scatter = ins, outs, sems, scatter

    def _local(self):
        _, me_idx = _peer(0)
        return [pltpu.make_async_copy(self.ins[a].at[me_idx] if self.scatter else self.ins[a],
                                      self.outs[a].at[me_idx], self.sems[2].at[a]) for a in range(len(self.ins))]

    def _remote(self, incoming):
        send_sems, recv_sems, _ = self.sems
        n = len(self.ins)
        _, me_idx = _peer(0)
        copies = []
        for k in range(1, N_DEV):
            peer, peer_idx = _peer(k)
            for a in range(n):
                sem = (k - 1) * n + a
                copies.append(pltpu.make_async_remote_copy(
                    src_ref=self.ins[a].at[peer_idx] if self.scatter else self.ins[a],
                    dst_ref=self.outs[a].at[peer_idx if incoming else me_idx],
                    send_sem=send_sems.at[sem], recv_sem=recv_sems.at[sem], device_id=peer, device_id_type=MESH_ID))
        return copies

    def start(self):
        for cp in self._local() + self._remote(incoming=False):
            cp.start()

    def wait(self):
        for cp in self._remote(incoming=True):
            cp.wait_recv()
        for cp in self._remote(incoming=False):
            cp.wait_send()
        for cp in self._local():
            cp.wait()


ADAMW_BLOCK_ELEMS = 128 * 1024


def adamw_sum(name, parts, w, m, v):
    nl, r, c = w.shape
    fits = [t for t in range(16, r + 1, 16) if r % t == 0 and t * c <= ADAMW_BLOCK_ELEMS]
    tr = max(fits) if fits else r
    c1 = 1.0 - ADAM_B1 ** ADAM_STEP
    c2 = 1.0 - ADAM_B2 ** ADAM_STEP

    def body(*refs):
        p_refs = refs[:nl]
        w_ref, m_ref, v_ref, g_out, d_out, m_out, v_out = refs[nl:]
        for layer in range(nl):
            @pl.when(pl.program_id(0) == layer)
            def _():
                g = p_refs[layer][0].astype(F32)
                for d in range(1, N_DEV):
                    g = g + p_refs[layer][d].astype(F32)
                mn = ADAM_B1 * m_ref[...] + (1.0 - ADAM_B1) * g
                vn = ADAM_B2 * v_ref[...] + (1.0 - ADAM_B2) * (g * g)
                g_out[...] = g
                m_out[...] = mn
                v_out[...] = vn
                d_out[...] = -ADAM_LR * ((mn / c1) / (jnp.sqrt(vn / c2) + ADAM_EPS) + ADAM_WD * w_ref[...])

    row = pl.BlockSpec((None, tr, c), lambda l, i: (l, i, 0))
    return pl.pallas_call(
        body, name=name, grid=(nl, r // tr),
        in_specs=[pl.BlockSpec((N_DEV, tr, c), lambda l, i: (0, i, 0))] * nl + [row, row, row],
        out_specs=[row] * 4, out_shape=[jax.ShapeDtypeStruct((nl, r, c), F32)] * 4,
        compiler_params=_params(("parallel", "parallel")))(*parts, w, m, v)


def _pack(arrays, dtype, row_mult):
    flat = jnp.concatenate([a.astype(dtype).reshape(-1) for a in arrays])
    n = flat.shape[0]
    quantum = row_mult * LANE
    total = -(-n // quantum) * quantum
    return jnp.pad(flat, (0, total - n)).reshape(total // LANE, LANE)


def _unpack(packed, shapes):
    flat = packed.reshape(-1)
    out, pos = [], 0
    for shp in shapes:
        n = int(np.prod(shp))
        out.append(flat[pos:pos + n].reshape(shp))
        pos += n
    return out


def _rows(a):
    return a.reshape(-1, a.shape[-1])


def _unshard(gathered, shard_shape, axis):
    t = jnp.moveaxis(gathered.reshape((N_DEV,) + tuple(shard_shape)), 0, axis)
    return t.reshape(tuple(shard_shape[:axis]) + (N_DEV * shard_shape[axis],) + tuple(shard_shape[axis + 1:]))


def _to_shards(full, axis):
    shp = full.shape
    t = full.reshape(shp[:axis] + (N_DEV, shp[axis] // N_DEV) + shp[axis + 1:])
    t = jnp.moveaxis(t, axis, 0)
    return t.reshape(N_DEV, -1, t.shape[-1])


A_COLS = len(A_GROUPS) * 3 * HW
KV_COLS = B_KV_HEADS * HEAD_DIM
KV_REP = N_HEADS // B_KV_HEADS
O_BQ = A_COLS
O_BK = O_BQ + HW
O_BV = O_BK + KV_COLS
O_CQ = O_BV + KV_COLS
O_CKV = O_CQ + C_Q_RANK
O_KR = O_CKV + C_KV_RANK
O_GATE = O_KR + C_ROPE


def _proj_weight(w_in):
    d = w_in.shape[0]

    def per_q_head(lo):
        t = w_in[:, lo:lo + KV_COLS].reshape(d, B_KV_HEADS, 1, HEAD_DIM)
        return jnp.broadcast_to(t, (d, B_KV_HEADS, KV_REP, HEAD_DIM)).reshape(d, HW)

    kr = jnp.pad(w_in[:, O_KR:O_GATE], ((0, 0), (C_NOPE, LANE - C_NOPE - C_ROPE)))
    return jnp.concatenate([w_in[:, O_GATE:], w_in[:, :O_BK], per_q_head(O_BK), per_q_head(O_BV),
                            w_in[:, O_CQ:O_KR], kr], axis=1)


def _unproj_grad(dw_p, d_model):
    d = dw_p.shape[0]
    g = 3 * d_model
    p_bk = g + O_BK
    p_bv = p_bk + HW
    p_cq = p_bv + HW
    p_kr = p_cq + C_Q_RANK + C_KV_RANK

    def sum_heads(lo):
        return dw_p[:, lo:lo + HW].reshape(d, B_KV_HEADS, KV_REP, HEAD_DIM).sum(axis=2).reshape(d, KV_COLS)

    return jnp.concatenate([dw_p[:, g:p_bk], sum_heads(p_bk), sum_heads(p_bv), dw_p[:, p_cq:p_kr],
                            dw_p[:, p_kr + C_NOPE:p_kr + C_NOPE + C_ROPE], dw_p[:, :g]], axis=1)


def _pad_heads(w, per_head, lo, hi):
    r = w.shape[0]
    t = w.reshape(r, N_HEADS, per_head)[:, :, lo:hi]
    return jnp.pad(t, ((0, 0), (0, 0), (0, LANE - (hi - lo)))).reshape(r, N_HEADS * LANE)


def _unpad_heads(w, width):
    r = w.shape[0]
    return w.reshape(r, N_HEADS, LANE)[:, :, :width]


def _t5_bucket(dist):
    n = jnp.maximum(dist, 0)
    max_exact = REL_BUCKETS // 2
    scaled = jnp.log(jnp.maximum(n, 1).astype(F32) / max_exact) / math.log(REL_MAX_DIST / max_exact)
    large = max_exact + (scaled * (REL_BUCKETS - max_exact)).astype(jnp.int32)
    return jnp.where(n < max_exact, n, jnp.minimum(large, REL_BUCKETS - 1))


def _buckets(dil):
    qi = jnp.arange(BLOCK)[:, None]
    ki = jnp.arange(2 * BLOCK)[None, :]
    return _t5_bucket((BLOCK + qi - ki) * dil).astype(jnp.int32)


def _rope_tables(s):
    pos = jnp.arange(s, dtype=F32)
    inv_freq = ROPE_BASE ** (-jnp.arange(0, C_ROPE, 2, dtype=F32) / C_ROPE)
    ang = pos[:, None] * inv_freq[None, :]
    cos, sin = jnp.cos(ang), jnp.sin(ang)
    ones = jnp.ones((s, C_NOPE), F32)
    tail = LANE - C_NOPE - C_ROPE
    c = jnp.concatenate([ones, cos, cos, jnp.ones((s, tail), F32)], axis=1)
    sn = jnp.concatenate([jnp.zeros((s, C_NOPE), F32), sin, sin, jnp.zeros((s, tail), F32)], axis=1)
    return c, sn


ATTN_COLS = A_COLS + 3 * HW


def _cols(d_model):
    a = [(3 * i, 3 * i + 1, 3 * i + 2) for i in range(len(A_GROUPS))]
    b = (9, 10, 11)
    return a, b, 3 * d_model


def _layer_weights(full, d_model):
    lw = {}
    if 'w_in' in full:
        lw['w_in_p'] = _proj_weight(full['w_in'])
    if 'w_uq' in full:
        per = C_NOPE + C_ROPE
        w_uq, w_ukv, wb = full['w_uq'], full['w_ukv'], full['w_branch']
        wb2 = jnp.pad(wb[2].reshape(N_HEADS, HEAD_DIM, d_model), ((0, 0), (0, LANE - HEAD_DIM), (0, 0)))
        lw.update(
            wq=_pad_heads(w_uq, per, 0, per), wk=_pad_heads(w_ukv, 2 * C_NOPE, 0, C_NOPE),
            wv=_pad_heads(w_ukv, 2 * C_NOPE, C_NOPE, 2 * C_NOPE), wb0=wb[0], wb1=wb[1],
            wb2=wb2.reshape(N_HEADS * LANE, d_model), w_out=full['w_out'], w_up=full['w_ffn_up'],
            w_down=full['w_ffn_down'], conv_w=full['conv_w'])
    return lw


def _layer_fwd(l, x, x16, lw, small, tabs, comm=None, proj_comm=None):
    s, d = x.shape
    a_cols, b_cols, cq_lane = _cols(d)
    tag = f"l{l}_"
    g3 = 3 * d
    w_attn = lw['w_in_p'][:, g3:g3 + ATTN_COLS]
    w_rest = jnp.concatenate([lw['w_in_p'][:, :g3], lw['w_in_p'][:, g3 + ATTN_COLS:]], axis=1)
    if proj_comm is None:
        pa = matmul(tag + "proj_attn", x16, w_attn, 'nn', out_dtype=BF16)
    else:
        pa, got = matmul(tag + "proj_attn", x16, w_attn, 'nn', out_dtype=BF16, comm=proj_comm[0])
        lw = dict(lw, **_layer_weights(proj_comm[1](got), d))
    pr = matmul(tag + "proj_rest", x16, w_rest, 'nn')
    res = dict(x=x, x16=x16, pa=pa, pr=pr, lw=lw)
    outs, lses, slabs = [], [], []
    for gi, (window, dil) in enumerate(A_GROUPS):
        if dil == 1:
            slab, cols = pa, a_cols[gi]
        else:
            slab = to_classes(pa[:, a_cols[gi][0] * HW:(a_cols[gi][2] + 1) * HW], dil)
            cols = (0, 1, 2)
        o, lg = band_fwd(tag + f"a{gi}_fwd", slab, dil, cols, tabs['bias'][gi], None, True)
        outs.append(from_classes(o, dil))
        lses.append(from_classes(lg, dil))
        slabs.append((slab, cols))
    res['a_o'], res['a_l'], res['a_slabs'] = outs, lses, slabs
    (o_a,) = rows_fwd(tag + "a_combine", combine_fn, [(o, HW, 0) for o in outs] + [(lg, HW, 0) for lg in lses], [],
                      [(HW, BF16)])
    sink = small['sinks'][l].reshape(N_HEADS, 1, 1)
    o_b, _ = band_fwd(tag + "b_fwd", pa, 1, b_cols, tabs['bias'][3], sink, False)
    gq = small['q_norm_g'][l].reshape(1, -1)
    gkv = small['kv_norm_g'][l].reshape(1, -1)
    (qc,) = rows_fwd(tag + "c_q", mla_q_fn, [(pr, C_Q_RANK, cq_lane // C_Q_RANK), (tabs['cos'], LANE, 0),
                                            (tabs['sin'], LANE, 0)], [gq, lw['wq']], [(N_HEADS * LANE, BF16)])
    ckv_blk = (cq_lane + C_Q_RANK) // LANE
    kc, vc = rows_fwd(tag + "c_kv", mla_kv_fn, [(pr, LANE, ckv_blk), (pr, LANE, ckv_blk + 1), (tabs['cos'], LANE, 0),
                                                (tabs['sin'], LANE, 0)], [gkv, lw['wk'], lw['wv']],
                      [(N_HEADS * LANE, BF16)] * 2)
    o_c, lse_c, res['comm'] = mla_fwd(tag + "mla_fwd", qc, kc, vc, comm)
    res.update(o_a=o_a, o_b=o_b, o_c=o_c, lse_c=lse_c, qc=qc, kc=kc, vc=vc)
    bg = small['b_gate'][l].reshape(3, 1, d)
    ys = [matmul(tag + f"branch{i}", o, lw[f'wb{i}'], 'nn') for i, o in enumerate((o_a, o_b, o_c))]
    (merged,) = rows_fwd(tag + "merge", merge_fn, [(y, d, 0) for y in ys] + [(pr, d, i) for i in range(3)],
                         [bg[0], bg[1], bg[2]], [(d, BF16)])
    mix = matmul(tag + "out_proj", merged, lw['w_out'], 'nn')
    ln1 = [small['ln1_g'][l].reshape(1, d), small['ln1_b'][l].reshape(1, d)]
    x1, x1_16 = rows_fwd(tag + "ln1", ln_twice, [(x, d, 0), (mix, d, 0)], ln1, [(d, F32), (d, BF16)])
    u = matmul(tag + "ffn_up", x1_16, lw['w_up'], 'nn')
    h = conv_glu_fwd(u, lw['conv_w'], small['conv_b'][l].reshape(1, -1))
    ff = matmul(tag + "ffn_down", h, lw['w_down'], 'nn')
    ln2 = [small['ln2_g'][l].reshape(1, d), small['ln2_b'][l].reshape(1, d)]
    x2, x2_16 = rows_fwd(tag + "ln2", ln_twice, [(x1, d, 0), (ff, d, 0)], ln2, [(d, F32), (d, BF16)])
    res.update(merged=merged, mix=mix, x1=x1, x1_16=x1_16, u=u, h=h, ff=ff, ys=ys)
    return x2, x2_16, res


def _layer_bwd(l, dy, res, lw, small, tabs, comm=None, dw_comm=None):
    x, proj, pa = res['x'], res['pr'], res['pa']
    s, d = x.shape
    a_cols, b_cols, cq_lane = _cols(d)
    tag = f"l{l}_"
    g = {}
    ln2 = [small['ln2_g'][l].reshape(1, d), small['ln2_b'][l].reshape(1, d)]
    dff, g['ln2_g'], g['ln2_b'] = rows_bwd(tag + "ln2_bwd", ln_fn, [(res['x1'], d, 0), (res['ff'], d, 0)], ln2,
                                           [(dy, d)], [(1, F32)])
    g['w_ffn_down'] = matmul(tag + "dw_down", res['h'], dff, 'tn')
    dh = matmul(tag + "dh", dff, lw['w_down'], 'nt')
    conv_b = small['conv_b'][l].reshape(1, -1)
    dug, duv, dwg, dwv, dbg, dbv = conv_glu_bwd(res['u'], dh, lw['conv_w'], conv_b)
    du = jnp.concatenate([dug, duv], axis=1)
    g['conv_w'] = jnp.concatenate([dwg, dwv], axis=1)
    g['conv_b'] = jnp.concatenate([dbg, dbv], axis=1).reshape(-1)
    g['w_ffn_up'] = matmul(tag + "dw_up", res['x1_16'], du, 'tn')
    dx1 = matmul(tag + "dx1", du, lw['w_up'], 'nt', add=dff, scale=ALPHA)
    ln1 = [small['ln1_g'][l].reshape(1, d), small['ln1_b'][l].reshape(1, d)]
    dmix, g['ln1_g'], g['ln1_b'] = rows_bwd(tag + "ln1_bwd", ln_fn, [(x, d, 0), (res['mix'], d, 0)], ln1,
                                            [(dx1, d)], [(1, F32)])
    g['w_out'] = matmul(tag + "dw_out", res['merged'], dmix, 'tn')
    dmerged = matmul(tag + "dmerged", dmix, lw['w_out'], 'nt')
    bg = small['b_gate'][l].reshape(3, 1, d)
    mrows = [(y, d, 0) for y in res['ys']] + [(proj, d, i) for i in range(3)]
    (dy0, dy1, dy2, dg0, dg1, dg2, db0, db1, db2) = rows_bwd(
        tag + "merge_bwd", merge_fn, mrows, [bg[0], bg[1], bg[2]], [(dmerged, d)], [(i, BF16) for i in range(6)])
    g['b_gate'] = jnp.concatenate([db0, db1, db2], axis=1).reshape(-1)
    branch_in = (res['o_a'], res['o_b'], res['o_c'])
    dwb = [matmul(tag + f"dw_branch{i}", o, dyi, 'tn') for i, (o, dyi) in enumerate(zip(branch_in, (dy0, dy1, dy2)))]
    dwb[2] = dwb[2].reshape(N_HEADS, LANE, d)[:, :HEAD_DIM].reshape(HW, d)
    g['w_branch'] = jnp.stack(dwb, axis=0)
    do_a, do_b, do_c = [matmul(tag + f"do_branch{i}", dyi, lw[f'wb{i}'], 'nt')
                        for i, dyi in enumerate((dy0, dy1, dy2))]
    cw = N_HEADS * LANE
    delta, do_c16 = rows_fwd(tag + "c_delta", mla_delta_fn, [(do_c, cw, 0), (res['o_c'], cw, 0)], [],
                             [(cw, F32), (cw, BF16)])
    dqc, dkc, dvc, g['comm'] = mla_bwd(tag + "mla_bwd", res['qc'], res['kc'], res['vc'], do_c16, res['lse_c'], delta,
                                       comm)
    gq = small['q_norm_g'][l].reshape(1, -1)
    gkv = small['kv_norm_g'][l].reshape(1, -1)
    dcq, dgq, dwq = rows_bwd(tag + "c_q_bwd", mla_q_fn,
                             [(proj, C_Q_RANK, cq_lane // C_Q_RANK), (tabs['cos'], LANE, 0), (tabs['sin'], LANE, 0)],
                             [gq, lw['wq']], [(dqc, N_HEADS * LANE)], [(0, BF16)])
    ckv_blk = (cq_lane + C_Q_RANK) // LANE
    dckv, dkr, dgkv, dwk, dwv = rows_bwd(
        tag + "c_kv_bwd", mla_kv_fn,
        [(proj, LANE, ckv_blk), (proj, LANE, ckv_blk + 1), (tabs['cos'], LANE, 0), (tabs['sin'], LANE, 0)],
        [gkv, lw['wk'], lw['wv']], [(dkc, N_HEADS * LANE), (dvc, N_HEADS * LANE)], [(0, BF16), (1, BF16)])
    g['q_norm_g'], g['kv_norm_g'] = dgq.reshape(-1), dgkv.reshape(-1)
    per = C_NOPE + C_ROPE
    g['w_uq'] = _unpad_heads(dwq, per).reshape(C_Q_RANK, N_HEADS * per)
    g['w_ukv'] = jnp.concatenate([_unpad_heads(dwk, C_NOPE), _unpad_heads(dwv, C_NOPE)], axis=2).reshape(
        C_KV_RANK, N_HEADS * 2 * C_NOPE)
    sink = small['sinks'][l].reshape(N_HEADS, 1, 1)
    piece_b, dbias_b, dsink = band_bwd(tag + "b_bwd", pa, 1, b_cols, tabs['bias'][3], sink, False, do_b, None)
    pieces_b = [piece_b]
    g['sinks'] = dsink.reshape(-1)
    dbias = [None] * 4
    dbias[3] = dbias_b
    combo = [(o, HW, 0) for o in res['a_o']] + [(lg, HW, 0) for lg in res['a_l']]
    a_cts = rows_bwd(tag + "a_combine_bwd", combine_fn, combo, [], [(do_a, HW)],
                     [(i, BF16) for i in range(3)] + [(i, F32) for i in range(3, 6)])
    pieces_a = []
    for gi, (window, dil) in enumerate(A_GROUPS):
        slab, cols = res['a_slabs'][gi]
        group, dbias[gi], _ = band_bwd(
            tag + f"a{gi}_bwd", slab, dil, cols, tabs['bias'][gi], None, True, to_classes(a_cts[gi], dil),
            to_classes(a_cts[3 + gi], dil))
        pieces_a.append(from_classes(group, dil))
    g['dbias'] = dbias
    dproj = jnp.concatenate([dg0, dg1, dg2] + pieces_a + pieces_b + [dcq, dckv, dkr], axis=1)
    if dw_comm is None:
        dw_in_p = matmul(tag + "dw_in", res['x16'], dproj, 'tn')
    else:
        dw_in_p, g['dw_comm'] = matmul(tag + "dw_in", res['x16'], dproj, 'tn', comm=dw_comm(g))
    g['w_in'] = _unproj_grad(dw_in_p, d)
    dx = matmul(tag + "dx", dproj, lw['w_in_p'], 'nt', add=dmix, scale=ALPHA)
    return dx, g


def kernel(x, rel_table, w_in, b_gate, sinks, q_norm_g, kv_norm_g, w_uq, w_ukv, w_branch, w_out, ln1_g, ln1_b, w_ffn_up, conv_w, conv_b, w_ffn_down, ln2_g, ln2_b, loss_target, m_rel_table, m_w_in, m_b_gate, m_sinks, m_q_norm_g, m_kv_norm_g, m_w_uq, m_w_ukv, m_w_branch, m_w_out, m_ln1_g, m_ln1_b, m_w_ffn_up, m_conv_w, m_conv_b, m_w_ffn_down, m_ln2_g, m_ln2_b, v_rel_table, v_w_in, v_b_gate, v_sinks, v_q_norm_g, v_kv_norm_g, v_w_uq, v_w_ukv, v_w_branch, v_w_out, v_ln1_g, v_ln1_b, v_w_ffn_up, v_conv_w, v_conv_b, v_w_ffn_down, v_ln2_g, v_ln2_b):
    args = locals()
    w = {n: args[n] for n in WEIGHTS}
    mom = {n: args["m_" + n] for n in WEIGHTS}
    var = {n: args["v_" + n] for n in WEIGHTS}
    s, d = x.shape[1], x.shape[2]
    xs = x.reshape(s, d)
    target = loss_target.reshape(s, d)

    big = [n for n in WEIGHTS if n in SHARDED]
    small_names = [n for n in WEIGHTS if n not in SHARDED]
    assert DEPTH == 2
    wire = lambda n: BF16 if n in GATHER_BF16 else F32
    rest = [n for n in big if n != 'w_in']

    def shards(l, names):
        return [_rows(w[n][l]).astype(wire(n)) for n in names]

    def to_full(names, gathered):
        return {n: _unshard(t, w[n].shape[1:], SHARDED[n] - 1) for n, t in zip(names, gathered)}

    def grad_shards(names, g):
        return [_to_shards(g[n], SHARDED[n] - 1).astype(wire(n)) for n in names]

    w_in0 = to_full(['w_in'], exchange("gather_w_in0", shards(0, ['w_in']), scatter=False))
    fused = dict(proj0=((shards(0, rest), False), lambda got: to_full(rest, got)),
                 fwd0=((shards(1, big), False), lambda got: to_full(big, got)),
                 bwd0=lambda g1: (grad_shards(big, g1), True),
                 dw0=lambda g0: (grad_shards(rest, g0), True))
    small = {n: w[n] for n in small_names}
    loss_part, grad_x, grads, g_rel, recv1, recv0_rest = _local_grads(xs, target, [w_in0, None], small, fused)
    grad_x = grad_x.reshape(x.shape)
    recv0 = dict(zip(rest, recv0_rest))
    (recv0['w_in'],) = exchange("exchange_dw_in0", grad_shards(['w_in'], grads[0]), scatter=True)

    big_res = [{}, {}, {}, {}]
    per_layer = lambda t: t.reshape((DEPTH, -1, t.shape[-1]))
    for i, n in enumerate(big):
        outs = adamw_sum("adamw_" + n, [recv0[n], recv1[i]], per_layer(w[n]), per_layer(mom[n]), per_layer(var[n]))
        for k in range(4):
            big_res[k][n] = outs[k].reshape(w[n].shape)

    g_small = {'rel_table': g_rel}
    for n in small_names:
        if n != 'rel_table':
            g_small[n] = jnp.stack([grads[l][n] for l in range(DEPTH)], axis=0)
    small_pack = _pack([g_small[n].reshape(w[n].shape) for n in small_names] + [loss_part[0, 0:1]], F32, 8)
    (small_all,) = exchange("gather_small_grads", [small_pack], scatter=False)
    pks = lambda tree: _pack([tree[n] for n in small_names] + [jnp.zeros((1,), F32)], F32, 8)[None]
    small_out = adamw_sum("adamw_replicated", [small_all], pks(w), pks(mom), pks(var))
    small_shapes = [w[n].shape for n in small_names] + [(1,)]
    small_res = [dict(zip(small_names + ['loss'], _unpack(o, small_shapes))) for o in small_out]

    loss = small_res[0]['loss'].reshape(())
    out = [loss, grad_x]
    for k in range(4):
        out += [big_res[k][n] if n in SHARDED else small_res[k][n] for n in WEIGHTS]
    return tuple(out)


def _local_grads(xs, target, fulls, small, fused=None):
    s, d = xs.shape
    rel_table = small['rel_table']
    fulls = list(fulls)

    cos, sin = _rope_tables(s)
    buckets = [_buckets(dil) for _, dil in A_GROUPS] + [_buckets(1)]
    def lookup(table, bucket):
        out = jnp.zeros((N_HEADS,) + bucket.shape, F32)
        for b in range(REL_BUCKETS):
            out = jnp.where((bucket == b)[None], table[b][:, None, None], out)
        return out

    bias = [lookup(rel_table[:, gi * N_HEADS:(gi + 1) * N_HEADS], buckets[gi]) for gi in range(4)]
    tabs = dict(cos=cos, sin=sin, bias=bias)

    act, act16, saved = xs, xs.astype(BF16), []
    for l in range(DEPTH):
        lw = _layer_weights(fulls[l], d)
        ride = fused is not None and l == 0
        act, act16, res = _layer_fwd(l, act, act16, lw, small, tabs, fused['fwd0'][0] if ride else None,
                                     fused['proj0'] if ride else None)
        if ride:
            fulls[1] = fused['fwd0'][1](res['comm'])
        saved.append(res)
    dy, loss_part = loss_head(act, target)

    grads, recv1, recv0_rest = [None] * DEPTH, None, None
    for l in reversed(range(DEPTH)):
        ride = fused is not None and l == 0
        dy, grads[l] = _layer_bwd(l, dy, saved[l], saved[l]['lw'], small, tabs,
                                  fused['bwd0'](grads[1]) if ride else None, fused['dw0'] if ride else None)
        if ride:
            recv1, recv0_rest = grads[l]['comm'], grads[l]['dw_comm']
    grad_x = dy

    rel_cols = []
    for gi in range(4):
        both = grads[0]['dbias'][gi] + grads[1]['dbias'][gi] if DEPTH == 2 else grads[0]['dbias'][gi]
        rel_cols.append(rel_grad(f"rel_grad{gi}", both, buckets[gi])[:, :REL_BUCKETS].T)
    g_rel = jnp.concatenate(rel_cols, axis=1)
    return loss_part, grad_x, grads, g_rel, recv1, recv0_rest
```

```python
import functools
import math

import numpy as np
import jax
import jax.numpy as jnp
from jax import lax
from jax.experimental import pallas as pl
from jax.experimental.pallas import tpu as pltpu

F32 = jnp.float32
BF16 = jnp.bfloat16

N_DEV = 8
DEPTH = 2
HEAD_DIM = 64
BLOCK = 128
A_GROUPS = ((128, 1), (512, 4), (2048, 16))
N_HEADS = 8
HW = N_HEADS * HEAD_DIM
B_KV_HEADS = 2
C_Q_RANK = 256
C_KV_RANK = 128
C_NOPE = 64
C_ROPE = 32
ROPE_BASE = 10000.0
REL_BUCKETS = 32
REL_MAX_DIST = 2048
ALPHA = (2 * DEPTH) ** 0.25
LN_EPS = 1e-5
RMS_EPS = 1e-6
NEG = -1e30
ADAM_LR = 0.001
ADAM_B1 = 0.9
ADAM_B2 = 0.999
ADAM_EPS = 1e-08
ADAM_WD = 0.01
ADAM_STEP = 10

LANE = 128
TM = 256
VMEM_LIMIT = 56 * 1024 * 1024
MESH_ID = pl.DeviceIdType.MESH

WEIGHTS = ('rel_table', 'w_in', 'b_gate', 'sinks', 'q_norm_g', 'kv_norm_g', 'w_uq', 'w_ukv', 'w_branch',
           'w_out', 'ln1_g', 'ln1_b', 'w_ffn_up', 'conv_w', 'conv_b', 'w_ffn_down', 'ln2_g', 'ln2_b')
SHARDED = {'w_in': 2, 'w_uq': 2, 'w_ukv': 2, 'w_branch': 3, 'w_out': 1, 'w_ffn_up': 2, 'conv_w': 2,
           'w_ffn_down': 1}
GATHER_BF16 = ('w_in', 'w_uq', 'w_ukv', 'w_branch', 'w_out', 'w_ffn_up', 'w_ffn_down')


def _params(sem=None):
    return pltpu.CompilerParams(dimension_semantics=sem, vmem_limit_bytes=VMEM_LIMIT)


def _pick(n, target):
    if n <= target:
        return n
    best = None
    for t in range(LANE, target + 1, LANE):
        if n % t == 0:
            best = t
    assert best is not None, (n, target)
    return best


def _dot(a, b, ca, cb):
    return lax.dot_general(a.astype(BF16), b.astype(BF16), (((ca,), (cb,)), ((), ())),
                           preferred_element_type=F32)


@jax.custom_vjp
def mm(a, b):
    return _dot(a, b, 1, 0)


def _mm_fwd(a, b):
    return _dot(a, b, 1, 0), (a, b)


def _mm_bwd(res, g):
    a, b = res
    return _dot(g, b, 1, 1), _dot(a, g, 0, 0)


mm.defvjp(_mm_fwd, _mm_bwd)


@jax.custom_vjp
def mm_nt(a, b):
    return _dot(a, b, 1, 1)


def _mm_nt_fwd(a, b):
    return _dot(a, b, 1, 1), (a, b)


def _mm_nt_bwd(res, g):
    a, b = res
    return _dot(g, b, 1, 0), _dot(g, a, 0, 0)


mm_nt.defvjp(_mm_nt_fwd, _mm_nt_bwd)


def _split_impl(x, n):
    w = x.shape[-1] // n
    return tuple(x[:, i * w:(i + 1) * w] for i in range(n))


@functools.partial(jax.custom_vjp, nondiff_argnums=(1,))
def split_lanes(x, n):
    return _split_impl(x, n)


def _split_fwd(x, n):
    return _split_impl(x, n), None


def _split_bwd(n, _, gs):
    return (jnp.concatenate(gs, axis=-1),)


split_lanes.defvjp(_split_fwd, _split_bwd)


@jax.custom_vjp
def concat_lanes(xs):
    return jnp.concatenate(xs, axis=-1)


def _concat_fwd(xs):
    return jnp.concatenate(xs, axis=-1), len(xs)


def _concat_bwd(n, g):
    return (_split_impl(g, n),)


concat_lanes.defvjp(_concat_fwd, _concat_bwd)


@jax.custom_vjp
def concat_rows(a, b):
    return jnp.concatenate([a, b], axis=0)


def _crow_fwd(a, b):
    return jnp.concatenate([a, b], axis=0), a.shape[0]


def _crow_bwd(na, g):
    return g[:na], g[na:]


concat_rows.defvjp(_crow_fwd, _crow_bwd)


def _rot_half(x):
    n = x.shape[-1]
    lane = lax.broadcasted_iota(jnp.int32, (1, n), 1) % LANE
    lo = (lane >= C_NOPE) & (lane < C_NOPE + C_ROPE // 2)
    hi = (lane >= C_NOPE + C_ROPE // 2) & (lane < C_NOPE + C_ROPE)
    up = pltpu.roll(x, n - C_ROPE // 2, 1)
    dn = pltpu.roll(x, C_ROPE // 2, 1)
    return jnp.where(lo, -up, jnp.where(hi, dn, 0.0))


@jax.custom_vjp
def rope(x, c, s):
    return x * c + _rot_half(x) * s


def _rope_fwd(x, c, s):
    return x * c + _rot_half(x) * s, (c, s)


def _rope_bwd(res, g):
    c, s = res
    return g * c - _rot_half(g * s), jnp.zeros_like(c), jnp.zeros_like(s)


rope.defvjp(_rope_fwd, _rope_bwd)


def _sigmoid(x):
    return 0.5 * jnp.tanh(0.5 * x) + 0.5


MATMUL_VMEM = 44 * 1024 * 1024


def _matmul_tiles(m, n, k, a_bytes, b_bytes, o_bytes, has_add):
    tm, tn, tk = _pick(m, 1536), _pick(n, 2560), _pick(k, 2560)

    def need(tn_, tk_):
        acc = tm * tn_ * 4 if k // tk_ > 1 else 0
        return 2 * (tm * tk_ * a_bytes + tk_ * tn_ * b_bytes + tm * tn_ * o_bytes) + acc + 2 * tm * tn_ * 4 * int(has_add)

    while need(tn, tk) > MATMUL_VMEM:
        if tk >= tn and tk > LANE:
            tk = _pick(k, tk - LANE)
        else:
            tn = _pick(n, tn - LANE)
    return tm, tn, tk


def matmul(name, a, b, mode, out_dtype=F32, add=None, scale=1.0, comm=None):
    if mode == 'nn':
        (m, k), n = a.shape, b.shape[1]
    elif mode == 'nt':
        (m, k), n = a.shape, b.shape[0]
    else:
        (k, m), n = a.shape, b.shape[1]
    tm, tn, tk = _matmul_tiles(m, n, k, a.dtype.itemsize, b.dtype.itemsize, jnp.dtype(out_dtype).itemsize,
                               add is not None)
    nk = k // tk
    grid = (m // tm, n // tn, nk)
    c_in, c_specs, c_shapes, c_sems, c_hook = _ride_along(comm, grid)
    nc = len(c_in)
    a_spec = pl.BlockSpec((tk, tm), lambda i, j, kk: (kk, i)) if mode == 'tn' else \
        pl.BlockSpec((tm, tk), lambda i, j, kk: (i, kk))
    b_spec = pl.BlockSpec((tn, tk), lambda i, j, kk: (j, kk)) if mode == 'nt' else \
        pl.BlockSpec((tk, tn), lambda i, j, kk: (kk, j))
    o_spec = pl.BlockSpec((tm, tn), lambda i, j, kk: (i, j))
    ca = 0 if mode == 'tn' else 1
    cb = 1 if mode == 'nt' else 0
    has_add = add is not None

    n_in = 2 + int(has_add)

    def body(*refs):
        a_ref, b_ref = refs[:2]
        add_ref = refs[2] if has_add else None
        o_ref = refs[n_in + nc]
        c_hook(refs[n_in:n_in + nc], refs[n_in + nc + 1:n_in + 2 * nc + 1], refs[len(refs) - 3:])

        def finish(r):
            if has_add:
                r = r + scale * add_ref[...]
            o_ref[...] = r.astype(out_dtype)

        if nk == 1:
            finish(_dot(a_ref[...], b_ref[...], ca, cb))
            return
        acc_ref = refs[n_in + 2 * nc + 1]
        kk = pl.program_id(2)

        @pl.when(kk == 0)
        def _():
            acc_ref[...] = jnp.zeros_like(acc_ref)

        acc_ref[...] += _dot(a_ref[...], b_ref[...], ca, cb)

        @pl.when(kk == nk - 1)
        def _():
            finish(acc_ref[...])

    ins = [a, b] + ([add] if has_add else []) + c_in
    specs = [a_spec, b_spec] + ([o_spec] if has_add else []) + c_specs
    res = pl.pallas_call(
        body, name=name, grid=grid, in_specs=specs, out_specs=[o_spec] + c_specs,
        out_shape=[jax.ShapeDtypeStruct((m, n), out_dtype)] + c_shapes,
        scratch_shapes=([pltpu.VMEM((tm, tn), F32)] if nk > 1 else []) + c_sems,
        compiler_params=_params(("arbitrary",) * 3 if comm is not None else ("parallel", "parallel", "arbitrary")))(*ins)
    return res[0] if comm is None else (res[0], list(res[1:]))


def _row_spec(width, col_block=0, tm=TM):
    return pl.BlockSpec((tm, width), lambda i: (i, col_block))


def _full_spec(shape):
    nd = len(shape)
    return pl.BlockSpec(shape, lambda i: (0,) * nd)


def rows_fwd(name, fn, rows, params, outs, tm=TM):
    s = rows[0][0].shape[0]
    nr, npar = len(rows), len(params)

    def body(*refs):
        r = [refs[i][...].astype(F32) for i in range(nr)]
        p = [refs[nr + i][...] for i in range(npar)]
        res = fn(*r, *p)
        for o_ref, val in zip(refs[nr + npar:], res):
            o_ref[...] = val.astype(o_ref.dtype)

    return pl.pallas_call(
        body, name=name, grid=(s // tm,),
        in_specs=[_row_spec(w, cb, tm) for _, w, cb in rows] + [_full_spec(p.shape) for p in params],
        out_specs=[_row_spec(w, 0, tm) for w, _ in outs],
        out_shape=[jax.ShapeDtypeStruct((s, w), dt) for w, dt in outs],
        compiler_params=_params(("parallel",)))(*[a for a, _, _ in rows], *params)


def rows_bwd(name, fn, rows, params, cts, row_grads, tm=TM):
    s = rows[0][0].shape[0]
    nr, npar, nct, nrg = len(rows), len(params), len(cts), len(row_grads)

    def body(*refs):
        r = [refs[i][...].astype(F32) for i in range(nr)]
        p = [refs[nr + i][...].astype(F32) for i in range(npar)]
        g = tuple(refs[nr + npar + i][...].astype(F32) for i in range(nct))
        _, vjp = jax.vjp(lambda *a: tuple(fn(*a)), *r, *p)
        grads = vjp(g)
        outs = refs[nr + npar + nct:]
        for k, (idx, _) in enumerate(row_grads):
            outs[k][...] = grads[idx].astype(outs[k].dtype)

        @pl.when(pl.program_id(0) == 0)
        def _():
            for k in range(npar):
                outs[nrg + k][...] = jnp.zeros_like(outs[nrg + k])

        for k in range(npar):
            outs[nrg + k][...] += grads[nr + k]

    return pl.pallas_call(
        body, name=name, grid=(s // tm,),
        in_specs=[_row_spec(w, cb, tm) for _, w, cb in rows] + [_full_spec(p.shape) for p in params]
        + [_row_spec(w, 0, tm) for _, w in cts],
        out_specs=[_row_spec(rows[idx][1], 0, tm) for idx, _ in row_grads] + [_full_spec(p.shape) for p in params],
        out_shape=[jax.ShapeDtypeStruct((s, rows[idx][1]), dt) for idx, dt in row_grads]
        + [jax.ShapeDtypeStruct(p.shape, F32) for p in params],
        compiler_params=_params(("arbitrary",)))(*[a for a, _, _ in rows], *params, *[a for a, _ in cts])


def ln_fn(x, r, g, b):
    z = ALPHA * x + r
    mu = jnp.mean(z, axis=-1, keepdims=True)
    zc = z - mu
    var = jnp.mean(zc * zc, axis=-1, keepdims=True)
    return (zc * lax.rsqrt(var + LN_EPS) * g + b,)


def ln_twice(x, r, g, b):
    (y,) = ln_fn(x, r, g, b)
    return y, y


def _rms(x, g):
    return x * lax.rsqrt(jnp.mean(x * x, axis=-1, keepdims=True) + RMS_EPS) * g


def mla_q_fn(cq, c, s, g, wq):
    q = mm(_rms(cq, g), wq)
    return (rope(q, jnp.concatenate([c] * N_HEADS, axis=-1), jnp.concatenate([s] * N_HEADS, axis=-1)) * MLA_SCALE,)


@jax.custom_vjp
def tile_heads(x):
    return jnp.concatenate([x] * N_HEADS, axis=-1)


def _tile_fwd(x):
    return jnp.concatenate([x] * N_HEADS, axis=-1), None


def _tile_bwd(_, g):
    parts = _split_impl(g, N_HEADS)
    acc = parts[0]
    for p in parts[1:]:
        acc = acc + p
    return (acc,)


tile_heads.defvjp(_tile_fwd, _tile_bwd)


def mla_kv_fn(ckv, kr, c, s, g, wk, wv):
    n = _rms(ckv, g)
    lane = lax.broadcasted_iota(jnp.int32, (1, N_HEADS * LANE), 1) % LANE
    one = (lane == MLA_ONE).astype(F32)
    return mm(n, wk) + tile_heads(rope(kr, c, s)), mm(n, wv) + one


def merge_fn(y0, y1, y2, g0, g1, g2, b0, b1, b2):
    return (_sigmoid(g0 + b0) * y0 + _sigmoid(g1 + b1) * y1 + _sigmoid(g2 + b2) * y2,)


def combine_fn(o1, o2, o3, l1, l2, l3):
    mx = lax.stop_gradient(jnp.maximum(jnp.maximum(l1, l2), l3))
    e1, e2, e3 = jnp.exp(l1 - mx), jnp.exp(l2 - mx), jnp.exp(l3 - mx)
    return ((e1 * o1 + e2 * o2 + e3 * o3) / (e1 + e2 + e3),)


def loss_head(y, t):
    s, d = y.shape

    def body(y_ref, t_ref, dy_ref, loss_ref):
        err = y_ref[...] - t_ref[...]
        dy_ref[...] = err * (1.0 / d)

        @pl.when(pl.program_id(0) == 0)
        def _():
            loss_ref[...] = jnp.zeros_like(loss_ref)

        loss_ref[...] += jnp.sum(err * err) * (0.5 / d)

    return pl.pallas_call(
        body, name="loss_head", grid=(s // TM,), in_specs=[_row_spec(d), _row_spec(d)],
        out_specs=[_row_spec(d), _full_spec((8, LANE))],
        out_shape=[jax.ShapeDtypeStruct((s, d), F32), jax.ShapeDtypeStruct((8, LANE), F32)],
        compiler_params=_params(("arbitrary",)))(y, t)


CONV_TN = 256
CONV_TM = 512


def _conv_taps(u_ext, w_ref, b_ref):
    d1, d2 = pltpu.roll(u_ext, 1, 0), pltpu.roll(u_ext, 2, 0)
    return w_ref[0:1, :] * d2 + w_ref[1:2, :] * d1 + w_ref[2:3, :] * u_ext + b_ref[...], d1, d2


def conv_glu_fwd(u, conv_w, conv_b):
    s, f2 = u.shape
    f = f2 // 2
    tn, tm = CONV_TN, min(CONV_TM, s)
    off = f // tn
    hb = tm // 8

    def body(ug, uv, hg, hv, wg, wv, bg, bv, o_ref):
        first = pl.program_id(0) == 0

        def conv(u_ref, h_ref, w_ref, b_ref):
            halo = jnp.where(first, 0.0, h_ref[...])
            ext = jnp.concatenate([halo, u_ref[...]], axis=0)
            return _conv_taps(ext, w_ref, b_ref)[0][8:]

        cg, cv = conv(ug, hg, wg, bg), conv(uv, hv, wv, bv)
        o_ref[...] = (cg * _sigmoid(cg) * cv).astype(o_ref.dtype)

    blk = lambda o: pl.BlockSpec((tm, tn), lambda i, j: (i, j + o))
    halo = lambda o: pl.BlockSpec((8, tn), lambda i, j: (jnp.maximum(i * hb - 1, 0), j + o))
    par = lambda r, o: pl.BlockSpec((r, tn), lambda i, j: (0, j + o))
    return pl.pallas_call(
        body, name="conv_glu_fwd", grid=(s // tm, off),
        in_specs=[blk(0), blk(off), halo(0), halo(off), par(3, 0), par(3, off), par(1, 0), par(1, off)],
        out_specs=pl.BlockSpec((tm, tn), lambda i, j: (i, j)),
        out_shape=jax.ShapeDtypeStruct((s, f), BF16),
        compiler_params=_params(("parallel", "parallel")))(u, u, u, u, conv_w, conv_w, conv_b, conv_b)


def conv_glu_bwd(u, dh, conv_w, conv_b):
    s, f2 = u.shape
    f = f2 // 2
    tn, tm = CONV_TN, min(CONV_TM, s)
    off = f // tn
    hb = tm // 8
    n_rows = s // tm

    def body(ug, uv, pg, pv, ng, nv, dh_ref, dhn_ref, wg, wv, bg, bv, dug, duv, dwg, dwv, dbg, dbv):
        i = pl.program_id(1)
        first, last = i == 0, i == n_rows - 1

        def ext(u_ref, p_ref, n_ref):
            return jnp.concatenate([jnp.where(first, 0.0, p_ref[...]), u_ref[...], n_ref[...]], axis=0)

        eg, ev = ext(ug, pg, ng), ext(uv, pv, nv)
        (cg, eg1, eg2), (cv, ev1, ev2) = _conv_taps(eg, wg, bg), _conv_taps(ev, wv, bv)
        dhe = jnp.concatenate([jnp.zeros((8, tn), F32), dh_ref[...], jnp.where(last, 0.0, dhn_ref[...])], axis=0)
        sg = _sigmoid(cg)
        dcg = dhe * cv * (sg * (1.0 + cg * (1.0 - sg)))
        dcv = dhe * (cg * sg)
        n = tm + 16

        @pl.when(i == 0)
        def _():
            for r in (dwg, dwv, dbg, dbv):
                r[...] = jnp.zeros_like(r)

        def back(dc, e, e1, e2, w_ref, du_ref, dw_ref, db_ref):
            du = w_ref[2:3, :] * dc + w_ref[1:2, :] * pltpu.roll(dc, n - 1, 0) + w_ref[0:1, :] * pltpu.roll(dc, n - 2, 0)
            du_ref[...] = du[8:8 + tm].astype(du_ref.dtype)
            own = dc[8:8 + tm]
            dw_ref[0:1, :] += jnp.sum(own * e2[8:8 + tm], axis=0, keepdims=True)
            dw_ref[1:2, :] += jnp.sum(own * e1[8:8 + tm], axis=0, keepdims=True)
            dw_ref[2:3, :] += jnp.sum(own * e[8:8 + tm], axis=0, keepdims=True)
            db_ref[...] += jnp.sum(own, axis=0, keepdims=True)

        back(dcg, eg, eg1, eg2, wg, dug, dwg, dbg)
        back(dcv, ev, ev1, ev2, wv, duv, dwv, dbv)

    blk = lambda o: pl.BlockSpec((tm, tn), lambda j, i: (i, j + o))
    prev = lambda o: pl.BlockSpec((8, tn), lambda j, i: (jnp.maximum(i * hb - 1, 0), j + o))
    nxt = lambda o: pl.BlockSpec((8, tn), lambda j, i: (jnp.minimum((i + 1) * hb, s // 8 - 1), j + o))
    par = lambda r, o: pl.BlockSpec((r, tn), lambda j, i: (0, j + o))
    return pl.pallas_call(
        body, name="conv_glu_bwd", grid=(off, n_rows),
        in_specs=[blk(0), blk(off), prev(0), prev(off), nxt(0), nxt(off), blk(0), nxt(0),
                  par(3, 0), par(3, off), par(1, 0), par(1, off)],
        out_specs=[blk(0), blk(0), par(3, 0), par(3, 0), par(1, 0), par(1, 0)],
        out_shape=[jax.ShapeDtypeStruct((s, f), BF16)] * 2 + [jax.ShapeDtypeStruct((3, f), F32)] * 2
        + [jax.ShapeDtypeStruct((1, f), F32)] * 2,
        compiler_params=_params(("parallel", "arbitrary")))(u, u, u, u, u, u, dh, dh, conv_w, conv_w, conv_b, conv_b)


def _band_fn(q, kp, kc, vp, vc, bias, sink, first, inclusive):
    k = concat_rows(kp, kc)
    v = concat_rows(vp, vc)
    qi = lax.broadcasted_iota(jnp.int32, (BLOCK, 2 * BLOCK), 0)
    ki = lax.broadcasted_iota(jnp.int32, (BLOCK, 2 * BLOCK), 1)
    step = BLOCK + qi - ki
    valid = (step >= 0) & ((step <= BLOCK) if inclusive else (step < BLOCK))
    valid = valid & (jnp.logical_not(first) | (ki >= BLOCK))
    lane = lax.broadcasted_iota(jnp.int32, (1, LANE), 1)
    scale = HEAD_DIM ** -0.5
    qs, ks, vs = split_lanes(q, 4), split_lanes(k, 4), split_lanes(v, 4)
    o_parts, l_parts = [], []
    for pair in range(4):
        o_acc, l_acc = None, None
        for sub in range(2):
            h = 2 * pair + sub
            mh = ((lane >= sub * HEAD_DIM) & (lane < (sub + 1) * HEAD_DIM)).astype(F32)
            logits = mm_nt(qs[pair] * mh, ks[pair]) * scale + bias[h]
            logits = jnp.where(valid, logits, NEG)
            m = lax.stop_gradient(jnp.max(logits, axis=-1, keepdims=True))
            if sink is not None:
                m = jnp.maximum(m, lax.stop_gradient(sink[h]))
            p = jnp.exp(logits - m)
            den = jnp.sum(p, axis=-1, keepdims=True)
            if sink is not None:
                den = den + jnp.exp(sink[h] - m)
            oh = mm(p * (1.0 / den), vs[pair] * mh)
            lh = (m + jnp.log(den)) * mh
            o_acc = oh if o_acc is None else o_acc + oh
            l_acc = lh if l_acc is None else l_acc + lh
        o_parts.append(o_acc)
        l_parts.append(l_acc)
    return concat_lanes(tuple(o_parts)), concat_lanes(tuple(l_parts))


def _band_specs(nb, cq, ck, cv):
    own = lambda n: jnp.minimum(n, nb - 1)
    prev = lambda n: jnp.maximum(own(n) - 1, 0)
    q = pl.BlockSpec((BLOCK, HW), lambda r, n: (r * nb + own(n), cq))
    kp = pl.BlockSpec((BLOCK, HW), lambda r, n: (r * nb + prev(n), ck))
    kc = pl.BlockSpec((BLOCK, HW), lambda r, n: (r * nb + own(n), ck))
    vp = pl.BlockSpec((BLOCK, HW), lambda r, n: (r * nb + prev(n), cv))
    vc = pl.BlockSpec((BLOCK, HW), lambda r, n: (r * nb + own(n), cv))
    return [q, kp, kc, vp, vc]


def to_classes(a, dil):
    if dil == 1:
        return a
    s, c = a.shape
    return a.reshape(s // dil, dil, c).transpose(1, 0, 2).reshape(s, c)


def from_classes(a, dil):
    if dil == 1:
        return a
    s, c = a.shape
    return a.reshape(dil, s // dil, c).transpose(1, 0, 2).reshape(s, c)


def _const_spec(shape):
    nd = len(shape)
    return pl.BlockSpec(shape, lambda r, n: (0,) * nd)


def band_fwd(name, qkv, dil, cols, bias, sink, inclusive):
    s = qkv.shape[0]
    nb = s // (BLOCK * dil)
    has_sink = sink is not None

    def body(*refs):
        q, kp, kc, vp, vc, b_ref = refs[:6]
        s_ref = refs[6] if has_sink else None
        o_ref, l_ref = refs[-2:]
        first = pl.program_id(1) == 0
        bias_l = tuple(b_ref[h] for h in range(N_HEADS))
        sink_l = tuple(s_ref[h] for h in range(N_HEADS)) if has_sink else None
        o, l = _band_fn(q[...], kp[...], kc[...], vp[...], vc[...], bias_l, sink_l, first, inclusive)
        o_ref[...] = o
        l_ref[...] = l

    out_spec = pl.BlockSpec((BLOCK, HW), lambda r, n: (r * nb + n, 0))
    ins = [qkv] * 5 + [bias] + ([sink] if has_sink else [])
    specs = _band_specs(nb, *cols) + [_const_spec(bias.shape)] + ([_const_spec(sink.shape)] if has_sink else [])
    return pl.pallas_call(
        body, name=name, grid=(dil, nb), in_specs=specs, out_specs=[out_spec, out_spec],
        out_shape=[jax.ShapeDtypeStruct((s, HW), F32)] * 2,
        compiler_params=_params(("parallel", "parallel")))(*ins)


def band_bwd(name, qkv, dil, cols, bias, sink, inclusive, do, dl):
    s = qkv.shape[0]
    nb = s // (BLOCK * dil)
    has_sink, has_dl = sink is not None, dl is not None
    n_in = 6 + int(has_sink) + 1 + int(has_dl)

    def body(*refs):
        q, kp, kc, vp, vc, b_ref = refs[:6]
        s_ref = refs[6] if has_sink else None
        do_ref = refs[6 + int(has_sink)]
        dl_ref = refs[7 + int(has_sink)] if has_dl else None
        out_ref, db_ref = refs[n_in:n_in + 2]
        ds_ref = refs[n_in + 2] if has_sink else None
        cq, ck, cv = refs[-3:]
        n = pl.program_id(1)

        def emit(dq, dk, dv):
            out_ref[:, 0:HW] = dq.astype(out_ref.dtype)
            out_ref[:, HW:2 * HW] = dk.astype(out_ref.dtype)
            out_ref[:, 2 * HW:3 * HW] = dv.astype(out_ref.dtype)

        @pl.when((pl.program_id(0) == 0) & (n == 0))
        def _():
            db_ref[...] = jnp.zeros_like(db_ref)
            if has_sink:
                ds_ref[...] = jnp.zeros_like(ds_ref)

        @pl.when(n < nb)
        def _():
            first = n == 0
            bias_l = tuple(b_ref[h] for h in range(N_HEADS))
            sink_l = tuple(s_ref[h] for h in range(N_HEADS)) if has_sink else ()

            def fn(q_, kp_, kc_, vp_, vc_, b_, s_):
                o, l = _band_fn(q_, kp_, kc_, vp_, vc_, b_, s_ if has_sink else None, first, inclusive)
                return (o, l) if has_dl else (o,)

            blocks = [r[...].astype(F32) for r in (q, kp, kc, vp, vc)]
            _, vjp = jax.vjp(fn, *blocks, bias_l, sink_l)
            ct = (do_ref[...].astype(F32), dl_ref[...]) if has_dl else (do_ref[...].astype(F32),)
            g = vjp(ct)

            @pl.when(n > 0)
            def _():
                emit(cq[...], ck[...] + g[1], cv[...] + g[3])

            cq[...] = g[0]
            ck[...] = g[2]
            cv[...] = g[4]
            for h in range(N_HEADS):
                db_ref[h] += g[5][h]
                if has_sink:
                    ds_ref[h] += g[6][h]

        @pl.when(n == nb)
        def _():
            emit(cq[...], ck[...], cv[...])

    blk = pl.BlockSpec((BLOCK, HW), lambda r, n: (r * nb + jnp.minimum(n, nb - 1), 0))
    late = pl.BlockSpec((BLOCK, 3 * HW), lambda r, n: (r * nb + jnp.maximum(n - 1, 0), 0))
    ins = [qkv] * 5 + [bias] + ([sink] if has_sink else []) + [do] + ([dl] if has_dl else [])
    specs = _band_specs(nb, *cols) + [_const_spec(bias.shape)] \
        + ([_const_spec(sink.shape)] if has_sink else []) + [blk] * (1 + int(has_dl))
    out_shape = [jax.ShapeDtypeStruct((s, 3 * HW), BF16), jax.ShapeDtypeStruct(bias.shape, F32)] \
        + ([jax.ShapeDtypeStruct(sink.shape, F32)] if has_sink else [])
    out_specs = [late, _const_spec(bias.shape)] + ([_const_spec(sink.shape)] if has_sink else [])
    res = pl.pallas_call(
        body, name=name, grid=(dil, nb + 1), in_specs=specs, out_specs=out_specs, out_shape=out_shape,
        scratch_shapes=[pltpu.VMEM((BLOCK, HW), F32)] * 3,
        compiler_params=_params(("arbitrary", "arbitrary")))(*ins)
    return res[0], res[1], (res[2] if has_sink else None)


def rel_grad(name, dbias, bucket):
    def body(d_ref, b_ref, o_ref):
        lane = lax.broadcasted_iota(jnp.int32, (1, LANE), 1)
        bk = b_ref[...]
        for h in range(N_HEADS):
            d = d_ref[h]
            row = jnp.zeros((1, LANE), F32)
            for b in range(REL_BUCKETS):
                tot = jnp.sum(jnp.where(bk == b, d, 0.0))
                row = row + jnp.where(lane == b, tot, 0.0)
            o_ref[h:h + 1, :] = row

    return pl.pallas_call(
        body, name=name, out_shape=jax.ShapeDtypeStruct((N_HEADS, LANE), F32),
        in_specs=[pl.BlockSpec(memory_space=pltpu.VMEM)] * 2, out_specs=pl.BlockSpec(memory_space=pltpu.VMEM),
        compiler_params=_params())(dbias, bucket)


MLA_T = 512
MLA_HPS = 2
MLA_SCALE = (C_NOPE + C_ROPE) ** -0.5


MLA_ONE = C_NOPE


def _lane_tiles(x, n):
    return jnp.concatenate([x] * n, axis=-1)


def _diag_mask(t):
    r = lax.broadcasted_iota(jnp.int32, (t, t), 0)
    c = lax.broadcasted_iota(jnp.int32, (t, t), 1)
    return c <= r


def _ride_along(comm, grid):
    if comm is None:
        return [], [], [], [], lambda *_: None
    arrays, scatter = comm
    out_shape, specs, sems = _push_shapes(arrays, scatter)

    def hook(ins, outs, sem_refs):
        ids = [pl.program_id(a) for a in range(len(grid))]
        first, last = ids[0] == 0, ids[0] == grid[0] - 1
        for a in range(1, len(grid)):
            first, last = first & (ids[a] == 0), last & (ids[a] == grid[a] - 1)

        @pl.when(first)
        def _():
            _pushes(ins, outs, sem_refs, scatter).start()

        @pl.when(last)
        def _():
            _pushes(ins, outs, sem_refs, scatter).wait()

    return list(arrays), specs, out_shape, sems, hook


def mla_fwd(name, q, k, v, comm=None):
    s = q.shape[0]
    t = min(MLA_T, s)
    nq = s // t
    grid = (N_HEADS // MLA_HPS, nq)
    c_in, c_specs, c_shapes, c_sems, c_hook = _ride_along(comm, grid)
    nc = len(c_in)

    def body(*refs):
        q_ref, k_ref, v_ref = refs[:3]
        o_ref, l_ref = refs[3 + nc:5 + nc]
        m_s, acc_s = refs[5 + 2 * nc:7 + 2 * nc]
        c_hook(refs[3:3 + nc], refs[5 + nc:5 + 2 * nc], refs[7 + 2 * nc:])
        i = pl.program_id(1)
        m_s[...] = jnp.full_like(m_s, NEG)
        acc_s[...] = jnp.zeros_like(acc_s)

        def chunk(j, masked):
            rows = pl.ds(pl.multiple_of(j * t, t), t)
            for hh in range(MLA_HPS):
                ln = slice(hh * LANE, (hh + 1) * LANE)
                sc = _dot(q_ref[:, ln], k_ref[rows, ln], 1, 1)
                if masked:
                    sc = jnp.where(_diag_mask(t), sc, NEG)
                m_old = m_s[:, ln]
                m_new = jnp.maximum(m_old, jnp.max(sc, axis=-1, keepdims=True))
                a = jnp.exp(m_old - m_new)
                p = jnp.exp(sc - _lane_tiles(m_new, t // LANE))
                acc_s[:, ln] = a * acc_s[:, ln] + _dot(p, v_ref[rows, ln], 1, 0)
                m_s[:, ln] = m_new

        def step(j, carry):
            chunk(j, False)
            return carry

        lax.fori_loop(0, i, step, 0)
        chunk(i, True)
        for hh in range(MLA_HPS):
            ln = slice(hh * LANE, (hh + 1) * LANE)
            acc = acc_s[:, ln]
            den = acc[:, MLA_ONE:MLA_ONE + 1]
            o_ref[:, ln] = acc / den
            l_ref[:, ln] = m_s[:, ln] + jnp.log(den)

    w = MLA_HPS * LANE
    qs = pl.BlockSpec((t, w), lambda h, i: (i, h))
    ks = pl.BlockSpec((s, w), lambda h, i: (0, h))
    res = pl.pallas_call(
        body, name=name, grid=grid, in_specs=[qs, ks, ks] + c_specs, out_specs=[qs, qs] + c_specs,
        out_shape=[jax.ShapeDtypeStruct((s, N_HEADS * LANE), F32)] * 2 + c_shapes,
        scratch_shapes=[pltpu.VMEM((t, w), F32)] * 2 + c_sems,
        compiler_params=_params(("arbitrary", "arbitrary")))(q, k, v, *c_in)
    return res[0], res[1], list(res[2:])


def mla_delta_fn(do, o):
    parts = []
    for a, b in zip(_split_impl(do, N_HEADS), _split_impl(o, N_HEADS)):
        parts.append(jnp.sum(a * b, axis=-1, keepdims=True) + jnp.zeros_like(a))
    return jnp.concatenate(parts, axis=-1), do


def mla_bwd(name, q, k, v, do, lse, delta, comm=None):
    s = q.shape[0]
    t = min(MLA_T, s)
    nq = s // t
    grid = (N_HEADS, nq)
    c_in, c_specs, c_shapes, c_sems, c_hook = _ride_along(comm, grid)
    nc = len(c_in)

    def body(*refs):
        q_ref, k_ref, v_ref, do_ref, l_ref, d_ref = refs[:6]
        dq_ref, dk_ref, dv_ref = refs[6 + nc:9 + nc]
        dk_s, dv_s = refs[9 + 2 * nc:11 + 2 * nc]
        c_hook(refs[6:6 + nc], refs[9 + nc:9 + 2 * nc], refs[11 + 2 * nc:])
        j = pl.program_id(1)

        @pl.when(j == 0)
        def _():
            dq_ref[...] = jnp.zeros_like(dq_ref)

        dk_s[...] = jnp.zeros_like(dk_s)
        dv_s[...] = jnp.zeros_like(dv_s)
        kb, vb = k_ref[...], v_ref[...]

        def chunk(i, masked):
            rows = pl.ds(pl.multiple_of(i * t, t), t)
            qi, doi = q_ref[rows, :], do_ref[rows, :]
            sc = _dot(qi, kb, 1, 1)
            if masked:
                sc = jnp.where(_diag_mask(t), sc, NEG)
            p = jnp.exp(sc - _lane_tiles(l_ref[rows, :], t // LANE))
            ds = p * (_dot(doi, vb, 1, 1) - _lane_tiles(d_ref[rows, :], t // LANE))
            dv_s[...] += _dot(p, doi, 0, 0)
            dk_s[...] += _dot(ds, qi, 0, 0)
            dq_ref[rows, :] += _dot(ds, kb, 1, 0)

        def pair(p, carry):
            chunk(j + 1 + 2 * p, False)
            chunk(j + 2 + 2 * p, False)
            return carry

        chunk(j, True)
        n_off = nq - 1 - j
        lax.fori_loop(0, n_off // 2, pair, 0)

        @pl.when(n_off % 2 == 1)
        def _():
            chunk(nq - 1, False)

        dk_ref[...] = dk_s[...]
        dv_ref[...] = dv_s[...]

    ks = pl.BlockSpec((t, LANE), lambda h, j: (j, h))
    full = pl.BlockSpec((s, LANE), lambda h, j: (0, h))
    res = pl.pallas_call(
        body, name=name, grid=grid, in_specs=[full, ks, ks, full, full, full] + c_specs,
        out_specs=[full, ks, ks] + c_specs,
        out_shape=[jax.ShapeDtypeStruct((s, N_HEADS * LANE), F32)] * 3 + c_shapes,
        scratch_shapes=[pltpu.VMEM((t, LANE), F32)] * 2 + c_sems,
        compiler_params=_params(("arbitrary", "arbitrary")))(q, k, v, do, lse, delta, *c_in)
    return res[0], res[1], res[2], list(res[3:])


def _peer(k):
    x, y, c = lax.axis_index("x"), lax.axis_index("y"), lax.axis_index("c")
    px, py, pc = (x ^ ((k >> 2) & 1)), (y ^ ((k >> 1) & 1)), (c ^ (k & 1))
    return (px, py, pc), 4 * px + 2 * py + pc


def exchange(name, arrays, scatter):
    n = len(arrays)

    def body(*refs):
        push = _pushes(refs[:n], refs[n:2 * n], refs[2 * n:], scatter)
        push.start()
        push.wait()

    out_shape, specs, sems = _push_shapes(arrays, scatter)
    return pl.pallas_call(body, name=name, out_shape=out_shape, in_specs=specs, out_specs=specs,
                          scratch_shapes=sems)(*arrays)


def _push_shapes(arrays, scatter):
    n = len(arrays)
    out_shape = [jax.ShapeDtypeStruct(a.shape if scatter else (N_DEV,) + a.shape, a.dtype) for a in arrays]
    n_sem = (N_DEV - 1) * n
    sems = [pltpu.SemaphoreType.DMA((n_sem,)), pltpu.SemaphoreType.DMA((n_sem,)), pltpu.SemaphoreType.DMA((n,))]
    return out_shape, [pl.BlockSpec(memory_space=pl.ANY)] * n, sems


class _pushes:
    def __init__(self, ins, outs, sems, scatter):
        self.ins, self.outs, self.sems, self.scatter = ins, outs, sems, scatter

    def _local(self):
        _, me_idx = _peer(0)
        return [pltpu.make_async_copy(self.ins[a].at[me_idx] if self.scatter else self.ins[a],
                                      self.outs[a].at[me_idx], self.sems[2].at[a]) for a in range(len(self.ins))]

    def _remote(self, incoming):
        send_sems, recv_sems, _ = self.sems
        n = len(self.ins)
        _, me_idx = _peer(0)
        copies = []
        for k in range(1, N_DEV):
            peer, peer_idx = _peer(k)
            for a in range(n):
                sem = (k - 1) * n + a
                copies.append(pltpu.make_async_remote_copy(
                    src_ref=self.ins[a].at[peer_idx] if self.scatter else self.ins[a],
                    dst_ref=self.outs[a].at[peer_idx if incoming else me_idx],
                    send_sem=send_sems.at[sem], recv_sem=recv_sems.at[sem], device_id=peer, device_id_type=MESH_ID))
        return copies

    def start(self):
        for cp in self._local() + self._remote(incoming=False):
            cp.start()

    def wait(self):
        for cp in self._remote(incoming=True):
            cp.wait_recv()
        for cp in self._remote(incoming=False):
            cp.wait_send()
        for cp in self._local():
            cp.wait()


ADAMW_BLOCK_ELEMS = 128 * 1024


def adamw_sum(name, parts, w, m, v):
    nl, r, c = w.shape
    fits = [t for t in range(16, r + 1, 16) if r % t == 0 and t * c <= ADAMW_BLOCK_ELEMS]
    tr = max(fits) if fits else r
    c1 = 1.0 - ADAM_B1 ** ADAM_STEP
    c2 = 1.0 - ADAM_B2 ** ADAM_STEP

    def body(*refs):
        p_refs = refs[:nl]
        w_ref, m_ref, v_ref, g_out, d_out, m_out, v_out = refs[nl:]
        for layer in range(nl):
            @pl.when(pl.program_id(0) == layer)
            def _():
                g = p_refs[layer][0].astype(F32)
                for d in range(1, N_DEV):
                    g = g + p_refs[layer][d].astype(F32)
                mn = ADAM_B1 * m_ref[...] + (1.0 - ADAM_B1) * g
                vn = ADAM_B2 * v_ref[...] + (1.0 - ADAM_B2) * (g * g)
                g_out[...] = g
                m_out[...] = mn
                v_out[...] = vn
                d_out[...] = -ADAM_LR * ((mn / c1) / (jnp.sqrt(vn / c2) + ADAM_EPS) + ADAM_WD * w_ref[...])

    row = pl.BlockSpec((None, tr, c), lambda l, i: (l, i, 0))
    return pl.pallas_call(
        body, name=name, grid=(nl, r // tr),
        in_specs=[pl.BlockSpec((N_DEV, tr, c), lambda l, i: (0, i, 0))] * nl + [row, row, row],
        out_specs=[row] * 4, out_shape=[jax.ShapeDtypeStruct((nl, r, c), F32)] * 4,
        compiler_params=_params(("parallel", "parallel")))(*parts, w, m, v)


def _pack(arrays, dtype, row_mult):
    flat = jnp.concatenate([a.astype(dtype).reshape(-1) for a in arrays])
    n = flat.shape[0]
    quantum = row_mult * LANE
    total = -(-n // quantum) * quantum
    return jnp.pad(flat, (0, total - n)).reshape(total // LANE, LANE)


def _unpack(packed, shapes):
    flat = packed.reshape(-1)
    out, pos = [], 0
    for shp in shapes:
        n = int(np.prod(shp))
        out.append(flat[pos:pos + n].reshape(shp))
        pos += n
    return out


def _rows(a):
    return a.reshape(-1, a.shape[-1])


def _unshard(gathered, shard_shape, axis):
    t = jnp.moveaxis(gathered.reshape((N_DEV,) + tuple(shard_shape)), 0, axis)
    return t.reshape(tuple(shard_shape[:axis]) + (N_DEV * shard_shape[axis],) + tuple(shard_shape[axis + 1:]))


def _to_shards(full, axis):
    shp = full.shape
    t = full.reshape(shp[:axis] + (N_DEV, shp[axis] // N_DEV) + shp[axis + 1:])
    t = jnp.moveaxis(t, axis, 0)
    return t.reshape(N_DEV, -1, t.shape[-1])


A_COLS = len(A_GROUPS) * 3 * HW
KV_COLS = B_KV_HEADS * HEAD_DIM
KV_REP = N_HEADS // B_KV_HEADS
O_BQ = A_COLS
O_BK = O_BQ + HW
O_BV = O_BK + KV_COLS
O_CQ = O_BV + KV_COLS
O_CKV = O_CQ + C_Q_RANK
O_KR = O_CKV + C_KV_RANK
O_GATE = O_KR + C_ROPE


def _proj_weight(w_in):
    d = w_in.shape[0]

    def per_q_head(lo):
        t = w_in[:, lo:lo + KV_COLS].reshape(d, B_KV_HEADS, 1, HEAD_DIM)
        return jnp.broadcast_to(t, (d, B_KV_HEADS, KV_REP, HEAD_DIM)).reshape(d, HW)

    kr = jnp.pad(w_in[:, O_KR:O_GATE], ((0, 0), (C_NOPE, LANE - C_NOPE - C_ROPE)))
    return jnp.concatenate([w_in[:, O_GATE:], w_in[:, :O_BK], per_q_head(O_BK), per_q_head(O_BV),
                            w_in[:, O_CQ:O_KR], kr], axis=1)


def _unproj_grad(dw_p, d_model):
    d = dw_p.shape[0]
    g = 3 * d_model
    p_bk = g + O_BK
    p_bv = p_bk + HW
    p_cq = p_bv + HW
    p_kr = p_cq + C_Q_RANK + C_KV_RANK

    def sum_heads(lo):
        return dw_p[:, lo:lo + HW].reshape(d, B_KV_HEADS, KV_REP, HEAD_DIM).sum(axis=2).reshape(d, KV_COLS)

    return jnp.concatenate([dw_p[:, g:p_bk], sum_heads(p_bk), sum_heads(p_bv), dw_p[:, p_cq:p_kr],
                            dw_p[:, p_kr + C_NOPE:p_kr + C_NOPE + C_ROPE], dw_p[:, :g]], axis=1)


def _pad_heads(w, per_head, lo, hi):
    r = w.shape[0]
    t = w.reshape(r, N_HEADS, per_head)[:, :, lo:hi]
    return jnp.pad(t, ((0, 0), (0, 0), (0, LANE - (hi - lo)))).reshape(r, N_HEADS * LANE)


def _unpad_heads(w, width):
    r = w.shape[0]
    return w.reshape(r, N_HEADS, LANE)[:, :, :width]


def _t5_bucket(dist):
    n = jnp.maximum(dist, 0)
    max_exact = REL_BUCKETS // 2
    scaled = jnp.log(jnp.maximum(n, 1).astype(F32) / max_exact) / math.log(REL_MAX_DIST / max_exact)
    large = max_exact + (scaled * (REL_BUCKETS - max_exact)).astype(jnp.int32)
    return jnp.where(n < max_exact, n, jnp.minimum(large, REL_BUCKETS - 1))


def _buckets(dil):
    qi = jnp.arange(BLOCK)[:, None]
    ki = jnp.arange(2 * BLOCK)[None, :]
    return _t5_bucket((BLOCK + qi - ki) * dil).astype(jnp.int32)


def _rope_tables(s):
    pos = jnp.arange(s, dtype=F32)
    inv_freq = ROPE_BASE ** (-jnp.arange(0, C_ROPE, 2, dtype=F32) / C_ROPE)
    ang = pos[:, None] * inv_freq[None, :]
    cos, sin = jnp.cos(ang), jnp.sin(ang)
    ones = jnp.ones((s, C_NOPE), F32)
    tail = LANE - C_NOPE - C_ROPE
    c = jnp.concatenate([ones, cos, cos, jnp.ones((s, tail), F32)], axis=1)
    sn = jnp.concatenate([jnp.zeros((s, C_NOPE), F32), sin, sin, jnp.zeros((s, tail), F32)], axis=1)
    return c, sn


ATTN_COLS = A_COLS + 3 * HW


def _cols(d_model):
    a = [(3 * i, 3 * i + 1, 3 * i + 2) for i in range(len(A_GROUPS))]
    b = (9, 10, 11)
    return a, b, 3 * d_model


def _layer_weights(full, d_model):
    lw = {}
    if 'w_in' in full:
        lw['w_in_p'] = _proj_weight(full['w_in'])
    if 'w_uq' in full:
        per = C_NOPE + C_ROPE
        w_uq, w_ukv, wb = full['w_uq'], full['w_ukv'], full['w_branch']
        wb2 = jnp.pad(wb[2].reshape(N_HEADS, HEAD_DIM, d_model), ((0, 0), (0, LANE - HEAD_DIM), (0, 0)))
        lw.update(
            wq=_pad_heads(w_uq, per, 0, per), wk=_pad_heads(w_ukv, 2 * C_NOPE, 0, C_NOPE),
            wv=_pad_heads(w_ukv, 2 * C_NOPE, C_NOPE, 2 * C_NOPE), wb0=wb[0], wb1=wb[1],
            wb2=wb2.reshape(N_HEADS * LANE, d_model), w_out=full['w_out'], w_up=full['w_ffn_up'],
            w_down=full['w_ffn_down'], conv_w=full['conv_w'])
    return lw


def _layer_fwd(l, x, x16, lw, small, tabs, comm=None, proj_comm=None):
    s, d = x.shape
    a_cols, b_cols, cq_lane = _cols(d)
    tag = f"l{l}_"
    g3 = 3 * d
    w_attn = lw['w_in_p'][:, g3:g3 + ATTN_COLS]
    w_rest = jnp.concatenate([lw['w_in_p'][:, :g3], lw['w_in_p'][:, g3 + ATTN_COLS:]], axis=1)
    if proj_comm is None:
        pa = matmul(tag + "proj_attn", x16, w_attn, 'nn', out_dtype=BF16)
    else:
        pa, got = matmul(tag + "proj_attn", x16, w_attn, 'nn', out_dtype=BF16, comm=proj_comm[0])
        lw = dict(lw, **_layer_weights(proj_comm[1](got), d))
    pr = matmul(tag + "proj_rest", x16, w_rest, 'nn')
    res = dict(x=x, x16=x16, pa=pa, pr=pr, lw=lw)
    outs, lses, slabs = [], [], []
    for gi, (window, dil) in enumerate(A_GROUPS):
        if dil == 1:
            slab, cols = pa, a_cols[gi]
        else:
            slab = to_classes(pa[:, a_cols[gi][0] * HW:(a_cols[gi][2] + 1) * HW], dil)
            cols = (0, 1, 2)
        o, lg = band_fwd(tag + f"a{gi}_fwd", slab, dil, cols, tabs['bias'][gi], None, True)
        outs.append(from_classes(o, dil))
        lses.append(from_classes(lg, dil))
        slabs.append((slab, cols))
    res['a_o'], res['a_l'], res['a_slabs'] = outs, lses, slabs
    (o_a,) = rows_fwd(tag + "a_combine", combine_fn, [(o, HW, 0) for o in outs] + [(lg, HW, 0) for lg in lses], [],
                      [(HW, BF16)])
    sink = small['sinks'][l].reshape(N_HEADS, 1, 1)
    o_b, _ = band_fwd(tag + "b_fwd", pa, 1, b_cols, tabs['bias'][3], sink, False)
    gq = small['q_norm_g'][l].reshape(1, -1)
    gkv = small['kv_norm_g'][l].reshape(1, -1)
    (qc,) = rows_fwd(tag + "c_q", mla_q_fn, [(pr, C_Q_RANK, cq_lane // C_Q_RANK), (tabs['cos'], LANE, 0),
                                            (tabs['sin'], LANE, 0)], [gq, lw['wq']], [(N_HEADS * LANE, BF16)])
    ckv_blk = (cq_lane + C_Q_RANK) // LANE
    kc, vc = rows_fwd(tag + "c_kv", mla_kv_fn, [(pr, LANE, ckv_blk), (pr, LANE, ckv_blk + 1), (tabs['cos'], LANE, 0),
                                                (tabs['sin'], LANE, 0)], [gkv, lw['wk'], lw['wv']],
                      [(N_HEADS * LANE, BF16)] * 2)
    o_c, lse_c, res['comm'] = mla_fwd(tag + "mla_fwd", qc, kc, vc, comm)
    res.update(o_a=o_a, o_b=o_b, o_c=o_c, lse_c=lse_c, qc=qc, kc=kc, vc=vc)
    bg = small['b_gate'][l].reshape(3, 1, d)
    ys = [matmul(tag + f"branch{i}", o, lw[f'wb{i}'], 'nn') for i, o in enumerate((o_a, o_b, o_c))]
    (merged,) = rows_fwd(tag + "merge", merge_fn, [(y, d, 0) for y in ys] + [(pr, d, i) for i in range(3)],
                         [bg[0], bg[1], bg[2]], [(d, BF16)])
    mix = matmul(tag + "out_proj", merged, lw['w_out'], 'nn')
    ln1 = [small['ln1_g'][l].reshape(1, d), small['ln1_b'][l].reshape(1, d)]
    x1, x1_16 = rows_fwd(tag + "ln1", ln_twice, [(x, d, 0), (mix, d, 0)], ln1, [(d, F32), (d, BF16)])
    u = matmul(tag + "ffn_up", x1_16, lw['w_up'], 'nn')
    h = conv_glu_fwd(u, lw['conv_w'], small['conv_b'][l].reshape(1, -1))
    ff = matmul(tag + "ffn_down", h, lw['w_down'], 'nn')
    ln2 = [small['ln2_g'][l].reshape(1, d), small['ln2_b'][l].reshape(1, d)]
    x2, x2_16 = rows_fwd(tag + "ln2", ln_twice, [(x1, d, 0), (ff, d, 0)], ln2, [(d, F32), (d, BF16)])
    res.update(merged=merged, mix=mix, x1=x1, x1_16=x1_16, u=u, h=h, ff=ff, ys=ys)
    return x2, x2_16, res


def _layer_bwd(l, dy, res, lw, small, tabs, comm=None, dw_comm=None):
    x, proj, pa = res['x'], res['pr'], res['pa']
    s, d = x.shape
    a_cols, b_cols, cq_lane = _cols(d)
    tag = f"l{l}_"
    g = {}
    ln2 = [small['ln2_g'][l].reshape(1, d), small['ln2_b'][l].reshape(1, d)]
    dff, g['ln2_g'], g['ln2_b'] = rows_bwd(tag + "ln2_bwd", ln_fn, [(res['x1'], d, 0), (res['ff'], d, 0)], ln2,
                                           [(dy, d)], [(1, F32)])
    g['w_ffn_down'] = matmul(tag + "dw_down", res['h'], dff, 'tn')
    dh = matmul(tag + "dh", dff, lw['w_down'], 'nt')
    conv_b = small['conv_b'][l].reshape(1, -1)
    dug, duv, dwg, dwv, dbg, dbv = conv_glu_bwd(res['u'], dh, lw['conv_w'], conv_b)
    du = jnp.concatenate([dug, duv], axis=1)
    g['conv_w'] = jnp.concatenate([dwg, dwv], axis=1)
    g['conv_b'] = jnp.concatenate([dbg, dbv], axis=1).reshape(-1)
    g['w_ffn_up'] = matmul(tag + "dw_up", res['x1_16'], du, 'tn')
    dx1 = matmul(tag + "dx1", du, lw['w_up'], 'nt', add=dff, scale=ALPHA)
    ln1 = [small['ln1_g'][l].reshape(1, d), small['ln1_b'][l].reshape(1, d)]
    dmix, g['ln1_g'], g['ln1_b'] = rows_bwd(tag + "ln1_bwd", ln_fn, [(x, d, 0), (res['mix'], d, 0)], ln1,
                                            [(dx1, d)], [(1, F32)])
    g['w_out'] = matmul(tag + "dw_out", res['merged'], dmix, 'tn')
    dmerged = matmul(tag + "dmerged", dmix, lw['w_out'], 'nt')
    bg = small['b_gate'][l].reshape(3, 1, d)
    mrows = [(y, d, 0) for y in res['ys']] + [(proj, d, i) for i in range(3)]
    (dy0, dy1, dy2, dg0, dg1, dg2, db0, db1, db2) = rows_bwd(
        tag + "merge_bwd", merge_fn, mrows, [bg[0], bg[1], bg[2]], [(dmerged, d)], [(i, BF16) for i in range(6)])
    g['b_gate'] = jnp.concatenate([db0, db1, db2], axis=1).reshape(-1)
    branch_in = (res['o_a'], res['o_b'], res['o_c'])
    dwb = [matmul(tag + f"dw_branch{i}", o, dyi, 'tn') for i, (o, dyi) in enumerate(zip(branch_in, (dy0, dy1, dy2)))]
    dwb[2] = dwb[2].reshape(N_HEADS, LANE, d)[:, :HEAD_DIM].reshape(HW, d)
    g['w_branch'] = jnp.stack(dwb, axis=0)
    do_a, do_b, do_c = [matmul(tag + f"do_branch{i}", dyi, lw[f'wb{i}'], 'nt')
                        for i, dyi in enumerate((dy0, dy1, dy2))]
    cw = N_HEADS * LANE
    delta, do_c16 = rows_fwd(tag + "c_delta", mla_delta_fn, [(do_c, cw, 0), (res['o_c'], cw, 0)], [],
                             [(cw, F32), (cw, BF16)])
    dqc, dkc, dvc, g['comm'] = mla_bwd(tag + "mla_bwd", res['qc'], res['kc'], res['vc'], do_c16, res['lse_c'], delta,
                                       comm)
    gq = small['q_norm_g'][l].reshape(1, -1)
    gkv = small['kv_norm_g'][l].reshape(1, -1)
    dcq, dgq, dwq = rows_bwd(tag + "c_q_bwd", mla_q_fn,
                             [(proj, C_Q_RANK, cq_lane // C_Q_RANK), (tabs['cos'], LANE, 0), (tabs['sin'], LANE, 0)],
                             [gq, lw['wq']], [(dqc, N_HEADS * LANE)], [(0, BF16)])
    ckv_blk = (cq_lane + C_Q_RANK) // LANE
    dckv, dkr, dgkv, dwk, dwv = rows_bwd(
        tag + "c_kv_bwd", mla_kv_fn,
        [(proj, LANE, ckv_blk), (proj, LANE, ckv_blk + 1), (tabs['cos'], LANE, 0), (tabs['sin'], LANE, 0)],
        [gkv, lw['wk'], lw['wv']], [(dkc, N_HEADS * LANE), (dvc, N_HEADS * LANE)], [(0, BF16), (1, BF16)])
    g['q_norm_g'], g['kv_norm_g'] = dgq.reshape(-1), dgkv.reshape(-1)
    per = C_NOPE + C_ROPE
    g['w_uq'] = _unpad_heads(dwq, per).reshape(C_Q_RANK, N_HEADS * per)
    g['w_ukv'] = jnp.concatenate([_unpad_heads(dwk, C_NOPE), _unpad_heads(dwv, C_NOPE)], axis=2).reshape(
        C_KV_RANK, N_HEADS * 2 * C_NOPE)
    sink = small['sinks'][l].reshape(N_HEADS, 1, 1)
    piece_b, dbias_b, dsink = band_bwd(tag + "b_bwd", pa, 1, b_cols, tabs['bias'][3], sink, False, do_b, None)
    pieces_b = [piece_b]
    g['sinks'] = dsink.reshape(-1)
    dbias = [None] * 4
    dbias[3] = dbias_b
    combo = [(o, HW, 0) for o in res['a_o']] + [(lg, HW, 0) for lg in res['a_l']]
    a_cts = rows_bwd(tag + "a_combine_bwd", combine_fn, combo, [], [(do_a, HW)],
                     [(i, BF16) for i in range(3)] + [(i, F32) for i in range(3, 6)])
    pieces_a = []
    for gi, (window, dil) in enumerate(A_GROUPS):
        slab, cols = res['a_slabs'][gi]
        group, dbias[gi], _ = band_bwd(
            tag + f"a{gi}_bwd", slab, dil, cols, tabs['bias'][gi], None, True, to_classes(a_cts[gi], dil),
            to_classes(a_cts[3 + gi], dil))
        pieces_a.append(from_classes(group, dil))
    g['dbias'] = dbias
    dproj = jnp.concatenate([dg0, dg1, dg2] + pieces_a + pieces_b + [dcq, dckv, dkr], axis=1)
    if dw_comm is None:
        dw_in_p = matmul(tag + "dw_in", res['x16'], dproj, 'tn')
    else:
        dw_in_p, g['dw_comm'] = matmul(tag + "dw_in", res['x16'], dproj, 'tn', comm=dw_comm(g))
    g['w_in'] = _unproj_grad(dw_in_p, d)
    dx = matmul(tag + "dx", dproj, lw['w_in_p'], 'nt', add=dmix, scale=ALPHA)
    return dx, g


def kernel(x, rel_table, w_in, b_gate, sinks, q_norm_g, kv_norm_g, w_uq, w_ukv, w_branch, w_out, ln1_g, ln1_b, w_ffn_up, conv_w, conv_b, w_ffn_down, ln2_g, ln2_b, loss_target, m_rel_table, m_w_in, m_b_gate, m_sinks, m_q_norm_g, m_kv_norm_g, m_w_uq, m_w_ukv, m_w_branch, m_w_out, m_ln1_g, m_ln1_b, m_w_ffn_up, m_conv_w, m_conv_b, m_w_ffn_down, m_ln2_g, m_ln2_b, v_rel_table, v_w_in, v_b_gate, v_sinks, v_q_norm_g, v_kv_norm_g, v_w_uq, v_w_ukv, v_w_branch, v_w_out, v_ln1_g, v_ln1_b, v_w_ffn_up, v_conv_w, v_conv_b, v_w_ffn_down, v_ln2_g, v_ln2_b):
    args = locals()
    w = {n: args[n] for n in WEIGHTS}
    mom = {n: args["m_" + n] for n in WEIGHTS}
    var = {n: args["v_" + n] for n in WEIGHTS}
    s, d = x.shape[1], x.shape[2]
    xs = x.reshape(s, d)
    target = loss_target.reshape(s, d)

    big = [n for n in WEIGHTS if n in SHARDED]
    small_names = [n for n in WEIGHTS if n not in SHARDED]
    assert DEPTH == 2
    wire = lambda n: BF16 if n in GATHER_BF16 else F32
    rest = [n for n in big if n != 'w_in']

    def shards(l, names):
        return [_rows(w[n][l]).astype(wire(n)) for n in names]

    def to_full(names, gathered):
        return {n: _unshard(t, w[n].shape[1:], SHARDED[n] - 1) for n, t in zip(names, gathered)}

    def grad_shards(names, g):
        return [_to_shards(g[n], SHARDED[n] - 1).astype(wire(n)) for n in names]

    w_in0 = to_full(['w_in'], exchange("gather_w_in0", shards(0, ['w_in']), scatter=False))
    fused = dict(proj0=((shards(0, rest), False), lambda got: to_full(rest, got)),
                 fwd0=((shards(1, big), False), lambda got: to_full(big, got)),
                 bwd0=lambda g1: (grad_shards(big, g1), True),
                 dw0=lambda g0: (grad_shards(rest, g0), True))
    small = {n: w[n] for n in small_names}
    loss_part, grad_x, grads, g_rel, recv1, recv0_rest = _local_grads(xs, target, [w_in0, None], small, fused)
    grad_x = grad_x.reshape(x.shape)
    recv0 = dict(zip(rest, recv0_rest))
    (recv0['w_in'],) = exchange("exchange_dw_in0", grad_shards(['w_in'], grads[0]), scatter=True)

    big_res = [{}, {}, {}, {}]
    per_layer = lambda t: t.reshape((DEPTH, -1, t.shape[-1]))
    for i, n in enumerate(big):
        outs = adamw_sum("adamw_" + n, [recv0[n], recv1[i]], per_layer(w[n]), per_layer(mom[n]), per_layer(var[n]))
        for k in range(4):
            big_res[k][n] = outs[k].reshape(w[n].shape)

    g_small = {'rel_table': g_rel}
    for n in small_names:
        if n != 'rel_table':
            g_small[n] = jnp.stack([grads[l][n] for l in range(DEPTH)], axis=0)
    small_pack = _pack([g_small[n].reshape(w[n].shape) for n in small_names] + [loss_part[0, 0:1]], F32, 8)
    (small_all,) = exchange("gather_small_grads", [small_pack], scatter=False)
    pks = lambda tree: _pack([tree[n] for n in small_names] + [jnp.zeros((1,), F32)], F32, 8)[None]
    small_out = adamw_sum("adamw_replicated", [small_all], pks(w), pks(mom), pks(var))
    small_shapes = [w[n].shape for n in small_names] + [(1,)]
    small_res = [dict(zip(small_names + ['loss'], _unpack(o, small_shapes))) for o in small_out]

    loss = small_res[0]['loss'].reshape(())
    out = [loss, grad_x]
    for k in range(4):
        out += [big_res[k][n] if n in SHARDED else small_res[k][n] for n in WEIGHTS]
    return tuple(out)


def _local_grads(xs, target, fulls, small, fused=None):
    s, d = xs.shape
    rel_table = small['rel_table']
    fulls = list(fulls)

    cos, sin = _rope_tables(s)
    buckets = [_buckets(dil) for _, dil in A_GROUPS] + [_buckets(1)]
    def lookup(table, bucket):
        out = jnp.zeros((N_HEADS,) + bucket.shape, F32)
        for b in range(REL_BUCKETS):
            out = jnp.where((bucket == b)[None], table[b][:, None, None], out)
        return out

    bias = [lookup(rel_table[:, gi * N_HEADS:(gi + 1) * N_HEADS], buckets[gi]) for gi in range(4)]
    tabs = dict(cos=cos, sin=sin, bias=bias)

    act, act16, saved = xs, xs.astype(BF16), []
    for l in range(DEPTH):
        lw = _layer_weights(fulls[l], d)
        ride = fused is not None and l == 0
        act, act16, res = _layer_fwd(l, act, act16, lw, small, tabs, fused['fwd0'][0] if ride else None,
                                     fused['proj0'] if ride else None)
        if ride:
            fulls[1] = fused['fwd0'][1](res['comm'])
        saved.append(res)
    dy, loss_part = loss_head(act, target)

    grads, recv1, recv0_rest = [None] * DEPTH, None, None
    for l in reversed(range(DEPTH)):
        ride = fused is not None and l == 0
        dy, grads[l] = _layer_bwd(l, dy, saved[l], saved[l]['lw'], small, tabs,
                                  fused['bwd0'](grads[1]) if ride else None, fused['dw0'] if ride else None)
        if ride:
            recv1, recv0_rest = grads[l]['comm'], grads[l]['dw_comm']
    grad_x = dy

    rel_cols = []
    for gi in range(4):
        both = grads[0]['dbias'][gi] + grads[1]['dbias'][gi] if DEPTH == 2 else grads[0]['dbias'][gi]
        rel_cols.append(rel_grad(f"rel_grad{gi}", both, buckets[gi])[:, :REL_BUCKETS].T)
    g_rel = jnp.concatenate(rel_cols, axis=1)
    return loss_part, grad_x, grads, g_rel, recv1, recv0_rest
```

```python
import functools
import math

import numpy as np
import jax
import jax.numpy as jnp
from jax import lax
from jax.experimental import pallas as pl
from jax.experimental.pallas import tpu as pltpu

F32 = jnp.float32
BF16 = jnp.bfloat16

N_DEV = 8
DEPTH = 2
HEAD_DIM = 64
BLOCK = 128
A_GROUPS = ((128, 1), (512, 4), (2048, 16))
N_HEADS = 8
HW = N_HEADS * HEAD_DIM
B_KV_HEADS = 2
C_Q_RANK = 256
C_KV_RANK = 128
C_NOPE = 64
C_ROPE = 32
ROPE_BASE = 10000.0
REL_BUCKETS = 32
REL_MAX_DIST = 2048
ALPHA = (2 * DEPTH) ** 0.25
LN_EPS = 1e-5
RMS_EPS = 1e-6
NEG = -1e30
ADAM_LR = 0.001
ADAM_B1 = 0.9
ADAM_B2 = 0.999
ADAM_EPS = 1e-08
ADAM_WD = 0.01
ADAM_STEP = 10

LANE = 128
TM = 256
TM_LIGHT = 512
VMEM_LIMIT = 56 * 1024 * 1024
MESH_ID = pl.DeviceIdType.MESH

WEIGHTS = ('rel_table', 'w_in', 'b_gate', 'sinks', 'q_norm_g', 'kv_norm_g', 'w_uq', 'w_ukv', 'w_branch',
           'w_out', 'ln1_g', 'ln1_b', 'w_ffn_up', 'conv_w', 'conv_b', 'w_ffn_down', 'ln2_g', 'ln2_b')
SHARDED = {'w_in': 2, 'w_uq': 2, 'w_ukv': 2, 'w_branch': 3, 'w_out': 1, 'w_ffn_up': 2, 'conv_w': 2,
           'w_ffn_down': 1}
GATHER_BF16 = ('w_in', 'w_uq', 'w_ukv', 'w_branch', 'w_out', 'w_ffn_up', 'w_ffn_down')


def _params(sem=None):
    return pltpu.CompilerParams(dimension_semantics=sem, vmem_limit_bytes=VMEM_LIMIT)


def _pick(n, target):
    if n <= target:
        return n
    best = None
    for t in range(LANE, target + 1, LANE):
        if n % t == 0:
            best = t
    assert best is not None, (n, target)
    return best


def _dot(a, b, ca, cb):
    return lax.dot_general(a.astype(BF16), b.astype(BF16), (((ca,), (cb,)), ((), ())),
                           preferred_element_type=F32)


@jax.custom_vjp
def mm(a, b):
    return _dot(a, b, 1, 0)


def _mm_fwd(a, b):
    return _dot(a, b, 1, 0), (a, b)


def _mm_bwd(res, g):
    a, b = res
    return _dot(g, b, 1, 1), _dot(a, g, 0, 0)


mm.defvjp(_mm_fwd, _mm_bwd)


@jax.custom_vjp
def mm_nt(a, b):
    return _dot(a, b, 1, 1)


def _mm_nt_fwd(a, b):
    return _dot(a, b, 1, 1), (a, b)


def _mm_nt_bwd(res, g):
    a, b = res
    return _dot(g, b, 1, 0), _dot(g, a, 0, 0)


mm_nt.defvjp(_mm_nt_fwd, _mm_nt_bwd)


def _split_impl(x, n):
    w = x.shape[-1] // n
    return tuple(x[:, i * w:(i + 1) * w] for i in range(n))


@functools.partial(jax.custom_vjp, nondiff_argnums=(1,))
def split_lanes(x, n):
    return _split_impl(x, n)


def _split_fwd(x, n):
    return _split_impl(x, n), None


def _split_bwd(n, _, gs):
    return (jnp.concatenate(gs, axis=-1),)


split_lanes.defvjp(_split_fwd, _split_bwd)


@jax.custom_vjp
def concat_lanes(xs):
    return jnp.concatenate(xs, axis=-1)


def _concat_fwd(xs):
    return jnp.concatenate(xs, axis=-1), len(xs)


def _concat_bwd(n, g):
    return (_split_impl(g, n),)


concat_lanes.defvjp(_concat_fwd, _concat_bwd)


@jax.custom_vjp
def concat_rows(a, b):
    return jnp.concatenate([a, b], axis=0)


def _crow_fwd(a, b):
    return jnp.concatenate([a, b], axis=0), a.shape[0]


def _crow_bwd(na, g):
    return g[:na], g[na:]


concat_rows.defvjp(_crow_fwd, _crow_bwd)


def _rot_half(x):
    n = x.shape[-1]
    lane = lax.broadcasted_iota(jnp.int32, (1, n), 1) % LANE
    lo = (lane >= C_NOPE) & (lane < C_NOPE + C_ROPE // 2)
    hi = (lane >= C_NOPE + C_ROPE // 2) & (lane < C_NOPE + C_ROPE)
    up = pltpu.roll(x, n - C_ROPE // 2, 1)
    dn = pltpu.roll(x, C_ROPE // 2, 1)
    return jnp.where(lo, -up, jnp.where(hi, dn, 0.0))


@jax.custom_vjp
def rope(x, c, s):
    return x * c + _rot_half(x) * s


def _rope_fwd(x, c, s):
    return x * c + _rot_half(x) * s, (c, s)


def _rope_bwd(res, g):
    c, s = res
    return g * c - _rot_half(g * s), jnp.zeros_like(c), jnp.zeros_like(s)


rope.defvjp(_rope_fwd, _rope_bwd)


def _sigmoid(x):
    return 0.5 * jnp.tanh(0.5 * x) + 0.5


MATMUL_VMEM = 44 * 1024 * 1024


def _matmul_tiles(m, n, k, a_bytes, b_bytes, o_bytes, has_add):
    tm, tn, tk = _pick(m, 1536), _pick(n, 2560), _pick(k, 2560)

    def need(tn_, tk_):
        acc = tm * tn_ * 4 if k // tk_ > 1 else 0
        return 2 * (tm * tk_ * a_bytes + tk_ * tn_ * b_bytes + tm * tn_ * o_bytes) + acc + 2 * tm * tn_ * 4 * int(has_add)

    while need(tn, tk) > MATMUL_VMEM:
        if tk >= tn and tk > LANE:
            tk = _pick(k, tk - LANE)
        else:
            tn = _pick(n, tn - LANE)
    return tm, tn, tk


def matmul(name, a, b, mode, out_dtype=F32, add=None, scale=1.0, comm=None):
    if mode == 'nn':
        (m, k), n = a.shape, b.shape[1]
    elif mode == 'nt':
        (m, k), n = a.shape, b.shape[0]
    else:
        (k, m), n = a.shape, b.shape[1]
    tm, tn, tk = _matmul_tiles(m, n, k, a.dtype.itemsize, b.dtype.itemsize, jnp.dtype(out_dtype).itemsize,
                               add is not None)
    nk = k // tk
    grid = (m // tm, n // tn, nk)
    c_in, c_specs, c_shapes, c_sems, c_hook = _ride_along(comm, grid)
    nc = len(c_in)
    a_spec = pl.BlockSpec((tk, tm), lambda i, j, kk: (kk, i)) if mode == 'tn' else \
        pl.BlockSpec((tm, tk), lambda i, j, kk: (i, kk))
    b_spec = pl.BlockSpec((tn, tk), lambda i, j, kk: (j, kk)) if mode == 'nt' else \
        pl.BlockSpec((tk, tn), lambda i, j, kk: (kk, j))
    o_spec = pl.BlockSpec((tm, tn), lambda i, j, kk: (i, j))
    ca = 0 if mode == 'tn' else 1
    cb = 1 if mode == 'nt' else 0
    has_add = add is not None

    n_in = 2 + int(has_add)

    def body(*refs):
        a_ref, b_ref = refs[:2]
        add_ref = refs[2] if has_add else None
        o_ref = refs[n_in + nc]
        c_hook(refs[n_in:n_in + nc], refs[n_in + nc + 1:n_in + 2 * nc + 1], refs[len(refs) - 3:])

        def finish(r):
            if has_add:
                r = r + scale * add_ref[...]
            o_ref[...] = r.astype(out_dtype)

        if nk == 1:
            finish(_dot(a_ref[...], b_ref[...], ca, cb))
            return
        acc_ref = refs[n_in + 2 * nc + 1]
        kk = pl.program_id(2)

        @pl.when(kk == 0)
        def _():
            acc_ref[...] = jnp.zeros_like(acc_ref)

        acc_ref[...] += _dot(a_ref[...], b_ref[...], ca, cb)

        @pl.when(kk == nk - 1)
        def _():
            finish(acc_ref[...])

    ins = [a, b] + ([add] if has_add else []) + c_in
    specs = [a_spec, b_spec] + ([o_spec] if has_add else []) + c_specs
    res = pl.pallas_call(
        body, name=name, grid=grid, in_specs=specs, out_specs=[o_spec] + c_specs,
        out_shape=[jax.ShapeDtypeStruct((m, n), out_dtype)] + c_shapes,
        scratch_shapes=([pltpu.VMEM((tm, tn), F32)] if nk > 1 else []) + c_sems,
        compiler_params=_params(("arbitrary",) * 3 if comm is not None else ("parallel", "parallel", "arbitrary")))(*ins)
    return res[0] if comm is None else (res[0], list(res[1:]))


def _row_spec(width, col_block=0, tm=TM):
    return pl.BlockSpec((tm, width), lambda i: (i, col_block))


def _full_spec(shape):
    nd = len(shape)
    return pl.BlockSpec(shape, lambda i: (0,) * nd)


def rows_fwd(name, fn, rows, params, outs, tm=TM):
    s = rows[0][0].shape[0]
    nr, npar = len(rows), len(params)

    def body(*refs):
        r = [refs[i][...].astype(F32) for i in range(nr)]
        p = [refs[nr + i][...] for i in range(npar)]
        res = fn(*r, *p)
        for o_ref, val in zip(refs[nr + npar:], res):
            o_ref[...] = val.astype(o_ref.dtype)

    return pl.pallas_call(
        body, name=name, grid=(s // tm,),
        in_specs=[_row_spec(w, cb, tm) for _, w, cb in rows] + [_full_spec(p.shape) for p in params],
        out_specs=[_row_spec(w, 0, tm) for w, _ in outs],
        out_shape=[jax.ShapeDtypeStruct((s, w), dt) for w, dt in outs],
        compiler_params=_params(("parallel",)))(*[a for a, _, _ in rows], *params)


def rows_bwd(name, fn, rows, params, cts, row_grads, tm=TM):
    s = rows[0][0].shape[0]
    nr, npar, nct, nrg = len(rows), len(params), len(cts), len(row_grads)

    def body(*refs):
        r = [refs[i][...].astype(F32) for i in range(nr)]
        p = [refs[nr + i][...].astype(F32) for i in range(npar)]
        g = tuple(refs[nr + npar + i][...].astype(F32) for i in range(nct))
        _, vjp = jax.vjp(lambda *a: tuple(fn(*a)), *r, *p)
        grads = vjp(g)
        outs = refs[nr + npar + nct:]
        for k, (idx, _) in enumerate(row_grads):
            outs[k][...] = grads[idx].astype(outs[k].dtype)

        @pl.when(pl.program_id(0) == 0)
        def _():
            for k in range(npar):
                outs[nrg + k][...] = jnp.zeros_like(outs[nrg + k])

        for k in range(npar):
            outs[nrg + k][...] += grads[nr + k]

    return pl.pallas_call(
        body, name=name, grid=(s // tm,),
        in_specs=[_row_spec(w, cb, tm) for _, w, cb in rows] + [_full_spec(p.shape) for p in params]
        + [_row_spec(w, 0, tm) for _, w in cts],
        out_specs=[_row_spec(rows[idx][1], 0, tm) for idx, _ in row_grads] + [_full_spec(p.shape) for p in params],
        out_shape=[jax.ShapeDtypeStruct((s, rows[idx][1]), dt) for idx, dt in row_grads]
        + [jax.ShapeDtypeStruct(p.shape, F32) for p in params],
        compiler_params=_params(("arbitrary",)))(*[a for a, _, _ in rows], *params, *[a for a, _ in cts])


def ln_fn(x, r, g, b):
    z = ALPHA * x + r
    mu = jnp.mean(z, axis=-1, keepdims=True)
    zc = z - mu
    var = jnp.mean(zc * zc, axis=-1, keepdims=True)
    return (zc * lax.rsqrt(var + LN_EPS) * g + b,)


def ln_twice(x, r, g, b):
    (y,) = ln_fn(x, r, g, b)
    return y, y


def _rms(x, g):
    return x * lax.rsqrt(jnp.mean(x * x, axis=-1, keepdims=True) + RMS_EPS) * g


def mla_q_fn(cq, c, s, g, wq):
    q = mm(_rms(cq, g), wq)
    return (rope(q, jnp.concatenate([c] * N_HEADS, axis=-1), jnp.concatenate([s] * N_HEADS, axis=-1)) * MLA_SCALE,)


@jax.custom_vjp
def tile_heads(x):
    return jnp.concatenate([x] * N_HEADS, axis=-1)


def _tile_fwd(x):
    return jnp.concatenate([x] * N_HEADS, axis=-1), None


def _tile_bwd(_, g):
    parts = _split_impl(g, N_HEADS)
    acc = parts[0]
    for p in parts[1:]:
        acc = acc + p
    return (acc,)


tile_heads.defvjp(_tile_fwd, _tile_bwd)


def mla_kv_fn(ckv, kr, c, s, g, wk, wv):
    n = _rms(ckv, g)
    lane = lax.broadcasted_iota(jnp.int32, (1, N_HEADS * LANE), 1) % LANE
    one = (lane == MLA_ONE).astype(F32)
    return mm(n, wk) + tile_heads(rope(kr, c, s)), mm(n, wv) + one


def merge_fn(y0, y1, y2, g0, g1, g2, b0, b1, b2):
    return (_sigmoid(g0 + b0) * y0 + _sigmoid(g1 + b1) * y1 + _sigmoid(g2 + b2) * y2,)


def combine_fn(o1, o2, o3, l1, l2, l3):
    mx = lax.stop_gradient(jnp.maximum(jnp.maximum(l1, l2), l3))
    e1, e2, e3 = jnp.exp(l1 - mx), jnp.exp(l2 - mx), jnp.exp(l3 - mx)
    return ((e1 * o1 + e2 * o2 + e3 * o3) / (e1 + e2 + e3),)


def loss_head(y, t):
    s, d = y.shape

    def body(y_ref, t_ref, dy_ref, loss_ref):
        err = y_ref[...] - t_ref[...]
        dy_ref[...] = err * (1.0 / d)

        @pl.when(pl.program_id(0) == 0)
        def _():
            loss_ref[...] = jnp.zeros_like(loss_ref)

        loss_ref[...] += jnp.sum(err * err) * (0.5 / d)

    return pl.pallas_call(
        body, name="loss_head", grid=(s // TM,), in_specs=[_row_spec(d), _row_spec(d)],
        out_specs=[_row_spec(d), _full_spec((8, LANE))],
        out_shape=[jax.ShapeDtypeStruct((s, d), F32), jax.ShapeDtypeStruct((8, LANE), F32)],
        compiler_params=_params(("arbitrary",)))(y, t)


CONV_TN = 256
CONV_TM = 512


def _conv_taps(u_ext, w_ref, b_ref):
    d1, d2 = pltpu.roll(u_ext, 1, 0), pltpu.roll(u_ext, 2, 0)
    return w_ref[0:1, :] * d2 + w_ref[1:2, :] * d1 + w_ref[2:3, :] * u_ext + b_ref[...], d1, d2


def conv_glu_fwd(u, conv_w, conv_b):
    s, f2 = u.shape
    f = f2 // 2
    tn, tm = CONV_TN, min(CONV_TM, s)
    off = f // tn
    hb = tm // 8

    def body(ug, uv, hg, hv, wg, wv, bg, bv, o_ref):
        first = pl.program_id(0) == 0

        def conv(u_ref, h_ref, w_ref, b_ref):
            halo = jnp.where(first, 0.0, h_ref[...])
            ext = jnp.concatenate([halo, u_ref[...]], axis=0)
            return _conv_taps(ext, w_ref, b_ref)[0][8:]

        cg, cv = conv(ug, hg, wg, bg), conv(uv, hv, wv, bv)
        o_ref[...] = (cg * _sigmoid(cg) * cv).astype(o_ref.dtype)

    blk = lambda o: pl.BlockSpec((tm, tn), lambda i, j: (i, j + o))
    halo = lambda o: pl.BlockSpec((8, tn), lambda i, j: (jnp.maximum(i * hb - 1, 0), j + o))
    par = lambda r, o: pl.BlockSpec((r, tn), lambda i, j: (0, j + o))
    return pl.pallas_call(
        body, name="conv_glu_fwd", grid=(s // tm, off),
        in_specs=[blk(0), blk(off), halo(0), halo(off), par(3, 0), par(3, off), par(1, 0), par(1, off)],
        out_specs=pl.BlockSpec((tm, tn), lambda i, j: (i, j)),
        out_shape=jax.ShapeDtypeStruct((s, f), BF16),
        compiler_params=_params(("parallel", "parallel")))(u, u, u, u, conv_w, conv_w, conv_b, conv_b)


def conv_glu_bwd(u, dh, conv_w, conv_b):
    s, f2 = u.shape
    f = f2 // 2
    tn, tm = CONV_TN, min(CONV_TM, s)
    off = f // tn
    hb = tm // 8
    n_rows = s // tm

    def body(ug, uv, pg, pv, ng, nv, dh_ref, dhn_ref, wg, wv, bg, bv, dug, duv, dwg, dwv, dbg, dbv):
        i = pl.program_id(1)
        first, last = i == 0, i == n_rows - 1

        def ext(u_ref, p_ref, n_ref):
            return jnp.concatenate([jnp.where(first, 0.0, p_ref[...]), u_ref[...], n_ref[...]], axis=0)

        eg, ev = ext(ug, pg, ng), ext(uv, pv, nv)
        (cg, eg1, eg2), (cv, ev1, ev2) = _conv_taps(eg, wg, bg), _conv_taps(ev, wv, bv)
        dhe = jnp.concatenate([jnp.zeros((8, tn), F32), dh_ref[...], jnp.where(last, 0.0, dhn_ref[...])], axis=0)
        sg = _sigmoid(cg)
        dcg = dhe * cv * (sg * (1.0 + cg * (1.0 - sg)))
        dcv = dhe * (cg * sg)
        n = tm + 16

        @pl.when(i == 0)
        def _():
            for r in (dwg, dwv, dbg, dbv):
                r[...] = jnp.zeros_like(r)

        def back(dc, e, e1, e2, w_ref, du_ref, dw_ref, db_ref):
            du = w_ref[2:3, :] * dc + w_ref[1:2, :] * pltpu.roll(dc, n - 1, 0) + w_ref[0:1, :] * pltpu.roll(dc, n - 2, 0)
            du_ref[...] = du[8:8 + tm].astype(du_ref.dtype)
            own = dc[8:8 + tm]
            dw_ref[0:1, :] += jnp.sum(own * e2[8:8 + tm], axis=0, keepdims=True)
            dw_ref[1:2, :] += jnp.sum(own * e1[8:8 + tm], axis=0, keepdims=True)
            dw_ref[2:3, :] += jnp.sum(own * e[8:8 + tm], axis=0, keepdims=True)
            db_ref[...] += jnp.sum(own, axis=0, keepdims=True)

        back(dcg, eg, eg1, eg2, wg, dug, dwg, dbg)
        back(dcv, ev, ev1, ev2, wv, duv, dwv, dbv)

    blk = lambda o: pl.BlockSpec((tm, tn), lambda j, i: (i, j + o))
    prev = lambda o: pl.BlockSpec((8, tn), lambda j, i: (jnp.maximum(i * hb - 1, 0), j + o))
    nxt = lambda o: pl.BlockSpec((8, tn), lambda j, i: (jnp.minimum((i + 1) * hb, s // 8 - 1), j + o))
    par = lambda r, o: pl.BlockSpec((r, tn), lambda j, i: (0, j + o))
    return pl.pallas_call(
        body, name="conv_glu_bwd", grid=(off, n_rows),
        in_specs=[blk(0), blk(off), prev(0), prev(off), nxt(0), nxt(off), blk(0), nxt(0),
                  par(3, 0), par(3, off), par(1, 0), par(1, off)],
        out_specs=[blk(0), blk(0), par(3, 0), par(3, 0), par(1, 0), par(1, 0)],
        out_shape=[jax.ShapeDtypeStruct((s, f), BF16)] * 2 + [jax.ShapeDtypeStruct((3, f), F32)] * 2
        + [jax.ShapeDtypeStruct((1, f), F32)] * 2,
        compiler_params=_params(("parallel", "arbitrary")))(u, u, u, u, u, u, dh, dh, conv_w, conv_w, conv_b, conv_b)


def _band_fn(q, kp, kc, vp, vc, bias, sink, first, inclusive):
    k = concat_rows(kp, kc)
    v = concat_rows(vp, vc)
    qi = lax.broadcasted_iota(jnp.int32, (BLOCK, 2 * BLOCK), 0)
    ki = lax.broadcasted_iota(jnp.int32, (BLOCK, 2 * BLOCK), 1)
    step = BLOCK + qi - ki
    valid = (step >= 0) & ((step <= BLOCK) if inclusive else (step < BLOCK))
    valid = valid & (jnp.logical_not(first) | (ki >= BLOCK))
    lane = lax.broadcasted_iota(jnp.int32, (1, LANE), 1)
    scale = HEAD_DIM ** -0.5
    qs, ks, vs = split_lanes(q, 4), split_lanes(k, 4), split_lanes(v, 4)
    o_parts, l_parts = [], []
    for pair in range(4):
        o_acc, l_acc = None, None
        for sub in range(2):
            h = 2 * pair + sub
            mh = ((lane >= sub * HEAD_DIM) & (lane < (sub + 1) * HEAD_DIM)).astype(F32)
            logits = mm_nt(qs[pair] * mh, ks[pair]) * scale + bias[h]
            logits = jnp.where(valid, logits, NEG)
            m = lax.stop_gradient(jnp.max(logits, axis=-1, keepdims=True))
            if sink is not None:
                m = jnp.maximum(m, lax.stop_gradient(sink[h]))
            p = jnp.exp(logits - m)
            den = jnp.sum(p, axis=-1, keepdims=True)
            if sink is not None:
                den = den + jnp.exp(sink[h] - m)
            oh = mm(p * (1.0 / den), vs[pair] * mh)
            lh = (m + jnp.log(den)) * mh
            o_acc = oh if o_acc is None else o_acc + oh
            l_acc = lh if l_acc is None else l_acc + lh
        o_parts.append(o_acc)
        l_parts.append(l_acc)
    return concat_lanes(tuple(o_parts)), concat_lanes(tuple(l_parts))


def _band_specs(nb, cq, ck, cv):
    own = lambda n: jnp.minimum(n, nb - 1)
    prev = lambda n: jnp.maximum(own(n) - 1, 0)
    q = pl.BlockSpec((BLOCK, HW), lambda r, n: (r * nb + own(n), cq))
    kp = pl.BlockSpec((BLOCK, HW), lambda r, n: (r * nb + prev(n), ck))
    kc = pl.BlockSpec((BLOCK, HW), lambda r, n: (r * nb + own(n), ck))
    vp = pl.BlockSpec((BLOCK, HW), lambda r, n: (r * nb + prev(n), cv))
    vc = pl.BlockSpec((BLOCK, HW), lambda r, n: (r * nb + own(n), cv))
    return [q, kp, kc, vp, vc]


def to_classes(a, dil):
    if dil == 1:
        return a
    s, c = a.shape
    return a.reshape(s // dil, dil, c).transpose(1, 0, 2).reshape(s, c)


def from_classes(a, dil):
    if dil == 1:
        return a
    s, c = a.shape
    return a.reshape(dil, s // dil, c).transpose(1, 0, 2).reshape(s, c)


def _const_spec(shape):
    nd = len(shape)
    return pl.BlockSpec(shape, lambda r, n: (0,) * nd)


def band_fwd(name, qkv, dil, cols, bias, sink, inclusive):
    s = qkv.shape[0]
    nb = s // (BLOCK * dil)
    has_sink = sink is not None

    def body(*refs):
        q, kp, kc, vp, vc, b_ref = refs[:6]
        s_ref = refs[6] if has_sink else None
        o_ref, l_ref = refs[-2:]
        first = pl.program_id(1) == 0
        bias_l = tuple(b_ref[h] for h in range(N_HEADS))
        sink_l = tuple(s_ref[h] for h in range(N_HEADS)) if has_sink else None
        o, l = _band_fn(q[...], kp[...], kc[...], vp[...], vc[...], bias_l, sink_l, first, inclusive)
        o_ref[...] = o
        l_ref[...] = l

    out_spec = pl.BlockSpec((BLOCK, HW), lambda r, n: (r * nb + n, 0))
    ins = [qkv] * 5 + [bias] + ([sink] if has_sink else [])
    specs = _band_specs(nb, *cols) + [_const_spec(bias.shape)] + ([_const_spec(sink.shape)] if has_sink else [])
    return pl.pallas_call(
        body, name=name, grid=(dil, nb), in_specs=specs, out_specs=[out_spec, out_spec],
        out_shape=[jax.ShapeDtypeStruct((s, HW), F32)] * 2,
        compiler_params=_params(("parallel", "parallel")))(*ins)


def band_bwd(name, qkv, dil, cols, bias, sink, inclusive, do, dl):
    s = qkv.shape[0]
    nb = s // (BLOCK * dil)
    has_sink, has_dl = sink is not None, dl is not None
    n_in = 6 + int(has_sink) + 1 + int(has_dl)

    def body(*refs):
        q, kp, kc, vp, vc, b_ref = refs[:6]
        s_ref = refs[6] if has_sink else None
        do_ref = refs[6 + int(has_sink)]
        dl_ref = refs[7 + int(has_sink)] if has_dl else None
        out_ref, db_ref = refs[n_in:n_in + 2]
        ds_ref = refs[n_in + 2] if has_sink else None
        cq, ck, cv = refs[-3:]
        n = pl.program_id(1)

        def emit(dq, dk, dv):
            out_ref[:, 0:HW] = dq.astype(out_ref.dtype)
            out_ref[:, HW:2 * HW] = dk.astype(out_ref.dtype)
            out_ref[:, 2 * HW:3 * HW] = dv.astype(out_ref.dtype)

        @pl.when((pl.program_id(0) == 0) & (n == 0))
        def _():
            db_ref[...] = jnp.zeros_like(db_ref)
            if has_sink:
                ds_ref[...] = jnp.zeros_like(ds_ref)

        @pl.when(n < nb)
        def _():
            first = n == 0
            bias_l = tuple(b_ref[h] for h in range(N_HEADS))
            sink_l = tuple(s_ref[h] for h in range(N_HEADS)) if has_sink else ()

            def fn(q_, kp_, kc_, vp_, vc_, b_, s_):
                o, l = _band_fn(q_, kp_, kc_, vp_, vc_, b_, s_ if has_sink else None, first, inclusive)
                return (o, l) if has_dl else (o,)

            blocks = [r[...].astype(F32) for r in (q, kp, kc, vp, vc)]
            _, vjp = jax.vjp(fn, *blocks, bias_l, sink_l)
            ct = (do_ref[...].astype(F32), dl_ref[...]) if has_dl else (do_ref[...].astype(F32),)
            g = vjp(ct)

            @pl.when(n > 0)
            def _():
                emit(cq[...], ck[...] + g[1], cv[...] + g[3])

            cq[...] = g[0]
            ck[...] = g[2]
            cv[...] = g[4]
            for h in range(N_HEADS):
                db_ref[h] += g[5][h]
                if has_sink:
                    ds_ref[h] += g[6][h]

        @pl.when(n == nb)
        def _():
            emit(cq[...], ck[...], cv[...])

    blk = pl.BlockSpec((BLOCK, HW), lambda r, n: (r * nb + jnp.minimum(n, nb - 1), 0))
    late = pl.BlockSpec((BLOCK, 3 * HW), lambda r, n: (r * nb + jnp.maximum(n - 1, 0), 0))
    ins = [qkv] * 5 + [bias] + ([sink] if has_sink else []) + [do] + ([dl] if has_dl else [])
    specs = _band_specs(nb, *cols) + [_const_spec(bias.shape)] \
        + ([_const_spec(sink.shape)] if has_sink else []) + [blk] * (1 + int(has_dl))
    out_shape = [jax.ShapeDtypeStruct((s, 3 * HW), BF16), jax.ShapeDtypeStruct(bias.shape, F32)] \
        + ([jax.ShapeDtypeStruct(sink.shape, F32)] if has_sink else [])
    out_specs = [late, _const_spec(bias.shape)] + ([_const_spec(sink.shape)] if has_sink else [])
    res = pl.pallas_call(
        body, name=name, grid=(dil, nb + 1), in_specs=specs, out_specs=out_specs, out_shape=out_shape,
        scratch_shapes=[pltpu.VMEM((BLOCK, HW), F32)] * 3,
        compiler_params=_params(("arbitrary", "arbitrary")))(*ins)
    return res[0], res[1], (res[2] if has_sink else None)


def rel_grad(name, dbias, bucket):
    def body(d_ref, b_ref, o_ref):
        lane = lax.broadcasted_iota(jnp.int32, (1, LANE), 1)
        bk = b_ref[...]
        for h in range(N_HEADS):
            d = d_ref[h]
            row = jnp.zeros((1, LANE), F32)
            for b in range(REL_BUCKETS):
                tot = jnp.sum(jnp.where(bk == b, d, 0.0))
                row = row + jnp.where(lane == b, tot, 0.0)
            o_ref[h:h + 1, :] = row

    return pl.pallas_call(
        body, name=name, out_shape=jax.ShapeDtypeStruct((N_HEADS, LANE), F32),
        in_specs=[pl.BlockSpec(memory_space=pltpu.VMEM)] * 2, out_specs=pl.BlockSpec(memory_space=pltpu.VMEM),
        compiler_params=_params())(dbias, bucket)


MLA_T = 512
MLA_HPS = 2
MLA_SCALE = (C_NOPE + C_ROPE) ** -0.5


MLA_ONE = C_NOPE


def _lane_tiles(x, n):
    return jnp.concatenate([x] * n, axis=-1)


def _diag_mask(t):
    r = lax.broadcasted_iota(jnp.int32, (t, t), 0)
    c = lax.broadcasted_iota(jnp.int32, (t, t), 1)
    return c <= r


def _ride_along(comm, grid):
    if comm is None:
        return [], [], [], [], lambda *_: None
    arrays, scatter = comm
    out_shape, specs, sems = _push_shapes(arrays, scatter)

    def hook(ins, outs, sem_refs):
        ids = [pl.program_id(a) for a in range(len(grid))]
        first, last = ids[0] == 0, ids[0] == grid[0] - 1
        for a in range(1, len(grid)):
            first, last = first & (ids[a] == 0), last & (ids[a] == grid[a] - 1)

        @pl.when(first)
        def _():
            _pushes(ins, outs, sem_refs, scatter).start()

        @pl.when(last)
        def _():
            _pushes(ins, outs, sem_refs, scatter).wait()

    return list(arrays), specs, out_shape, sems, hook


def mla_fwd(name, q, k, v, comm=None):
    s = q.shape[0]
    t = min(MLA_T, s)
    nq = s // t
    grid = (N_HEADS // MLA_HPS, nq)
    c_in, c_specs, c_shapes, c_sems, c_hook = _ride_along(comm, grid)
    nc = len(c_in)

    def body(*refs):
        q_ref, k_ref, v_ref = refs[:3]
        o_ref, l_ref = refs[3 + nc:5 + nc]
        m_s, acc_s = refs[5 + 2 * nc:7 + 2 * nc]
        c_hook(refs[3:3 + nc], refs[5 + nc:5 + 2 * nc], refs[7 + 2 * nc:])
        i = pl.program_id(1)
        m_s[...] = jnp.full_like(m_s, NEG)
        acc_s[...] = jnp.zeros_like(acc_s)

        def chunk(j, masked):
            rows = pl.ds(pl.multiple_of(j * t, t), t)
            for hh in range(MLA_HPS):
                ln = slice(hh * LANE, (hh + 1) * LANE)
                sc = _dot(q_ref[:, ln], k_ref[rows, ln], 1, 1)
                if masked:
                    sc = jnp.where(_diag_mask(t), sc, NEG)
                m_old = m_s[:, ln]
                m_new = jnp.maximum(m_old, jnp.max(sc, axis=-1, keepdims=True))
                a = jnp.exp(m_old - m_new)
                p = jnp.exp(sc - _lane_tiles(m_new, t // LANE))
                acc_s[:, ln] = a * acc_s[:, ln] + _dot(p, v_ref[rows, ln], 1, 0)
                m_s[:, ln] = m_new

        def step(j, carry):
            chunk(j, False)
            return carry

        lax.fori_loop(0, i, step, 0)
        chunk(i, True)
        for hh in range(MLA_HPS):
            ln = slice(hh * LANE, (hh + 1) * LANE)
            acc = acc_s[:, ln]
            den = acc[:, MLA_ONE:MLA_ONE + 1]
            o_ref[:, ln] = acc / den
            l_ref[:, ln] = m_s[:, ln] + jnp.log(den)

    w = MLA_HPS * LANE
    qs = pl.BlockSpec((t, w), lambda h, i: (i, h))
    ks = pl.BlockSpec((s, w), lambda h, i: (0, h))
    res = pl.pallas_call(
        body, name=name, grid=grid, in_specs=[qs, ks, ks] + c_specs, out_specs=[qs, qs] + c_specs,
        out_shape=[jax.ShapeDtypeStruct((s, N_HEADS * LANE), F32)] * 2 + c_shapes,
        scratch_shapes=[pltpu.VMEM((t, w), F32)] * 2 + c_sems,
        compiler_params=_params(("arbitrary", "arbitrary")))(q, k, v, *c_in)
    return res[0], res[1], list(res[2:])


def mla_delta_fn(do, o):
    parts = []
    for a, b in zip(_split_impl(do, N_HEADS), _split_impl(o, N_HEADS)):
        parts.append(jnp.sum(a * b, axis=-1, keepdims=True) + jnp.zeros_like(a))
    return jnp.concatenate(parts, axis=-1), do


def mla_bwd(name, q, k, v, do, lse, delta, comm=None):
    s = q.shape[0]
    t = min(MLA_T, s)
    nq = s // t
    grid = (N_HEADS, nq)
    c_in, c_specs, c_shapes, c_sems, c_hook = _ride_along(comm, grid)
    nc = len(c_in)

    def body(*refs):
        q_ref, k_ref, v_ref, do_ref, l_ref, d_ref = refs[:6]
        dq_ref, dk_ref, dv_ref = refs[6 + nc:9 + nc]
        dk_s, dv_s = refs[9 + 2 * nc:11 + 2 * nc]
        c_hook(refs[6:6 + nc], refs[9 + nc:9 + 2 * nc], refs[11 + 2 * nc:])
        j = pl.program_id(1)

        @pl.when(j == 0)
        def _():
            dq_ref[...] = jnp.zeros_like(dq_ref)

        dk_s[...] = jnp.zeros_like(dk_s)
        dv_s[...] = jnp.zeros_like(dv_s)
        kb, vb = k_ref[...], v_ref[...]

        def chunk(i, masked):
            rows = pl.ds(pl.multiple_of(i * t, t), t)
            qi, doi = q_ref[rows, :], do_ref[rows, :]
            sc = _dot(qi, kb, 1, 1)
            if masked:
                sc = jnp.where(_diag_mask(t), sc, NEG)
            p = jnp.exp(sc - _lane_tiles(l_ref[rows, :], t // LANE))
            ds = p * (_dot(doi, vb, 1, 1) - _lane_tiles(d_ref[rows, :], t // LANE))
            dv_s[...] += _dot(p, doi, 0, 0)
            dk_s[...] += _dot(ds, qi, 0, 0)
            dq_ref[rows, :] += _dot(ds, kb, 1, 0)

        def pair(p, carry):
            chunk(j + 1 + 2 * p, False)
            chunk(j + 2 + 2 * p, False)
            return carry

        chunk(j, True)
        n_off = nq - 1 - j
        lax.fori_loop(0, n_off // 2, pair, 0)

        @pl.when(n_off % 2 == 1)
        def _():
            chunk(nq - 1, False)

        dk_ref[...] = dk_s[...]
        dv_ref[...] = dv_s[...]

    ks = pl.BlockSpec((t, LANE), lambda h, j: (j, h))
    full = pl.BlockSpec((s, LANE), lambda h, j: (0, h))
    res = pl.pallas_call(
        body, name=name, grid=grid, in_specs=[full, ks, ks, full, full, full] + c_specs,
        out_specs=[full, ks, ks] + c_specs,
        out_shape=[jax.ShapeDtypeStruct((s, N_HEADS * LANE), F32)] * 3 + c_shapes,
        scratch_shapes=[pltpu.VMEM((t, LANE), F32)] * 2 + c_sems,
        compiler_params=_params(("arbitrary", "arbitrary")))(q, k, v, do, lse, delta, *c_in)
    return res[0], res[1], res[2], list(res[3:])


def _peer(k):
    x, y, c = lax.axis_index("x"), lax.axis_index("y"), lax.axis_index("c")
    px, py, pc = (x ^ ((k >> 2) & 1)), (y ^ ((k >> 1) & 1)), (c ^ (k & 1))
    return (px, py, pc), 4 * px + 2 * py + pc


def exchange(name, arrays, scatter):
    n = len(arrays)

    def body(*refs):
        push = _pushes(refs[:n], refs[n:2 * n], refs[2 * n:], scatter)
        push.start()
        push.wait()

    out_shape, specs, sems = _push_shapes(arrays, scatter)
    return pl.pallas_call(body, name=name, out_shape=out_shape, in_specs=specs, out_specs=specs,
                          scratch_shapes=sems)(*arrays)


def _push_shapes(arrays, scatter):
    n = len(arrays)
    out_shape = [jax.ShapeDtypeStruct(a.shape if scatter else (N_DEV,) + a.shape, a.dtype) for a in arrays]
    n_sem = (N_DEV - 1) * n
    sems = [pltpu.SemaphoreType.DMA((n_sem,)), pltpu.SemaphoreType.DMA((n_sem,)), pltpu.SemaphoreType.DMA((n,))]
    return out_shape, [pl.BlockSpec(memory_space=pl.ANY)] * n, sems


class _pushes:
    def __init__(self, ins, outs, sems, scatter):
        self.ins, self.outs, self.sems, self.scatter = ins, outs, sems, scatter

    def _local(self):
        _, me_idx = _peer(0)
        return [pltpu.make_async_copy(self.ins[a].at[me_idx] if self.scatter else self.ins[a],
                                      self.outs[a].at[me_idx], self.sems[2].at[a]) for a in range(len(self.ins))]

    def _remote(self, incoming):
        send_sems, recv_sems, _ = self.sems
        n = len(self.ins)
        _, me_idx = _peer(0)
        copies = []
        for k in range(1, N_DEV):
            peer, peer_idx = _peer(k)
            for a in range(n):
                sem = (k - 1) * n + a
                copies.append(pltpu.make_async_remote_copy(
                    src_ref=self.ins[a].at[peer_idx] if self.scatter else self.ins[a],
                    dst_ref=self.outs[a].at[peer_idx if incoming else me_idx],
                    send_sem=send_sems.at[sem], recv_sem=recv_sems.at[sem], device_id=peer, device_id_type=MESH_ID))
        return copies

    def start(self):
        for cp in self._local() + self._remote(incoming=False):
            cp.start()

    def wait(self):
        for cp in self._remote(incoming=True):
            cp.wait_recv()
        for cp in self._remote(incoming=False):
            cp.wait_send()
        for cp in self._local():
            cp.wait()


ADAMW_BLOCK_ELEMS = 128 * 1024


def adamw_sum(name, parts, w, m, v):
    nl, r, c = w.shape
    fits = [t for t in range(16, r + 1, 16) if r % t == 0 and t * c <= ADAMW_BLOCK_ELEMS]
    tr = max(fits) if fits else r
    c1 = 1.0 - ADAM_B1 ** ADAM_STEP
    c2 = 1.0 - ADAM_B2 ** ADAM_STEP

    def body(*refs):
        p_refs = refs[:nl]
        w_ref, m_ref, v_ref, g_out, d_out, m_out, v_out = refs[nl:]
        for layer in range(nl):
            @pl.when(pl.program_id(0) == layer)
            def _():
                g = p_refs[layer][0].astype(F32)
                for d in range(1, N_DEV):
                    g = g + p_refs[layer][d].astype(F32)
                mn = ADAM_B1 * m_ref[...] + (1.0 - ADAM_B1) * g
                vn = ADAM_B2 * v_ref[...] + (1.0 - ADAM_B2) * (g * g)
                g_out[...] = g
                m_out[...] = mn
                v_out[...] = vn
                d_out[...] = -ADAM_LR * ((mn / c1) / (jnp.sqrt(vn / c2) + ADAM_EPS) + ADAM_WD * w_ref[...])

    row = pl.BlockSpec((None, tr, c), lambda l, i: (l, i, 0))
    return pl.pallas_call(
        body, name=name, grid=(nl, r // tr),
        in_specs=[pl.BlockSpec((N_DEV, tr, c), lambda l, i: (0, i, 0))] * nl + [row, row, row],
        out_specs=[row] * 4, out_shape=[jax.ShapeDtypeStruct((nl, r, c), F32)] * 4,
        compiler_params=_params(("parallel", "parallel")))(*parts, w, m, v)


def _pack(arrays, dtype, row_mult):
    flat = jnp.concatenate([a.astype(dtype).reshape(-1) for a in arrays])
    n = flat.shape[0]
    quantum = row_mult * LANE
    total = -(-n // quantum) * quantum
    return jnp.pad(flat, (0, total - n)).reshape(total // LANE, LANE)


def _unpack(packed, shapes):
    flat = packed.reshape(-1)
    out, pos = [], 0
    for shp in shapes:
        n = int(np.prod(shp))
        out.append(flat[pos:pos + n].reshape(shp))
        pos += n
    return out


def _rows(a):
    return a.reshape(-1, a.shape[-1])


def _unshard(gathered, shard_shape, axis):
    t = jnp.moveaxis(gathered.reshape((N_DEV,) + tuple(shard_shape)), 0, axis)
    return t.reshape(tuple(shard_shape[:axis]) + (N_DEV * shard_shape[axis],) + tuple(shard_shape[axis + 1:]))


def _to_shards(full, axis):
    shp = full.shape
    t = full.reshape(shp[:axis] + (N_DEV, shp[axis] // N_DEV) + shp[axis + 1:])
    t = jnp.moveaxis(t, axis, 0)
    return t.reshape(N_DEV, -1, t.shape[-1])


A_COLS = len(A_GROUPS) * 3 * HW
KV_COLS = B_KV_HEADS * HEAD_DIM
KV_REP = N_HEADS // B_KV_HEADS
O_BQ = A_COLS
O_BK = O_BQ + HW
O_BV = O_BK + KV_COLS
O_CQ = O_BV + KV_COLS
O_CKV = O_CQ + C_Q_RANK
O_KR = O_CKV + C_KV_RANK
O_GATE = O_KR + C_ROPE


def _proj_weight(w_in):
    d = w_in.shape[0]

    def per_q_head(lo):
        t = w_in[:, lo:lo + KV_COLS].reshape(d, B_KV_HEADS, 1, HEAD_DIM)
        return jnp.broadcast_to(t, (d, B_KV_HEADS, KV_REP, HEAD_DIM)).reshape(d, HW)

    kr = jnp.pad(w_in[:, O_KR:O_GATE], ((0, 0), (C_NOPE, LANE - C_NOPE - C_ROPE)))
    return jnp.concatenate([w_in[:, O_GATE:], w_in[:, :O_BK], per_q_head(O_BK), per_q_head(O_BV),
                            w_in[:, O_CQ:O_KR], kr], axis=1)


def _unproj_grad(dw_p, d_model):
    d = dw_p.shape[0]
    g = 3 * d_model
    p_bk = g + O_BK
    p_bv = p_bk + HW
    p_cq = p_bv + HW
    p_kr = p_cq + C_Q_RANK + C_KV_RANK

    def sum_heads(lo):
        return dw_p[:, lo:lo + HW].reshape(d, B_KV_HEADS, KV_REP, HEAD_DIM).sum(axis=2).reshape(d, KV_COLS)

    return jnp.concatenate([dw_p[:, g:p_bk], sum_heads(p_bk), sum_heads(p_bv), dw_p[:, p_cq:p_kr],
                            dw_p[:, p_kr + C_NOPE:p_kr + C_NOPE + C_ROPE], dw_p[:, :g]], axis=1)


def _pad_heads(w, per_head, lo, hi):
    r = w.shape[0]
    t = w.reshape(r, N_HEADS, per_head)[:, :, lo:hi]
    return jnp.pad(t, ((0, 0), (0, 0), (0, LANE - (hi - lo)))).reshape(r, N_HEADS * LANE)


def _unpad_heads(w, width):
    r = w.shape[0]
    return w.reshape(r, N_HEADS, LANE)[:, :, :width]


def _t5_bucket(dist):
    n = jnp.maximum(dist, 0)
    max_exact = REL_BUCKETS // 2
    scaled = jnp.log(jnp.maximum(n, 1).astype(F32) / max_exact) / math.log(REL_MAX_DIST / max_exact)
    large = max_exact + (scaled * (REL_BUCKETS - max_exact)).astype(jnp.int32)
    return jnp.where(n < max_exact, n, jnp.minimum(large, REL_BUCKETS - 1))


def _buckets(dil):
    qi = jnp.arange(BLOCK)[:, None]
    ki = jnp.arange(2 * BLOCK)[None, :]
    return _t5_bucket((BLOCK + qi - ki) * dil).astype(jnp.int32)


def _rope_tables(s):
    pos = jnp.arange(s, dtype=F32)
    inv_freq = ROPE_BASE ** (-jnp.arange(0, C_ROPE, 2, dtype=F32) / C_ROPE)
    ang = pos[:, None] * inv_freq[None, :]
    cos, sin = jnp.cos(ang), jnp.sin(ang)
    ones = jnp.ones((s, C_NOPE), F32)
    tail = LANE - C_NOPE - C_ROPE
    c = jnp.concatenate([ones, cos, cos, jnp.ones((s, tail), F32)], axis=1)
    sn = jnp.concatenate([jnp.zeros((s, C_NOPE), F32), sin, sin, jnp.zeros((s, tail), F32)], axis=1)
    return c, sn


ATTN_COLS = A_COLS + 3 * HW


def _cols(d_model):
    a = [(3 * i, 3 * i + 1, 3 * i + 2) for i in range(len(A_GROUPS))]
    b = (9, 10, 11)
    return a, b, 3 * d_model


def _layer_weights(full, d_model):
    lw = {}
    if 'w_in' in full:
        lw['w_in_p'] = _proj_weight(full['w_in'])
    if 'w_uq' in full:
        per = C_NOPE + C_ROPE
        w_uq, w_ukv, wb = full['w_uq'], full['w_ukv'], full['w_branch']
        wb2 = jnp.pad(wb[2].reshape(N_HEADS, HEAD_DIM, d_model), ((0, 0), (0, LANE - HEAD_DIM), (0, 0)))
        lw.update(
            wq=_pad_heads(w_uq, per, 0, per), wk=_pad_heads(w_ukv, 2 * C_NOPE, 0, C_NOPE),
            wv=_pad_heads(w_ukv, 2 * C_NOPE, C_NOPE, 2 * C_NOPE), wb0=wb[0], wb1=wb[1],
            wb2=wb2.reshape(N_HEADS * LANE, d_model), w_out=full['w_out'], w_up=full['w_ffn_up'],
            w_down=full['w_ffn_down'], conv_w=full['conv_w'])
    return lw


def _layer_fwd(l, x, x16, lw, small, tabs, comm=None, proj_comm=None):
    s, d = x.shape
    a_cols, b_cols, cq_lane = _cols(d)
    tag = f"l{l}_"
    g3 = 3 * d
    w_attn = lw['w_in_p'][:, g3:g3 + ATTN_COLS]
    w_rest = jnp.concatenate([lw['w_in_p'][:, :g3], lw['w_in_p'][:, g3 + ATTN_COLS:]], axis=1)
    if proj_comm is None:
        pa = matmul(tag + "proj_attn", x16, w_attn, 'nn', out_dtype=BF16)
    else:
        pa, got = matmul(tag + "proj_attn", x16, w_attn, 'nn', out_dtype=BF16, comm=proj_comm[0])
        lw = dict(lw, **_layer_weights(proj_comm[1](got), d))
    pr = matmul(tag + "proj_rest", x16, w_rest, 'nn')
    res = dict(x=x, x16=x16, pa=pa, pr=pr, lw=lw)
    outs, lses, slabs = [], [], []
    for gi, (window, dil) in enumerate(A_GROUPS):
        if dil == 1:
            slab, cols = pa, a_cols[gi]
        else:
            slab = to_classes(pa[:, a_cols[gi][0] * HW:(a_cols[gi][2] + 1) * HW], dil)
            cols = (0, 1, 2)
        o, lg = band_fwd(tag + f"a{gi}_fwd", slab, dil, cols, tabs['bias'][gi], None, True)
        outs.append(from_classes(o, dil))
        lses.append(from_classes(lg, dil))
        slabs.append((slab, cols))
    res['a_o'], res['a_l'], res['a_slabs'] = outs, lses, slabs
    (o_a,) = rows_fwd(tag + "a_combine", combine_fn, [(o, HW, 0) for o in outs] + [(lg, HW, 0) for lg in lses], [],
                      [(HW, BF16)])
    sink = small['sinks'][l].reshape(N_HEADS, 1, 1)
    o_b, _ = band_fwd(tag + "b_fwd", pa, 1, b_cols, tabs['bias'][3], sink, False)
    gq = small['q_norm_g'][l].reshape(1, -1)
    gkv = small['kv_norm_g'][l].reshape(1, -1)
    (qc,) = rows_fwd(tag + "c_q", mla_q_fn, [(pr, C_Q_RANK, cq_lane // C_Q_RANK), (tabs['cos'], LANE, 0),
                                            (tabs['sin'], LANE, 0)], [gq, lw['wq']], [(N_HEADS * LANE, BF16)])
    ckv_blk = (cq_lane + C_Q_RANK) // LANE
    kc, vc = rows_fwd(tag + "c_kv", mla_kv_fn, [(pr, LANE, ckv_blk), (pr, LANE, ckv_blk + 1), (tabs['cos'], LANE, 0),
                                                (tabs['sin'], LANE, 0)], [gkv, lw['wk'], lw['wv']],
                      [(N_HEADS * LANE, BF16)] * 2)
    o_c, lse_c, res['comm'] = mla_fwd(tag + "mla_fwd", qc, kc, vc, comm)
    res.update(o_a=o_a, o_b=o_b, o_c=o_c, lse_c=lse_c, qc=qc, kc=kc, vc=vc)
    bg = small['b_gate'][l].reshape(3, 1, d)
    ys = [matmul(tag + f"branch{i}", o, lw[f'wb{i}'], 'nn') for i, o in enumerate((o_a, o_b, o_c))]
    (merged,) = rows_fwd(tag + "merge", merge_fn, [(y, d, 0) for y in ys] + [(pr, d, i) for i in range(3)],
                         [bg[0], bg[1], bg[2]], [(d, BF16)])
    mix = matmul(tag + "out_proj", merged, lw['w_out'], 'nn')
    ln1 = [small['ln1_g'][l].reshape(1, d), small['ln1_b'][l].reshape(1, d)]
    x1, x1_16 = rows_fwd(tag + "ln1", ln_twice, [(x, d, 0), (mix, d, 0)], ln1, [(d, F32), (d, BF16)], tm=TM_LIGHT)
    u = matmul(tag + "ffn_up", x1_16, lw['w_up'], 'nn')
    h = conv_glu_fwd(u, lw['conv_w'], small['conv_b'][l].reshape(1, -1))
    ff = matmul(tag + "ffn_down", h, lw['w_down'], 'nn')
    ln2 = [small['ln2_g'][l].reshape(1, d), small['ln2_b'][l].reshape(1, d)]
    x2, x2_16 = rows_fwd(tag + "ln2", ln_twice, [(x1, d, 0), (ff, d, 0)], ln2, [(d, F32), (d, BF16)], tm=TM_LIGHT)
    res.update(merged=merged, mix=mix, x1=x1, x1_16=x1_16, u=u, h=h, ff=ff, ys=ys)
    return x2, x2_16, res


def _layer_bwd(l, dy, res, lw, small, tabs, comm=None, dw_comm=None):
    x, proj, pa = res['x'], res['pr'], res['pa']
    s, d = x.shape
    a_cols, b_cols, cq_lane = _cols(d)
    tag = f"l{l}_"
    g = {}
    ln2 = [small['ln2_g'][l].reshape(1, d), small['ln2_b'][l].reshape(1, d)]
    dff, g['ln2_g'], g['ln2_b'] = rows_bwd(tag + "ln2_bwd", ln_fn, [(res['x1'], d, 0), (res['ff'], d, 0)], ln2,
                                           [(dy, d)], [(1, F32)], tm=TM_LIGHT)
    g['w_ffn_down'] = matmul(tag + "dw_down", res['h'], dff, 'tn')
    dh = matmul(tag + "dh", dff, lw['w_down'], 'nt')
    conv_b = small['conv_b'][l].reshape(1, -1)
    dug, duv, dwg, dwv, dbg, dbv = conv_glu_bwd(res['u'], dh, lw['conv_w'], conv_b)
    du = jnp.concatenate([dug, duv], axis=1)
    g['conv_w'] = jnp.concatenate([dwg, dwv], axis=1)
    g['conv_b'] = jnp.concatenate([dbg, dbv], axis=1).reshape(-1)
    g['w_ffn_up'] = matmul(tag + "dw_up", res['x1_16'], du, 'tn')
    dx1 = matmul(tag + "dx1", du, lw['w_up'], 'nt', add=dff, scale=ALPHA)
    ln1 = [small['ln1_g'][l].reshape(1, d), small['ln1_b'][l].reshape(1, d)]
    dmix, g['ln1_g'], g['ln1_b'] = rows_bwd(tag + "ln1_bwd", ln_fn, [(x, d, 0), (res['mix'], d, 0)], ln1,
                                            [(dx1, d)], [(1, F32)], tm=TM_LIGHT)
    g['w_out'] = matmul(tag + "dw_out", res['merged'], dmix, 'tn')
    dmerged = matmul(tag + "dmerged", dmix, lw['w_out'], 'nt')
    bg = small['b_gate'][l].reshape(3, 1, d)
    mrows = [(y, d, 0) for y in res['ys']] + [(proj, d, i) for i in range(3)]
    (dy0, dy1, dy2, dg0, dg1, dg2, db0, db1, db2) = rows_bwd(
        tag + "merge_bwd", merge_fn, mrows, [bg[0], bg[1], bg[2]], [(dmerged, d)], [(i, BF16) for i in range(6)])
    g['b_gate'] = jnp.concatenate([db0, db1, db2], axis=1).reshape(-1)
    branch_in = (res['o_a'], res['o_b'], res['o_c'])
    dwb = [matmul(tag + f"dw_branch{i}", o, dyi, 'tn') for i, (o, dyi) in enumerate(zip(branch_in, (dy0, dy1, dy2)))]
    dwb[2] = dwb[2].reshape(N_HEADS, LANE, d)[:, :HEAD_DIM].reshape(HW, d)
    g['w_branch'] = jnp.stack(dwb, axis=0)
    do_a, do_b, do_c = [matmul(tag + f"do_branch{i}", dyi, lw[f'wb{i}'], 'nt')
                        for i, dyi in enumerate((dy0, dy1, dy2))]
    cw = N_HEADS * LANE
    delta, do_c16 = rows_fwd(tag + "c_delta", mla_delta_fn, [(do_c, cw, 0), (res['o_c'], cw, 0)], [],
                             [(cw, F32), (cw, BF16)])
    dqc, dkc, dvc, g['comm'] = mla_bwd(tag + "mla_bwd", res['qc'], res['kc'], res['vc'], do_c16, res['lse_c'], delta,
                                       comm)
    gq = small['q_norm_g'][l].reshape(1, -1)
    gkv = small['kv_norm_g'][l].reshape(1, -1)
    dcq, dgq, dwq = rows_bwd(tag + "c_q_bwd", mla_q_fn,
                             [(proj, C_Q_RANK, cq_lane // C_Q_RANK), (tabs['cos'], LANE, 0), (tabs['sin'], LANE, 0)],
                             [gq, lw['wq']], [(dqc, N_HEADS * LANE)], [(0, BF16)])
    ckv_blk = (cq_lane + C_Q_RANK) // LANE
    dckv, dkr, dgkv, dwk, dwv = rows_bwd(
        tag + "c_kv_bwd", mla_kv_fn,
        [(proj, LANE, ckv_blk), (proj, LANE, ckv_blk + 1), (tabs['cos'], LANE, 0), (tabs['sin'], LANE, 0)],
        [gkv, lw['wk'], lw['wv']], [(dkc, N_HEADS * LANE), (dvc, N_HEADS * LANE)], [(0, BF16), (1, BF16)])
    g['q_norm_g'], g['kv_norm_g'] = dgq.reshape(-1), dgkv.reshape(-1)
    per = C_NOPE + C_ROPE
    g['w_uq'] = _unpad_heads(dwq, per).reshape(C_Q_RANK, N_HEADS * per)
    g['w_ukv'] = jnp.concatenate([_unpad_heads(dwk, C_NOPE), _unpad_heads(dwv, C_NOPE)], axis=2).reshape(
        C_KV_RANK, N_HEADS * 2 * C_NOPE)
    sink = small['sinks'][l].reshape(N_HEADS, 1, 1)
    piece_b, dbias_b, dsink = band_bwd(tag + "b_bwd", pa, 1, b_cols, tabs['bias'][3], sink, False, do_b, None)
    pieces_b = [piece_b]
    g['sinks'] = dsink.reshape(-1)
    dbias = [None] * 4
    dbias[3] = dbias_b
    combo = [(o, HW, 0) for o in res['a_o']] + [(lg, HW, 0) for lg in res['a_l']]
    a_cts = rows_bwd(tag + "a_combine_bwd", combine_fn, combo, [], [(do_a, HW)],
                     [(i, BF16) for i in range(3)] + [(i, F32) for i in range(3, 6)])
    pieces_a = []
    for gi, (window, dil) in enumerate(A_GROUPS):
        slab, cols = res['a_slabs'][gi]
        group, dbias[gi], _ = band_bwd(
            tag + f"a{gi}_bwd", slab, dil, cols, tabs['bias'][gi], None, True, to_classes(a_cts[gi], dil),
            to_classes(a_cts[3 + gi], dil))
        pieces_a.append(from_classes(group, dil))
    g['dbias'] = dbias
    dproj = jnp.concatenate([dg0, dg1, dg2] + pieces_a + pieces_b + [dcq, dckv, dkr], axis=1)
    if dw_comm is None:
        dw_in_p = matmul(tag + "dw_in", res['x16'], dproj, 'tn')
    else:
        dw_in_p, g['dw_comm'] = matmul(tag + "dw_in", res['x16'], dproj, 'tn', comm=dw_comm(g))
    g['w_in'] = _unproj_grad(dw_in_p, d)
    dx = matmul(tag + "dx", dproj, lw['w_in_p'], 'nt', add=dmix, scale=ALPHA)
    return dx, g


def kernel(x, rel_table, w_in, b_gate, sinks, q_norm_g, kv_norm_g, w_uq, w_ukv, w_branch, w_out, ln1_g, ln1_b, w_ffn_up, conv_w, conv_b, w_ffn_down, ln2_g, ln2_b, loss_target, m_rel_table, m_w_in, m_b_gate, m_sinks, m_q_norm_g, m_kv_norm_g, m_w_uq, m_w_ukv, m_w_branch, m_w_out, m_ln1_g, m_ln1_b, m_w_ffn_up, m_conv_w, m_conv_b, m_w_ffn_down, m_ln2_g, m_ln2_b, v_rel_table, v_w_in, v_b_gate, v_sinks, v_q_norm_g, v_kv_norm_g, v_w_uq, v_w_ukv, v_w_branch, v_w_out, v_ln1_g, v_ln1_b, v_w_ffn_up, v_conv_w, v_conv_b, v_w_ffn_down, v_ln2_g, v_ln2_b):
    args = locals()
    w = {n: args[n] for n in WEIGHTS}
    mom = {n: args["m_" + n] for n in WEIGHTS}
    var = {n: args["v_" + n] for n in WEIGHTS}
    s, d = x.shape[1], x.shape[2]
    xs = x.reshape(s, d)
    target = loss_target.reshape(s, d)

    big = [n for n in WEIGHTS if n in SHARDED]
    small_names = [n for n in WEIGHTS if n not in SHARDED]
    assert DEPTH == 2
    wire = lambda n: BF16 if n in GATHER_BF16 else F32
    rest = [n for n in big if n != 'w_in']

    def shards(l, names):
        return [_rows(w[n][l]).astype(wire(n)) for n in names]

    def to_full(names, gathered):
        return {n: _unshard(t, w[n].shape[1:], SHARDED[n] - 1) for n, t in zip(names, gathered)}

    def grad_shards(names, g):
        return [_to_shards(g[n], SHARDED[n] - 1).astype(wire(n)) for n in names]

    w_in0 = to_full(['w_in'], exchange("gather_w_in0", shards(0, ['w_in']), scatter=False))
    fused = dict(proj0=((shards(0, rest), False), lambda got: to_full(rest, got)),
                 fwd0=((shards(1, big), False), lambda got: to_full(big, got)),
                 bwd0=lambda g1: (grad_shards(big, g1), True),
                 dw0=lambda g0: (grad_shards(rest, g0), True))
    small = {n: w[n] for n in small_names}
    loss_part, grad_x, grads, g_rel, recv1, recv0_rest = _local_grads(xs, target, [w_in0, None], small, fused)
    grad_x = grad_x.reshape(x.shape)
    recv0 = dict(zip(rest, recv0_rest))
    (recv0['w_in'],) = exchange("exchange_dw_in0", grad_shards(['w_in'], grads[0]), scatter=True)

    big_res = [{}, {}, {}, {}]
    per_layer = lambda t: t.reshape((DEPTH, -1, t.shape[-1]))
    for i, n in enumerate(big):
        outs = adamw_sum("adamw_" + n, [recv0[n], recv1[i]], per_layer(w[n]), per_layer(mom[n]), per_layer(var[n]))
        for k in range(4):
            big_res[k][n] = outs[k].reshape(w[n].shape)

    g_small = {'rel_table': g_rel}
    for n in small_names:
        if n != 'rel_table':
            g_small[n] = jnp.stack([grads[l][n] for l in range(DEPTH)], axis=0)
    small_pack = _pack([g_small[n].reshape(w[n].shape) for n in small_names] + [loss_part[0, 0:1]], F32, 8)
    (small_all,) = exchange("gather_small_grads", [small_pack], scatter=False)
    pks = lambda tree: _pack([tree[n] for n in small_names] + [jnp.zeros((1,), F32)], F32, 8)[None]
    small_out = adamw_sum("adamw_replicated", [small_all], pks(w), pks(mom), pks(var))
    small_shapes = [w[n].shape for n in small_names] + [(1,)]
    small_res = [dict(zip(small_names + ['loss'], _unpack(o, small_shapes))) for o in small_out]

    loss = small_res[0]['loss'].reshape(())
    out = [loss, grad_x]
    for k in range(4):
        out += [big_res[k][n] if n in SHARDED else small_res[k][n] for n in WEIGHTS]
    return tuple(out)


def _local_grads(xs, target, fulls, small, fused=None):
    s, d = xs.shape
    rel_table = small['rel_table']
    fulls = list(fulls)

    cos, sin = _rope_tables(s)
    buckets = [_buckets(dil) for _, dil in A_GROUPS] + [_buckets(1)]
    def lookup(table, bucket):
        out = jnp.zeros((N_HEADS,) + bucket.shape, F32)
        for b in range(REL_BUCKETS):
            out = jnp.where((bucket == b)[None], table[b][:, None, None], out)
        return out

    bias = [lookup(rel_table[:, gi * N_HEADS:(gi + 1) * N_HEADS], buckets[gi]) for gi in range(4)]
    tabs = dict(cos=cos, sin=sin, bias=bias)

    act, act16, saved = xs, xs.astype(BF16), []
    for l in range(DEPTH):
        lw = _layer_weights(fulls[l], d)
        ride = fused is not None and l == 0
        act, act16, res = _layer_fwd(l, act, act16, lw, small, tabs, fused['fwd0'][0] if ride else None,
                                     fused['proj0'] if ride else None)
        if ride:
            fulls[1] = fused['fwd0'][1](res['comm'])
        saved.append(res)
    dy, loss_part = loss_head(act, target)

    grads, recv1, recv0_rest = [None] * DEPTH, None, None
    for l in reversed(range(DEPTH)):
        ride = fused is not None and l == 0
        dy, grads[l] = _layer_bwd(l, dy, saved[l], saved[l]['lw'], small, tabs,
                                  fused['bwd0'](grads[1]) if ride else None, fused['dw0'] if ride else None)
        if ride:
            recv1, recv0_rest = grads[l]['comm'], grads[l]['dw_comm']
    grad_x = dy

    rel_cols = []
    for gi in range(4):
        both = grads[0]['dbias'][gi] + grads[1]['dbias'][gi] if DEPTH == 2 else grads[0]['dbias'][gi]
        rel_cols.append(rel_grad(f"rel_grad{gi}", both, buckets[gi])[:, :REL_BUCKETS].T)
    g_rel = jnp.concatenate(rel_cols, axis=1)
    return loss_part, grad_x, grads, g_rel, recv1, recv0_rest
```

```python
import functools
import math

import numpy as np
import jax
import jax.numpy as jnp
from jax import lax
from jax.experimental import pallas as pl
from jax.experimental.pallas import tpu as pltpu

F32 = jnp.float32
BF16 = jnp.bfloat16

N_DEV = 8
DEPTH = 2
HEAD_DIM = 64
BLOCK = 128
A_GROUPS = ((128, 1), (512, 4), (2048, 16))
N_HEADS = 8
HW = N_HEADS * HEAD_DIM
B_KV_HEADS = 2
C_Q_RANK = 256
C_KV_RANK = 128
C_NOPE = 64
C_ROPE = 32
ROPE_BASE = 10000.0
REL_BUCKETS = 32
REL_MAX_DIST = 2048
ALPHA = (2 * DEPTH) ** 0.25
LN_EPS = 1e-5
RMS_EPS = 1e-6
NEG = -1e30
ADAM_LR = 0.001
ADAM_B1 = 0.9
ADAM_B2 = 0.999
ADAM_EPS = 1e-08
ADAM_WD = 0.01
ADAM_STEP = 10

LANE = 128
TM = 256
TM_LIGHT = 512
VMEM_LIMIT = 56 * 1024 * 1024
MESH_ID = pl.DeviceIdType.MESH

WEIGHTS = ('rel_table', 'w_in', 'b_gate', 'sinks', 'q_norm_g', 'kv_norm_g', 'w_uq', 'w_ukv', 'w_branch',
           'w_out', 'ln1_g', 'ln1_b', 'w_ffn_up', 'conv_w', 'conv_b', 'w_ffn_down', 'ln2_g', 'ln2_b')
SHARDED = {'w_in': 2, 'w_uq': 2, 'w_ukv': 2, 'w_branch': 3, 'w_out': 1, 'w_ffn_up': 2, 'conv_w': 2,
           'w_ffn_down': 1}
GATHER_BF16 = ('w_in', 'w_uq', 'w_ukv', 'w_branch', 'w_out', 'w_ffn_up', 'w_ffn_down')


def _params(sem=None):
    return pltpu.CompilerParams(dimension_semantics=sem, vmem_limit_bytes=VMEM_LIMIT)


def _pick(n, target):
    if n <= target:
        return n
    best = None
    for t in range(LANE, target + 1, LANE):
        if n % t == 0:
            best = t
    assert best is not None, (n, target)
    return best


def _dot(a, b, ca, cb):
    return lax.dot_general(a.astype(BF16), b.astype(BF16), (((ca,), (cb,)), ((), ())),
                           preferred_element_type=F32)


@jax.custom_vjp
def mm(a, b):
    return _dot(a, b, 1, 0)


def _mm_fwd(a, b):
    return _dot(a, b, 1, 0), (a, b)


def _mm_bwd(res, g):
    a, b = res
    return _dot(g, b, 1, 1), _dot(a, g, 0, 0)


mm.defvjp(_mm_fwd, _mm_bwd)


@jax.custom_vjp
def mm_nt(a, b):
    return _dot(a, b, 1, 1)


def _mm_nt_fwd(a, b):
    return _dot(a, b, 1, 1), (a, b)


def _mm_nt_bwd(res, g):
    a, b = res
    return _dot(g, b, 1, 0), _dot(g, a, 0, 0)


mm_nt.defvjp(_mm_nt_fwd, _mm_nt_bwd)


def _split_impl(x, n):
    w = x.shape[-1] // n
    return tuple(x[:, i * w:(i + 1) * w] for i in range(n))


@functools.partial(jax.custom_vjp, nondiff_argnums=(1,))
def split_lanes(x, n):
    return _split_impl(x, n)


def _split_fwd(x, n):
    return _split_impl(x, n), None


def _split_bwd(n, _, gs):
    return (jnp.concatenate(gs, axis=-1),)


split_lanes.defvjp(_split_fwd, _split_bwd)


@jax.custom_vjp
def concat_lanes(xs):
    return jnp.concatenate(xs, axis=-1)


def _concat_fwd(xs):
    return jnp.concatenate(xs, axis=-1), len(xs)


def _concat_bwd(n, g):
    return (_split_impl(g, n),)


concat_lanes.defvjp(_concat_fwd, _concat_bwd)


@jax.custom_vjp
def concat_rows(a, b):
    return jnp.concatenate([a, b], axis=0)


def _crow_fwd(a, b):
    return jnp.concatenate([a, b], axis=0), a.shape[0]


def _crow_bwd(na, g):
    return g[:na], g[na:]


concat_rows.defvjp(_crow_fwd, _crow_bwd)


def _rot_half(x):
    n = x.shape[-1]
    lane = lax.broadcasted_iota(jnp.int32, (1, n), 1) % LANE
    lo = (lane >= C_NOPE) & (lane < C_NOPE + C_ROPE // 2)
    hi = (lane >= C_NOPE + C_ROPE // 2) & (lane < C_NOPE + C_ROPE)
    up = pltpu.roll(x, n - C_ROPE // 2, 1)
    dn = pltpu.roll(x, C_ROPE // 2, 1)
    return jnp.where(lo, -up, jnp.where(hi, dn, 0.0))


@jax.custom_vjp
def rope(x, c, s):
    return x * c + _rot_half(x) * s


def _rope_fwd(x, c, s):
    return x * c + _rot_half(x) * s, (c, s)


def _rope_bwd(res, g):
    c, s = res
    return g * c - _rot_half(g * s), jnp.zeros_like(c), jnp.zeros_like(s)


rope.defvjp(_rope_fwd, _rope_bwd)


def _sigmoid(x):
    return 0.5 * jnp.tanh(0.5 * x) + 0.5


MATMUL_VMEM = 44 * 1024 * 1024


def _matmul_tiles(m, n, k, a_bytes, b_bytes, o_bytes, has_add):
    tm, tn, tk = _pick(m, 1536), _pick(n, 2560), _pick(k, 2560)

    def need(tn_, tk_):
        acc = tm * tn_ * 4 if k // tk_ > 1 else 0
        return 2 * (tm * tk_ * a_bytes + tk_ * tn_ * b_bytes + tm * tn_ * o_bytes) + acc + 2 * tm * tn_ * 4 * int(has_add)

    while need(tn, tk) > MATMUL_VMEM:
        if tk >= tn and tk > LANE:
            tk = _pick(k, tk - LANE)
        else:
            tn = _pick(n, tn - LANE)
    return tm, tn, tk


def matmul(name, a, b, mode, out_dtype=F32, add=None, scale=1.0, comm=None):
    if mode == 'nn':
        (m, k), n = a.shape, b.shape[1]
    elif mode == 'nt':
        (m, k), n = a.shape, b.shape[0]
    else:
        (k, m), n = a.shape, b.shape[1]
    tm, tn, tk = _matmul_tiles(m, n, k, a.dtype.itemsize, b.dtype.itemsize, jnp.dtype(out_dtype).itemsize,
                               add is not None)
    nk = k // tk
    grid = (m // tm, n // tn, nk)
    c_in, c_specs, c_shapes, c_sems, c_hook = _ride_along(comm, grid)
    nc = len(c_in)
    a_spec = pl.BlockSpec((tk, tm), lambda i, j, kk: (kk, i)) if mode == 'tn' else \
        pl.BlockSpec((tm, tk), lambda i, j, kk: (i, kk))
    b_spec = pl.BlockSpec((tn, tk), lambda i, j, kk: (j, kk)) if mode == 'nt' else \
        pl.BlockSpec((tk, tn), lambda i, j, kk: (kk, j))
    o_spec = pl.BlockSpec((tm, tn), lambda i, j, kk: (i, j))
    ca = 0 if mode == 'tn' else 1
    cb = 1 if mode == 'nt' else 0
    has_add = add is not None

    n_in = 2 + int(has_add)

    def body(*refs):
        a_ref, b_ref = refs[:2]
        add_ref = refs[2] if has_add else None
        o_ref = refs[n_in + nc]
        c_hook(refs[n_in:n_in + nc], refs[n_in + nc + 1:n_in + 2 * nc + 1], refs[len(refs) - 3:])

        def finish(r):
            if has_add:
                r = r + scale * add_ref[...]
            o_ref[...] = r.astype(out_dtype)

        if nk == 1:
            finish(_dot(a_ref[...], b_ref[...], ca, cb))
            return
        acc_ref = refs[n_in + 2 * nc + 1]
        kk = pl.program_id(2)

        @pl.when(kk == 0)
        def _():
            acc_ref[...] = jnp.zeros_like(acc_ref)

        acc_ref[...] += _dot(a_ref[...], b_ref[...], ca, cb)

        @pl.when(kk == nk - 1)
        def _():
            finish(acc_ref[...])

    ins = [a, b] + ([add] if has_add else []) + c_in
    specs = [a_spec, b_spec] + ([o_spec] if has_add else []) + c_specs
    res = pl.pallas_call(
        body, name=name, grid=grid, in_specs=specs, out_specs=[o_spec] + c_specs,
        out_shape=[jax.ShapeDtypeStruct((m, n), out_dtype)] + c_shapes,
        scratch_shapes=([pltpu.VMEM((tm, tn), F32)] if nk > 1 else []) + c_sems,
        compiler_params=_params(("arbitrary",) * 3 if comm is not None else ("parallel", "parallel", "arbitrary")))(*ins)
    return res[0] if comm is None else (res[0], list(res[1:]))


def _row_spec(width, col_block=0, tm=TM):
    return pl.BlockSpec((tm, width), lambda i: (i, col_block))


def _full_spec(shape):
    nd = len(shape)
    return pl.BlockSpec(shape, lambda i: (0,) * nd)


def rows_fwd(name, fn, rows, params, outs, tm=TM):
    s = rows[0][0].shape[0]
    nr, npar = len(rows), len(params)

    def body(*refs):
        r = [refs[i][...].astype(F32) for i in range(nr)]
        p = [refs[nr + i][...] for i in range(npar)]
        res = fn(*r, *p)
        for o_ref, val in zip(refs[nr + npar:], res):
            o_ref[...] = val.astype(o_ref.dtype)

    return pl.pallas_call(
        body, name=name, grid=(s // tm,),
        in_specs=[_row_spec(w, cb, tm) for _, w, cb in rows] + [_full_spec(p.shape) for p in params],
        out_specs=[_row_spec(w, 0, tm) for w, _ in outs],
        out_shape=[jax.ShapeDtypeStruct((s, w), dt) for w, dt in outs],
        compiler_params=_params(("parallel",)))(*[a for a, _, _ in rows], *params)


def rows_bwd(name, fn, rows, params, cts, row_grads, tm=TM):
    s = rows[0][0].shape[0]
    nr, npar, nct, nrg = len(rows), len(params), len(cts), len(row_grads)

    def body(*refs):
        r = [refs[i][...].astype(F32) for i in range(nr)]
        p = [refs[nr + i][...].astype(F32) for i in range(npar)]
        g = tuple(refs[nr + npar + i][...].astype(F32) for i in range(nct))
        _, vjp = jax.vjp(lambda *a: tuple(fn(*a)), *r, *p)
        grads = vjp(g)
        outs = refs[nr + npar + nct:]
        for k, (idx, _) in enumerate(row_grads):
            outs[k][...] = grads[idx].astype(outs[k].dtype)

        @pl.when(pl.program_id(0) == 0)
        def _():
            for k in range(npar):
                outs[nrg + k][...] = jnp.zeros_like(outs[nrg + k])

        for k in range(npar):
            outs[nrg + k][...] += grads[nr + k]

    return pl.pallas_call(
        body, name=name, grid=(s // tm,),
        in_specs=[_row_spec(w, cb, tm) for _, w, cb in rows] + [_full_spec(p.shape) for p in params]
        + [_row_spec(w, 0, tm) for _, w in cts],
        out_specs=[_row_spec(rows[idx][1], 0, tm) for idx, _ in row_grads] + [_full_spec(p.shape) for p in params],
        out_shape=[jax.ShapeDtypeStruct((s, rows[idx][1]), dt) for idx, dt in row_grads]
        + [jax.ShapeDtypeStruct(p.shape, F32) for p in params],
        compiler_params=_params(("arbitrary",)))(*[a for a, _, _ in rows], *params, *[a for a, _ in cts])


def ln_fn(x, r, g, b):
    z = ALPHA * x + r
    mu = jnp.mean(z, axis=-1, keepdims=True)
    zc = z - mu
    var = jnp.mean(zc * zc, axis=-1, keepdims=True)
    return (zc * lax.rsqrt(var + LN_EPS) * g + b,)


def ln_twice(x, r, g, b):
    (y,) = ln_fn(x, r, g, b)
    return y, y


def _rms(x, g):
    return x * lax.rsqrt(jnp.mean(x * x, axis=-1, keepdims=True) + RMS_EPS) * g


def mla_q_fn(cq, c, s, g, wq):
    q = mm(_rms(cq, g), wq)
    return (rope(q, jnp.concatenate([c] * N_HEADS, axis=-1), jnp.concatenate([s] * N_HEADS, axis=-1)) * MLA_SCALE,)


@jax.custom_vjp
def tile_heads(x):
    return jnp.concatenate([x] * N_HEADS, axis=-1)


def _tile_fwd(x):
    return jnp.concatenate([x] * N_HEADS, axis=-1), None


def _tile_bwd(_, g):
    parts = _split_impl(g, N_HEADS)
    acc = parts[0]
    for p in parts[1:]:
        acc = acc + p
    return (acc,)


tile_heads.defvjp(_tile_fwd, _tile_bwd)


def mla_kv_fn(ckv, kr, c, s, g, wk, wv):
    n = _rms(ckv, g)
    lane = lax.broadcasted_iota(jnp.int32, (1, N_HEADS * LANE), 1) % LANE
    one = (lane == MLA_ONE).astype(F32)
    return mm(n, wk) + tile_heads(rope(kr, c, s)), mm(n, wv) + one


def merge_fn(y0, y1, y2, g0, g1, g2, b0, b1, b2):
    return (_sigmoid(g0 + b0) * y0 + _sigmoid(g1 + b1) * y1 + _sigmoid(g2 + b2) * y2,)


def combine_fn(o1, o2, o3, l1, l2, l3):
    mx = lax.stop_gradient(jnp.maximum(jnp.maximum(l1, l2), l3))
    e1, e2, e3 = jnp.exp(l1 - mx), jnp.exp(l2 - mx), jnp.exp(l3 - mx)
    return ((e1 * o1 + e2 * o2 + e3 * o3) / (e1 + e2 + e3),)


def loss_head(y, t):
    s, d = y.shape

    def body(y_ref, t_ref, dy_ref, loss_ref):
        err = y_ref[...] - t_ref[...]
        dy_ref[...] = err * (1.0 / d)

        @pl.when(pl.program_id(0) == 0)
        def _():
            loss_ref[...] = jnp.zeros_like(loss_ref)

        loss_ref[...] += jnp.sum(err * err) * (0.5 / d)

    return pl.pallas_call(
        body, name="loss_head", grid=(s // TM,), in_specs=[_row_spec(d), _row_spec(d)],
        out_specs=[_row_spec(d), _full_spec((8, LANE))],
        out_shape=[jax.ShapeDtypeStruct((s, d), F32), jax.ShapeDtypeStruct((8, LANE), F32)],
        compiler_params=_params(("arbitrary",)))(y, t)


CONV_TN = 256
CONV_TM = 1024


def _conv_taps(u_ext, w_ref, b_ref):
    d1, d2 = pltpu.roll(u_ext, 1, 0), pltpu.roll(u_ext, 2, 0)
    return w_ref[0:1, :] * d2 + w_ref[1:2, :] * d1 + w_ref[2:3, :] * u_ext + b_ref[...], d1, d2


def conv_glu_fwd(u, conv_w, conv_b):
    s, f2 = u.shape
    f = f2 // 2
    tn, tm = CONV_TN, min(CONV_TM, s)
    off = f // tn
    hb = tm // 8

    def body(ug, uv, hg, hv, wg, wv, bg, bv, o_ref):
        first = pl.program_id(0) == 0

        def conv(u_ref, h_ref, w_ref, b_ref):
            halo = jnp.where(first, 0.0, h_ref[...])
            ext = jnp.concatenate([halo, u_ref[...]], axis=0)
            return _conv_taps(ext, w_ref, b_ref)[0][8:]

        cg, cv = conv(ug, hg, wg, bg), conv(uv, hv, wv, bv)
        o_ref[...] = (cg * _sigmoid(cg) * cv).astype(o_ref.dtype)

    blk = lambda o: pl.BlockSpec((tm, tn), lambda i, j: (i, j + o))
    halo = lambda o: pl.BlockSpec((8, tn), lambda i, j: (jnp.maximum(i * hb - 1, 0), j + o))
    par = lambda r, o: pl.BlockSpec((r, tn), lambda i, j: (0, j + o))
    return pl.pallas_call(
        body, name="conv_glu_fwd", grid=(s // tm, off),
        in_specs=[blk(0), blk(off), halo(0), halo(off), par(3, 0), par(3, off), par(1, 0), par(1, off)],
        out_specs=pl.BlockSpec((tm, tn), lambda i, j: (i, j)),
        out_shape=jax.ShapeDtypeStruct((s, f), BF16),
        compiler_params=_params(("parallel", "parallel")))(u, u, u, u, conv_w, conv_w, conv_b, conv_b)


def conv_glu_bwd(u, dh, conv_w, conv_b):
    s, f2 = u.shape
    f = f2 // 2
    tn, tm = CONV_TN, min(CONV_TM, s)
    off = f // tn
    hb = tm // 8
    n_rows = s // tm

    def body(ug, uv, pg, pv, ng, nv, dh_ref, dhn_ref, wg, wv, bg, bv, dug, duv, dwg, dwv, dbg, dbv):
        i = pl.program_id(1)
        first, last = i == 0, i == n_rows - 1

        def ext(u_ref, p_ref, n_ref):
            return jnp.concatenate([jnp.where(first, 0.0, p_ref[...]), u_ref[...], n_ref[...]], axis=0)

        eg, ev = ext(ug, pg, ng), ext(uv, pv, nv)
        (cg, eg1, eg2), (cv, ev1, ev2) = _conv_taps(eg, wg, bg), _conv_taps(ev, wv, bv)
        dhe = jnp.concatenate([jnp.zeros((8, tn), F32), dh_ref[...], jnp.where(last, 0.0, dhn_ref[...])], axis=0)
        sg = _sigmoid(cg)
        dcg = dhe * cv * (sg * (1.0 + cg * (1.0 - sg)))
        dcv = dhe * (cg * sg)
        n = tm + 16

        @pl.when(i == 0)
        def _():
            for r in (dwg, dwv, dbg, dbv):
                r[...] = jnp.zeros_like(r)

        def back(dc, e, e1, e2, w_ref, du_ref, dw_ref, db_ref):
            du = w_ref[2:3, :] * dc + w_ref[1:2, :] * pltpu.roll(dc, n - 1, 0) + w_ref[0:1, :] * pltpu.roll(dc, n - 2, 0)
            du_ref[...] = du[8:8 + tm].astype(du_ref.dtype)
            own = dc[8:8 + tm]
            dw_ref[0:1, :] += jnp.sum(own * e2[8:8 + tm], axis=0, keepdims=True)
            dw_ref[1:2, :] += jnp.sum(own * e1[8:8 + tm], axis=0, keepdims=True)
            dw_ref[2:3, :] += jnp.sum(own * e[8:8 + tm], axis=0, keepdims=True)
            db_ref[...] += jnp.sum(own, axis=0, keepdims=True)

        back(dcg, eg, eg1, eg2, wg, dug, dwg, dbg)
        back(dcv, ev, ev1, ev2, wv, duv, dwv, dbv)

    blk = lambda o: pl.BlockSpec((tm, tn), lambda j, i: (i, j + o))
    prev = lambda o: pl.BlockSpec((8, tn), lambda j, i: (jnp.maximum(i * hb - 1, 0), j + o))
    nxt = lambda o: pl.BlockSpec((8, tn), lambda j, i: (jnp.minimum((i + 1) * hb, s // 8 - 1), j + o))
    par = lambda r, o: pl.BlockSpec((r, tn), lambda j, i: (0, j + o))
    return pl.pallas_call(
        body, name="conv_glu_bwd", grid=(off, n_rows),
        in_specs=[blk(0), blk(off), prev(0), prev(off), nxt(0), nxt(off), blk(0), nxt(0),
                  par(3, 0), par(3, off), par(1, 0), par(1, off)],
        out_specs=[blk(0), blk(0), par(3, 0), par(3, 0), par(1, 0), par(1, 0)],
        out_shape=[jax.ShapeDtypeStruct((s, f), BF16)] * 2 + [jax.ShapeDtypeStruct((3, f), F32)] * 2
        + [jax.ShapeDtypeStruct((1, f), F32)] * 2,
        compiler_params=_params(("parallel", "arbitrary")))(u, u, u, u, u, u, dh, dh, conv_w, conv_w, conv_b, conv_b)


def _band_fn(q, kp, kc, vp, vc, bias, sink, first, inclusive):
    k = concat_rows(kp, kc)
    v = concat_rows(vp, vc)
    qi = lax.broadcasted_iota(jnp.int32, (BLOCK, 2 * BLOCK), 0)
    ki = lax.broadcasted_iota(jnp.int32, (BLOCK, 2 * BLOCK), 1)
    step = BLOCK + qi - ki
    valid = (step >= 0) & ((step <= BLOCK) if inclusive else (step < BLOCK))
    valid = valid & (jnp.logical_not(first) | (ki >= BLOCK))
    lane = lax.broadcasted_iota(jnp.int32, (1, LANE), 1)
    scale = HEAD_DIM ** -0.5
    qs, ks, vs = split_lanes(q, 4), split_lanes(k, 4), split_lanes(v, 4)
    o_parts, l_parts = [], []
    for pair in range(4):
        o_acc, l_acc = None, None
        for sub in range(2):
            h = 2 * pair + sub
            mh = ((lane >= sub * HEAD_DIM) & (lane < (sub + 1) * HEAD_DIM)).astype(F32)
            logits = mm_nt(qs[pair] * mh, ks[pair]) * scale + bias[h]
            logits = jnp.where(valid, logits, NEG)
            m = lax.stop_gradient(jnp.max(logits, axis=-1, keepdims=True))
            if sink is not None:
                m = jnp.maximum(m, lax.stop_gradient(sink[h]))
            p = jnp.exp(logits - m)
            den = jnp.sum(p, axis=-1, keepdims=True)
            if sink is not None:
                den = den + jnp.exp(sink[h] - m)
            oh = mm(p * (1.0 / den), vs[pair] * mh)
            lh = (m + jnp.log(den)) * mh
            o_acc = oh if o_acc is None else o_acc + oh
            l_acc = lh if l_acc is None else l_acc + lh
        o_parts.append(o_acc)
        l_parts.append(l_acc)
    return concat_lanes(tuple(o_parts)), concat_lanes(tuple(l_parts))


def _band_specs(nb, cq, ck, cv):
    own = lambda n: jnp.minimum(n, nb - 1)
    prev = lambda n: jnp.maximum(own(n) - 1, 0)
    q = pl.BlockSpec((BLOCK, HW), lambda r, n: (r * nb + own(n), cq))
    kp = pl.BlockSpec((BLOCK, HW), lambda r, n: (r * nb + prev(n), ck))
    kc = pl.BlockSpec((BLOCK, HW), lambda r, n: (r * nb + own(n), ck))
    vp = pl.BlockSpec((BLOCK, HW), lambda r, n: (r * nb + prev(n), cv))
    vc = pl.BlockSpec((BLOCK, HW), lambda r, n: (r * nb + own(n), cv))
    return [q, kp, kc, vp, vc]


def to_classes(a, dil):
    if dil == 1:
        return a
    s, c = a.shape
    return a.reshape(s // dil, dil, c).transpose(1, 0, 2).reshape(s, c)


def from_classes(a, dil):
    if dil == 1:
        return a
    s, c = a.shape
    return a.reshape(dil, s // dil, c).transpose(1, 0, 2).reshape(s, c)


def _const_spec(shape):
    nd = len(shape)
    return pl.BlockSpec(shape, lambda r, n: (0,) * nd)


def band_fwd(name, qkv, dil, cols, bias, sink, inclusive):
    s = qkv.shape[0]
    nb = s // (BLOCK * dil)
    has_sink = sink is not None

    def body(*refs):
        q, kp, kc, vp, vc, b_ref = refs[:6]
        s_ref = refs[6] if has_sink else None
        o_ref, l_ref = refs[-2:]
        first = pl.program_id(1) == 0
        bias_l = tuple(b_ref[h] for h in range(N_HEADS))
        sink_l = tuple(s_ref[h] for h in range(N_HEADS)) if has_sink else None
        o, l = _band_fn(q[...], kp[...], kc[...], vp[...], vc[...], bias_l, sink_l, first, inclusive)
        o_ref[...] = o
        l_ref[...] = l

    out_spec = pl.BlockSpec((BLOCK, HW), lambda r, n: (r * nb + n, 0))
    ins = [qkv] * 5 + [bias] + ([sink] if has_sink else [])
    specs = _band_specs(nb, *cols) + [_const_spec(bias.shape)] + ([_const_spec(sink.shape)] if has_sink else [])
    return pl.pallas_call(
        body, name=name, grid=(dil, nb), in_specs=specs, out_specs=[out_spec, out_spec],
        out_shape=[jax.ShapeDtypeStruct((s, HW), F32)] * 2,
        compiler_params=_params(("parallel", "parallel")))(*ins)


def band_bwd(name, qkv, dil, cols, bias, sink, inclusive, do, dl):
    s = qkv.shape[0]
    nb = s // (BLOCK * dil)
    has_sink, has_dl = sink is not None, dl is not None
    n_in = 6 + int(has_sink) + 1 + int(has_dl)

    def body(*refs):
        q, kp, kc, vp, vc, b_ref = refs[:6]
        s_ref = refs[6] if has_sink else None
        do_ref = refs[6 + int(has_sink)]
        dl_ref = refs[7 + int(has_sink)] if has_dl else None
        out_ref, db_ref = refs[n_in:n_in + 2]
        ds_ref = refs[n_in + 2] if has_sink else None
        cq, ck, cv = refs[-3:]
        n = pl.program_id(1)

        def emit(dq, dk, dv):
            out_ref[:, 0:HW] = dq.astype(out_ref.dtype)
            out_ref[:, HW:2 * HW] = dk.astype(out_ref.dtype)
            out_ref[:, 2 * HW:3 * HW] = dv.astype(out_ref.dtype)

        @pl.when((pl.program_id(0) == 0) & (n == 0))
        def _():
            db_ref[...] = jnp.zeros_like(db_ref)
            if has_sink:
                ds_ref[...] = jnp.zeros_like(ds_ref)

        @pl.when(n < nb)
        def _():
            first = n == 0
            bias_l = tuple(b_ref[h] for h in range(N_HEADS))
            sink_l = tuple(s_ref[h] for h in range(N_HEADS)) if has_sink else ()

            def fn(q_, kp_, kc_, vp_, vc_, b_, s_):
                o, l = _band_fn(q_, kp_, kc_, vp_, vc_, b_, s_ if has_sink else None, first, inclusive)
                return (o, l) if has_dl else (o,)

            blocks = [r[...].astype(F32) for r in (q, kp, kc, vp, vc)]
            _, vjp = jax.vjp(fn, *blocks, bias_l, sink_l)
            ct = (do_ref[...].astype(F32), dl_ref[...]) if has_dl else (do_ref[...].astype(F32),)
            g = vjp(ct)

            @pl.when(n > 0)
            def _():
                emit(cq[...], ck[...] + g[1], cv[...] + g[3])

            cq[...] = g[0]
            ck[...] = g[2]
            cv[...] = g[4]
            for h in range(N_HEADS):
                db_ref[h] += g[5][h]
                if has_sink:
                    ds_ref[h] += g[6][h]

        @pl.when(n == nb)
        def _():
            emit(cq[...], ck[...], cv[...])

    blk = pl.BlockSpec((BLOCK, HW), lambda r, n: (r * nb + jnp.minimum(n, nb - 1), 0))
    late = pl.BlockSpec((BLOCK, 3 * HW), lambda r, n: (r * nb + jnp.maximum(n - 1, 0), 0))
    ins = [qkv] * 5 + [bias] + ([sink] if has_sink else []) + [do] + ([dl] if has_dl else [])
    specs = _band_specs(nb, *cols) + [_const_spec(bias.shape)] \
        + ([_const_spec(sink.shape)] if has_sink else []) + [blk] * (1 + int(has_dl))
    out_shape = [jax.ShapeDtypeStruct((s, 3 * HW), BF16), jax.ShapeDtypeStruct(bias.shape, F32)] \
        + ([jax.ShapeDtypeStruct(sink.shape, F32)] if has_sink else [])
    out_specs = [late, _const_spec(bias.shape)] + ([_const_spec(sink.shape)] if has_sink else [])
    res = pl.pallas_call(
        body, name=name, grid=(dil, nb + 1), in_specs=specs, out_specs=out_specs, out_shape=out_shape,
        scratch_shapes=[pltpu.VMEM((BLOCK, HW), F32)] * 3,
        compiler_params=_params(("arbitrary", "arbitrary")))(*ins)
    return res[0], res[1], (res[2] if has_sink else None)


def rel_grad(name, dbias, bucket):
    def body(d_ref, b_ref, o_ref):
        lane = lax.broadcasted_iota(jnp.int32, (1, LANE), 1)
        bk = b_ref[...]
        for h in range(N_HEADS):
            d = d_ref[h]
            row = jnp.zeros((1, LANE), F32)
            for b in range(REL_BUCKETS):
                tot = jnp.sum(jnp.where(bk == b, d, 0.0))
                row = row + jnp.where(lane == b, tot, 0.0)
            o_ref[h:h + 1, :] = row

    return pl.pallas_call(
        body, name=name, out_shape=jax.ShapeDtypeStruct((N_HEADS, LANE), F32),
        in_specs=[pl.BlockSpec(memory_space=pltpu.VMEM)] * 2, out_specs=pl.BlockSpec(memory_space=pltpu.VMEM),
        compiler_params=_params())(dbias, bucket)


MLA_T = 512
MLA_HPS = 2
MLA_SCALE = (C_NOPE + C_ROPE) ** -0.5


MLA_ONE = C_NOPE


def _lane_tiles(x, n):
    return jnp.concatenate([x] * n, axis=-1)


def _diag_mask(t):
    r = lax.broadcasted_iota(jnp.int32, (t, t), 0)
    c = lax.broadcasted_iota(jnp.int32, (t, t), 1)
    return c <= r


def _ride_along(comm, grid):
    if comm is None:
        return [], [], [], [], lambda *_: None
    arrays, scatter = comm
    out_shape, specs, sems = _push_shapes(arrays, scatter)

    def hook(ins, outs, sem_refs):
        ids = [pl.program_id(a) for a in range(len(grid))]
        first, last = ids[0] == 0, ids[0] == grid[0] - 1
        for a in range(1, len(grid)):
            first, last = first & (ids[a] == 0), last & (ids[a] == grid[a] - 1)

        @pl.when(first)
        def _():
            _pushes(ins, outs, sem_refs, scatter).start()

        @pl.when(last)
        def _():
            _pushes(ins, outs, sem_refs, scatter).wait()

    return list(arrays), specs, out_shape, sems, hook


def mla_fwd(name, q, k, v, comm=None):
    s = q.shape[0]
    t = min(MLA_T, s)
    nq = s // t
    grid = (N_HEADS // MLA_HPS, nq)
    c_in, c_specs, c_shapes, c_sems, c_hook = _ride_along(comm, grid)
    nc = len(c_in)

    def body(*refs):
        q_ref, k_ref, v_ref = refs[:3]
        o_ref, l_ref = refs[3 + nc:5 + nc]
        m_s, acc_s = refs[5 + 2 * nc:7 + 2 * nc]
        c_hook(refs[3:3 + nc], refs[5 + nc:5 + 2 * nc], refs[7 + 2 * nc:])
        i = pl.program_id(1)
        m_s[...] = jnp.full_like(m_s, NEG)
        acc_s[...] = jnp.zeros_like(acc_s)

        def chunk(j, masked):
            rows = pl.ds(pl.multiple_of(j * t, t), t)
            for hh in range(MLA_HPS):
                ln = slice(hh * LANE, (hh + 1) * LANE)
                sc = _dot(q_ref[:, ln], k_ref[rows, ln], 1, 1)
                if masked:
                    sc = jnp.where(_diag_mask(t), sc, NEG)
                m_old = m_s[:, ln]
                m_new = jnp.maximum(m_old, jnp.max(sc, axis=-1, keepdims=True))
                a = jnp.exp(m_old - m_new)
                p = jnp.exp(sc - _lane_tiles(m_new, t // LANE))
                acc_s[:, ln] = a * acc_s[:, ln] + _dot(p, v_ref[rows, ln], 1, 0)
                m_s[:, ln] = m_new

        def step(j, carry):
            chunk(j, False)
            return carry

        lax.fori_loop(0, i, step, 0)
        chunk(i, True)
        for hh in range(MLA_HPS):
            ln = slice(hh * LANE, (hh + 1) * LANE)
            acc = acc_s[:, ln]
            den = acc[:, MLA_ONE:MLA_ONE + 1]
            o_ref[:, ln] = acc / den
            l_ref[:, ln] = m_s[:, ln] + jnp.log(den)

    w = MLA_HPS * LANE
    qs = pl.BlockSpec((t, w), lambda h, i: (i, h))
    ks = pl.BlockSpec((s, w), lambda h, i: (0, h))
    res = pl.pallas_call(
        body, name=name, grid=grid, in_specs=[qs, ks, ks] + c_specs, out_specs=[qs, qs] + c_specs,
        out_shape=[jax.ShapeDtypeStruct((s, N_HEADS * LANE), F32)] * 2 + c_shapes,
        scratch_shapes=[pltpu.VMEM((t, w), F32)] * 2 + c_sems,
        compiler_params=_params(("arbitrary", "arbitrary")))(q, k, v, *c_in)
    return res[0], res[1], list(res[2:])


def mla_delta_fn(do, o):
    parts = []
    for a, b in zip(_split_impl(do, N_HEADS), _split_impl(o, N_HEADS)):
        parts.append(jnp.sum(a * b, axis=-1, keepdims=True) + jnp.zeros_like(a))
    return jnp.concatenate(parts, axis=-1), do


def mla_bwd(name, q, k, v, do, lse, delta, comm=None):
    s = q.shape[0]
    t = min(MLA_T, s)
    nq = s // t
    grid = (N_HEADS, nq)
    c_in, c_specs, c_shapes, c_sems, c_hook = _ride_along(comm, grid)
    nc = len(c_in)

    def body(*refs):
        q_ref, k_ref, v_ref, do_ref, l_ref, d_ref = refs[:6]
        dq_ref, dk_ref, dv_ref = refs[6 + nc:9 + nc]
        dk_s, dv_s = refs[9 + 2 * nc:11 + 2 * nc]
        c_hook(refs[6:6 + nc], refs[9 + nc:9 + 2 * nc], refs[11 + 2 * nc:])
        j = pl.program_id(1)

        @pl.when(j == 0)
        def _():
            dq_ref[...] = jnp.zeros_like(dq_ref)

        dk_s[...] = jnp.zeros_like(dk_s)
        dv_s[...] = jnp.zeros_like(dv_s)
        kb, vb = k_ref[...], v_ref[...]

        def chunk(i, masked):
            rows = pl.ds(pl.multiple_of(i * t, t), t)
            qi, doi = q_ref[rows, :], do_ref[rows, :]
            sc = _dot(qi, kb, 1, 1)
            if masked:
                sc = jnp.where(_diag_mask(t), sc, NEG)
            p = jnp.exp(sc - _lane_tiles(l_ref[rows, :], t // LANE))
            ds = p * (_dot(doi, vb, 1, 1) - _lane_tiles(d_ref[rows, :], t // LANE))
            dv_s[...] += _dot(p, doi, 0, 0)
            dk_s[...] += _dot(ds, qi, 0, 0)
            dq_ref[rows, :] += _dot(ds, kb, 1, 0)

        def pair(p, carry):
            chunk(j + 1 + 2 * p, False)
            chunk(j + 2 + 2 * p, False)
            return carry

        chunk(j, True)
        n_off = nq - 1 - j
        lax.fori_loop(0, n_off // 2, pair, 0)

        @pl.when(n_off % 2 == 1)
        def _():
            chunk(nq - 1, False)

        dk_ref[...] = dk_s[...]
        dv_ref[...] = dv_s[...]

    ks = pl.BlockSpec((t, LANE), lambda h, j: (j, h))
    full = pl.BlockSpec((s, LANE), lambda h, j: (0, h))
    res = pl.pallas_call(
        body, name=name, grid=grid, in_specs=[full, ks, ks, full, full, full] + c_specs,
        out_specs=[full, ks, ks] + c_specs,
        out_shape=[jax.ShapeDtypeStruct((s, N_HEADS * LANE), F32)] * 3 + c_shapes,
        scratch_shapes=[pltpu.VMEM((t, LANE), F32)] * 2 + c_sems,
        compiler_params=_params(("arbitrary", "arbitrary")))(q, k, v, do, lse, delta, *c_in)
    return res[0], res[1], res[2], list(res[3:])


def _peer(k):
    x, y, c = lax.axis_index("x"), lax.axis_index("y"), lax.axis_index("c")
    px, py, pc = (x ^ ((k >> 2) & 1)), (y ^ ((k >> 1) & 1)), (c ^ (k & 1))
    return (px, py, pc), 4 * px + 2 * py + pc


def exchange(name, arrays, scatter):
    n = len(arrays)

    def body(*refs):
        push = _pushes(refs[:n], refs[n:2 * n], refs[2 * n:], scatter)
        push.start()
        push.wait()

    out_shape, specs, sems = _push_shapes(arrays, scatter)
    return pl.pallas_call(body, name=name, out_shape=out_shape, in_specs=specs, out_specs=specs,
                          scratch_shapes=sems)(*arrays)


def _push_shapes(arrays, scatter):
    n = len(arrays)
    out_shape = [jax.ShapeDtypeStruct(a.shape if scatter else (N_DEV,) + a.shape, a.dtype) for a in arrays]
    n_sem = (N_DEV - 1) * n
    sems = [pltpu.SemaphoreType.DMA((n_sem,)), pltpu.SemaphoreType.DMA((n_sem,)), pltpu.SemaphoreType.DMA((n,))]
    return out_shape, [pl.BlockSpec(memory_space=pl.ANY)] * n, sems


class _pushes:
    def __init__(self, ins, outs, sems, scatter):
        self.ins, self.outs, self.sems, self.scatter = ins, outs, sems, scatter

    def _local(self):
        _, me_idx = _peer(0)
        return [pltpu.make_async_copy(self.ins[a].at[me_idx] if self.scatter else self.ins[a],
                                      self.outs[a].at[me_idx], self.sems[2].at[a]) for a in range(len(self.ins))]

    def _remote(self, incoming):
        send_sems, recv_sems, _ = self.sems
        n = len(self.ins)
        _, me_idx = _peer(0)
        copies = []
        for k in range(1, N_DEV):
            peer, peer_idx = _peer(k)
            for a in range(n):
                sem = (k - 1) * n + a
                copies.append(pltpu.make_async_remote_copy(
                    src_ref=self.ins[a].at[peer_idx] if self.scatter else self.ins[a],
                    dst_ref=self.outs[a].at[peer_idx if incoming else me_idx],
                    send_sem=send_sems.at[sem], recv_sem=recv_sems.at[sem], device_id=peer, device_id_type=MESH_ID))
        return copies

    def start(self):
        for cp in self._local() + self._remote(incoming=False):
            cp.start()

    def wait(self):
        for cp in self._remote(incoming=True):
            cp.wait_recv()
        for cp in self._remote(incoming=False):
            cp.wait_send()
        for cp in self._local():
            cp.wait()


ADAMW_BLOCK_ELEMS = 128 * 1024


def adamw_sum(name, parts, w, m, v):
    nl, r, c = w.shape
    fits = [t for t in range(16, r + 1, 16) if r % t == 0 and t * c <= ADAMW_BLOCK_ELEMS]
    tr = max(fits) if fits else r
    c1 = 1.0 - ADAM_B1 ** ADAM_STEP
    c2 = 1.0 - ADAM_B2 ** ADAM_STEP

    def body(*refs):
        p_refs = refs[:nl]
        w_ref, m_ref, v_ref, g_out, d_out, m_out, v_out = refs[nl:]
        for layer in range(nl):
            @pl.when(pl.program_id(0) == layer)
            def _():
                g = p_refs[layer][0].astype(F32)
                for d in range(1, N_DEV):
                    g = g + p_refs[layer][d].astype(F32)
                mn = ADAM_B1 * m_ref[...] + (1.0 - ADAM_B1) * g
                vn = ADAM_B2 * v_ref[...] + (1.0 - ADAM_B2) * (g * g)
                g_out[...] = g
                m_out[...] = mn
                v_out[...] = vn
                d_out[...] = -ADAM_LR * ((mn / c1) / (jnp.sqrt(vn / c2) + ADAM_EPS) + ADAM_WD * w_ref[...])

    row = pl.BlockSpec((None, tr, c), lambda l, i: (l, i, 0))
    return pl.pallas_call(
        body, name=name, grid=(nl, r // tr),
        in_specs=[pl.BlockSpec((N_DEV, tr, c), lambda l, i: (0, i, 0))] * nl + [row, row, row],
        out_specs=[row] * 4, out_shape=[jax.ShapeDtypeStruct((nl, r, c), F32)] * 4,
        compiler_params=_params(("parallel", "parallel")))(*parts, w, m, v)


def _pack(arrays, dtype, row_mult):
    flat = jnp.concatenate([a.astype(dtype).reshape(-1) for a in arrays])
    n = flat.shape[0]
    quantum = row_mult * LANE
    total = -(-n // quantum) * quantum
    return jnp.pad(flat, (0, total - n)).reshape(total // LANE, LANE)


def _unpack(packed, shapes):
    flat = packed.reshape(-1)
    out, pos = [], 0
    for shp in shapes:
        n = int(np.prod(shp))
        out.append(flat[pos:pos + n].reshape(shp))
        pos += n
    return out


def _rows(a):
    return a.reshape(-1, a.shape[-1])


def _unshard(gathered, shard_shape, axis):
    t = jnp.moveaxis(gathered.reshape((N_DEV,) + tuple(shard_shape)), 0, axis)
    return t.reshape(tuple(shard_shape[:axis]) + (N_DEV * shard_shape[axis],) + tuple(shard_shape[axis + 1:]))


def _to_shards(full, axis):
    shp = full.shape
    t = full.reshape(shp[:axis] + (N_DEV, shp[axis] // N_DEV) + shp[axis + 1:])
    t = jnp.moveaxis(t, axis, 0)
    return t.reshape(N_DEV, -1, t.shape[-1])


A_COLS = len(A_GROUPS) * 3 * HW
KV_COLS = B_KV_HEADS * HEAD_DIM
KV_REP = N_HEADS // B_KV_HEADS
O_BQ = A_COLS
O_BK = O_BQ + HW
O_BV = O_BK + KV_COLS
O_CQ = O_BV + KV_COLS
O_CKV = O_CQ + C_Q_RANK
O_KR = O_CKV + C_KV_RANK
O_GATE = O_KR + C_ROPE


def _proj_weight(w_in):
    d = w_in.shape[0]

    def per_q_head(lo):
        t = w_in[:, lo:lo + KV_COLS].reshape(d, B_KV_HEADS, 1, HEAD_DIM)
        return jnp.broadcast_to(t, (d, B_KV_HEADS, KV_REP, HEAD_DIM)).reshape(d, HW)

    kr = jnp.pad(w_in[:, O_KR:O_GATE], ((0, 0), (C_NOPE, LANE - C_NOPE - C_ROPE)))
    return jnp.concatenate([w_in[:, O_GATE:], w_in[:, :O_BK], per_q_head(O_BK), per_q_head(O_BV),
                            w_in[:, O_CQ:O_KR], kr], axis=1)


def _unproj_grad(dw_p, d_model):
    d = dw_p.shape[0]
    g = 3 * d_model
    p_bk = g + O_BK
    p_bv = p_bk + HW
    p_cq = p_bv + HW
    p_kr = p_cq + C_Q_RANK + C_KV_RANK

    def sum_heads(lo):
        return dw_p[:, lo:lo + HW].reshape(d, B_KV_HEADS, KV_REP, HEAD_DIM).sum(axis=2).reshape(d, KV_COLS)

    return jnp.concatenate([dw_p[:, g:p_bk], sum_heads(p_bk), sum_heads(p_bv), dw_p[:, p_cq:p_kr],
                            dw_p[:, p_kr + C_NOPE:p_kr + C_NOPE + C_ROPE], dw_p[:, :g]], axis=1)


def _pad_heads(w, per_head, lo, hi):
    r = w.shape[0]
    t = w.reshape(r, N_HEADS, per_head)[:, :, lo:hi]
    return jnp.pad(t, ((0, 0), (0, 0), (0, LANE - (hi - lo)))).reshape(r, N_HEADS * LANE)


def _unpad_heads(w, width):
    r = w.shape[0]
    return w.reshape(r, N_HEADS, LANE)[:, :, :width]


def _t5_bucket(dist):
    n = jnp.maximum(dist, 0)
    max_exact = REL_BUCKETS // 2
    scaled = jnp.log(jnp.maximum(n, 1).astype(F32) / max_exact) / math.log(REL_MAX_DIST / max_exact)
    large = max_exact + (scaled * (REL_BUCKETS - max_exact)).astype(jnp.int32)
    return jnp.where(n < max_exact, n, jnp.minimum(large, REL_BUCKETS - 1))


def _buckets(dil):
    qi = jnp.arange(BLOCK)[:, None]
    ki = jnp.arange(2 * BLOCK)[None, :]
    return _t5_bucket((BLOCK + qi - ki) * dil).astype(jnp.int32)


def _rope_tables(s):
    pos = jnp.arange(s, dtype=F32)
    inv_freq = ROPE_BASE ** (-jnp.arange(0, C_ROPE, 2, dtype=F32) / C_ROPE)
    ang = pos[:, None] * inv_freq[None, :]
    cos, sin = jnp.cos(ang), jnp.sin(ang)
    ones = jnp.ones((s, C_NOPE), F32)
    tail = LANE - C_NOPE - C_ROPE
    c = jnp.concatenate([ones, cos, cos, jnp.ones((s, tail), F32)], axis=1)
    sn = jnp.concatenate([jnp.zeros((s, C_NOPE), F32), sin, sin, jnp.zeros((s, tail), F32)], axis=1)
    return c, sn


ATTN_COLS = A_COLS + 3 * HW


def _cols(d_model):
    a = [(3 * i, 3 * i + 1, 3 * i + 2) for i in range(len(A_GROUPS))]
    b = (9, 10, 11)
    return a, b, 3 * d_model


def _layer_weights(full, d_model):
    lw = {}
    if 'w_in' in full:
        lw['w_in_p'] = _proj_weight(full['w_in'])
    if 'w_uq' in full:
        per = C_NOPE + C_ROPE
        w_uq, w_ukv, wb = full['w_uq'], full['w_ukv'], full['w_branch']
        wb2 = jnp.pad(wb[2].reshape(N_HEADS, HEAD_DIM, d_model), ((0, 0), (0, LANE - HEAD_DIM), (0, 0)))
        lw.update(
            wq=_pad_heads(w_uq, per, 0, per), wk=_pad_heads(w_ukv, 2 * C_NOPE, 0, C_NOPE),
            wv=_pad_heads(w_ukv, 2 * C_NOPE, C_NOPE, 2 * C_NOPE), wb0=wb[0], wb1=wb[1],
            wb2=wb2.reshape(N_HEADS * LANE, d_model), w_out=full['w_out'], w_up=full['w_ffn_up'],
            w_down=full['w_ffn_down'], conv_w=full['conv_w'])
    return lw


def _layer_fwd(l, x, x16, lw, small, tabs, comm=None, proj_comm=None):
    s, d = x.shape
    a_cols, b_cols, cq_lane = _cols(d)
    tag = f"l{l}_"
    g3 = 3 * d
    w_attn = lw['w_in_p'][:, g3:g3 + ATTN_COLS]
    w_rest = jnp.concatenate([lw['w_in_p'][:, :g3], lw['w_in_p'][:, g3 + ATTN_COLS:]], axis=1)
    if proj_comm is None:
        pa = matmul(tag + "proj_attn", x16, w_attn, 'nn', out_dtype=BF16)
    else:
        pa, got = matmul(tag + "proj_attn", x16, w_attn, 'nn', out_dtype=BF16, comm=proj_comm[0])
        lw = dict(lw, **_layer_weights(proj_comm[1](got), d))
    pr = matmul(tag + "proj_rest", x16, w_rest, 'nn')
    res = dict(x=x, x16=x16, pa=pa, pr=pr, lw=lw)
    outs, lses, slabs = [], [], []
    for gi, (window, dil) in enumerate(A_GROUPS):
        if dil == 1:
            slab, cols = pa, a_cols[gi]
        else:
            slab = to_classes(pa[:, a_cols[gi][0] * HW:(a_cols[gi][2] + 1) * HW], dil)
            cols = (0, 1, 2)
        o, lg = band_fwd(tag + f"a{gi}_fwd", slab, dil, cols, tabs['bias'][gi], None, True)
        outs.append(from_classes(o, dil))
        lses.append(from_classes(lg, dil))
        slabs.append((slab, cols))
    res['a_o'], res['a_l'], res['a_slabs'] = outs, lses, slabs
    (o_a,) = rows_fwd(tag + "a_combine", combine_fn, [(o, HW, 0) for o in outs] + [(lg, HW, 0) for lg in lses], [],
                      [(HW, BF16)])
    sink = small['sinks'][l].reshape(N_HEADS, 1, 1)
    o_b, _ = band_fwd(tag + "b_fwd", pa, 1, b_cols, tabs['bias'][3], sink, False)
    gq = small['q_norm_g'][l].reshape(1, -1)
    gkv = small['kv_norm_g'][l].reshape(1, -1)
    (qc,) = rows_fwd(tag + "c_q", mla_q_fn, [(pr, C_Q_RANK, cq_lane // C_Q_RANK), (tabs['cos'], LANE, 0),
                                            (tabs['sin'], LANE, 0)], [gq, lw['wq']], [(N_HEADS * LANE, BF16)])
    ckv_blk = (cq_lane + C_Q_RANK) // LANE
    kc, vc = rows_fwd(tag + "c_kv", mla_kv_fn, [(pr, LANE, ckv_blk), (pr, LANE, ckv_blk + 1), (tabs['cos'], LANE, 0),
                                                (tabs['sin'], LANE, 0)], [gkv, lw['wk'], lw['wv']],
                      [(N_HEADS * LANE, BF16)] * 2)
    o_c, lse_c, res['comm'] = mla_fwd(tag + "mla_fwd", qc, kc, vc, comm)
    res.update(o_a=o_a, o_b=o_b, o_c=o_c, lse_c=lse_c, qc=qc, kc=kc, vc=vc)
    bg = small['b_gate'][l].reshape(3, 1, d)
    ys = [matmul(tag + f"branch{i}", o, lw[f'wb{i}'], 'nn') for i, o in enumerate((o_a, o_b, o_c))]
    (merged,) = rows_fwd(tag + "merge", merge_fn, [(y, d, 0) for y in ys] + [(pr, d, i) for i in range(3)],
                         [bg[0], bg[1], bg[2]], [(d, BF16)])
    mix = matmul(tag + "out_proj", merged, lw['w_out'], 'nn')
    ln1 = [small['ln1_g'][l].reshape(1, d), small['ln1_b'][l].reshape(1, d)]
    x1, x1_16 = rows_fwd(tag + "ln1", ln_twice, [(x, d, 0), (mix, d, 0)], ln1, [(d, F32), (d, BF16)], tm=TM_LIGHT)
    u = matmul(tag + "ffn_up", x1_16, lw['w_up'], 'nn')
    h = conv_glu_fwd(u, lw['conv_w'], small['conv_b'][l].reshape(1, -1))
    ff = matmul(tag + "ffn_down", h, lw['w_down'], 'nn')
    ln2 = [small['ln2_g'][l].reshape(1, d), small['ln2_b'][l].reshape(1, d)]
    x2, x2_16 = rows_fwd(tag + "ln2", ln_twice, [(x1, d, 0), (ff, d, 0)], ln2, [(d, F32), (d, BF16)], tm=TM_LIGHT)
    res.update(merged=merged, mix=mix, x1=x1, x1_16=x1_16, u=u, h=h, ff=ff, ys=ys)
    return x2, x2_16, res


def _layer_bwd(l, dy, res, lw, small, tabs, comm=None, dw_comm=None):
    x, proj, pa = res['x'], res['pr'], res['pa']
    s, d = x.shape
    a_cols, b_cols, cq_lane = _cols(d)
    tag = f"l{l}_"
    g = {}
    ln2 = [small['ln2_g'][l].reshape(1, d), small['ln2_b'][l].reshape(1, d)]
    dff, g['ln2_g'], g['ln2_b'] = rows_bwd(tag + "ln2_bwd", ln_fn, [(res['x1'], d, 0), (res['ff'], d, 0)], ln2,
                                           [(dy, d)], [(1, F32)], tm=TM_LIGHT)
    g['w_ffn_down'] = matmul(tag + "dw_down", res['h'], dff, 'tn')
    dh = matmul(tag + "dh", dff, lw['w_down'], 'nt')
    conv_b = small['conv_b'][l].reshape(1, -1)
    dug, duv, dwg, dwv, dbg, dbv = conv_glu_bwd(res['u'], dh, lw['conv_w'], conv_b)
    du = jnp.concatenate([dug, duv], axis=1)
    g['conv_w'] = jnp.concatenate([dwg, dwv], axis=1)
    g['conv_b'] = jnp.concatenate([dbg, dbv], axis=1).reshape(-1)
    g['w_ffn_up'] = matmul(tag + "dw_up", res['x1_16'], du, 'tn')
    dx1 = matmul(tag + "dx1", du, lw['w_up'], 'nt', add=dff, scale=ALPHA)
    ln1 = [small['ln1_g'][l].reshape(1, d), small['ln1_b'][l].reshape(1, d)]
    dmix, g['ln1_g'], g['ln1_b'] = rows_bwd(tag + "ln1_bwd", ln_fn, [(x, d, 0), (res['mix'], d, 0)], ln1,
                                            [(dx1, d)], [(1, F32)], tm=TM_LIGHT)
    g['w_out'] = matmul(tag + "dw_out", res['merged'], dmix, 'tn')
    dmerged = matmul(tag + "dmerged", dmix, lw['w_out'], 'nt')
    bg = small['b_gate'][l].reshape(3, 1, d)
    mrows = [(y, d, 0) for y in res['ys']] + [(proj, d, i) for i in range(3)]
    (dy0, dy1, dy2, dg0, dg1, dg2, db0, db1, db2) = rows_bwd(
        tag + "merge_bwd", merge_fn, mrows, [bg[0], bg[1], bg[2]], [(dmerged, d)], [(i, BF16) for i in range(6)])
    g['b_gate'] = jnp.concatenate([db0, db1, db2], axis=1).reshape(-1)
    branch_in = (res['o_a'], res['o_b'], res['o_c'])
    dwb = [matmul(tag + f"dw_branch{i}", o, dyi, 'tn') for i, (o, dyi) in enumerate(zip(branch_in, (dy0, dy1, dy2)))]
    dwb[2] = dwb[2].reshape(N_HEADS, LANE, d)[:, :HEAD_DIM].reshape(HW, d)
    g['w_branch'] = jnp.stack(dwb, axis=0)
    do_a, do_b, do_c = [matmul(tag + f"do_branch{i}", dyi, lw[f'wb{i}'], 'nt')
                        for i, dyi in enumerate((dy0, dy1, dy2))]
    cw = N_HEADS * LANE
    delta, do_c16 = rows_fwd(tag + "c_delta", mla_delta_fn, [(do_c, cw, 0), (res['o_c'], cw, 0)], [],
                             [(cw, F32), (cw, BF16)])
    dqc, dkc, dvc, g['comm'] = mla_bwd(tag + "mla_bwd", res['qc'], res['kc'], res['vc'], do_c16, res['lse_c'], delta,
                                       comm)
    gq = small['q_norm_g'][l].reshape(1, -1)
    gkv = small['kv_norm_g'][l].reshape(1, -1)
    dcq, dgq, dwq = rows_bwd(tag + "c_q_bwd", mla_q_fn,
                             [(proj, C_Q_RANK, cq_lane // C_Q_RANK), (tabs['cos'], LANE, 0), (tabs['sin'], LANE, 0)],
                             [gq, lw['wq']], [(dqc, N_HEADS * LANE)], [(0, BF16)])
    ckv_blk = (cq_lane + C_Q_RANK) // LANE
    dckv, dkr, dgkv, dwk, dwv = rows_bwd(
        tag + "c_kv_bwd", mla_kv_fn,
        [(proj, LANE, ckv_blk), (proj, LANE, ckv_blk + 1), (tabs['cos'], LANE, 0), (tabs['sin'], LANE, 0)],
        [gkv, lw['wk'], lw['wv']], [(dkc, N_HEADS * LANE), (dvc, N_HEADS * LANE)], [(0, BF16), (1, BF16)])
    g['q_norm_g'], g['kv_norm_g'] = dgq.reshape(-1), dgkv.reshape(-1)
    per = C_NOPE + C_ROPE
    g['w_uq'] = _unpad_heads(dwq, per).reshape(C_Q_RANK, N_HEADS * per)
    g['w_ukv'] = jnp.concatenate([_unpad_heads(dwk, C_NOPE), _unpad_heads(dwv, C_NOPE)], axis=2).reshape(
        C_KV_RANK, N_HEADS * 2 * C_NOPE)
    sink = small['sinks'][l].reshape(N_HEADS, 1, 1)
    piece_b, dbias_b, dsink = band_bwd(tag + "b_bwd", pa, 1, b_cols, tabs['bias'][3], sink, False, do_b, None)
    pieces_b = [piece_b]
    g['sinks'] = dsink.reshape(-1)
    dbias = [None] * 4
    dbias[3] = dbias_b
    combo = [(o, HW, 0) for o in res['a_o']] + [(lg, HW, 0) for lg in res['a_l']]
    a_cts = rows_bwd(tag + "a_combine_bwd", combine_fn, combo, [], [(do_a, HW)],
                     [(i, BF16) for i in range(3)] + [(i, F32) for i in range(3, 6)])
    pieces_a = []
    for gi, (window, dil) in enumerate(A_GROUPS):
        slab, cols = res['a_slabs'][gi]
        group, dbias[gi], _ = band_bwd(
            tag + f"a{gi}_bwd", slab, dil, cols, tabs['bias'][gi], None, True, to_classes(a_cts[gi], dil),
            to_classes(a_cts[3 + gi], dil))
        pieces_a.append(from_classes(group, dil))
    g['dbias'] = dbias
    dproj = jnp.concatenate([dg0, dg1, dg2] + pieces_a + pieces_b + [dcq, dckv, dkr], axis=1)
    if dw_comm is None:
        dw_in_p = matmul(tag + "dw_in", res['x16'], dproj, 'tn')
    else:
        dw_in_p, g['dw_comm'] = matmul(tag + "dw_in", res['x16'], dproj, 'tn', comm=dw_comm(g))
    g['w_in'] = _unproj_grad(dw_in_p, d)
    dx = matmul(tag + "dx", dproj, lw['w_in_p'], 'nt', add=dmix, scale=ALPHA)
    return dx, g


def kernel(x, rel_table, w_in, b_gate, sinks, q_norm_g, kv_norm_g, w_uq, w_ukv, w_branch, w_out, ln1_g, ln1_b, w_ffn_up, conv_w, conv_b, w_ffn_down, ln2_g, ln2_b, loss_target, m_rel_table, m_w_in, m_b_gate, m_sinks, m_q_norm_g, m_kv_norm_g, m_w_uq, m_w_ukv, m_w_branch, m_w_out, m_ln1_g, m_ln1_b, m_w_ffn_up, m_conv_w, m_conv_b, m_w_ffn_down, m_ln2_g, m_ln2_b, v_rel_table, v_w_in, v_b_gate, v_sinks, v_q_norm_g, v_kv_norm_g, v_w_uq, v_w_ukv, v_w_branch, v_w_out, v_ln1_g, v_ln1_b, v_w_ffn_up, v_conv_w, v_conv_b, v_w_ffn_down, v_ln2_g, v_ln2_b):
    args = locals()
    w = {n: args[n] for n in WEIGHTS}
    mom = {n: args["m_" + n] for n in WEIGHTS}
    var = {n: args["v_" + n] for n in WEIGHTS}
    s, d = x.shape[1], x.shape[2]
    xs = x.reshape(s, d)
    target = loss_target.reshape(s, d)

    big = [n for n in WEIGHTS if n in SHARDED]
    small_names = [n for n in WEIGHTS if n not in SHARDED]
    assert DEPTH == 2
    wire = lambda n: BF16 if n in GATHER_BF16 else F32
    rest = [n for n in big if n != 'w_in']

    def shards(l, names):
        return [_rows(w[n][l]).astype(wire(n)) for n in names]

    def to_full(names, gathered):
        return {n: _unshard(t, w[n].shape[1:], SHARDED[n] - 1) for n, t in zip(names, gathered)}

    def grad_shards(names, g):
        return [_to_shards(g[n], SHARDED[n] - 1).astype(wire(n)) for n in names]

    w_in0 = to_full(['w_in'], exchange("gather_w_in0", shards(0, ['w_in']), scatter=False))
    fused = dict(proj0=((shards(0, rest), False), lambda got: to_full(rest, got)),
                 fwd0=((shards(1, big), False), lambda got: to_full(big, got)),
                 bwd0=lambda g1: (grad_shards(big, g1), True),
                 dw0=lambda g0: (grad_shards(rest, g0), True))
    small = {n: w[n] for n in small_names}
    loss_part, grad_x, grads, g_rel, recv1, recv0_rest = _local_grads(xs, target, [w_in0, None], small, fused)
    grad_x = grad_x.reshape(x.shape)
    recv0 = dict(zip(rest, recv0_rest))
    (recv0['w_in'],) = exchange("exchange_dw_in0", grad_shards(['w_in'], grads[0]), scatter=True)

    big_res = [{}, {}, {}, {}]
    per_layer = lambda t: t.reshape((DEPTH, -1, t.shape[-1]))
    for i, n in enumerate(big):
        outs = adamw_sum("adamw_" + n, [recv0[n], recv1[i]], per_layer(w[n]), per_layer(mom[n]), per_layer(var[n]))
        for k in range(4):
            big_res[k][n] = outs[k].reshape(w[n].shape)

    g_small = {'rel_table': g_rel}
    for n in small_names:
        if n != 'rel_table':
            g_small[n] = jnp.stack([grads[l][n] for l in range(DEPTH)], axis=0)
    small_pack = _pack([g_small[n].reshape(w[n].shape) for n in small_names] + [loss_part[0, 0:1]], F32, 8)
    (small_all,) = exchange("gather_small_grads", [small_pack], scatter=False)
    pks = lambda tree: _pack([tree[n] for n in small_names] + [jnp.zeros((1,), F32)], F32, 8)[None]
    small_out = adamw_sum("adamw_replicated", [small_all], pks(w), pks(mom), pks(var))
    small_shapes = [w[n].shape for n in small_names] + [(1,)]
    small_res = [dict(zip(small_names + ['loss'], _unpack(o, small_shapes))) for o in small_out]

    loss = small_res[0]['loss'].reshape(())
    out = [loss, grad_x]
    for k in range(4):
        out += [big_res[k][n] if n in SHARDED else small_res[k][n] for n in WEIGHTS]
    return tuple(out)


def _local_grads(xs, target, fulls, small, fused=None):
    s, d = xs.shape
    rel_table = small['rel_table']
    fulls = list(fulls)

    cos, sin = _rope_tables(s)
    buckets = [_buckets(dil) for _, dil in A_GROUPS] + [_buckets(1)]
    def lookup(table, bucket):
        out = jnp.zeros((N_HEADS,) + bucket.shape, F32)
        for b in range(REL_BUCKETS):
            out = jnp.where((bucket == b)[None], table[b][:, None, None], out)
        return out

    bias = [lookup(rel_table[:, gi * N_HEADS:(gi + 1) * N_HEADS], buckets[gi]) for gi in range(4)]
    tabs = dict(cos=cos, sin=sin, bias=bias)

    act, act16, saved = xs, xs.astype(BF16), []
    for l in range(DEPTH):
        lw = _layer_weights(fulls[l], d)
        ride = fused is not None and l == 0
        act, act16, res = _layer_fwd(l, act, act16, lw, small, tabs, fused['fwd0'][0] if ride else None,
                                     fused['proj0'] if ride else None)
        if ride:
            fulls[1] = fused['fwd0'][1](res['comm'])
        saved.append(res)
    dy, loss_part = loss_head(act, target)

    grads, recv1, recv0_rest = [None] * DEPTH, None, None
    for l in reversed(range(DEPTH)):
        ride = fused is not None and l == 0
        dy, grads[l] = _layer_bwd(l, dy, saved[l], saved[l]['lw'], small, tabs,
                                  fused['bwd0'](grads[1]) if ride else None, fused['dw0'] if ride else None)
        if ride:
            recv1, recv0_rest = grads[l]['comm'], grads[l]['dw_comm']
    grad_x = dy

    rel_cols = []
    for gi in range(4):
        both = grads[0]['dbias'][gi] + grads[1]['dbias'][gi] if DEPTH == 2 else grads[0]['dbias'][gi]
        rel_cols.append(rel_grad(f"rel_grad{gi}", both, buckets[gi])[:, :REL_BUCKETS].T)
    g_rel = jnp.concatenate(rel_cols, axis=1)
    return loss_part, grad_x, grads, g_rel, recv1, recv0_rest
```
